```python
import jax, jax.numpy as jnp
from jax import lax
import numpy as np

D_MODEL = 2048
BATCH = 2
SEQ = 4096
DEPTH = 1
DEC_BATCH = 32
DEC_SEQ = 8
PAST_LEN = 8192
PAGE_SIZE = 128

N_HEADS = 16
HEAD_DIM = 128
N_KV_HEADS = 4
GROUP = N_HEADS // N_KV_HEADS
CMP_STRIDE = 16
CMP_LEN = 2 * CMP_STRIDE
CMP_HIDDEN = HEAD_DIM
SEL_LEN = 64
SEL_TOPK = 16
WINDOW = 512
WIN_Q_BLOCK = 128
SEL_Q_BLOCK = 64
FORCE_BONUS = 1e4
CONV_CH = D_MODEL // 2
CONV_WIDTH = 31
PEER_HEADS = 8
PEER_NKEYS = 128
PEER_EXPERTS = PEER_NKEYS * PEER_NKEYS
PEER_DKEY = 256
PEER_TOPK = 16
PEER_CHUNK = 128

QW = N_HEADS * HEAD_DIM
KVW = N_KV_HEADS * HEAD_DIM
IN_WIDTHS = (QW, KVW, KVW, KVW, KVW, KVW, KVW, 3 * N_HEADS, 2 * CONV_CH, 2 * D_MODEL)
IN_COLS = sum(IN_WIDTHS)
SCALE = HEAD_DIM ** -0.5
EPS = 1e-6
NEG = -1e30

kernel_name = 'nsa_conformer_peer_hybrid_step'


def rmsnorm(x, g):
    xf = x.astype(jnp.float32)
    y = xf * lax.rsqrt(jnp.mean(xf * xf, axis=-1, keepdims=True) + EPS)
    return (y * g.astype(jnp.float32)).astype(x.dtype)


def layernorm(x, g, b):
    xf = x.astype(jnp.float32)
    mu = jnp.mean(xf, axis=-1, keepdims=True)
    var = jnp.mean(jnp.square(xf - mu), axis=-1, keepdims=True)
    return ((xf - mu) * lax.rsqrt(var + EPS) * g.astype(jnp.float32) + b.astype(jnp.float32)).astype(x.dtype)


def masked_probs(s, mask):
    s = jnp.where(mask, s, NEG)
    m = jnp.max(s, axis=-1, keepdims=True)
    e = jnp.where(mask, jnp.exp(s - m), 0.0)
    return e / jnp.maximum(jnp.sum(e, axis=-1, keepdims=True), 1e-30)


def attn_probs(q, k, mask):
    s = jnp.einsum('...tgrd,...kgd->...grtk', q, k).astype(jnp.float32) * SCALE
    return masked_probs(s, mask[..., None, None, :, :])


def attn_out(p, v):
    return jnp.einsum('...grtk,...kgd->...tgrd', p.astype(v.dtype), v)


def gather_pages(pool, page_table):
    g = pool[page_table]
    return g.reshape(page_table.shape[0], -1, pool.shape[2], pool.shape[3])


def compress(kv, pe, w1, w2):
    B, L, G, D = kv.shape
    ch = kv.reshape(B, L // CMP_STRIDE, CMP_STRIDE, G, D)
    h_lo = jnp.einsum('bjlgd,ldh->bjgh', ch, w1[:CMP_STRIDE])
    h_hi = jnp.einsum('bjlgd,ldh->bjgh', ch, w1[CMP_STRIDE:])
    h = h_lo[:, :-1] + h_hi[:, 1:] + jnp.einsum('ld,ldh->h', pe, w1)
    return jnp.einsum('bigh,he->bige', jax.nn.gelu(h), w2)


def cmp_to_sel(n_cmp, n_sel):
    cs = jnp.arange(n_cmp)[:, None] * CMP_STRIDE
    ss = jnp.arange(n_sel)[None, :] * SEL_LEN
    ov = jnp.clip(jnp.minimum(cs + CMP_LEN, ss + SEL_LEN) - jnp.maximum(cs, ss), 0, None)
    return ov.astype(jnp.float32) / CMP_LEN


def sel_attend(q, kb, vb, idx, valid, qpos):
    B, T, G, R, D = q.shape
    bi = jnp.arange(B)[:, None, None, None]
    gi = jnp.arange(G)[None, :, None, None]
    kg = kb[bi, idx, :, gi].reshape(B, G, T, -1, D)
    vg = vb[bi, idx, :, gi].reshape(B, G, T, -1, D)
    kpos = idx[..., None] * SEL_LEN + jnp.arange(SEL_LEN)
    mask = (valid[..., None] & (kpos <= qpos[:, None, None])).reshape(B, G, T, -1)
    s = jnp.einsum('btgrd,bgtkd->bgrtk', q, kg).astype(jnp.float32) * SCALE
    p = masked_probs(s, mask[:, :, None])
    return jnp.einsum('bgrtk,bgtkd->btgrd', p.astype(vg.dtype), vg)


def nsa_cmp_sel(q, k_c, v_c, k_s, v_s, qpos, lp, sweep_queries):
    B, T, G, R, D = q.shape
    L = k_c.shape[1]
    L_pad = -(-L // SEL_LEN) * SEL_LEN
    pad = ((0, 0), (0, L_pad - L), (0, 0), (0, 0))
    k_c, v_c, k_s, v_s = jnp.pad(k_c, pad), jnp.pad(v_c, pad), jnp.pad(k_s, pad), jnp.pad(v_s, pad)
    k_cmp = compress(k_c, lp['cmp_pe_k'], lp['cmp_w1_k'], lp['cmp_w2_k'])
    v_cmp = compress(v_c, lp['cmp_pe_v'], lp['cmp_w1_v'], lp['cmp_w2_v'])
    n_cmp = k_cmp.shape[1]
    cmp_end = jnp.arange(n_cmp) * CMP_STRIDE + CMP_LEN - 1
    p_cmp = attn_probs(q, k_cmp, cmp_end[None, :] <= qpos[:, None])
    o_cmp = attn_out(p_cmp, v_cmp)
    n_sel = L_pad // SEL_LEN
    p_slc = jnp.einsum('bgrti,ij->bgtj', p_cmp, cmp_to_sel(n_cmp, n_sel))
    blk = jnp.arange(n_sel)[None, :]
    cur = (qpos // SEL_LEN)[:, None]
    forced = (blk == 0) | (blk == cur) | (blk == cur - 1)
    score = jnp.where(blk <= cur, p_slc + jnp.where(forced, FORCE_BONUS, 0.0), NEG)
    top_s, top_i = lax.top_k(score, min(SEL_TOPK, n_sel))
    valid = top_s > 0.5 * NEG
    kb = k_s.reshape(B, n_sel, SEL_LEN, G, D)
    vb = v_s.reshape(B, n_sel, SEL_LEN, G, D)
    if sweep_queries:
        nq = T // SEL_Q_BLOCK
        xs = (q.reshape(B, nq, SEL_Q_BLOCK, G, R, D).swapaxes(0, 1),
              top_i.reshape(B, G, nq, SEL_Q_BLOCK, -1).transpose(2, 0, 1, 3, 4),
              valid.reshape(B, G, nq, SEL_Q_BLOCK, -1).transpose(2, 0, 1, 3, 4),
              qpos.reshape(nq, SEL_Q_BLOCK))
        o = lax.map(lambda a: sel_attend(a[0], kb, vb, a[1], a[2], a[3]), xs)
        o_sel = o.swapaxes(0, 1).reshape(B, T, G, R, D)
    else:
        xs = (q[:, None], kb[:, None], vb[:, None], top_i[:, None], valid[:, None])
        o_sel = lax.map(lambda a: sel_attend(a[0], a[1], a[2], a[3], a[4], qpos)[0], xs)
    return o_cmp, o_sel


def window_banded(q, k, v):
    B, T, G, R, D = q.shape
    nb = T // WIN_Q_BLOCK
    pad = ((0, 0), (WINDOW, 0), (0, 0), (0, 0))
    kidx = jnp.arange(nb)[:, None] * WIN_Q_BLOCK + jnp.arange(WINDOW + WIN_Q_BLOCK)[None, :]
    kblk = jnp.pad(k, pad)[:, kidx]
    vblk = jnp.pad(v, pad)[:, kidx]
    kpos = kidx - WINDOW
    qpos = jnp.arange(T).reshape(nb, WIN_Q_BLOCK)
    d = qpos[:, :, None] - kpos[:, None, :]
    mask = (kpos[:, None, :] >= 0) & (d >= 0) & (d < WINDOW)
    o = attn_out(attn_probs(q.reshape(B, nb, WIN_Q_BLOCK, G, R, D), kblk, mask), vblk)
    return o.reshape(B, T, G, R, D)


def window_dense(q, k, v, qpos, kpos):
    d = qpos[:, None] - kpos[None, :]
    return attn_out(attn_probs(q, k, (d >= 0) & (d < WINDOW)), v)


def conv_module(u, buf, w_dw, b_dw, ln_g, ln_b):
    xp = jnp.concatenate([buf, u], axis=1)
    y = lax.conv_general_dilated(xp, w_dw[:, None, :], (1,), 'VALID',
                                 dimension_numbers=('NWC', 'WIO', 'NWC'),
                                 feature_group_count=CONV_CH) + b_dw
    return jax.nn.silu(layernorm(y, ln_g, ln_b)), xp[:, -(CONV_WIDTH - 1):]


def peer_ffn(xn, w_q, sub_k1, sub_k2, w_u, w_v):
    B, T, D = xn.shape
    n = B * T
    n_pad = -(-n // PEER_CHUNK) * PEER_CHUNK
    xf = jnp.pad(xn.reshape(n, D), ((0, n_pad - n), (0, 0))).reshape(n_pad // PEER_CHUNK, PEER_CHUNK, D)

    def chunk(xc):
        qh = (xc @ w_q).reshape(-1, PEER_HEADS, 2, PEER_DKEY // 2)
        s1 = jnp.einsum('nhd,hkd->nhk', qh[:, :, 0], sub_k1)
        s2 = jnp.einsum('nhd,hkd->nhk', qh[:, :, 1], sub_k2)
        v1, i1 = lax.top_k(s1, PEER_TOPK)
        v2, i2 = lax.top_k(s2, PEER_TOPK)
        cand = (v1[..., :, None] + v2[..., None, :]).reshape(-1, PEER_HEADS, PEER_TOPK * PEER_TOPK)
        cidx = (i1[..., :, None] * PEER_NKEYS + i2[..., None, :]).reshape(-1, PEER_HEADS, PEER_TOPK * PEER_TOPK)
        sc, pos = lax.top_k(cand, PEER_TOPK)
        eidx = jnp.take_along_axis(cidx, pos, axis=-1)
        gate = jax.nn.softmax(sc.astype(jnp.float32), axis=-1).astype(xc.dtype)
        act = jax.nn.gelu(jnp.einsum('nd,nhkd->nhk', xc, w_u[eidx]))
        return jnp.einsum('nhk,nhkd->nd', gate * act, w_v[eidx])

    return lax.map(chunk, xf).reshape(n_pad, D)[:n].reshape(B, T, D)


def mixer_inputs(xn, w_in):
    B, T, _ = xn.shape
    z = xn @ w_in
    cuts = [int(c) for c in np.cumsum(IN_WIDTHS)[:-1]]
    q, k_c, v_c, k_s, v_s, k_w, v_w, g, glu, gm = jnp.split(z, cuts, axis=-1)
    kv = lambda a: a.reshape(B, T, N_KV_HEADS, HEAD_DIM)
    q = q.reshape(B, T, N_KV_HEADS, GROUP, HEAD_DIM)
    g = jax.nn.sigmoid(g.reshape(B, T, N_KV_HEADS, GROUP, 3))
    a, b = jnp.split(glu, 2, axis=-1)
    u = a * jax.nn.sigmoid(b)
    g_a, g_b = jnp.split(jax.nn.sigmoid(gm), 2, axis=-1)
    return q, kv(k_c), kv(v_c), kv(k_s), kv(v_s), kv(k_w), kv(v_w), g, u, g_a, g_b


def layer_forward(x, lp, past, page_table, prompt):
    B, T, _ = x.shape
    xn = rmsnorm(x, lp['norm_mix'])
    q, k_c, v_c, k_s, v_s, k_w, v_w, g_nsa, u, g_a, g_b = mixer_inputs(xn, lp['w_in'])
    if prompt:
        qpos = jnp.arange(T)
        kc_all, vc_all, ks_all, vs_all = k_c, v_c, k_s, v_s
        o_win = window_banded(q, k_w, v_w)
        wl = min(WINDOW, T)
        new_wk, new_wv = k_w[:, T - wl:], v_w[:, T - wl:]
        conv_buf = jnp.zeros((B, CONV_WIDTH - 1, CONV_CH), u.dtype)
    else:
        qpos = PAST_LEN + jnp.arange(T)
        kc_all = jnp.concatenate([gather_pages(past['cmp_k'], page_table), k_c], axis=1)
        vc_all = jnp.concatenate([gather_pages(past['cmp_v'], page_table), v_c], axis=1)
        ks_all = jnp.concatenate([gather_pages(past['sel_k'], page_table), k_s], axis=1)
        vs_all = jnp.concatenate([gather_pages(past['sel_v'], page_table), v_s], axis=1)
        wl = past['win_k'].shape[1]
        kw_all = jnp.concatenate([past['win_k'], k_w], axis=1)
        vw_all = jnp.concatenate([past['win_v'], v_w], axis=1)
        kpos = PAST_LEN - wl + jnp.arange(wl + T)
        o_win = window_dense(q, kw_all, vw_all, qpos, kpos)
        new_wk, new_wv = kw_all[:, -wl:], vw_all[:, -wl:]
        conv_buf = past['conv']
    o_cmp, o_sel = nsa_cmp_sel(q, kc_all, vc_all, ks_all, vs_all, qpos, lp, prompt)
    o_nsa = g_nsa[..., 0:1] * o_cmp + g_nsa[..., 1:2] * o_sel + g_nsa[..., 2:3] * o_win
    conv_y, new_conv = conv_module(u, conv_buf, lp['conv_w'], lp['conv_b'], lp['conv_ln_g'], lp['conv_ln_b'])
    merged = g_a * (o_nsa.reshape(B, T, QW) @ lp['w_nsa_out']) + g_b * (conv_y @ lp['w_conv_out'])
    h = x + merged @ lp['w_out']
    y = h + peer_ffn(rmsnorm(h, lp['norm_ffn']), lp['peer_wq'], lp['peer_k1'], lp['peer_k2'],
                     lp['peer_u'], lp['peer_v'])
    return y, (k_c, v_c, k_s, v_s, new_wk, new_wv, new_conv)


def setup_inputs(seed: int = 0) -> dict:
    key = jax.random.key(seed)
    it = iter(jax.random.split(key, 40))

    def nrm(shape, scale):
        return jax.random.normal(next(it), shape, jnp.float32) * scale

    n_pages = PAST_LEN // PAGE_SIZE
    n_used = DEC_BATCH * n_pages
    n_pool = n_used + (n_used + 3) // 4
    win_len = min(WINDOW, PAST_LEN)
    pool_shape = (DEPTH, n_pool, PAGE_SIZE, N_KV_HEADS, HEAD_DIM)
    win_shape = (DEPTH, DEC_BATCH, win_len, N_KV_HEADS, HEAD_DIM)
    x_prompt = nrm((BATCH, SEQ, D_MODEL), 1.0)
    x_sample = nrm((DEC_BATCH, DEC_SEQ, D_MODEL), 1.0)
    cache_cmp_k = nrm(pool_shape, 1.0)
    cache_cmp_v = nrm(pool_shape, 1.0)
    cache_sel_k = nrm(pool_shape, 1.0)
    cache_sel_v = nrm(pool_shape, 1.0)
    cache_win_k = nrm(win_shape, 1.0)
    cache_win_v = nrm(win_shape, 1.0)
    state_conv = nrm((DEPTH, DEC_BATCH, CONV_WIDTH - 1, CONV_CH), 0.5)
    page_table = jax.random.permutation(next(it), n_pool)[:n_used].reshape(DEC_BATCH, n_pages).astype(jnp.int32)
    return {
        'x_prompt': x_prompt, 'x_sample': x_sample,
        'cache_cmp_k': cache_cmp_k, 'cache_cmp_v': cache_cmp_v,
        'cache_sel_k': cache_sel_k, 'cache_sel_v': cache_sel_v,
        'cache_win_k': cache_win_k, 'cache_win_v': cache_win_v,
        'state_conv': state_conv, 'page_table': page_table,
        'norm_mix': 1.0 + nrm((DEPTH, D_MODEL), 0.02),
        'w_in': nrm((DEPTH, D_MODEL, IN_COLS), D_MODEL ** -0.5),
        'cmp_pe_k': nrm((DEPTH, CMP_LEN, HEAD_DIM), 0.5),
        'cmp_w1_k': nrm((DEPTH, CMP_LEN, HEAD_DIM, CMP_HIDDEN), (CMP_LEN * HEAD_DIM) ** -0.5),
        'cmp_w2_k': nrm((DEPTH, CMP_HIDDEN, HEAD_DIM), CMP_HIDDEN ** -0.5),
        'cmp_pe_v': nrm((DEPTH, CMP_LEN, HEAD_DIM), 0.5),
        'cmp_w1_v': nrm((DEPTH, CMP_LEN, HEAD_DIM, CMP_HIDDEN), (CMP_LEN * HEAD_DIM) ** -0.5),
        'cmp_w2_v': nrm((DEPTH, CMP_HIDDEN, HEAD_DIM), CMP_HIDDEN ** -0.5),
        'w_nsa_out': nrm((DEPTH, QW, D_MODEL), QW ** -0.5),
        'conv_w': nrm((DEPTH, CONV_WIDTH, CONV_CH), CONV_WIDTH ** -0.5),
        'conv_b': nrm((DEPTH, CONV_CH), 0.02),
        'conv_ln_g': 1.0 + nrm((DEPTH, CONV_CH), 0.02),
        'conv_ln_b': nrm((DEPTH, CONV_CH), 0.02),
        'w_conv_out': nrm((DEPTH, CONV_CH, D_MODEL), CONV_CH ** -0.5),
        'w_out': nrm((DEPTH, D_MODEL, D_MODEL), D_MODEL ** -0.5),
        'norm_ffn': 1.0 + nrm((DEPTH, D_MODEL), 0.02),
        'peer_wq': nrm((DEPTH, D_MODEL, PEER_HEADS * PEER_DKEY), D_MODEL ** -0.5),
        'peer_k1': nrm((DEPTH, PEER_HEADS, PEER_NKEYS, PEER_DKEY // 2), (PEER_DKEY // 2) ** -0.5),
        'peer_k2': nrm((DEPTH, PEER_HEADS, PEER_NKEYS, PEER_DKEY // 2), (PEER_DKEY // 2) ** -0.5),
        'peer_u': nrm((DEPTH, PEER_EXPERTS, D_MODEL), D_MODEL ** -0.5),
        'peer_v': nrm((DEPTH, PEER_EXPERTS, D_MODEL), PEER_HEADS ** -0.5),
        'norm_final': 1.0 + nrm((D_MODEL,), 0.02),
    }


def reference(x_prompt, x_sample, cache_cmp_k, cache_cmp_v, cache_sel_k, cache_sel_v,
              cache_win_k, cache_win_v, state_conv, page_table,
              norm_mix, w_in, cmp_pe_k, cmp_w1_k, cmp_w2_k, cmp_pe_v, cmp_w1_v, cmp_w2_v,
              w_nsa_out, conv_w, conv_b, conv_ln_g, conv_ln_b, w_conv_out, w_out, norm_ffn,
              peer_wq, peer_k1, peer_k2, peer_u, peer_v, norm_final):
    xp, xs = x_prompt, x_sample
    p_states, s_states = [], []
    for l in range(DEPTH):
        lp = {'norm_mix': norm_mix[l], 'w_in': w_in[l],
              'cmp_pe_k': cmp_pe_k[l], 'cmp_w1_k': cmp_w1_k[l], 'cmp_w2_k': cmp_w2_k[l],
              'cmp_pe_v': cmp_pe_v[l], 'cmp_w1_v': cmp_w1_v[l], 'cmp_w2_v': cmp_w2_v[l],
              'w_nsa_out': w_nsa_out[l], 'conv_w': conv_w[l], 'conv_b': conv_b[l],
              'conv_ln_g': conv_ln_g[l], 'conv_ln_b': conv_ln_b[l], 'w_conv_out': w_conv_out[l],
              'w_out': w_out[l], 'norm_ffn': norm_ffn[l], 'peer_wq': peer_wq[l],
              'peer_k1': peer_k1[l], 'peer_k2': peer_k2[l], 'peer_u': peer_u[l], 'peer_v': peer_v[l]}
        past = {'cmp_k': cache_cmp_k[l], 'cmp_v': cache_cmp_v[l], 'sel_k': cache_sel_k[l],
                'sel_v': cache_sel_v[l], 'win_k': cache_win_k[l], 'win_v': cache_win_v[l],
                'conv': state_conv[l]}
        xp, st_p = layer_forward(xp, lp, None, None, True)
        xs, st_s = layer_forward(xs, lp, past, page_table, False)
        p_states.append(st_p)
        s_states.append(st_s)
    p_ck, p_cv, p_sk, p_sv, p_wk, p_wv, p_conv = [jnp.stack(a) for a in zip(*p_states)]
    s_ck, s_cv, s_sk, s_sv, s_wk, s_wv, s_conv = [jnp.stack(a) for a in zip(*s_states)]
    y_prompt = rmsnorm(xp, norm_final)
    y_sample = rmsnorm(xs, norm_final)
    return (y_prompt, y_sample, p_ck, s_ck, p_cv, s_cv, p_sk, s_sk, p_sv, s_sv,
            p_wk, s_wk, p_wv, s_wv, p_conv, s_conv)
```

```python
import functools

import jax
import jax.numpy as jnp
import numpy as np
from jax import lax
from jax.experimental import pallas as pl
from jax.experimental.pallas import tpu as pltpu

D_MODEL = 2048
BATCH = 2
SEQ = 4096
DEPTH = 1
DEC_BATCH = 32
DEC_SEQ = 8
PAST_LEN = 8192
PAGE_SIZE = 128
N_HEADS = 16
HEAD_DIM = 128
N_KV_HEADS = 4
GROUP = N_HEADS // N_KV_HEADS
CMP_STRIDE = 16
CMP_LEN = 2 * CMP_STRIDE
SEL_LEN = 64
SEL_TOPK = 16
WINDOW = 512
WIN_Q_BLOCK = 128
SEL_Q_BLOCK = 64
FORCE_BONUS = 1e4
CONV_CH = D_MODEL // 2
CONV_WIDTH = 31
PEER_HEADS = 8
PEER_NKEYS = 128
PEER_DKEY = 256
PEER_TOPK = 16
PEER_CHUNK = 128
QW = N_HEADS * HEAD_DIM
KVW = N_KV_HEADS * HEAD_DIM
N_GATES = 3 * N_HEADS
SCALE = HEAD_DIM ** -0.5
EPS = 1e-6
NEG = -1e30

VMEM_LIMIT_BYTES = 48 * 1024 * 1024

GATE_PAD = 256
COL_Q = 0
COL_KC = COL_Q + QW
COL_VC = COL_KC + KVW
COL_KS = COL_VC + KVW
COL_VS = COL_KS + KVW
COL_KW = COL_VS + KVW
COL_VW = COL_KW + KVW
COL_GLU_A = COL_VW + KVW
COL_GLU_B = COL_GLU_A + CONV_CH
COL_GM_A = COL_GLU_B + CONV_CH
COL_GM_B = COL_GM_A + D_MODEL
COL_G = COL_GM_B + D_MODEL
N_COLS = COL_G + GATE_PAD


def _permute_w_in(w_in):
    g0 = QW + 6 * KVW
    g1 = g0 + N_GATES
    parts = [w_in[:, :g0], w_in[:, g1:], w_in[:, g0:g1],
             jnp.zeros((w_in.shape[0], GATE_PAD - N_GATES), w_in.dtype)]
    return jnp.concatenate(parts, axis=1).astype(jnp.bfloat16)


def _proj_in_kernel(x_ref, g_ref, w_ref, o_ref, xn_ref):
    @pl.when(pl.program_id(1) == 0)
    def _():
        x = x_ref[...]
        ms = jnp.mean(x * x, axis=-1, keepdims=True)
        xn_ref[...] = (x * lax.rsqrt(ms + EPS) * g_ref[...]).astype(jnp.bfloat16)

    o_ref[...] = jnp.dot(xn_ref[...], w_ref[...], preferred_element_type=jnp.float32)


def _proj_in(x, gain, w_bf16, tm, tn):
    n, d = x.shape
    nc = w_bf16.shape[1]
    return pl.pallas_call(
        _proj_in_kernel,
        out_shape=jax.ShapeDtypeStruct((n, nc), jnp.float32),
        grid=(n // tm, nc // tn),
        in_specs=[pl.BlockSpec((tm, d), lambda i, j: (i, 0)),
                  pl.BlockSpec((1, d), lambda i, j: (0, 0)),
                  pl.BlockSpec((d, tn), lambda i, j: (0, j))],
        out_specs=pl.BlockSpec((tm, tn), lambda i, j: (i, j)),
        scratch_shapes=[pltpu.VMEM((tm, d), jnp.bfloat16)],
        compiler_params=pltpu.CompilerParams(
            dimension_semantics=("parallel", "arbitrary"),
            vmem_limit_bytes=VMEM_LIMIT_BYTES),
        name="proj_in",
    )(x, gain.reshape(1, d), w_bf16)


def _merge_kernel(onsa_ref, convy_ref, ga_ref, gb_ref, wn_ref, wc_ref, o_ref):
    a = jnp.dot(onsa_ref[...].astype(jnp.bfloat16), wn_ref[...],
                preferred_element_type=jnp.float32)
    b = jnp.dot(convy_ref[...].astype(jnp.bfloat16), wc_ref[...],
                preferred_element_type=jnp.float32)
    o_ref[...] = (jax.nn.sigmoid(ga_ref[...]) * a
                  + jax.nn.sigmoid(gb_ref[...]) * b).astype(o_ref.dtype)


def _merge(o_nsa, conv_y, z, wn_bf16, wc_bf16, tm, tn):
    n = o_nsa.shape[0]
    ja = COL_GM_A // tn
    jb = COL_GM_B // tn
    return pl.pallas_call(
        _merge_kernel,
        out_shape=jax.ShapeDtypeStruct((n, D_MODEL), jnp.bfloat16),
        grid=(n // tm, D_MODEL // tn),
        in_specs=[pl.BlockSpec((tm, QW), lambda i, j: (i, 0)),
                  pl.BlockSpec((tm, CONV_CH), lambda i, j: (i, 0)),
                  pl.BlockSpec((tm, tn), lambda i, j: (i, ja + j)),
                  pl.BlockSpec((tm, tn), lambda i, j: (i, jb + j)),
                  pl.BlockSpec((QW, tn), lambda i, j: (0, j)),
                  pl.BlockSpec((CONV_CH, tn), lambda i, j: (0, j))],
        out_specs=pl.BlockSpec((tm, tn), lambda i, j: (i, j)),
        compiler_params=pltpu.CompilerParams(
            dimension_semantics=("parallel", "arbitrary"),
            vmem_limit_bytes=VMEM_LIMIT_BYTES),
        name="merge",
    )(o_nsa, conv_y, z, z, wn_bf16, wc_bf16)


def _out_proj_kernel(m_ref, x_ref, w_ref, g_ref, h_ref, hn_ref):
    h = x_ref[...] + jnp.dot(m_ref[...], w_ref[...], preferred_element_type=jnp.float32)
    h_ref[...] = h
    ms = jnp.mean(h * h, axis=-1, keepdims=True)
    hn_ref[...] = (h * lax.rsqrt(ms + EPS) * g_ref[...]).astype(hn_ref.dtype)


def _out_proj(merged, x, w_bf16, gain, tm):
    n = x.shape[0]
    return pl.pallas_call(
        _out_proj_kernel,
        out_shape=(jax.ShapeDtypeStruct((n, D_MODEL), jnp.float32),
                   jax.ShapeDtypeStruct((n, D_MODEL), jnp.float32)),
        grid=(n // tm,),
        in_specs=[pl.BlockSpec((tm, D_MODEL), lambda i: (i, 0)),
                  pl.BlockSpec((tm, D_MODEL), lambda i: (i, 0)),
                  pl.BlockSpec((D_MODEL, D_MODEL), lambda i: (0, 0)),
                  pl.BlockSpec((1, D_MODEL), lambda i: (0, 0))],
        out_specs=(pl.BlockSpec((tm, D_MODEL), lambda i: (i, 0)),
                   pl.BlockSpec((tm, D_MODEL), lambda i: (i, 0))),
        compiler_params=pltpu.CompilerParams(
            dimension_semantics=("parallel",),
            vmem_limit_bytes=VMEM_LIMIT_BYTES),
        name="out_proj",
    )(merged, x, w_bf16, gain.reshape(1, D_MODEL))


def _rmsnorm(x, g):
    xf = x.astype(jnp.float32)
    y = xf * lax.rsqrt(jnp.mean(xf * xf, axis=-1, keepdims=True) + EPS)
    return (y * g.astype(jnp.float32)).astype(x.dtype)


def _layernorm(x, g, b):
    xf = x.astype(jnp.float32)
    mu = jnp.mean(xf, axis=-1, keepdims=True)
    var = jnp.mean(jnp.square(xf - mu), axis=-1, keepdims=True)
    return ((xf - mu) * lax.rsqrt(var + EPS) * g.astype(jnp.float32) + b.astype(jnp.float32)).astype(x.dtype)


def _masked_probs(s, mask):
    s = jnp.where(mask, s, NEG)
    m = jnp.max(s, axis=-1, keepdims=True)
    e = jnp.where(mask, jnp.exp(s - m), 0.0)
    return e / jnp.maximum(jnp.sum(e, axis=-1, keepdims=True), 1e-30)


def _attn_probs(q, k, mask):
    s = jnp.einsum('...tgrd,...kgd->...grtk', q, k).astype(jnp.float32) * SCALE
    return _masked_probs(s, mask[..., None, None, :, :])


def _attn_out(p, v):
    return jnp.einsum('...grtk,...kgd->...tgrd', p.astype(v.dtype), v)


def _gather_pages(pool, page_table):
    g = pool[page_table]
    return g.reshape(page_table.shape[0], -1, pool.shape[2], pool.shape[3])


def _compress(kv, pe, w1, w2):
    B, L, G, D = kv.shape
    ch = kv.reshape(B, L // CMP_STRIDE, CMP_STRIDE, G, D)
    h_lo = jnp.einsum('bjlgd,ldh->bjgh', ch, w1[:CMP_STRIDE])
    h_hi = jnp.einsum('bjlgd,ldh->bjgh', ch, w1[CMP_STRIDE:])
    h = h_lo[:, :-1] + h_hi[:, 1:] + jnp.einsum('ld,ldh->h', pe, w1)
    return jnp.einsum('bigh,he->bige', jax.nn.gelu(h), w2)


def _cmp_to_sel(n_cmp, n_sel):
    cs = jnp.arange(n_cmp)[:, None] * CMP_STRIDE
    ss = jnp.arange(n_sel)[None, :] * SEL_LEN
    ov = jnp.clip(jnp.minimum(cs + CMP_LEN, ss + SEL_LEN) - jnp.maximum(cs, ss), 0, None)
    return ov.astype(jnp.float32) / CMP_LEN


def _sel_attend(q, kb, vb, idx, valid, qpos):
    B, T, G, R, D = q.shape
    bi = jnp.arange(B)[:, None, None, None]
    gi = jnp.arange(G)[None, :, None, None]
    kg = kb[bi, idx, :, gi].reshape(B, G, T, -1, D)
    vg = vb[bi, idx, :, gi].reshape(B, G, T, -1, D)
    kpos = idx[..., None] * SEL_LEN + jnp.arange(SEL_LEN)
    mask = (valid[..., None] & (kpos <= qpos[:, None, None])).reshape(B, G, T, -1)
    s = jnp.einsum('btgrd,bgtkd->bgrtk', q, kg).astype(jnp.float32) * SCALE
    p = _masked_probs(s, mask[:, :, None])
    return jnp.einsum('bgrtk,bgtkd->btgrd', p.astype(vg.dtype), vg)


def _nsa_cmp_sel(q, k_c, v_c, k_s, v_s, qpos, lp, sweep_queries):
    B, T, G, R, D = q.shape
    L = k_c.shape[1]
    L_pad = -(-L // SEL_LEN) * SEL_LEN
    pad = ((0, 0), (0, L_pad - L), (0, 0), (0, 0))
    k_c, v_c, k_s, v_s = jnp.pad(k_c, pad), jnp.pad(v_c, pad), jnp.pad(k_s, pad), jnp.pad(v_s, pad)
    k_cmp = _compress(k_c, lp['cmp_pe_k'], lp['cmp_w1_k'], lp['cmp_w2_k'])
    v_cmp = _compress(v_c, lp['cmp_pe_v'], lp['cmp_w1_v'], lp['cmp_w2_v'])
    n_cmp = k_cmp.shape[1]
    cmp_end = jnp.arange(n_cmp) * CMP_STRIDE + CMP_LEN - 1
    p_cmp = _attn_probs(q, k_cmp, cmp_end[None, :] <= qpos[:, None])
    o_cmp = _attn_out(p_cmp, v_cmp)
    n_sel = L_pad // SEL_LEN
    p_slc = jnp.einsum('bgrti,ij->bgtj', p_cmp, _cmp_to_sel(n_cmp, n_sel))
    blk = jnp.arange(n_sel)[None, :]
    cur = (qpos // SEL_LEN)[:, None]
    forced = (blk == 0) | (blk == cur) | (blk == cur - 1)
    score = jnp.where(blk <= cur, p_slc + jnp.where(forced, FORCE_BONUS, 0.0), NEG)
    top_s, top_i = lax.top_k(score, min(SEL_TOPK, n_sel))
    valid = top_s > 0.5 * NEG
    kb = k_s.reshape(B, n_sel, SEL_LEN, G, D)
    vb = v_s.reshape(B, n_sel, SEL_LEN, G, D)
    if sweep_queries:
        nq = T // SEL_Q_BLOCK
        xs = (q.reshape(B, nq, SEL_Q_BLOCK, G, R, D).swapaxes(0, 1),
              top_i.reshape(B, G, nq, SEL_Q_BLOCK, -1).transpose(2, 0, 1, 3, 4),
              valid.reshape(B, G, nq, SEL_Q_BLOCK, -1).transpose(2, 0, 1, 3, 4),
              qpos.reshape(nq, SEL_Q_BLOCK))
        o = lax.map(lambda a: _sel_attend(a[0], kb, vb, a[1], a[2], a[3]), xs)
        o_sel = o.swapaxes(0, 1).reshape(B, T, G, R, D)
    else:
        xs = (q[:, None], kb[:, None], vb[:, None], top_i[:, None], valid[:, None])
        o_sel = lax.map(lambda a: _sel_attend(a[0], a[1], a[2], a[3], a[4], qpos)[0], xs)
    return o_cmp, o_sel


def _window_banded(q, k, v):
    B, T, G, R, D = q.shape
    nb = T // WIN_Q_BLOCK
    pad = ((0, 0), (WINDOW, 0), (0, 0), (0, 0))
    kidx = jnp.arange(nb)[:, None] * WIN_Q_BLOCK + jnp.arange(WINDOW + WIN_Q_BLOCK)[None, :]
    kblk = jnp.pad(k, pad)[:, kidx]
    vblk = jnp.pad(v, pad)[:, kidx]
    kpos = kidx - WINDOW
    qpos = jnp.arange(T).reshape(nb, WIN_Q_BLOCK)
    d = qpos[:, :, None] - kpos[:, None, :]
    mask = (kpos[:, None, :] >= 0) & (d >= 0) & (d < WINDOW)
    o = _attn_out(_attn_probs(q.reshape(B, nb, WIN_Q_BLOCK, G, R, D), kblk, mask), vblk)
    return o.reshape(B, T, G, R, D)


def _window_dense(q, k, v, qpos, kpos):
    d = qpos[:, None] - kpos[None, :]
    return _attn_out(_attn_probs(q, k, (d >= 0) & (d < WINDOW)), v)


def _conv_module(u, buf, w_dw, b_dw, ln_g, ln_b):
    xp = jnp.concatenate([buf, u], axis=1)
    y = lax.conv_general_dilated(xp, w_dw[:, None, :], (1,), 'VALID',
                                 dimension_numbers=('NWC', 'WIO', 'NWC'),
                                 feature_group_count=CONV_CH) + b_dw
    return jax.nn.silu(_layernorm(y, ln_g, ln_b)), xp[:, -(CONV_WIDTH - 1):]


def _peer_ffn(xn, w_q, sub_k1, sub_k2, w_u, w_v):
    B, T, D = xn.shape
    n = B * T
    n_pad = -(-n // PEER_CHUNK) * PEER_CHUNK
    xf = jnp.pad(xn.reshape(n, D), ((0, n_pad - n), (0, 0))).reshape(n_pad // PEER_CHUNK, PEER_CHUNK, D)

    def chunk(xc):
        qh = (xc @ w_q).reshape(-1, PEER_HEADS, 2, PEER_DKEY // 2)
        s1 = jnp.einsum('nhd,hkd->nhk', qh[:, :, 0], sub_k1)
        s2 = jnp.einsum('nhd,hkd->nhk', qh[:, :, 1], sub_k2)
        v1, i1 = lax.top_k(s1, PEER_TOPK)
        v2, i2 = lax.top_k(s2, PEER_TOPK)
        cand = (v1[..., :, None] + v2[..., None, :]).reshape(-1, PEER_HEADS, PEER_TOPK * PEER_TOPK)
        cidx = (i1[..., :, None] * PEER_NKEYS + i2[..., None, :]).reshape(-1, PEER_HEADS, PEER_TOPK * PEER_TOPK)
        sc, pos = lax.top_k(cand, PEER_TOPK)
        eidx = jnp.take_along_axis(cidx, pos, axis=-1)
        gate = jax.nn.softmax(sc.astype(jnp.float32), axis=-1).astype(xc.dtype)
        act = jax.nn.gelu(jnp.einsum('nd,nhkd->nhk', xc, w_u[eidx]))
        return jnp.einsum('nhk,nhkd->nd', gate * act, w_v[eidx])

    return lax.map(chunk, xf).reshape(n_pad, D)[:n].reshape(B, T, D)


def _split_z(z, B, T):
    kv = lambda c: z[:, c:c + KVW].reshape(B, T, N_KV_HEADS, HEAD_DIM)
    q = z[:, COL_Q:COL_Q + QW].reshape(B, T, N_KV_HEADS, GROUP, HEAD_DIM)
    g = jax.nn.sigmoid(z[:, COL_G:COL_G + N_GATES].reshape(B, T, N_KV_HEADS, GROUP, 3))
    a = z[:, COL_GLU_A:COL_GLU_A + CONV_CH]
    b = z[:, COL_GLU_B:COL_GLU_B + CONV_CH]
    u = (a * jax.nn.sigmoid(b)).reshape(B, T, CONV_CH)
    return q, kv(COL_KC), kv(COL_VC), kv(COL_KS), kv(COL_VS), kv(COL_KW), kv(COL_VW), g, u


def _layer(x, lp, past, page_table, prompt, tm):
    B, T, _ = x.shape
    n = B * T
    x2 = x.reshape(n, D_MODEL)
    z = _proj_in(x2, lp['norm_mix'], lp['w_in_p'], tm, 768)
    q, k_c, v_c, k_s, v_s, k_w, v_w, g_nsa, u = _split_z(z, B, T)
    if prompt:
        qpos = jnp.arange(T)
        kc_all, vc_all, ks_all, vs_all = k_c, v_c, k_s, v_s
        o_win = _window_banded(q, k_w, v_w)
        wl = min(WINDOW, T)
        new_wk, new_wv = k_w[:, T - wl:], v_w[:, T - wl:]
        conv_buf = jnp.zeros((B, CONV_WIDTH - 1, CONV_CH), u.dtype)
    else:
        qpos = PAST_LEN + jnp.arange(T)
        kc_all = jnp.concatenate([_gather_pages(past['cmp_k'], page_table), k_c], axis=1)
        vc_all = jnp.concatenate([_gather_pages(past['cmp_v'], page_table), v_c], axis=1)
        ks_all = jnp.concatenate([_gather_pages(past['sel_k'], page_table), k_s], axis=1)
        vs_all = jnp.concatenate([_gather_pages(past['sel_v'], page_table), v_s], axis=1)
        wl = past['win_k'].shape[1]
        kw_all = jnp.concatenate([past['win_k'], k_w], axis=1)
        vw_all = jnp.concatenate([past['win_v'], v_w], axis=1)
        kpos = PAST_LEN - wl + jnp.arange(wl + T)
        o_win = _window_dense(q, kw_all, vw_all, qpos, kpos)
        new_wk, new_wv = kw_all[:, -wl:], vw_all[:, -wl:]
        conv_buf = past['conv']
    o_cmp, o_sel = _nsa_cmp_sel(q, kc_all, vc_all, ks_all, vs_all, qpos, lp, prompt)
    o_nsa = g_nsa[..., 0:1] * o_cmp + g_nsa[..., 1:2] * o_sel + g_nsa[..., 2:3] * o_win
    conv_y, new_conv = _conv_module(u, conv_buf, lp['conv_w'], lp['conv_b'], lp['conv_ln_g'], lp['conv_ln_b'])
    merged = _merge(o_nsa.reshape(n, QW), conv_y.reshape(n, CONV_CH), z,
                    lp['w_nsa_out_b'], lp['w_conv_out_b'], tm, 512)
    h, hn = _out_proj(merged, x2, lp['w_out_b'], lp['norm_ffn'], min(tm, 256))
    y = h.reshape(B, T, D_MODEL) + _peer_ffn(hn.reshape(B, T, D_MODEL), lp['peer_wq'], lp['peer_k1'],
                                             lp['peer_k2'], lp['peer_u'], lp['peer_v'])
    return y, (k_c, v_c, k_s, v_s, new_wk, new_wv, new_conv)


def kernel(x_prompt, x_sample, cache_cmp_k, cache_cmp_v, cache_sel_k, cache_sel_v, cache_win_k, cache_win_v, state_conv, page_table, norm_mix, w_in, cmp_pe_k, cmp_w1_k, cmp_w2_k, cmp_pe_v, cmp_w1_v, cmp_w2_v, w_nsa_out, conv_w, conv_b, conv_ln_g, conv_ln_b, w_conv_out, w_out, norm_ffn, peer_wq, peer_k1, peer_k2, peer_u, peer_v, norm_final):
    xp, xs = x_prompt, x_sample
    p_states, s_states = [], []
    for l in range(DEPTH):
        lp = {'norm_mix': norm_mix[l], 'w_in_p': _permute_w_in(w_in[l]),
              'cmp_pe_k': cmp_pe_k[l], 'cmp_w1_k': cmp_w1_k[l], 'cmp_w2_k': cmp_w2_k[l],
              'cmp_pe_v': cmp_pe_v[l], 'cmp_w1_v': cmp_w1_v[l], 'cmp_w2_v': cmp_w2_v[l],
              'w_nsa_out_b': w_nsa_out[l].astype(jnp.bfloat16), 'conv_w': conv_w[l], 'conv_b': conv_b[l],
              'conv_ln_g': conv_ln_g[l], 'conv_ln_b': conv_ln_b[l],
              'w_conv_out_b': w_conv_out[l].astype(jnp.bfloat16),
              'w_out_b': w_out[l].astype(jnp.bfloat16), 'norm_ffn': norm_ffn[l], 'peer_wq': peer_wq[l],
              'peer_k1': peer_k1[l], 'peer_k2': peer_k2[l], 'peer_u': peer_u[l], 'peer_v': peer_v[l]}
        past = {'cmp_k': cache_cmp_k[l], 'cmp_v': cache_cmp_v[l], 'sel_k': cache_sel_k[l],
                'sel_v': cache_sel_v[l], 'win_k': cache_win_k[l], 'win_v': cache_win_v[l],
                'conv': state_conv[l]}
        xp, st_p = _layer(xp, lp, None, None, True, 512)
        xs, st_s = _layer(xs, lp, past, page_table, False, 256)
        p_states.append(st_p)
        s_states.append(st_s)
    p_ck, p_cv, p_sk, p_sv, p_wk, p_wv, p_conv = [jnp.stack(a) for a in zip(*p_states)]
    s_ck, s_cv, s_sk, s_sv, s_wk, s_wv, s_conv = [jnp.stack(a) for a in zip(*s_states)]
    y_prompt = _rmsnorm(xp, norm_final)
    y_sample = _rmsnorm(xs, norm_final)
    return (y_prompt, y_sample, p_ck, s_ck, p_cv, s_cv, p_sk, s_sk, p_sv, s_sv,
            p_wk, s_wk, p_wv, s_wv, p_conv, s_conv)
```

```python
import functools

import jax
import jax.numpy as jnp
import numpy as np
from jax import lax
from jax.experimental import pallas as pl
from jax.experimental.pallas import tpu as pltpu

D_MODEL = 2048
BATCH = 2
SEQ = 4096
DEPTH = 1
DEC_BATCH = 32
DEC_SEQ = 8
PAST_LEN = 8192
PAGE_SIZE = 128
N_HEADS = 16
HEAD_DIM = 128
N_KV_HEADS = 4
GROUP = N_HEADS // N_KV_HEADS
CMP_STRIDE = 16
CMP_LEN = 2 * CMP_STRIDE
SEL_LEN = 64
SEL_SHIFT = SEL_LEN.bit_length() - 1
SEL_TOPK = 16
WINDOW = 512
WIN_Q_BLOCK = 128
SEL_Q_BLOCK = 64
FORCE_BONUS = 1e4
CONV_CH = D_MODEL // 2
CONV_WIDTH = 31
PEER_HEADS = 8
PEER_NKEYS = 128
PEER_DKEY = 256
PEER_TOPK = 16
PEER_CHUNK = 128
QW = N_HEADS * HEAD_DIM
KVW = N_KV_HEADS * HEAD_DIM
N_GATES = 3 * N_HEADS
SCALE = HEAD_DIM ** -0.5
EPS = 1e-6
NEG = -1e30

VMEM_LIMIT_BYTES = 48 * 1024 * 1024

GATE_PAD = 256
COL_Q = 0
COL_KC = COL_Q + QW
COL_VC = COL_KC + KVW
COL_KS = COL_VC + KVW
COL_VS = COL_KS + KVW
COL_KW = COL_VS + KVW
COL_VW = COL_KW + KVW
COL_GLU_A = COL_VW + KVW
COL_GLU_B = COL_GLU_A + CONV_CH
COL_GM_A = COL_GLU_B + CONV_CH
COL_GM_B = COL_GM_A + D_MODEL
COL_G = COL_GM_B + D_MODEL
N_COLS = COL_G + GATE_PAD


def _permute_w_in(w_in):
    g0 = QW + 6 * KVW
    g1 = g0 + N_GATES
    parts = [w_in[:, :g0], w_in[:, g1:], w_in[:, g0:g1],
             jnp.zeros((w_in.shape[0], GATE_PAD - N_GATES), w_in.dtype)]
    return jnp.concatenate(parts, axis=1).astype(jnp.bfloat16)


def _proj_in_kernel(x_ref, g_ref, w_ref, o_ref, xn_ref):
    @pl.when(pl.program_id(1) == 0)
    def _():
        x = x_ref[...]
        ms = jnp.mean(x * x, axis=-1, keepdims=True)
        xn_ref[...] = (x * lax.rsqrt(ms + EPS) * g_ref[...]).astype(jnp.bfloat16)

    o_ref[...] = jnp.dot(xn_ref[...], w_ref[...], preferred_element_type=jnp.float32)


def _proj_in(x, gain, w_bf16, tm, tn):
    n, d = x.shape
    nc = w_bf16.shape[1]
    return pl.pallas_call(
        _proj_in_kernel,
        out_shape=jax.ShapeDtypeStruct((n, nc), jnp.float32),
        grid=(n // tm, nc // tn),
        in_specs=[pl.BlockSpec((tm, d), lambda i, j: (i, 0)),
                  pl.BlockSpec((1, d), lambda i, j: (0, 0)),
                  pl.BlockSpec((d, tn), lambda i, j: (0, j))],
        out_specs=pl.BlockSpec((tm, tn), lambda i, j: (i, j)),
        scratch_shapes=[pltpu.VMEM((tm, d), jnp.bfloat16)],
        compiler_params=pltpu.CompilerParams(
            dimension_semantics=("parallel", "arbitrary"),
            vmem_limit_bytes=VMEM_LIMIT_BYTES),
        name="proj_in",
    )(x, gain.reshape(1, d), w_bf16)


_NT = (((1,), (1,)), ((), ()))
MASK_BIAS = -1e9


def _stack_heads(q):
    return jnp.concatenate([q[:, r * HEAD_DIM:(r + 1) * HEAD_DIM] for r in range(GROUP)], axis=0)


def _unstack_heads(o, tq):
    return jnp.concatenate([o[r * tq:(r + 1) * tq] for r in range(GROUP)], axis=1)


def _cmp_select_kernel(q_ref, kc_ref, vc_ref, ct_ref, ocmp_ref, bias_ref, *, tq):
    t0 = pl.program_id(2) * tq
    q4 = _stack_heads(q_ref[...]).astype(jnp.bfloat16)
    kc = kc_ref[0, 0].astype(jnp.bfloat16)
    vc = vc_ref[0, 0].astype(jnp.bfloat16)
    s = lax.dot_general(q4, kc, _NT, preferred_element_type=jnp.float32) * SCALE
    t = t0 + (lax.broadcasted_iota(jnp.int32, s.shape, 0) & (tq - 1))
    i = lax.broadcasted_iota(jnp.int32, s.shape, 1)
    valid = (i * CMP_STRIDE + (CMP_LEN - 1)) <= t
    s = jnp.where(valid, s, NEG)
    m = jnp.max(s, axis=-1, keepdims=True)
    e = jnp.where(valid, jnp.exp(s - m), 0.0)
    p = (e / jnp.maximum(jnp.sum(e, axis=-1, keepdims=True), 1e-30)).astype(jnp.bfloat16)
    ocmp_ref[...] = _unstack_heads(jnp.dot(p, vc, preferred_element_type=jnp.float32), tq)

    ct = ct_ref[...]
    n_sel = ct.shape[0]
    pslc = lax.dot_general(ct, p[0:tq], _NT, preferred_element_type=jnp.float32)
    for r in range(1, GROUP):
        pslc = pslc + lax.dot_general(ct, p[r * tq:(r + 1) * tq], _NT,
                                      preferred_element_type=jnp.float32)
    j = lax.broadcasted_iota(jnp.int32, (n_sel, tq), 0)
    cur = (t0 + lax.broadcasted_iota(jnp.int32, (n_sel, tq), 1)) >> SEL_SHIFT
    forced = (j == 0) | (j == cur) | (j == cur - 1)
    score = jnp.where(j <= cur, pslc + jnp.where(forced, FORCE_BONUS, 0.0), NEG)
    rank = jnp.zeros((n_sel, tq), jnp.float32)
    for jp in range(n_sel):
        sj = score[jp:jp + 1, :]
        rank = rank + jnp.where(sj > score, 1.0, jnp.where((sj == score) & (j > jp), 1.0, 0.0))
    bias = jnp.where((rank < SEL_TOPK) & (j <= cur), 0.0, MASK_BIAS)
    bias = jnp.concatenate([bias, jnp.zeros((HEAD_DIM - n_sel, tq), jnp.float32)], axis=0)
    bias_ref[0, 0] = bias.T.astype(jnp.bfloat16)


def _cmp_select(z, kcmp, vcmp, ct, B, T, tq):
    n_cp = kcmp.shape[2]
    nq = T // tq
    return pl.pallas_call(
        functools.partial(_cmp_select_kernel, tq=tq),
        out_shape=(jax.ShapeDtypeStruct((B * T, QW), jnp.float32),
                   jax.ShapeDtypeStruct((B, N_KV_HEADS, T, HEAD_DIM), jnp.bfloat16)),
        grid=(B, N_KV_HEADS, nq),
        in_specs=[pl.BlockSpec((tq, GROUP * HEAD_DIM), lambda b, g, qi: (b * nq + qi, g)),
                  pl.BlockSpec((1, 1, n_cp, HEAD_DIM), lambda b, g, qi: (b, g, 0, 0)),
                  pl.BlockSpec((1, 1, n_cp, HEAD_DIM), lambda b, g, qi: (b, g, 0, 0)),
                  pl.BlockSpec(ct.shape, lambda b, g, qi: (0, 0))],
        out_specs=(pl.BlockSpec((tq, GROUP * HEAD_DIM), lambda b, g, qi: (b * nq + qi, g)),
                   pl.BlockSpec((1, 1, tq, HEAD_DIM), lambda b, g, qi: (b, g, qi, 0))),
        compiler_params=pltpu.CompilerParams(
            dimension_semantics=("parallel", "parallel", "arbitrary"),
            vmem_limit_bytes=VMEM_LIMIT_BYTES),
        name="cmp_select",
    )(z, kcmp, vcmp, ct)


def _sel_attn_kernel(q_ref, bias_ref, k_ref, v_ref, o_ref, kaug, vb, m_sc, l_sc, acc_sc, *, tq, kc):
    qi = pl.program_id(2)
    t0 = qi * tq
    T = k_ref.shape[0]

    @pl.when(qi == 0)
    def _():
        kaug[:, 0:HEAD_DIM] = k_ref[...].astype(jnp.bfloat16)
        blk = lax.broadcasted_iota(jnp.int32, (T, HEAD_DIM), 0) >> SEL_SHIFT
        col = lax.broadcasted_iota(jnp.int32, (T, HEAD_DIM), 1)
        kaug[:, HEAD_DIM:2 * HEAD_DIM] = jnp.where(blk == col, 1.0, 0.0).astype(jnp.bfloat16)
        vb[...] = v_ref[...].astype(jnp.bfloat16)

    q = q_ref[...]
    bias = bias_ref[0, 0]
    qa = jnp.concatenate(
        [jnp.concatenate([q[:, r * HEAD_DIM:(r + 1) * HEAD_DIM].astype(jnp.bfloat16), bias], axis=1)
         for r in range(GROUP)], axis=0)
    m_sc[...] = jnp.full(m_sc.shape, NEG, jnp.float32)
    l_sc[...] = jnp.zeros(l_sc.shape, jnp.float32)
    acc_sc[...] = jnp.zeros(acc_sc.shape, jnp.float32)
    t = t0 + (lax.broadcasted_iota(jnp.int32, (GROUP * tq, kc), 0) & (tq - 1))
    col = lax.broadcasted_iota(jnp.int32, (GROUP * tq, kc), 1)

    def body(c, carry):
        k0 = pl.multiple_of(c * kc, kc)
        s = lax.dot_general(qa, kaug[pl.ds(k0, kc), :], _NT,
                            preferred_element_type=jnp.float32) * SCALE
        s = jnp.where(k0 + col <= t, s, NEG)
        m_old = m_sc[...]
        m_new = jnp.maximum(m_old, jnp.max(s, axis=-1, keepdims=True))
        alpha = jnp.exp(m_old - m_new)
        p = jnp.exp(s - m_new)
        l_sc[...] = alpha * l_sc[...] + jnp.sum(p, axis=-1, keepdims=True)
        acc_sc[...] = alpha * acc_sc[...] + jnp.dot(p.astype(jnp.bfloat16), vb[pl.ds(k0, kc), :],
                                                    preferred_element_type=jnp.float32)
        m_sc[...] = m_new
        return carry

    lax.fori_loop(0, (t0 + tq - 1) // kc + 1, body, 0)
    o_ref[...] = _unstack_heads(acc_sc[...] / jnp.maximum(l_sc[...], 1e-30), tq)


def _sel_attn(z, bias, B, T, tq, kc):
    nq = T // tq
    ck = COL_KS // HEAD_DIM
    cv = COL_VS // HEAD_DIM
    return pl.pallas_call(
        functools.partial(_sel_attn_kernel, tq=tq, kc=kc),
        out_shape=jax.ShapeDtypeStruct((B * T, QW), jnp.float32),
        grid=(B, N_KV_HEADS, nq),
        in_specs=[pl.BlockSpec((tq, GROUP * HEAD_DIM), lambda b, g, qi: (b * nq + qi, g)),
                  pl.BlockSpec((1, 1, tq, HEAD_DIM), lambda b, g, qi: (b, g, qi, 0)),
                  pl.BlockSpec((T, HEAD_DIM), lambda b, g, qi: (b, ck + g)),
                  pl.BlockSpec((T, HEAD_DIM), lambda b, g, qi: (b, cv + g))],
        out_specs=pl.BlockSpec((tq, GROUP * HEAD_DIM), lambda b, g, qi: (b * nq + qi, g)),
        scratch_shapes=[pltpu.VMEM((T, 2 * HEAD_DIM), jnp.bfloat16),
                        pltpu.VMEM((T, HEAD_DIM), jnp.bfloat16),
                        pltpu.VMEM((GROUP * tq, 1), jnp.float32),
                        pltpu.VMEM((GROUP * tq, 1), jnp.float32),
                        pltpu.VMEM((GROUP * tq, HEAD_DIM), jnp.float32)],
        compiler_params=pltpu.CompilerParams(
            dimension_semantics=("parallel", "parallel", "arbitrary"),
            vmem_limit_bytes=VMEM_LIMIT_BYTES),
        name="sel_attn",
    )(z, bias, z, z)


def _win_attn_kernel(q_ref, k_ref, v_ref, o_ref, kpad, vpad, *, tq):
    qi = pl.program_id(2)
    t0 = pl.multiple_of(qi * tq, tq)
    T = k_ref.shape[0]
    span = WINDOW + tq

    @pl.when(qi == 0)
    def _():
        kpad[0:WINDOW, :] = jnp.zeros((WINDOW, HEAD_DIM), jnp.bfloat16)
        vpad[0:WINDOW, :] = jnp.zeros((WINDOW, HEAD_DIM), jnp.bfloat16)
        kpad[WINDOW:WINDOW + T, :] = k_ref[...].astype(jnp.bfloat16)
        vpad[WINDOW:WINDOW + T, :] = v_ref[...].astype(jnp.bfloat16)

    q4 = _stack_heads(q_ref[...]).astype(jnp.bfloat16)
    s = lax.dot_general(q4, kpad[pl.ds(t0, span), :], _NT,
                        preferred_element_type=jnp.float32) * SCALE
    t = t0 + (lax.broadcasted_iota(jnp.int32, s.shape, 0) & (tq - 1))
    kpos = t0 - WINDOW + lax.broadcasted_iota(jnp.int32, s.shape, 1)
    d = t - kpos
    valid = (kpos >= 0) & (d >= 0) & (d < WINDOW)
    s = jnp.where(valid, s, NEG)
    m = jnp.max(s, axis=-1, keepdims=True)
    e = jnp.where(valid, jnp.exp(s - m), 0.0)
    p = (e / jnp.maximum(jnp.sum(e, axis=-1, keepdims=True), 1e-30)).astype(jnp.bfloat16)
    o_ref[...] = _unstack_heads(jnp.dot(p, vpad[pl.ds(t0, span), :],
                                        preferred_element_type=jnp.float32), tq)


def _win_attn(z, B, T, tq):
    nq = T // tq
    ck = COL_KW // HEAD_DIM
    cv = COL_VW // HEAD_DIM
    return pl.pallas_call(
        functools.partial(_win_attn_kernel, tq=tq),
        out_shape=jax.ShapeDtypeStruct((B * T, QW), jnp.float32),
        grid=(B, N_KV_HEADS, nq),
        in_specs=[pl.BlockSpec((tq, GROUP * HEAD_DIM), lambda b, g, qi: (b * nq + qi, g)),
                  pl.BlockSpec((T, HEAD_DIM), lambda b, g, qi: (b, ck + g)),
                  pl.BlockSpec((T, HEAD_DIM), lambda b, g, qi: (b, cv + g))],
        out_specs=pl.BlockSpec((tq, GROUP * HEAD_DIM), lambda b, g, qi: (b * nq + qi, g)),
        scratch_shapes=[pltpu.VMEM((T + WINDOW, HEAD_DIM), jnp.bfloat16),
                        pltpu.VMEM((T + WINDOW, HEAD_DIM), jnp.bfloat16)],
        compiler_params=pltpu.CompilerParams(
            dimension_semantics=("parallel", "parallel", "arbitrary"),
            vmem_limit_bytes=VMEM_LIMIT_BYTES),
        name="win_attn",
    )(z, z, z)


def _merge_kernel(oc_ref, os_ref, ow_ref, g_ref, convy_ref, ga_ref, gb_ref, wn_ref, wc_ref,
                  o_ref, onsa_ref):
    @pl.when(pl.program_id(1) == 0)
    def _():
        gz = jax.nn.sigmoid(g_ref[...])
        for h in range(N_HEADS):
            sl = slice(h * HEAD_DIM, (h + 1) * HEAD_DIM)
            o = (gz[:, 3 * h:3 * h + 1] * oc_ref[:, sl]
                 + gz[:, 3 * h + 1:3 * h + 2] * os_ref[:, sl]
                 + gz[:, 3 * h + 2:3 * h + 3] * ow_ref[:, sl])
            onsa_ref[:, sl] = o.astype(jnp.bfloat16)

    a = jnp.dot(onsa_ref[...], wn_ref[...], preferred_element_type=jnp.float32)
    b = jnp.dot(convy_ref[...].astype(jnp.bfloat16), wc_ref[...],
                preferred_element_type=jnp.float32)
    o_ref[...] = (jax.nn.sigmoid(ga_ref[...]) * a
                  + jax.nn.sigmoid(gb_ref[...]) * b).astype(o_ref.dtype)


def _merge(o_cmp, o_sel, o_win, conv_y, z, wn_bf16, wc_bf16, tm, tn):
    n = o_cmp.shape[0]
    ja = COL_GM_A // tn
    jb = COL_GM_B // tn
    jg = COL_G // GATE_PAD
    row = lambda i, j: (i, 0)
    return pl.pallas_call(
        _merge_kernel,
        out_shape=jax.ShapeDtypeStruct((n, D_MODEL), jnp.bfloat16),
        grid=(n // tm, D_MODEL // tn),
        in_specs=[pl.BlockSpec((tm, QW), row),
                  pl.BlockSpec((tm, QW), row),
                  pl.BlockSpec((tm, QW), row),
                  pl.BlockSpec((tm, GATE_PAD), lambda i, j: (i, jg)),
                  pl.BlockSpec((tm, CONV_CH), row),
                  pl.BlockSpec((tm, tn), lambda i, j: (i, ja + j)),
                  pl.BlockSpec((tm, tn), lambda i, j: (i, jb + j)),
                  pl.BlockSpec((QW, tn), lambda i, j: (0, j)),
                  pl.BlockSpec((CONV_CH, tn), lambda i, j: (0, j))],
        out_specs=pl.BlockSpec((tm, tn), lambda i, j: (i, j)),
        scratch_shapes=[pltpu.VMEM((tm, QW), jnp.bfloat16)],
        compiler_params=pltpu.CompilerParams(
            dimension_semantics=("parallel", "arbitrary"),
            vmem_limit_bytes=VMEM_LIMIT_BYTES),
        name="merge",
    )(o_cmp, o_sel, o_win, z, conv_y, z, z, wn_bf16, wc_bf16)


def _out_proj_kernel(m_ref, x_ref, w_ref, g_ref, h_ref, hn_ref):
    h = x_ref[...] + jnp.dot(m_ref[...], w_ref[...], preferred_element_type=jnp.float32)
    h_ref[...] = h
    ms = jnp.mean(h * h, axis=-1, keepdims=True)
    hn_ref[...] = (h * lax.rsqrt(ms + EPS) * g_ref[...]).astype(hn_ref.dtype)


def _out_proj(merged, x, w_bf16, gain, tm):
    n = x.shape[0]
    return pl.pallas_call(
        _out_proj_kernel,
        out_shape=(jax.ShapeDtypeStruct((n, D_MODEL), jnp.float32),
                   jax.ShapeDtypeStruct((n, D_MODEL), jnp.bfloat16)),
        grid=(n // tm,),
        in_specs=[pl.BlockSpec((tm, D_MODEL), lambda i: (i, 0)),
                  pl.BlockSpec((tm, D_MODEL), lambda i: (i, 0)),
                  pl.BlockSpec((D_MODEL, D_MODEL), lambda i: (0, 0)),
                  pl.BlockSpec((1, D_MODEL), lambda i: (0, 0))],
        out_specs=(pl.BlockSpec((tm, D_MODEL), lambda i: (i, 0)),
                   pl.BlockSpec((tm, D_MODEL), lambda i: (i, 0))),
        compiler_params=pltpu.CompilerParams(
            dimension_semantics=("parallel",),
            vmem_limit_bytes=VMEM_LIMIT_BYTES),
        name="out_proj",
    )(merged, x, w_bf16, gain.reshape(1, D_MODEL))


PEER_HALF = PEER_DKEY // 2
PEER_SEL = PEER_HEADS * PEER_TOPK
PEER_TOPK_SHIFT = PEER_TOPK.bit_length() - 1


def _topk_rows(s, k):
    r, n = s.shape
    row = lax.broadcasted_iota(jnp.int32, (r, n), 0)
    slot = lax.broadcasted_iota(jnp.int32, (k, n), 0)

    def body(i, carry):
        s, vals, idxs = carry
        m = jnp.max(s, axis=0, keepdims=True)
        j = jnp.min(jnp.where(s == m, row, r), axis=0, keepdims=True)
        s = jnp.where(row == j, -jnp.inf, s)
        vals = jnp.where(slot == i, m, vals)
        idxs = jnp.where(slot == i, j, idxs)
        return s, vals, idxs

    init = (s, jnp.zeros((k, n), jnp.float32), jnp.zeros((k, n), jnp.int32))
    _, vals, idxs = lax.fori_loop(0, k, body, init)
    return vals, idxs


def _pick_rows(idx, table):
    out = jnp.zeros(idx.shape, table.dtype)
    for a in range(PEER_TOPK):
        out = jnp.where(idx == a, table[a:a + 1, :], out)
    return out


def _peer_route_kernel(hn_ref, wq_ref, k1_ref, k2_ref, ia_ref, ib_ref, gt_ref, a_sc, b_sc, g_sc):
    qh = jnp.dot(hn_ref[...], wq_ref[...], preferred_element_type=jnp.float32).astype(jnp.bfloat16)
    for h in range(PEER_HEADS):
        q1 = qh[:, h * PEER_DKEY:h * PEER_DKEY + PEER_HALF]
        q2 = qh[:, h * PEER_DKEY + PEER_HALF:(h + 1) * PEER_DKEY]
        s1 = lax.dot_general(k1_ref[h], q1, _NT, preferred_element_type=jnp.float32)
        s2 = lax.dot_general(k2_ref[h], q2, _NT, preferred_element_type=jnp.float32)
        v1, i1 = _topk_rows(s1, PEER_TOPK)
        v2, i2 = _topk_rows(s2, PEER_TOPK)
        cand = jnp.concatenate([v1[a:a + 1, :] + v2 for a in range(PEER_TOPK)], axis=0)
        sc, pos = _topk_rows(cand, PEER_TOPK)
        ia = _pick_rows(pos >> PEER_TOPK_SHIFT, i1)
        ib = _pick_rows(pos & (PEER_TOPK - 1), i2)
        e = jnp.exp(sc - sc[0:1, :])
        gate = e / jnp.sum(e, axis=0, keepdims=True)
        rows = slice(h * PEER_TOPK, (h + 1) * PEER_TOPK)
        a_sc[rows, :] = ia.astype(jnp.float32)
        b_sc[rows, :] = ib.astype(jnp.float32)
        g_sc[rows, :] = gate
    ia_ref[...] = a_sc[...].T
    ib_ref[...] = b_sc[...].T
    gt_ref[...] = g_sc[...].T


def _peer_route(hn, wq_bf16, k1_bf16, k2_bf16, tm):
    n, d = hn.shape
    out = jax.ShapeDtypeStruct((n, PEER_SEL), jnp.float32)
    ospec = pl.BlockSpec((tm, PEER_SEL), lambda i: (i, 0))
    return pl.pallas_call(
        _peer_route_kernel,
        out_shape=(out, out, out),
        grid=(n // tm,),
        in_specs=[pl.BlockSpec((tm, d), lambda i: (i, 0)),
                  pl.BlockSpec(wq_bf16.shape, lambda i: (0, 0)),
                  pl.BlockSpec(k1_bf16.shape, lambda i: (0, 0, 0)),
                  pl.BlockSpec(k2_bf16.shape, lambda i: (0, 0, 0))],
        out_specs=(ospec, ospec, ospec),
        scratch_shapes=[pltpu.VMEM((PEER_SEL, tm), jnp.float32)] * 3,
        compiler_params=pltpu.CompilerParams(
            dimension_semantics=("parallel",),
            vmem_limit_bytes=VMEM_LIMIT_BYTES),
        name="peer_route",
    )(hn, wq_bf16, k1_bf16, k2_bf16)


def _peer_ffn_kernel(hn_ref, h_ref, ia_ref, ib_ref, gt_ref, u_ref, v_ref, gf_ref, y_ref,
                     w_sc, acc_ref, *, tm, n_i1):
    c = pl.program_id(1)

    @pl.when(c == 0)
    def _():
        acc_ref[...] = jnp.zeros(acc_ref.shape, jnp.float32)
        sub = lax.broadcasted_iota(jnp.int32, (PEER_NKEYS, PEER_SEL), 0).astype(jnp.float32)

        def token(n, carry):
            a_row = ia_ref[pl.ds(n, 1), :]
            b_row = ib_ref[pl.ds(n, 1), :]
            g_row = gt_ref[pl.ds(n, 1), :]
            g_hi = g_row.astype(jnp.bfloat16).astype(jnp.float32)
            g_lo = g_row - g_hi
            oa = jnp.where(sub == a_row, 1.0, 0.0).astype(jnp.bfloat16)
            hit_b = sub == b_row
            gb_hi = jnp.where(hit_b, g_hi, 0.0).astype(jnp.bfloat16)
            gb_lo = jnp.where(hit_b, g_lo, 0.0).astype(jnp.bfloat16)
            w = lax.dot_general(jnp.concatenate([oa, oa], axis=1),
                                jnp.concatenate([gb_hi, gb_lo], axis=1), _NT,
                                preferred_element_type=jnp.float32)
            w_sc[pl.ds(pl.multiple_of(n * PEER_NKEYS, PEER_NKEYS), PEER_NKEYS), :] = w
            return carry

        lax.fori_loop(0, tm, token, 0)

    act = jax.nn.gelu(lax.dot_general(hn_ref[...], u_ref[...], _NT,
                                      preferred_element_type=jnp.float32))
    parts = []
    for j in range(n_i1):
        wj = w_sc[pl.ds(c * n_i1 + j, tm, stride=PEER_NKEYS), :]
        parts.append((act[:, j * PEER_NKEYS:(j + 1) * PEER_NKEYS] * wj).astype(jnp.bfloat16))
    acc_ref[...] += jnp.dot(jnp.concatenate(parts, axis=1), v_ref[...],
                            preferred_element_type=jnp.float32)

    @pl.when(c == pl.num_programs(1) - 1)
    def _():
        y = h_ref[...] + acc_ref[...]
        ms = jnp.mean(y * y, axis=-1, keepdims=True)
        y_ref[...] = y * lax.rsqrt(ms + EPS) * gf_ref[...]


def _peer_ffn(hn, h, ia, ib, gt, u_bf16, v_bf16, gain_final, tm, n_i1):
    n, d = hn.shape
    ec = n_i1 * PEER_NKEYS
    row = lambda i, c: (i, 0)
    return pl.pallas_call(
        functools.partial(_peer_ffn_kernel, tm=tm, n_i1=n_i1),
        out_shape=jax.ShapeDtypeStruct((n, d), jnp.float32),
        grid=(n // tm, u_bf16.shape[0] // ec),
        in_specs=[pl.BlockSpec((tm, d), row),
                  pl.BlockSpec((tm, d), row),
                  pl.BlockSpec((tm, PEER_SEL), row),
                  pl.BlockSpec((tm, PEER_SEL), row),
                  pl.BlockSpec((tm, PEER_SEL), row),
                  pl.BlockSpec((ec, d), lambda i, c: (c, 0)),
                  pl.BlockSpec((ec, d), lambda i, c: (c, 0)),
                  pl.BlockSpec((1, d), lambda i, c: (0, 0))],
        out_specs=pl.BlockSpec((tm, d), row),
        scratch_shapes=[pltpu.VMEM((tm * PEER_NKEYS, PEER_NKEYS), jnp.float32),
                        pltpu.VMEM((tm, d), jnp.float32)],
        compiler_params=pltpu.CompilerParams(
            dimension_semantics=("parallel", "arbitrary"),
            vmem_limit_bytes=VMEM_LIMIT_BYTES),
        name="peer_ffn",
    )(hn, h, ia, ib, gt, u_bf16, v_bf16, gain_final.reshape(1, d))


def _rmsnorm(x, g):
    xf = x.astype(jnp.float32)
    y = xf * lax.rsqrt(jnp.mean(xf * xf, axis=-1, keepdims=True) + EPS)
    return (y * g.astype(jnp.float32)).astype(x.dtype)


def _layernorm(x, g, b):
    xf = x.astype(jnp.float32)
    mu = jnp.mean(xf, axis=-1, keepdims=True)
    var = jnp.mean(jnp.square(xf - mu), axis=-1, keepdims=True)
    return ((xf - mu) * lax.rsqrt(var + EPS) * g.astype(jnp.float32) + b.astype(jnp.float32)).astype(x.dtype)


def _masked_probs(s, mask):
    s = jnp.where(mask, s, NEG)
    m = jnp.max(s, axis=-1, keepdims=True)
    e = jnp.where(mask, jnp.exp(s - m), 0.0)
    return e / jnp.maximum(jnp.sum(e, axis=-1, keepdims=True), 1e-30)


def _attn_probs(q, k, mask):
    s = jnp.einsum('...tgrd,...kgd->...grtk', q, k).astype(jnp.float32) * SCALE
    return _masked_probs(s, mask[..., None, None, :, :])


def _attn_out(p, v):
    return jnp.einsum('...grtk,...kgd->...tgrd', p.astype(v.dtype), v)


def _gather_pages(pool, page_table):
    g = pool[page_table]
    return g.reshape(page_table.shape[0], -1, pool.shape[2], pool.shape[3])


def _compress(kv, pe, w1, w2):
    B, L, G, D = kv.shape
    ch = kv.reshape(B, L // CMP_STRIDE, CMP_STRIDE, G, D)
    h_lo = jnp.einsum('bjlgd,ldh->bjgh', ch, w1[:CMP_STRIDE])
    h_hi = jnp.einsum('bjlgd,ldh->bjgh', ch, w1[CMP_STRIDE:])
    h = h_lo[:, :-1] + h_hi[:, 1:] + jnp.einsum('ld,ldh->h', pe, w1)
    return jnp.einsum('bigh,he->bige', jax.nn.gelu(h), w2)


def _cmp_to_sel(n_cmp, n_sel):
    cs = jnp.arange(n_cmp)[:, None] * CMP_STRIDE
    ss = jnp.arange(n_sel)[None, :] * SEL_LEN
    ov = jnp.clip(jnp.minimum(cs + CMP_LEN, ss + SEL_LEN) - jnp.maximum(cs, ss), 0, None)
    return ov.astype(jnp.float32) / CMP_LEN


def _sel_attend(q, kb, vb, idx, valid, qpos):
    B, T, G, R, D = q.shape
    bi = jnp.arange(B)[:, None, None, None]
    gi = jnp.arange(G)[None, :, None, None]
    kg = kb[bi, idx, :, gi].reshape(B, G, T, -1, D)
    vg = vb[bi, idx, :, gi].reshape(B, G, T, -1, D)
    kpos = idx[..., None] * SEL_LEN + jnp.arange(SEL_LEN)
    mask = (valid[..., None] & (kpos <= qpos[:, None, None])).reshape(B, G, T, -1)
    s = jnp.einsum('btgrd,bgtkd->bgrtk', q, kg).astype(jnp.float32) * SCALE
    p = _masked_probs(s, mask[:, :, None])
    return jnp.einsum('bgrtk,bgtkd->btgrd', p.astype(vg.dtype), vg)


def _nsa_cmp_sel(q, k_c, v_c, k_s, v_s, qpos, lp, sweep_queries):
    B, T, G, R, D = q.shape
    L = k_c.shape[1]
    L_pad = -(-L // SEL_LEN) * SEL_LEN
    pad = ((0, 0), (0, L_pad - L), (0, 0), (0, 0))
    k_c, v_c, k_s, v_s = jnp.pad(k_c, pad), jnp.pad(v_c, pad), jnp.pad(k_s, pad), jnp.pad(v_s, pad)
    k_cmp = _compress(k_c, lp['cmp_pe_k'], lp['cmp_w1_k'], lp['cmp_w2_k'])
    v_cmp = _compress(v_c, lp['cmp_pe_v'], lp['cmp_w1_v'], lp['cmp_w2_v'])
    n_cmp = k_cmp.shape[1]
    cmp_end = jnp.arange(n_cmp) * CMP_STRIDE + CMP_LEN - 1
    p_cmp = _attn_probs(q, k_cmp, cmp_end[None, :] <= qpos[:, None])
    o_cmp = _attn_out(p_cmp, v_cmp)
    n_sel = L_pad // SEL_LEN
    p_slc = jnp.einsum('bgrti,ij->bgtj', p_cmp, _cmp_to_sel(n_cmp, n_sel))
    blk = jnp.arange(n_sel)[None, :]
    cur = (qpos // SEL_LEN)[:, None]
    forced = (blk == 0) | (blk == cur) | (blk == cur - 1)
    score = jnp.where(blk <= cur, p_slc + jnp.where(forced, FORCE_BONUS, 0.0), NEG)
    top_s, top_i = lax.top_k(score, min(SEL_TOPK, n_sel))
    valid = top_s > 0.5 * NEG
    kb = k_s.reshape(B, n_sel, SEL_LEN, G, D)
    vb = v_s.reshape(B, n_sel, SEL_LEN, G, D)
    if sweep_queries:
        nq = T // SEL_Q_BLOCK
        xs = (q.reshape(B, nq, SEL_Q_BLOCK, G, R, D).swapaxes(0, 1),
              top_i.reshape(B, G, nq, SEL_Q_BLOCK, -1).transpose(2, 0, 1, 3, 4),
              valid.reshape(B, G, nq, SEL_Q_BLOCK, -1).transpose(2, 0, 1, 3, 4),
              qpos.reshape(nq, SEL_Q_BLOCK))
        o = lax.map(lambda a: _sel_attend(a[0], kb, vb, a[1], a[2], a[3]), xs)
        o_sel = o.swapaxes(0, 1).reshape(B, T, G, R, D)
    else:
        xs = (q[:, None], kb[:, None], vb[:, None], top_i[:, None], valid[:, None])
        o_sel = lax.map(lambda a: _sel_attend(a[0], a[1], a[2], a[3], a[4], qpos)[0], xs)
    return o_cmp, o_sel


def _window_banded(q, k, v):
    B, T, G, R, D = q.shape
    nb = T // WIN_Q_BLOCK
    pad = ((0, 0), (WINDOW, 0), (0, 0), (0, 0))
    kidx = jnp.arange(nb)[:, None] * WIN_Q_BLOCK + jnp.arange(WINDOW + WIN_Q_BLOCK)[None, :]
    kblk = jnp.pad(k, pad)[:, kidx]
    vblk = jnp.pad(v, pad)[:, kidx]
    kpos = kidx - WINDOW
    qpos = jnp.arange(T).reshape(nb, WIN_Q_BLOCK)
    d = qpos[:, :, None] - kpos[:, None, :]
    mask = (kpos[:, None, :] >= 0) & (d >= 0) & (d < WINDOW)
    o = _attn_out(_attn_probs(q.reshape(B, nb, WIN_Q_BLOCK, G, R, D), kblk, mask), vblk)
    return o.reshape(B, T, G, R, D)


def _window_dense(q, k, v, qpos, kpos):
    d = qpos[:, None] - kpos[None, :]
    return _attn_out(_attn_probs(q, k, (d >= 0) & (d < WINDOW)), v)


def _conv_module(u, buf, w_dw, b_dw, ln_g, ln_b):
    xp = jnp.concatenate([buf, u], axis=1)
    y = lax.conv_general_dilated(xp, w_dw[:, None, :], (1,), 'VALID',
                                 dimension_numbers=('NWC', 'WIO', 'NWC'),
                                 feature_group_count=CONV_CH) + b_dw
    return jax.nn.silu(_layernorm(y, ln_g, ln_b)), xp[:, -(CONV_WIDTH - 1):]


def _split_z(z, B, T):
    kv = lambda c: z[:, c:c + KVW].reshape(B, T, N_KV_HEADS, HEAD_DIM)
    q = z[:, COL_Q:COL_Q + QW].reshape(B, T, N_KV_HEADS, GROUP, HEAD_DIM)
    a = z[:, COL_GLU_A:COL_GLU_A + CONV_CH]
    b = z[:, COL_GLU_B:COL_GLU_B + CONV_CH]
    u = (a * jax.nn.sigmoid(b)).reshape(B, T, CONV_CH)
    return q, kv(COL_KC), kv(COL_VC), kv(COL_KS), kv(COL_VS), kv(COL_KW), kv(COL_VW), u


def _cmp_to_sel_t(n_cmp, n_sel, n_cmp_pad):
    cs = np.arange(n_cmp)[None, :] * CMP_STRIDE
    ss = np.arange(n_sel)[:, None] * SEL_LEN
    ov = np.clip(np.minimum(cs + CMP_LEN, ss + SEL_LEN) - np.maximum(cs, ss), 0, None)
    ct = np.zeros((n_sel, n_cmp_pad), np.float32)
    ct[:, :n_cmp] = ov.astype(np.float32) / CMP_LEN
    return jnp.asarray(ct, jnp.bfloat16)


def _pad_cmp(x_cmp, n_pad):
    x = jnp.transpose(x_cmp, (0, 2, 1, 3))
    return jnp.pad(x, ((0, 0), (0, 0), (0, n_pad - x.shape[2]), (0, 0)))


def _layer(x, lp, past, page_table, prompt, tm):
    B, T, _ = x.shape
    n = B * T
    x2 = x.reshape(n, D_MODEL)
    z = _proj_in(x2, lp['norm_mix'], lp['w_in_p'], tm, 768)
    q, k_c, v_c, k_s, v_s, k_w, v_w, u = _split_z(z, B, T)
    if prompt:
        k_cmp = _compress(k_c, lp['cmp_pe_k'], lp['cmp_w1_k'], lp['cmp_w2_k'])
        v_cmp = _compress(v_c, lp['cmp_pe_v'], lp['cmp_w1_v'], lp['cmp_w2_v'])
        n_cmp = k_cmp.shape[1]
        n_cp = -(-n_cmp // 128) * 128
        ct = _cmp_to_sel_t(n_cmp, T // SEL_LEN, n_cp)
        o_cmp, bias = _cmp_select(z, _pad_cmp(k_cmp, n_cp), _pad_cmp(v_cmp, n_cp), ct, B, T, 128)
        o_sel = _sel_attn(z, bias, B, T, 128, 256)
        o_win = _win_attn(z, B, T, 128)
        wl = min(WINDOW, T)
        new_wk, new_wv = k_w[:, T - wl:], v_w[:, T - wl:]
        conv_buf = jnp.zeros((B, CONV_WIDTH - 1, CONV_CH), u.dtype)
    else:
        qpos = PAST_LEN + jnp.arange(T)
        kc_all = jnp.concatenate([_gather_pages(past['cmp_k'], page_table), k_c], axis=1)
        vc_all = jnp.concatenate([_gather_pages(past['cmp_v'], page_table), v_c], axis=1)
        ks_all = jnp.concatenate([_gather_pages(past['sel_k'], page_table), k_s], axis=1)
        vs_all = jnp.concatenate([_gather_pages(past['sel_v'], page_table), v_s], axis=1)
        wl = past['win_k'].shape[1]
        kw_all = jnp.concatenate([past['win_k'], k_w], axis=1)
        vw_all = jnp.concatenate([past['win_v'], v_w], axis=1)
        kpos = PAST_LEN - wl + jnp.arange(wl + T)
        o_win = _window_dense(q, kw_all, vw_all, qpos, kpos).reshape(n, QW)
        new_wk, new_wv = kw_all[:, -wl:], vw_all[:, -wl:]
        conv_buf = past['conv']
        o_cmp, o_sel = _nsa_cmp_sel(q, kc_all, vc_all, ks_all, vs_all, qpos, lp, False)
        o_cmp, o_sel = o_cmp.reshape(n, QW), o_sel.reshape(n, QW)
    conv_y, new_conv = _conv_module(u, conv_buf, lp['conv_w'], lp['conv_b'], lp['conv_ln_g'], lp['conv_ln_b'])
    merged = _merge(o_cmp, o_sel, o_win, conv_y.reshape(n, CONV_CH), z,
                    lp['w_nsa_out_b'], lp['w_conv_out_b'], 256, 512)
    h, hn = _out_proj(merged, x2, lp['w_out_b'], lp['norm_ffn'], 256)
    ia, ib, gt = _peer_route(hn, lp['peer_wq_b'], lp['peer_k1_b'], lp['peer_k2_b'], 256)
    y = _peer_ffn(hn, h, ia, ib, gt, lp['peer_u_b'], lp['peer_v_b'], lp['norm_final'], 256, 4)
    return y.reshape(B, T, D_MODEL), (k_c, v_c, k_s, v_s, new_wk, new_wv, new_conv)


def kernel(x_prompt, x_sample, cache_cmp_k, cache_cmp_v, cache_sel_k, cache_sel_v, cache_win_k, cache_win_v, state_conv, page_table, norm_mix, w_in, cmp_pe_k, cmp_w1_k, cmp_w2_k, cmp_pe_v, cmp_w1_v, cmp_w2_v, w_nsa_out, conv_w, conv_b, conv_ln_g, conv_ln_b, w_conv_out, w_out, norm_ffn, peer_wq, peer_k1, peer_k2, peer_u, peer_v, norm_final):
    assert DEPTH == 1
    bf = lambda w: w.astype(jnp.bfloat16)
    l = 0
    lp = {'norm_mix': norm_mix[l], 'w_in_p': _permute_w_in(w_in[l]),
          'cmp_pe_k': cmp_pe_k[l], 'cmp_w1_k': cmp_w1_k[l], 'cmp_w2_k': cmp_w2_k[l],
          'cmp_pe_v': cmp_pe_v[l], 'cmp_w1_v': cmp_w1_v[l], 'cmp_w2_v': cmp_w2_v[l],
          'w_nsa_out_b': bf(w_nsa_out[l]), 'conv_w': conv_w[l], 'conv_b': conv_b[l],
          'conv_ln_g': conv_ln_g[l], 'conv_ln_b': conv_ln_b[l],
          'w_conv_out_b': bf(w_conv_out[l]), 'w_out_b': bf(w_out[l]), 'norm_ffn': norm_ffn[l],
          'peer_wq_b': bf(peer_wq[l]), 'peer_k1_b': bf(peer_k1[l]), 'peer_k2_b': bf(peer_k2[l]),
          'peer_u_b': bf(peer_u[l]), 'peer_v_b': bf(peer_v[l]), 'norm_final': norm_final}
    past = {'cmp_k': cache_cmp_k[l], 'cmp_v': cache_cmp_v[l], 'sel_k': cache_sel_k[l],
            'sel_v': cache_sel_v[l], 'win_k': cache_win_k[l], 'win_v': cache_win_v[l],
            'conv': state_conv[l]}
    y_prompt, st_p = _layer(x_prompt, lp, None, None, True, 512)
    y_sample, st_s = _layer(x_sample, lp, past, page_table, False, 256)
    p_ck, p_cv, p_sk, p_sv, p_wk, p_wv, p_conv = [a[None] for a in st_p]
    s_ck, s_cv, s_sk, s_sv, s_wk, s_wv, s_conv = [a[None] for a in st_s]
    return (y_prompt, y_sample, p_ck, s_ck, p_cv, s_cv, p_sk, s_sk, p_sv, s_sv,
            p_wk, s_wk, p_wv, s_wv, p_conv, s_conv)
```

```python
import functools

import jax
import jax.numpy as jnp
import numpy as np
from jax import lax
from jax.experimental import pallas as pl
from jax.experimental.pallas import tpu as pltpu

D_MODEL = 2048
BATCH = 2
SEQ = 4096
DEPTH = 1
DEC_BATCH = 32
DEC_SEQ = 8
PAST_LEN = 8192
PAGE_SIZE = 128
N_HEADS = 16
HEAD_DIM = 128
N_KV_HEADS = 4
GROUP = N_HEADS // N_KV_HEADS
CMP_STRIDE = 16
CMP_LEN = 2 * CMP_STRIDE
SEL_LEN = 64
SEL_SHIFT = SEL_LEN.bit_length() - 1
SEL_TOPK = 16
WINDOW = 512
WIN_Q_BLOCK = 128
SEL_Q_BLOCK = 64
FORCE_BONUS = 1e4
CONV_CH = D_MODEL // 2
CONV_WIDTH = 31
PEER_HEADS = 8
PEER_NKEYS = 128
PEER_DKEY = 256
PEER_TOPK = 16
PEER_CHUNK = 128
QW = N_HEADS * HEAD_DIM
KVW = N_KV_HEADS * HEAD_DIM
N_GATES = 3 * N_HEADS
SCALE = HEAD_DIM ** -0.5
EPS = 1e-6
NEG = -1e30

VMEM_LIMIT_BYTES = 48 * 1024 * 1024

GATE_PAD = 256
COL_Q = 0
COL_KC = COL_Q + QW
COL_VC = COL_KC + KVW
COL_KS = COL_VC + KVW
COL_VS = COL_KS + KVW
COL_KW = COL_VS + KVW
COL_VW = COL_KW + KVW
COL_GLU_A = COL_VW + KVW
COL_GLU_B = COL_GLU_A + CONV_CH
COL_GM_A = COL_GLU_B + CONV_CH
COL_GM_B = COL_GM_A + D_MODEL
COL_G = COL_GM_B + D_MODEL
N_COLS = COL_G + GATE_PAD


def _permute_w_in(w_in):
    g0 = QW + 6 * KVW
    g1 = g0 + N_GATES
    parts = [w_in[:, :g0], w_in[:, g1:], w_in[:, g0:g1],
             jnp.zeros((w_in.shape[0], GATE_PAD - N_GATES), w_in.dtype)]
    return jnp.concatenate(parts, axis=1).astype(jnp.bfloat16)


def _proj_in_kernel(x_ref, g_ref, w_ref, o_ref, xn_ref):
    @pl.when(pl.program_id(1) == 0)
    def _():
        x = x_ref[...]
        ms = jnp.mean(x * x, axis=-1, keepdims=True)
        xn_ref[...] = (x * lax.rsqrt(ms + EPS) * g_ref[...]).astype(jnp.bfloat16)

    o_ref[...] = jnp.dot(xn_ref[...], w_ref[...], preferred_element_type=jnp.float32)


def _proj_in(x, gain, w_bf16, tm, tn):
    n, d = x.shape
    nc = w_bf16.shape[1]
    return pl.pallas_call(
        _proj_in_kernel,
        out_shape=jax.ShapeDtypeStruct((n, nc), jnp.float32),
        grid=(n // tm, nc // tn),
        in_specs=[pl.BlockSpec((tm, d), lambda i, j: (i, 0)),
                  pl.BlockSpec((1, d), lambda i, j: (0, 0)),
                  pl.BlockSpec((d, tn), lambda i, j: (0, j))],
        out_specs=pl.BlockSpec((tm, tn), lambda i, j: (i, j)),
        scratch_shapes=[pltpu.VMEM((tm, d), jnp.bfloat16)],
        compiler_params=pltpu.CompilerParams(
            dimension_semantics=("parallel", "arbitrary"),
            vmem_limit_bytes=VMEM_LIMIT_BYTES),
        name="proj_in",
    )(x, gain.reshape(1, d), w_bf16)


_NT = (((1,), (1,)), ((), ()))
MASK_BIAS = -1e9


def _stack_heads(q):
    return jnp.concatenate([q[:, r * HEAD_DIM:(r + 1) * HEAD_DIM] for r in range(GROUP)], axis=0)


def _unstack_heads(o, tq):
    return jnp.concatenate([o[r * tq:(r + 1) * tq] for r in range(GROUP)], axis=1)


def _cmp_select_kernel(q_ref, kc_ref, vc_ref, ct_ref, ocmp_ref, bias_ref, *, tq):
    t0 = pl.program_id(2) * tq
    q4 = _stack_heads(q_ref[...]).astype(jnp.bfloat16)
    kc = kc_ref[0, 0].astype(jnp.bfloat16)
    vc = vc_ref[0, 0].astype(jnp.bfloat16)
    s = lax.dot_general(q4, kc, _NT, preferred_element_type=jnp.float32) * SCALE
    t = t0 + (lax.broadcasted_iota(jnp.int32, s.shape, 0) & (tq - 1))
    i = lax.broadcasted_iota(jnp.int32, s.shape, 1)
    valid = (i * CMP_STRIDE + (CMP_LEN - 1)) <= t
    s = jnp.where(valid, s, NEG)
    m = jnp.max(s, axis=-1, keepdims=True)
    e = jnp.where(valid, jnp.exp(s - m), 0.0)
    p = (e / jnp.maximum(jnp.sum(e, axis=-1, keepdims=True), 1e-30)).astype(jnp.bfloat16)
    ocmp_ref[...] = _unstack_heads(jnp.dot(p, vc, preferred_element_type=jnp.float32), tq)

    ct = ct_ref[...]
    n_sel = ct.shape[0]
    pslc = lax.dot_general(ct, p[0:tq], _NT, preferred_element_type=jnp.float32)
    for r in range(1, GROUP):
        pslc = pslc + lax.dot_general(ct, p[r * tq:(r + 1) * tq], _NT,
                                      preferred_element_type=jnp.float32)
    j = lax.broadcasted_iota(jnp.int32, (n_sel, tq), 0)
    cur = (t0 + lax.broadcasted_iota(jnp.int32, (n_sel, tq), 1)) >> SEL_SHIFT
    forced = (j == 0) | (j == cur) | (j == cur - 1)
    score = jnp.where(j <= cur, pslc + jnp.where(forced, FORCE_BONUS, 0.0), NEG)
    rank = jnp.zeros((n_sel, tq), jnp.float32)
    for jp in range(n_sel):
        sj = score[jp:jp + 1, :]
        rank = rank + jnp.where(sj > score, 1.0, jnp.where((sj == score) & (j > jp), 1.0, 0.0))
    bias = jnp.where((rank < SEL_TOPK) & (j <= cur), 0.0, MASK_BIAS)
    bias = jnp.concatenate([bias, jnp.zeros((HEAD_DIM - n_sel, tq), jnp.float32)], axis=0)
    bias_ref[0, 0] = bias.T.astype(jnp.bfloat16)


def _cmp_select(z, kcmp, vcmp, ct, B, T, tq):
    n_cp = kcmp.shape[2]
    nq = T // tq
    return pl.pallas_call(
        functools.partial(_cmp_select_kernel, tq=tq),
        out_shape=(jax.ShapeDtypeStruct((B * T, QW), jnp.float32),
                   jax.ShapeDtypeStruct((B, N_KV_HEADS, T, HEAD_DIM), jnp.bfloat16)),
        grid=(B, N_KV_HEADS, nq),
        in_specs=[pl.BlockSpec((tq, GROUP * HEAD_DIM), lambda b, g, qi: (b * nq + qi, g)),
                  pl.BlockSpec((1, 1, n_cp, HEAD_DIM), lambda b, g, qi: (b, g, 0, 0)),
                  pl.BlockSpec((1, 1, n_cp, HEAD_DIM), lambda b, g, qi: (b, g, 0, 0)),
                  pl.BlockSpec(ct.shape, lambda b, g, qi: (0, 0))],
        out_specs=(pl.BlockSpec((tq, GROUP * HEAD_DIM), lambda b, g, qi: (b * nq + qi, g)),
                   pl.BlockSpec((1, 1, tq, HEAD_DIM), lambda b, g, qi: (b, g, qi, 0))),
        compiler_params=pltpu.CompilerParams(
            dimension_semantics=("parallel", "parallel", "arbitrary"),
            vmem_limit_bytes=VMEM_LIMIT_BYTES),
        name="cmp_select",
    )(z, kcmp, vcmp, ct)


def _sel_attn_kernel(q_ref, bias_ref, k_ref, v_ref, o_ref, kaug, vt, acc_sc, *, tq, kc):
    qi = pl.program_id(2)
    t0 = qi * tq
    T = k_ref.shape[0]
    nq = GROUP * tq

    @pl.when(qi == 0)
    def _():
        kaug[:, 0:HEAD_DIM] = k_ref[...].astype(jnp.bfloat16)
        blk = lax.broadcasted_iota(jnp.int32, (T, HEAD_DIM), 0) >> SEL_SHIFT
        col = lax.broadcasted_iota(jnp.int32, (T, HEAD_DIM), 1)
        kaug[:, HEAD_DIM:2 * HEAD_DIM] = jnp.where(blk == col, 1.0, 0.0).astype(jnp.bfloat16)
        for c in range(T // kc):
            vt[c] = v_ref[c * kc:(c + 1) * kc, :].T.astype(jnp.bfloat16)

    q = q_ref[...]
    bias = bias_ref[0, 0]
    qa = jnp.concatenate(
        [jnp.concatenate([q[:, r * HEAD_DIM:(r + 1) * HEAD_DIM].astype(jnp.bfloat16), bias], axis=1)
         for r in range(GROUP)], axis=0)
    acc_sc[...] = jnp.zeros(acc_sc.shape, jnp.float32)
    t = t0 + (lax.broadcasted_iota(jnp.int32, (kc, nq), 1) & (tq - 1))
    key = lax.broadcasted_iota(jnp.int32, (kc, nq), 0)

    def step(c, m_old, l_old, causal):
        k0 = pl.multiple_of(c * kc, kc)
        s = lax.dot_general(kaug[pl.ds(k0, kc), :], qa, _NT,
                            preferred_element_type=jnp.float32) * SCALE
        if causal:
            s = jnp.where(k0 + key <= t, s, NEG)
        m_new = jnp.maximum(m_old, jnp.max(s, axis=0, keepdims=True))
        alpha = jnp.exp(m_old - m_new)
        p = jnp.exp(s - m_new)
        l_new = alpha * l_old + jnp.sum(p, axis=0, keepdims=True)
        acc_sc[...] = alpha * acc_sc[...] + jnp.dot(vt[c], p.astype(jnp.bfloat16),
                                                    preferred_element_type=jnp.float32)
        return m_new, l_new

    last = (t0 + tq - 1) // kc
    init = (jnp.full((1, nq), NEG, jnp.float32), jnp.zeros((1, nq), jnp.float32))
    m, l = lax.fori_loop(0, last, lambda c, ml: step(c, ml[0], ml[1], False), init)
    m, l = step(last, m, l, True)
    o_ref[...] = _unstack_heads((acc_sc[...] / jnp.maximum(l, 1e-30)).T, tq)


def _sel_attn(z, bias, B, T, tq, kc):
    nq = T // tq
    ck = COL_KS // HEAD_DIM
    cv = COL_VS // HEAD_DIM
    return pl.pallas_call(
        functools.partial(_sel_attn_kernel, tq=tq, kc=kc),
        out_shape=jax.ShapeDtypeStruct((B * T, QW), jnp.float32),
        grid=(B, N_KV_HEADS, nq),
        in_specs=[pl.BlockSpec((tq, GROUP * HEAD_DIM), lambda b, g, qi: (b * nq + qi, g)),
                  pl.BlockSpec((1, 1, tq, HEAD_DIM), lambda b, g, qi: (b, g, qi, 0)),
                  pl.BlockSpec((T, HEAD_DIM), lambda b, g, qi: (b, ck + g)),
                  pl.BlockSpec((T, HEAD_DIM), lambda b, g, qi: (b, cv + g))],
        out_specs=pl.BlockSpec((tq, GROUP * HEAD_DIM), lambda b, g, qi: (b * nq + qi, g)),
        scratch_shapes=[pltpu.VMEM((T, 2 * HEAD_DIM), jnp.bfloat16),
                        pltpu.VMEM((T // kc, HEAD_DIM, kc), jnp.bfloat16),
                        pltpu.VMEM((HEAD_DIM, GROUP * tq), jnp.float32)],
        compiler_params=pltpu.CompilerParams(
            dimension_semantics=("parallel", "parallel", "arbitrary"),
            vmem_limit_bytes=VMEM_LIMIT_BYTES),
        name="sel_attn",
    )(z, bias, z, z)


def _win_attn_kernel(q_ref, k_ref, v_ref, o_ref, kpad, vpad, *, tq):
    qi = pl.program_id(2)
    t0 = pl.multiple_of(qi * tq, tq)
    T = k_ref.shape[0]
    span = WINDOW + tq

    @pl.when(qi == 0)
    def _():
        kpad[0:WINDOW, :] = jnp.zeros((WINDOW, HEAD_DIM), jnp.bfloat16)
        vpad[0:WINDOW, :] = jnp.zeros((WINDOW, HEAD_DIM), jnp.bfloat16)
        kpad[WINDOW:WINDOW + T, :] = k_ref[...].astype(jnp.bfloat16)
        vpad[WINDOW:WINDOW + T, :] = v_ref[...].astype(jnp.bfloat16)

    q4 = _stack_heads(q_ref[...]).astype(jnp.bfloat16)
    s = lax.dot_general(q4, kpad[pl.ds(t0, span), :], _NT,
                        preferred_element_type=jnp.float32) * SCALE
    t = t0 + (lax.broadcasted_iota(jnp.int32, s.shape, 0) & (tq - 1))
    kpos = t0 - WINDOW + lax.broadcasted_iota(jnp.int32, s.shape, 1)
    d = t - kpos
    valid = (kpos >= 0) & (d >= 0) & (d < WINDOW)
    s = jnp.where(valid, s, NEG)
    m = jnp.max(s, axis=-1, keepdims=True)
    e = jnp.where(valid, jnp.exp(s - m), 0.0)
    p = (e / jnp.maximum(jnp.sum(e, axis=-1, keepdims=True), 1e-30)).astype(jnp.bfloat16)
    o_ref[...] = _unstack_heads(jnp.dot(p, vpad[pl.ds(t0, span), :],
                                        preferred_element_type=jnp.float32), tq)


def _win_attn(z, B, T, tq):
    nq = T // tq
    ck = COL_KW // HEAD_DIM
    cv = COL_VW // HEAD_DIM
    return pl.pallas_call(
        functools.partial(_win_attn_kernel, tq=tq),
        out_shape=jax.ShapeDtypeStruct((B * T, QW), jnp.float32),
        grid=(B, N_KV_HEADS, nq),
        in_specs=[pl.BlockSpec((tq, GROUP * HEAD_DIM), lambda b, g, qi: (b * nq + qi, g)),
                  pl.BlockSpec((T, HEAD_DIM), lambda b, g, qi: (b, ck + g)),
                  pl.BlockSpec((T, HEAD_DIM), lambda b, g, qi: (b, cv + g))],
        out_specs=pl.BlockSpec((tq, GROUP * HEAD_DIM), lambda b, g, qi: (b * nq + qi, g)),
        scratch_shapes=[pltpu.VMEM((T + WINDOW, HEAD_DIM), jnp.bfloat16),
                        pltpu.VMEM((T + WINDOW, HEAD_DIM), jnp.bfloat16)],
        compiler_params=pltpu.CompilerParams(
            dimension_semantics=("parallel", "parallel", "arbitrary"),
            vmem_limit_bytes=VMEM_LIMIT_BYTES),
        name="win_attn",
    )(z, z, z)


PAGES_PER_STEP = 8
CHUNKS_PER_PAGE = PAGE_SIZE // CMP_STRIDE
CHUNK_W = CMP_STRIDE * KVW
DEC_ROWS = N_HEADS * DEC_SEQ
assert DEC_ROWS == 128 and DEC_SEQ == 8


def _dec_rows(q):
    return jnp.concatenate([q[:, (g * GROUP + r) * HEAD_DIM:(g * GROUP + r + 1) * HEAD_DIM]
                            for r in range(GROUP) for g in range(N_KV_HEADS)], axis=0)


def _dec_cols(o):
    return jnp.concatenate(
        [o[(r * N_KV_HEADS + g) * DEC_SEQ:(r * N_KV_HEADS + g + 1) * DEC_SEQ]
         for g in range(N_KV_HEADS) for r in range(GROUP)], axis=1)


def _dec_row_ids():
    row = lax.broadcasted_iota(jnp.int32, (DEC_ROWS, 1), 0)
    return (row >> 3) & (N_KV_HEADS - 1), row & (DEC_SEQ - 1)


def _group_scores(qall, keys_of, row_g):
    out = None
    for g in range(N_KV_HEADS):
        s = lax.dot_general(qall, keys_of(g), _NT, preferred_element_type=jnp.float32)
        s = jnp.where(row_g == g, s, 0.0)
        out = s if out is None else out + s
    return out


def _group_values(p, vals_of, row_g):
    out = None
    for g in range(N_KV_HEADS):
        pg = jnp.where(row_g == g, p, 0.0).astype(jnp.bfloat16)
        o = jnp.dot(pg, vals_of(g), preferred_element_type=jnp.float32)
        out = o if out is None else out + o
    return out


def _chunk_hidden(page_refs, w1_ref):
    acc = None
    for l in range(CMP_STRIDE):
        x = jnp.concatenate(
            [p[0, :, (l * N_KV_HEADS + g) * HEAD_DIM:(l * N_KV_HEADS + g + 1) * HEAD_DIM]
             for g in range(N_KV_HEADS) for p in page_refs], axis=0)
        h = jnp.dot(x.astype(jnp.bfloat16), w1_ref[l], preferred_element_type=jnp.float32)
        acc = h if acc is None else acc + h
    return acc


def _dec_cmp_kernel(pt_ref, *refs, past_len, n_sel):
    nps = PAGES_PER_STEP
    kp, vp = refs[0:nps], refs[nps:2 * nps]
    (kt_ref, vt_ref, q_ref, w1k_ref, w1v_ref, ck_ref, cv_ref, w2k_ref, w2v_ref, ct_ref,
     ocmp_ref, bias_ref, hk_sc, hv_sc, score_sc) = refs[2 * nps:]
    j = pl.program_id(1)
    rows = nps * CHUNKS_PER_PAGE

    @pl.when(j == 0)
    def _():
        hk_sc[...] = jnp.zeros(hk_sc.shape, jnp.float32)
        hv_sc[...] = jnp.zeros(hv_sc.shape, jnp.float32)

    for pages, w1_ref, h_sc in ((kp, w1k_ref, hk_sc), (vp, w1v_ref, hv_sc)):
        h = _chunk_hidden(pages, w1_ref)
        for g in range(N_KV_HEADS):
            h_sc[g, pl.ds(pl.multiple_of(j * rows, rows), rows), :] = h[g * rows:(g + 1) * rows]

    @pl.when(j == pl.num_programs(1) - 1)
    def _():
        base = (past_len // PAGE_SIZE) * CHUNKS_PER_PAGE
        for t_ref, w1_ref, h_sc in ((kt_ref, w1k_ref, hk_sc), (vt_ref, w1v_ref, hv_sc)):
            h = _chunk_hidden([t_ref], w1_ref)
            for g in range(N_KV_HEADS):
                h_sc[g, base:base + CHUNKS_PER_PAGE, :] = h[g * CHUNKS_PER_PAGE:(g + 1) * CHUNKS_PER_PAGE]
        ncp = ct_ref.shape[1]
        kc, vc = [], []
        for g in range(N_KV_HEADS):
            for h_sc, c_ref, w2_ref, dst in ((hk_sc, ck_ref, w2k_ref, kc), (hv_sc, cv_ref, w2v_ref, vc)):
                hh = (h_sc[g, 0:ncp, 0:HEAD_DIM] + h_sc[g, 1:ncp + 1, HEAD_DIM:2 * HEAD_DIM]
                      + c_ref[...])
                dst.append(jnp.dot(jax.nn.gelu(hh).astype(jnp.bfloat16), w2_ref[...],
                                   preferred_element_type=jnp.float32).astype(jnp.bfloat16))
        qall = _dec_rows(q_ref[...]).astype(jnp.bfloat16)
        row_g, row_t = _dec_row_ids()
        s = _group_scores(qall, lambda g: kc[g], row_g) * SCALE
        i = lax.broadcasted_iota(jnp.int32, s.shape, 1)
        valid = (i * CMP_STRIDE + (CMP_LEN - 1)) <= (past_len + row_t)
        s = jnp.where(valid, s, NEG)
        m = jnp.max(s, axis=-1, keepdims=True)
        e = jnp.where(valid, jnp.exp(s - m), 0.0)
        p = e / jnp.maximum(jnp.sum(e, axis=-1, keepdims=True), 1e-30)
        ocmp_ref[...] = _dec_cols(_group_values(p, lambda g: vc[g], row_g))

        x = lax.dot_general(ct_ref[...], p.astype(jnp.bfloat16), _NT,
                            preferred_element_type=jnp.float32)
        quarter = DEC_ROWS // GROUP
        pslc = x
        for r in range(1, GROUP):
            pslc = pslc + pltpu.roll(x, r * quarter, 1)
        shape = pslc.shape
        jj = lax.broadcasted_iota(jnp.int32, shape, 0)
        cur = (past_len + (lax.broadcasted_iota(jnp.int32, shape, 1) & (DEC_SEQ - 1))) >> SEL_SHIFT
        forced = (jj == 0) | (jj == cur) | (jj == cur - 1)
        score = jnp.where(jj <= cur, pslc + jnp.where(forced, FORCE_BONUS, 0.0), NEG)
        score_sc[...] = score

        def rank_step(jp, rank):
            sj = score_sc[pl.ds(jp, 1), :]
            return rank + jnp.where(sj > score, 1.0, jnp.where((sj == score) & (jj > jp), 1.0, 0.0))

        rank = lax.fori_loop(0, n_sel, rank_step, jnp.zeros(shape, jnp.float32))
        bias = jnp.where((rank < SEL_TOPK) & (jj <= cur), 0.0, MASK_BIAS)
        pad = jnp.full((bias_ref.shape[2] - shape[0], shape[1]), MASK_BIAS, jnp.float32)
        bias_ref[0] = jnp.concatenate([bias, pad], axis=0).T.astype(jnp.bfloat16)


def _page_specs(block, n):
    return [pl.BlockSpec(block, lambda b, j, pt, i=i: (pt[b, j * PAGES_PER_STEP + i], 0, 0))
            for i in range(n)]


def _dec_cmp_select(page_table, pool_k, pool_v, tail_k, tail_v, z, w1k, w1v, ck, cv, w2k, w2v, ct,
                    past_len):
    nb = page_table.shape[0]
    n_pages = past_len // PAGE_SIZE
    n_sel, ncp = ct.shape
    n_sel_real = -(-(past_len + DEC_SEQ) // SEL_LEN)
    blk = (1, CHUNKS_PER_PAGE, CHUNK_W)
    full2 = lambda a: pl.BlockSpec(a.shape, lambda b, j, pt: (0, 0))
    full3 = lambda a: pl.BlockSpec(a.shape, lambda b, j, pt: (0, 0, 0))
    tail = pl.BlockSpec(blk, lambda b, j, pt: (b, 0, 0))
    n_blocks = 2 * HEAD_DIM
    grid_spec = pltpu.PrefetchScalarGridSpec(
        num_scalar_prefetch=1,
        grid=(nb, n_pages // PAGES_PER_STEP),
        in_specs=(_page_specs(blk, PAGES_PER_STEP) + _page_specs(blk, PAGES_PER_STEP)
                  + [tail, tail, pl.BlockSpec((DEC_SEQ, QW), lambda b, j, pt: (b, 0)),
                     full3(w1k), full3(w1v), full2(ck), full2(cv), full2(w2k), full2(w2v), full2(ct)]),
        out_specs=(pl.BlockSpec((DEC_SEQ, QW), lambda b, j, pt: (b, 0)),
                   pl.BlockSpec((1, DEC_ROWS, n_blocks), lambda b, j, pt: (b, 0, 0))),
        scratch_shapes=[pltpu.VMEM((N_KV_HEADS, ncp + 8, 2 * HEAD_DIM), jnp.float32),
                        pltpu.VMEM((N_KV_HEADS, ncp + 8, 2 * HEAD_DIM), jnp.float32),
                        pltpu.VMEM((n_sel, DEC_ROWS), jnp.float32)])
    return pl.pallas_call(
        functools.partial(_dec_cmp_kernel, past_len=past_len, n_sel=n_sel_real),
        out_shape=(jax.ShapeDtypeStruct((nb * DEC_SEQ, QW), jnp.float32),
                   jax.ShapeDtypeStruct((nb, DEC_ROWS, n_blocks), jnp.bfloat16)),
        grid_spec=grid_spec,
        compiler_params=pltpu.CompilerParams(
            dimension_semantics=("parallel", "arbitrary"),
            vmem_limit_bytes=VMEM_LIMIT_BYTES),
        name="dec_cmp_select",
    )(page_table, *([pool_k] * PAGES_PER_STEP), *([pool_v] * PAGES_PER_STEP), tail_k, tail_v, z,
      w1k, w1v, ck, cv, w2k, w2v, ct)


def _dec_sel_kernel(pt_ref, *refs, past_len):
    nps = PAGES_PER_STEP
    kp, vp = refs[0:nps], refs[nps:2 * nps]
    kt_ref, vt_ref, q_ref, bias_ref, o_ref, m_sc, l_sc, acc_sc = refs[2 * nps:]
    j = pl.program_id(1)
    qall = _dec_rows(q_ref[...]).astype(jnp.bfloat16)
    row_g, row_t = _dec_row_ids()
    selcols = bias_ref[0]

    @pl.when(j == 0)
    def _():
        m_sc[...] = jnp.full(m_sc.shape, NEG, jnp.float32)
        l_sc[...] = jnp.zeros(l_sc.shape, jnp.float32)
        acc_sc[...] = jnp.zeros(acc_sc.shape, jnp.float32)

    def update(k_refs, v_refs, first_blk, causal):
        grp = lambda ref, g: ref[0, :, g * HEAD_DIM:(g + 1) * HEAD_DIM].astype(jnp.bfloat16)
        s = jnp.concatenate([_group_scores(qall, functools.partial(grp, kr), row_g)
                             for kr in k_refs], axis=1) * SCALE
        nk = s.shape[1]
        n_blocks = selcols.shape[1]
        blk = first_blk + (lax.broadcasted_iota(jnp.int32, (n_blocks, nk), 1) >> SEL_SHIFT)
        onehot = jnp.where(lax.broadcasted_iota(jnp.int32, (n_blocks, nk), 0) == blk, 1.0, 0.0)
        s = s + jnp.dot(selcols, onehot.astype(jnp.bfloat16), preferred_element_type=jnp.float32)
        if causal:
            kpos = first_blk * SEL_LEN + lax.broadcasted_iota(jnp.int32, s.shape, 1)
            s = jnp.where(kpos <= past_len + row_t, s, NEG)
        m_old = m_sc[...]
        m_new = jnp.maximum(m_old, jnp.max(s, axis=-1, keepdims=True))
        alpha = jnp.exp(m_old - m_new)
        p = jnp.exp(s - m_new)
        l_sc[...] = alpha * l_sc[...] + jnp.sum(p, axis=-1, keepdims=True)
        pv = None
        for n, vr in enumerate(v_refs):
            o = _group_values(p[:, n * PAGE_SIZE:(n + 1) * PAGE_SIZE], functools.partial(grp, vr), row_g)
            pv = o if pv is None else pv + o
        acc_sc[...] = alpha * acc_sc[...] + pv
        m_sc[...] = m_new

    update(kp, vp, j * (nps * PAGE_SIZE // SEL_LEN), False)

    @pl.when(j == pl.num_programs(1) - 1)
    def _():
        update([kt_ref], [vt_ref], past_len // SEL_LEN, True)
        o_ref[...] = _dec_cols(acc_sc[...] / jnp.maximum(l_sc[...], 1e-30))


def _dec_sel_attn(page_table, pool_k, pool_v, tail_k, tail_v, z, bias, past_len):
    nb = page_table.shape[0]
    n_pages = past_len // PAGE_SIZE
    blk = (1, PAGE_SIZE, KVW)
    tail = pl.BlockSpec(blk, lambda b, j, pt: (b, 0, 0))
    grid_spec = pltpu.PrefetchScalarGridSpec(
        num_scalar_prefetch=1,
        grid=(nb, n_pages // PAGES_PER_STEP),
        in_specs=(_page_specs(blk, PAGES_PER_STEP) + _page_specs(blk, PAGES_PER_STEP)
                  + [tail, tail, pl.BlockSpec((DEC_SEQ, QW), lambda b, j, pt: (b, 0)),
                     pl.BlockSpec((1,) + bias.shape[1:], lambda b, j, pt: (b, 0, 0))]),
        out_specs=pl.BlockSpec((DEC_SEQ, QW), lambda b, j, pt: (b, 0)),
        scratch_shapes=[pltpu.VMEM((DEC_ROWS, 1), jnp.float32),
                        pltpu.VMEM((DEC_ROWS, 1), jnp.float32),
                        pltpu.VMEM((DEC_ROWS, HEAD_DIM), jnp.float32)])
    return pl.pallas_call(
        functools.partial(_dec_sel_kernel, past_len=past_len),
        out_shape=jax.ShapeDtypeStruct((nb * DEC_SEQ, QW), jnp.float32),
        grid_spec=grid_spec,
        compiler_params=pltpu.CompilerParams(
            dimension_semantics=("parallel", "arbitrary"),
            vmem_limit_bytes=VMEM_LIMIT_BYTES),
        name="dec_sel_attn",
    )(page_table, *([pool_k] * PAGES_PER_STEP), *([pool_v] * PAGES_PER_STEP), tail_k, tail_v, z, bias)


def _dec_win_kernel(q_ref, wk_ref, wv_ref, kt_ref, vt_ref, o_ref, nwk_ref, nwv_ref, *, past_len):
    wl = wk_ref.shape[1]
    qall = _dec_rows(q_ref[...]).astype(jnp.bfloat16)
    row_g, row_t = _dec_row_ids()

    def cat(w_ref, t_ref, g):
        sl = slice(g * HEAD_DIM, (g + 1) * HEAD_DIM)
        return jnp.concatenate([w_ref[0, :, sl], t_ref[0, :, sl]], axis=0).astype(jnp.bfloat16)

    s = _group_scores(qall, functools.partial(cat, wk_ref, kt_ref), row_g) * SCALE
    kpos = past_len - wl + lax.broadcasted_iota(jnp.int32, s.shape, 1)
    d = past_len + row_t - kpos
    valid = (d >= 0) & (d < WINDOW)
    s = jnp.where(valid, s, NEG)
    m = jnp.max(s, axis=-1, keepdims=True)
    e = jnp.where(valid, jnp.exp(s - m), 0.0)
    p = e / jnp.maximum(jnp.sum(e, axis=-1, keepdims=True), 1e-30)
    o_ref[...] = _dec_cols(_group_values(p, functools.partial(cat, wv_ref, vt_ref), row_g))
    for w_ref, t_ref, n_ref in ((wk_ref, kt_ref, nwk_ref), (wv_ref, vt_ref, nwv_ref)):
        n_ref[0, 0:wl - DEC_SEQ, :] = w_ref[0, DEC_SEQ:wl, :]
        n_ref[0, wl - DEC_SEQ:wl, :] = t_ref[0, 0:DEC_SEQ, :]


def _dec_win_attn(z, win_k, win_v, tail_k, tail_v, past_len):
    nb, wl, _ = win_k.shape
    cache = pl.BlockSpec((1, wl, KVW), lambda b: (b, 0, 0))
    tail = pl.BlockSpec((1, PAGE_SIZE, KVW), lambda b: (b, 0, 0))
    rows = pl.BlockSpec((DEC_SEQ, QW), lambda b: (b, 0))
    return pl.pallas_call(
        functools.partial(_dec_win_kernel, past_len=past_len),
        out_shape=(jax.ShapeDtypeStruct((nb * DEC_SEQ, QW), jnp.float32),
                   jax.ShapeDtypeStruct(win_k.shape, jnp.float32),
                   jax.ShapeDtypeStruct(win_v.shape, jnp.float32)),
        grid=(nb,),
        in_specs=[rows, cache, cache, tail, tail],
        out_specs=(rows, cache, cache),
        compiler_params=pltpu.CompilerParams(
            dimension_semantics=("parallel",),
            vmem_limit_bytes=VMEM_LIMIT_BYTES),
        name="dec_win_attn",
    )(z, win_k, win_v, tail_k, tail_v)


def _compress_kernel(xk_ref, xv_ref, w1k_ref, w1v_ref, ck_ref, cv_ref, w2k_ref, w2v_ref,
                     ko_ref, vo_ref, h_sc):
    nch = xk_ref.shape[0] // CMP_STRIDE
    h_sc[nch:nch + 8, :] = jnp.zeros((8, 2 * HEAD_DIM), jnp.float32)
    for x_ref, w1_ref, c_ref, w2_ref, o_ref in ((xk_ref, w1k_ref, ck_ref, w2k_ref, ko_ref),
                                                (xv_ref, w1v_ref, cv_ref, w2v_ref, vo_ref)):
        acc = None
        for l in range(CMP_STRIDE):
            x = x_ref[pl.ds(l, nch, stride=CMP_STRIDE), :].astype(jnp.bfloat16)
            h = jnp.dot(x, w1_ref[l], preferred_element_type=jnp.float32)
            acc = h if acc is None else acc + h
        h_sc[0:nch, :] = acc
        hh = h_sc[0:nch, 0:HEAD_DIM] + h_sc[1:nch + 1, HEAD_DIM:2 * HEAD_DIM] + c_ref[...]
        o_ref[0, 0] = jnp.dot(jax.nn.gelu(hh).astype(jnp.bfloat16), w2_ref[...],
                              preferred_element_type=jnp.float32)


def _compress_seq(z, w1k, w1v, ck, cv, w2k, w2v, B, T):
    nch = T // CMP_STRIDE
    ck0 = COL_KC // HEAD_DIM
    cv0 = COL_VC // HEAD_DIM
    full2 = lambda a: pl.BlockSpec(a.shape, lambda b, g: (0, 0))
    full3 = lambda a: pl.BlockSpec(a.shape, lambda b, g: (0, 0, 0))
    out = jax.ShapeDtypeStruct((B, N_KV_HEADS, nch, HEAD_DIM), jnp.float32)
    ospec = pl.BlockSpec((1, 1, nch, HEAD_DIM), lambda b, g: (b, g, 0, 0))
    return pl.pallas_call(
        _compress_kernel,
        out_shape=(out, out),
        grid=(B, N_KV_HEADS),
        in_specs=[pl.BlockSpec((T, HEAD_DIM), lambda b, g: (b, ck0 + g)),
                  pl.BlockSpec((T, HEAD_DIM), lambda b, g: (b, cv0 + g)),
                  full3(w1k), full3(w1v), full2(ck), full2(cv), full2(w2k), full2(w2v)],
        out_specs=(ospec, ospec),
        scratch_shapes=[pltpu.VMEM((nch + 8, 2 * HEAD_DIM), jnp.float32)],
        compiler_params=pltpu.CompilerParams(
            dimension_semantics=("parallel", "parallel"),
            vmem_limit_bytes=VMEM_LIMIT_BYTES),
        name="compress_seq",
    )(z, z, w1k, w1v, ck, cv, w2k, w2v)


CONV_HALO = 32
LANES = 128


def _conv_taps(xs, w_ref, first, rows, y_sc):
    for cb in range(CONV_CH // LANES):
        sl = slice(cb * LANES, (cb + 1) * LANES)
        acc = None
        for k in range(CONV_WIDTH):
            term = w_ref[k:k + 1, sl] * xs[first + k:first + k + rows, sl]
            acc = term if acc is None else acc + term
        y_sc[:, sl] = acc


def _conv_post(y, cb_ref, lg_ref, lb_ref):
    y = y + cb_ref[...]
    mu = jnp.mean(y, axis=-1, keepdims=True)
    var = jnp.mean(jnp.square(y - mu), axis=-1, keepdims=True)
    yn = (y - mu) * lax.rsqrt(var + EPS) * lg_ref[...] + lb_ref[...]
    return yn * jax.nn.sigmoid(yn)


def _conv_kernel(a_ref, b_ref, ap_ref, bp_ref, w_ref, cb_ref, lg_ref, lb_ref, y_ref, tail_ref,
                 xs, y_sc, *, tm):
    i = pl.program_id(1)
    prev = ap_ref[...] * jax.nn.sigmoid(bp_ref[...])
    xs[0:CONV_HALO, :] = jnp.where(i == 0, 0.0, prev)
    xs[CONV_HALO:CONV_HALO + tm, :] = a_ref[...] * jax.nn.sigmoid(b_ref[...])
    _conv_taps(xs, w_ref, CONV_HALO - (CONV_WIDTH - 1), tm, y_sc)
    y_ref[...] = _conv_post(y_sc[...], cb_ref, lg_ref, lb_ref).astype(y_ref.dtype)

    @pl.when(i == pl.num_programs(1) - 1)
    def _():
        tail_ref[0] = xs[tm:tm + CONV_HALO, :]


def _conv_seq(z, w, cb, lg, lb, B, T, tm):
    nt = T // tm
    ca = COL_GLU_A // CONV_CH
    cbk = COL_GLU_B // CONV_CH
    r = tm // CONV_HALO
    prev = lambda c: pl.BlockSpec((CONV_HALO, CONV_CH),
                                  lambda b, i: (jnp.maximum((b * nt + i) * r - 1, 0), c))
    cur = lambda c: pl.BlockSpec((tm, CONV_CH), lambda b, i: (b * nt + i, c))
    vec = pl.BlockSpec((1, CONV_CH), lambda b, i: (0, 0))
    return pl.pallas_call(
        functools.partial(_conv_kernel, tm=tm),
        out_shape=(jax.ShapeDtypeStruct((B * T, CONV_CH), jnp.bfloat16),
                   jax.ShapeDtypeStruct((B, CONV_HALO, CONV_CH), jnp.float32)),
        grid=(B, nt),
        in_specs=[cur(ca), cur(cbk), prev(ca), prev(cbk),
                  pl.BlockSpec(w.shape, lambda b, i: (0, 0)), vec, vec, vec],
        out_specs=(pl.BlockSpec((tm, CONV_CH), lambda b, i: (b * nt + i, 0)),
                   pl.BlockSpec((1, CONV_HALO, CONV_CH), lambda b, i: (b, 0, 0))),
        scratch_shapes=[pltpu.VMEM((tm + CONV_HALO, CONV_CH), jnp.float32),
                        pltpu.VMEM((tm, CONV_CH), jnp.float32)],
        compiler_params=pltpu.CompilerParams(
            dimension_semantics=("parallel", "arbitrary"),
            vmem_limit_bytes=VMEM_LIMIT_BYTES),
        name="conv_seq",
    )(z, z, z, z, w, cb.reshape(1, -1), lg.reshape(1, -1), lb.reshape(1, -1))


def _dec_conv_kernel(a_ref, b_ref, st_ref, w_ref, cb_ref, lg_ref, lb_ref, y_ref, ns_ref, xs, y_sc):
    nbuf = CONV_WIDTH - 1
    xs[0:nbuf, :] = st_ref[0]
    xs[nbuf:nbuf + DEC_SEQ, :] = a_ref[...] * jax.nn.sigmoid(b_ref[...])
    _conv_taps(xs, w_ref, 0, DEC_SEQ, y_sc)
    y_ref[...] = _conv_post(y_sc[...], cb_ref, lg_ref, lb_ref)
    ns_ref[0] = xs[DEC_SEQ:DEC_SEQ + nbuf, :]


def _dec_conv(z, state, w, cb, lg, lb):
    nb, nbuf, _ = state.shape
    ca = COL_GLU_A // CONV_CH
    cbk = COL_GLU_B // CONV_CH
    vec = pl.BlockSpec((1, CONV_CH), lambda b: (0, 0))
    st = pl.BlockSpec((1, nbuf, CONV_CH), lambda b: (b, 0, 0))
    return pl.pallas_call(
        _dec_conv_kernel,
        out_shape=(jax.ShapeDtypeStruct((nb * DEC_SEQ, CONV_CH), jnp.float32),
                   jax.ShapeDtypeStruct(state.shape, jnp.float32)),
        grid=(nb,),
        in_specs=[pl.BlockSpec((DEC_SEQ, CONV_CH), lambda b: (b, ca)),
                  pl.BlockSpec((DEC_SEQ, CONV_CH), lambda b: (b, cbk)),
                  st, pl.BlockSpec(w.shape, lambda b: (0, 0)), vec, vec, vec],
        out_specs=(pl.BlockSpec((DEC_SEQ, CONV_CH), lambda b: (b, 0)), st),
        scratch_shapes=[pltpu.VMEM((nbuf + DEC_SEQ + 2, CONV_CH), jnp.float32),
                        pltpu.VMEM((DEC_SEQ, CONV_CH), jnp.float32)],
        compiler_params=pltpu.CompilerParams(
            dimension_semantics=("parallel",),
            vmem_limit_bytes=VMEM_LIMIT_BYTES),
        name="dec_conv",
    )(z, z, state, w, cb.reshape(1, -1), lg.reshape(1, -1), lb.reshape(1, -1))


def _merge_kernel(oc_ref, os_ref, ow_ref, g_ref, convy_ref, ga_ref, gb_ref, wn_ref, wc_ref,
                  o_ref, onsa_ref):
    @pl.when(pl.program_id(1) == 0)
    def _():
        gz = jax.nn.sigmoid(g_ref[...])
        for h in range(N_HEADS):
            sl = slice(h * HEAD_DIM, (h + 1) * HEAD_DIM)
            o = (gz[:, 3 * h:3 * h + 1] * oc_ref[:, sl]
                 + gz[:, 3 * h + 1:3 * h + 2] * os_ref[:, sl]
                 + gz[:, 3 * h + 2:3 * h + 3] * ow_ref[:, sl])
            onsa_ref[:, sl] = o.astype(jnp.bfloat16)

    a = jnp.dot(onsa_ref[...], wn_ref[...], preferred_element_type=jnp.float32)
    b = jnp.dot(convy_ref[...].astype(jnp.bfloat16), wc_ref[...],
                preferred_element_type=jnp.float32)
    o_ref[...] = (jax.nn.sigmoid(ga_ref[...]) * a
                  + jax.nn.sigmoid(gb_ref[...]) * b).astype(o_ref.dtype)


def _merge(o_cmp, o_sel, o_win, conv_y, z, wn_bf16, wc_bf16, tm, tn):
    n = o_cmp.shape[0]
    ja = COL_GM_A // tn
    jb = COL_GM_B // tn
    jg = COL_G // GATE_PAD
    row = lambda i, j: (i, 0)
    return pl.pallas_call(
        _merge_kernel,
        out_shape=jax.ShapeDtypeStruct((n, D_MODEL), jnp.bfloat16),
        grid=(n // tm, D_MODEL // tn),
        in_specs=[pl.BlockSpec((tm, QW), row),
                  pl.BlockSpec((tm, QW), row),
                  pl.BlockSpec((tm, QW), row),
                  pl.BlockSpec((tm, GATE_PAD), lambda i, j: (i, jg)),
                  pl.BlockSpec((tm, CONV_CH), row),
                  pl.BlockSpec((tm, tn), lambda i, j: (i, ja + j)),
                  pl.BlockSpec((tm, tn), lambda i, j: (i, jb + j)),
                  pl.BlockSpec((QW, tn), lambda i, j: (0, j)),
                  pl.BlockSpec((CONV_CH, tn), lambda i, j: (0, j))],
        out_specs=pl.BlockSpec((tm, tn), lambda i, j: (i, j)),
        scratch_shapes=[pltpu.VMEM((tm, QW), jnp.bfloat16)],
        compiler_params=pltpu.CompilerParams(
            dimension_semantics=("parallel", "arbitrary"),
            vmem_limit_bytes=VMEM_LIMIT_BYTES),
        name="merge",
    )(o_cmp, o_sel, o_win, z, conv_y, z, z, wn_bf16, wc_bf16)


def _out_proj_kernel(m_ref, x_ref, w_ref, g_ref, h_ref, hn_ref):
    h = x_ref[...] + jnp.dot(m_ref[...], w_ref[...], preferred_element_type=jnp.float32)
    h_ref[...] = h
    ms = jnp.mean(h * h, axis=-1, keepdims=True)
    hn_ref[...] = (h * lax.rsqrt(ms + EPS) * g_ref[...]).astype(hn_ref.dtype)


def _out_proj(merged, x, w_bf16, gain, tm):
    n = x.shape[0]
    return pl.pallas_call(
        _out_proj_kernel,
        out_shape=(jax.ShapeDtypeStruct((n, D_MODEL), jnp.float32),
                   jax.ShapeDtypeStruct((n, D_MODEL), jnp.bfloat16)),
        grid=(n // tm,),
        in_specs=[pl.BlockSpec((tm, D_MODEL), lambda i: (i, 0)),
                  pl.BlockSpec((tm, D_MODEL), lambda i: (i, 0)),
                  pl.BlockSpec((D_MODEL, D_MODEL), lambda i: (0, 0)),
                  pl.BlockSpec((1, D_MODEL), lambda i: (0, 0))],
        out_specs=(pl.BlockSpec((tm, D_MODEL), lambda i: (i, 0)),
                   pl.BlockSpec((tm, D_MODEL), lambda i: (i, 0))),
        compiler_params=pltpu.CompilerParams(
            dimension_semantics=("parallel",),
            vmem_limit_bytes=VMEM_LIMIT_BYTES),
        name="out_proj",
    )(merged, x, w_bf16, gain.reshape(1, D_MODEL))


PEER_HALF = PEER_DKEY // 2
PEER_SEL = PEER_HEADS * PEER_TOPK
PEER_TOPK_SHIFT = PEER_TOPK.bit_length() - 1


def _topk_chains(s_sc, v_sc, i_sc, k):
    n_chain, r, n = s_sc.shape
    row = lax.broadcasted_iota(jnp.int32, (r, n), 0)

    def body(i, carry):
        for c in range(n_chain):
            s = s_sc[c]
            m = jnp.max(s, axis=0, keepdims=True)
            j = jnp.min(jnp.where(s == m, row, r), axis=0, keepdims=True)
            s_sc[c] = jnp.where(row == j, -jnp.inf, s)
            v_sc[c, pl.ds(i, 1), :] = m
            i_sc[c, pl.ds(i, 1), :] = j
        return carry

    lax.fori_loop(0, k, body, 0)


def _pick_rows(idx, table):
    out = jnp.zeros(idx.shape, table.dtype)
    for a in range(PEER_TOPK):
        out = jnp.where(idx == a, table[a:a + 1, :], out)
    return out


def _peer_route_kernel(hn_ref, wq_ref, k1_ref, k2_ref, ia_ref, ib_ref, gt_ref,
                       s_sc, v_sc, i_sc, c_sc, cv_sc, ci_sc, a_sc, b_sc, g_sc):
    qh = jnp.dot(hn_ref[...], wq_ref[...], preferred_element_type=jnp.float32).astype(jnp.bfloat16)
    for h in range(PEER_HEADS):
        q1 = qh[:, h * PEER_DKEY:h * PEER_DKEY + PEER_HALF]
        q2 = qh[:, h * PEER_DKEY + PEER_HALF:(h + 1) * PEER_DKEY]
        s_sc[2 * h] = lax.dot_general(k1_ref[h], q1, _NT, preferred_element_type=jnp.float32)
        s_sc[2 * h + 1] = lax.dot_general(k2_ref[h], q2, _NT, preferred_element_type=jnp.float32)
    _topk_chains(s_sc, v_sc, i_sc, PEER_TOPK)
    for h in range(PEER_HEADS):
        v1, v2 = v_sc[2 * h], v_sc[2 * h + 1]
        c_sc[h] = jnp.concatenate([v1[a:a + 1, :] + v2 for a in range(PEER_TOPK)], axis=0)
    _topk_chains(c_sc, cv_sc, ci_sc, PEER_TOPK)
    for h in range(PEER_HEADS):
        sc, pos = cv_sc[h], ci_sc[h]
        ia = _pick_rows(pos >> PEER_TOPK_SHIFT, i_sc[2 * h])
        ib = _pick_rows(pos & (PEER_TOPK - 1), i_sc[2 * h + 1])
        e = jnp.exp(sc - sc[0:1, :])
        gate = e / jnp.sum(e, axis=0, keepdims=True)
        rows = slice(h * PEER_TOPK, (h + 1) * PEER_TOPK)
        a_sc[rows, :] = ia.astype(jnp.float32)
        b_sc[rows, :] = ib.astype(jnp.float32)
        g_sc[rows, :] = gate
    ia_ref[...] = a_sc[...].T
    ib_ref[...] = b_sc[...].T
    gt_ref[...] = g_sc[...].T


def _peer_route(hn, wq_bf16, k1_bf16, k2_bf16, tm):
    n, d = hn.shape
    out = jax.ShapeDtypeStruct((n, PEER_SEL), jnp.float32)
    ospec = pl.BlockSpec((tm, PEER_SEL), lambda i: (i, 0))
    return pl.pallas_call(
        _peer_route_kernel,
        out_shape=(out, out, out),
        grid=(n // tm,),
        in_specs=[pl.BlockSpec((tm, d), lambda i: (i, 0)),
                  pl.BlockSpec(wq_bf16.shape, lambda i: (0, 0)),
                  pl.BlockSpec(k1_bf16.shape, lambda i: (0, 0, 0)),
                  pl.BlockSpec(k2_bf16.shape, lambda i: (0, 0, 0))],
        out_specs=(ospec, ospec, ospec),
        scratch_shapes=[pltpu.VMEM((2 * PEER_HEADS, PEER_NKEYS, tm), jnp.float32),
                        pltpu.VMEM((2 * PEER_HEADS, PEER_TOPK, tm), jnp.float32),
                        pltpu.VMEM((2 * PEER_HEADS, PEER_TOPK, tm), jnp.int32),
                        pltpu.VMEM((PEER_HEADS, PEER_TOPK * PEER_TOPK, tm), jnp.float32),
                        pltpu.VMEM((PEER_HEADS, PEER_TOPK, tm), jnp.float32),
                        pltpu.VMEM((PEER_HEADS, PEER_TOPK, tm), jnp.int32)]
        + [pltpu.VMEM((PEER_SEL, tm), jnp.float32)] * 3,
        compiler_params=pltpu.CompilerParams(
            dimension_semantics=("parallel",),
            vmem_limit_bytes=VMEM_LIMIT_BYTES),
        name="peer_route",
    )(hn, wq_bf16, k1_bf16, k2_bf16)


def _peer_ffn_kernel(hn_ref, h_ref, ia_ref, ib_ref, gt_ref, u_ref, v_ref, gf_ref, y_ref,
                     w_sc, acc_ref, *, tm, n_i1):
    c = pl.program_id(1)

    @pl.when(c == 0)
    def _():
        acc_ref[...] = jnp.zeros(acc_ref.shape, jnp.float32)
        sub = lax.broadcasted_iota(jnp.int32, (PEER_NKEYS, PEER_SEL), 0).astype(jnp.float32)

        def token(n, carry):
            a_row = ia_ref[pl.ds(n, 1), :]
            b_row = ib_ref[pl.ds(n, 1), :]
            g_row = gt_ref[pl.ds(n, 1), :]
            g_hi = g_row.astype(jnp.bfloat16).astype(jnp.float32)
            g_lo = g_row - g_hi
            oa = jnp.where(sub == a_row, 1.0, 0.0).astype(jnp.bfloat16)
            hit_b = sub == b_row
            gb_hi = jnp.where(hit_b, g_hi, 0.0).astype(jnp.bfloat16)
            gb_lo = jnp.where(hit_b, g_lo, 0.0).astype(jnp.bfloat16)
            w = lax.dot_general(jnp.concatenate([oa, oa], axis=1),
                                jnp.concatenate([gb_hi, gb_lo], axis=1), _NT,
                                preferred_element_type=jnp.float32)
            w_sc[pl.ds(pl.multiple_of(n * PEER_NKEYS, PEER_NKEYS), PEER_NKEYS), :] = w
            return carry

        lax.fori_loop(0, tm, token, 0, unroll=8)

    act = jax.nn.gelu(lax.dot_general(hn_ref[...], u_ref[...], _NT,
                                      preferred_element_type=jnp.float32))
    parts = []
    for j in range(n_i1):
        wj = w_sc[pl.ds(c * n_i1 + j, tm, stride=PEER_NKEYS), :]
        parts.append((act[:, j * PEER_NKEYS:(j + 1) * PEER_NKEYS] * wj).astype(jnp.bfloat16))
    acc_ref[...] += jnp.dot(jnp.concatenate(parts, axis=1), v_ref[...],
                            preferred_element_type=jnp.float32)

    @pl.when(c == pl.num_programs(1) - 1)
    def _():
        y = h_ref[...] + acc_ref[...]
        ms = jnp.mean(y * y, axis=-1, keepdims=True)
        y_ref[...] = y * lax.rsqrt(ms + EPS) * gf_ref[...]


def _peer_ffn(hn, h, ia, ib, gt, u_bf16, v_bf16, gain_final, tm, n_i1):
    n, d = hn.shape
    ec = n_i1 * PEER_NKEYS
    row = lambda i, c: (i, 0)
    return pl.pallas_call(
        functools.partial(_peer_ffn_kernel, tm=tm, n_i1=n_i1),
        out_shape=jax.ShapeDtypeStruct((n, d), jnp.float32),
        grid=(n // tm, u_bf16.shape[0] // ec),
        in_specs=[pl.BlockSpec((tm, d), row),
                  pl.BlockSpec((tm, d), row),
                  pl.BlockSpec((tm, PEER_SEL), row),
                  pl.BlockSpec((tm, PEER_SEL), row),
                  pl.BlockSpec((tm, PEER_SEL), row),
                  pl.BlockSpec((ec, d), lambda i, c: (c, 0)),
                  pl.BlockSpec((ec, d), lambda i, c: (c, 0)),
                  pl.BlockSpec((1, d), lambda i, c: (0, 0))],
        out_specs=pl.BlockSpec((tm, d), row),
        scratch_shapes=[pltpu.VMEM((tm * PEER_NKEYS, PEER_NKEYS), jnp.float32),
                        pltpu.VMEM((tm, d), jnp.float32)],
        compiler_params=pltpu.CompilerParams(
            dimension_semantics=("parallel", "arbitrary"),
            vmem_limit_bytes=VMEM_LIMIT_BYTES),
        name="peer_ffn",
    )(hn, h, ia, ib, gt, u_bf16, v_bf16, gain_final.reshape(1, d))


def _rmsnorm(x, g):
    xf = x.astype(jnp.float32)
    y = xf * lax.rsqrt(jnp.mean(xf * xf, axis=-1, keepdims=True) + EPS)
    return (y * g.astype(jnp.float32)).astype(x.dtype)


def _layernorm(x, g, b):
    xf = x.astype(jnp.float32)
    mu = jnp.mean(xf, axis=-1, keepdims=True)
    var = jnp.mean(jnp.square(xf - mu), axis=-1, keepdims=True)
    return ((xf - mu) * lax.rsqrt(var + EPS) * g.astype(jnp.float32) + b.astype(jnp.float32)).astype(x.dtype)


def _masked_probs(s, mask):
    s = jnp.where(mask, s, NEG)
    m = jnp.max(s, axis=-1, keepdims=True)
    e = jnp.where(mask, jnp.exp(s - m), 0.0)
    return e / jnp.maximum(jnp.sum(e, axis=-1, keepdims=True), 1e-30)


def _attn_probs(q, k, mask):
    s = jnp.einsum('...tgrd,...kgd->...grtk', q, k).astype(jnp.float32) * SCALE
    return _masked_probs(s, mask[..., None, None, :, :])


def _attn_out(p, v):
    return jnp.einsum('...grtk,...kgd->...tgrd', p.astype(v.dtype), v)


def _gather_pages(pool, page_table):
    g = pool[page_table]
    return g.reshape(page_table.shape[0], -1, pool.shape[2], pool.shape[3])


def _compress(kv, pe, w1, w2):
    B, L, G, D = kv.shape
    ch = kv.reshape(B, L // CMP_STRIDE, CMP_STRIDE, G, D)
    h_lo = jnp.einsum('bjlgd,ldh->bjgh', ch, w1[:CMP_STRIDE])
    h_hi = jnp.einsum('bjlgd,ldh->bjgh', ch, w1[CMP_STRIDE:])
    h = h_lo[:, :-1] + h_hi[:, 1:] + jnp.einsum('ld,ldh->h', pe, w1)
    return jnp.einsum('bigh,he->bige', jax.nn.gelu(h), w2)


def _cmp_to_sel(n_cmp, n_sel):
    cs = jnp.arange(n_cmp)[:, None] * CMP_STRIDE
    ss = jnp.arange(n_sel)[None, :] * SEL_LEN
    ov = jnp.clip(jnp.minimum(cs + CMP_LEN, ss + SEL_LEN) - jnp.maximum(cs, ss), 0, None)
    return ov.astype(jnp.float32) / CMP_LEN


def _sel_attend(q, kb, vb, idx, valid, qpos):
    B, T, G, R, D = q.shape
    bi = jnp.arange(B)[:, None, None, None]
    gi = jnp.arange(G)[None, :, None, None]
    kg = kb[bi, idx, :, gi].reshape(B, G, T, -1, D)
    vg = vb[bi, idx, :, gi].reshape(B, G, T, -1, D)
    kpos = idx[..., None] * SEL_LEN + jnp.arange(SEL_LEN)
    mask = (valid[..., None] & (kpos <= qpos[:, None, None])).reshape(B, G, T, -1)
    s = jnp.einsum('btgrd,bgtkd->bgrtk', q, kg).astype(jnp.float32) * SCALE
    p = _masked_probs(s, mask[:, :, None])
    return jnp.einsum('bgrtk,bgtkd->btgrd', p.astype(vg.dtype), vg)


def _nsa_cmp_sel(q, k_c, v_c, k_s, v_s, qpos, lp, sweep_queries):
    B, T, G, R, D = q.shape
    L = k_c.shape[1]
    L_pad = -(-L // SEL_LEN) * SEL_LEN
    pad = ((0, 0), (0, L_pad - L), (0, 0), (0, 0))
    k_c, v_c, k_s, v_s = jnp.pad(k_c, pad), jnp.pad(v_c, pad), jnp.pad(k_s, pad), jnp.pad(v_s, pad)
    k_cmp = _compress(k_c, lp['cmp_pe_k'], lp['cmp_w1_k'], lp['cmp_w2_k'])
    v_cmp = _compress(v_c, lp['cmp_pe_v'], lp['cmp_w1_v'], lp['cmp_w2_v'])
    n_cmp = k_cmp.shape[1]
    cmp_end = jnp.arange(n_cmp) * CMP_STRIDE + CMP_LEN - 1
    p_cmp = _attn_probs(q, k_cmp, cmp_end[None, :] <= qpos[:, None])
    o_cmp = _attn_out(p_cmp, v_cmp)
    n_sel = L_pad // SEL_LEN
    p_slc = jnp.einsum('bgrti,ij->bgtj', p_cmp, _cmp_to_sel(n_cmp, n_sel))
    blk = jnp.arange(n_sel)[None, :]
    cur = (qpos // SEL_LEN)[:, None]
    forced = (blk == 0) | (blk == cur) | (blk == cur - 1)
    score = jnp.where(blk <= cur, p_slc + jnp.where(forced, FORCE_BONUS, 0.0), NEG)
    top_s, top_i = lax.top_k(score, min(SEL_TOPK, n_sel))
    valid = top_s > 0.5 * NEG
    kb = k_s.reshape(B, n_sel, SEL_LEN, G, D)
    vb = v_s.reshape(B, n_sel, SEL_LEN, G, D)
    if sweep_queries:
        nq = T // SEL_Q_BLOCK
        xs = (q.reshape(B, nq, SEL_Q_BLOCK, G, R, D).swapaxes(0, 1),
              top_i.reshape(B, G, nq, SEL_Q_BLOCK, -1).transpose(2, 0, 1, 3, 4),
              valid.reshape(B, G, nq, SEL_Q_BLOCK, -1).transpose(2, 0, 1, 3, 4),
              qpos.reshape(nq, SEL_Q_BLOCK))
        o = lax.map(lambda a: _sel_attend(a[0], kb, vb, a[1], a[2], a[3]), xs)
        o_sel = o.swapaxes(0, 1).reshape(B, T, G, R, D)
    else:
        xs = (q[:, None], kb[:, None], vb[:, None], top_i[:, None], valid[:, None])
        o_sel = lax.map(lambda a: _sel_attend(a[0], a[1], a[2], a[3], a[4], qpos)[0], xs)
    return o_cmp, o_sel


def _window_banded(q, k, v):
    B, T, G, R, D = q.shape
    nb = T // WIN_Q_BLOCK
    pad = ((0, 0), (WINDOW, 0), (0, 0), (0, 0))
    kidx = jnp.arange(nb)[:, None] * WIN_Q_BLOCK + jnp.arange(WINDOW + WIN_Q_BLOCK)[None, :]
    kblk = jnp.pad(k, pad)[:, kidx]
    vblk = jnp.pad(v, pad)[:, kidx]
    kpos = kidx - WINDOW
    qpos = jnp.arange(T).reshape(nb, WIN_Q_BLOCK)
    d = qpos[:, :, None] - kpos[:, None, :]
    mask = (kpos[:, None, :] >= 0) & (d >= 0) & (d < WINDOW)
    o = _attn_out(_attn_probs(q.reshape(B, nb, WIN_Q_BLOCK, G, R, D), kblk, mask), vblk)
    return o.reshape(B, T, G, R, D)


def _window_dense(q, k, v, qpos, kpos):
    d = qpos[:, None] - kpos[None, :]
    return _attn_out(_attn_probs(q, k, (d >= 0) & (d < WINDOW)), v)


def _conv_module(u, buf, w_dw, b_dw, ln_g, ln_b):
    xp = jnp.concatenate([buf, u], axis=1)
    y = lax.conv_general_dilated(xp, w_dw[:, None, :], (1,), 'VALID',
                                 dimension_numbers=('NWC', 'WIO', 'NWC'),
                                 feature_group_count=CONV_CH) + b_dw
    return jax.nn.silu(_layernorm(y, ln_g, ln_b)), xp[:, -(CONV_WIDTH - 1):]


def _split_z(z, B, T):
    kv = lambda c: z[:, c:c + KVW].reshape(B, T, N_KV_HEADS, HEAD_DIM)
    q = z[:, COL_Q:COL_Q + QW].reshape(B, T, N_KV_HEADS, GROUP, HEAD_DIM)
    a = z[:, COL_GLU_A:COL_GLU_A + CONV_CH]
    b = z[:, COL_GLU_B:COL_GLU_B + CONV_CH]
    u = (a * jax.nn.sigmoid(b)).reshape(B, T, CONV_CH)
    return q, kv(COL_KC), kv(COL_VC), kv(COL_KS), kv(COL_VS), kv(COL_KW), kv(COL_VW), u


def _cmp_to_sel_t(n_cmp, n_sel, n_cmp_pad, n_sel_pad=None):
    cs = np.arange(n_cmp)[None, :] * CMP_STRIDE
    ss = np.arange(n_sel)[:, None] * SEL_LEN
    ov = np.clip(np.minimum(cs + CMP_LEN, ss + SEL_LEN) - np.maximum(cs, ss), 0, None)
    ct = np.zeros((n_sel_pad or n_sel, n_cmp_pad), np.float32)
    ct[:n_sel, :n_cmp] = ov.astype(np.float32) / CMP_LEN
    return jnp.asarray(ct, jnp.bfloat16)


def _compress_params(pe, w1, w2):
    w1cat = jnp.concatenate([w1[:CMP_STRIDE], w1[CMP_STRIDE:]], axis=-1).astype(jnp.bfloat16)
    c = jnp.einsum('ld,ldh->h', pe, w1).reshape(1, -1)
    return w1cat, c, w2.astype(jnp.bfloat16)


def _tail_page(z, col, nb):
    x = z[:, col:col + KVW].reshape(nb, DEC_SEQ, KVW)
    return jnp.pad(x, ((0, 0), (0, PAGE_SIZE - DEC_SEQ), (0, 0)))


def _tail_chunks(z, col, nb):
    x = z[:, col:col + KVW].reshape(nb, 1, DEC_SEQ * KVW)
    return jnp.pad(x, ((0, 0), (0, CHUNKS_PER_PAGE - 1), (0, CHUNK_W - DEC_SEQ * KVW)))


def _pad_cmp(x_cmp, n_pad):
    x = jnp.transpose(x_cmp, (0, 2, 1, 3))
    return jnp.pad(x, ((0, 0), (0, 0), (0, n_pad - x.shape[2]), (0, 0)))


def _layer(x, lp, past, page_table, prompt, tm):
    B, T, _ = x.shape
    n = B * T
    x2 = x.reshape(n, D_MODEL)
    z = _proj_in(x2, lp['norm_mix'], lp['w_in_p'], tm, 768)
    kv = lambda c: z[:, c:c + KVW].reshape(B, T, N_KV_HEADS, HEAD_DIM)
    k_c, v_c, k_s, v_s = kv(COL_KC), kv(COL_VC), kv(COL_KS), kv(COL_VS)
    cmp_k, cmp_v = lp['cmp_k'], lp['cmp_v']
    conv = (lp['conv_w'], lp['conv_b'], lp['conv_ln_g'], lp['conv_ln_b'])
    if prompt:
        n_ch = T // CMP_STRIDE
        ct = _cmp_to_sel_t(n_ch - 1, T // SEL_LEN, n_ch)
        kcmp, vcmp = _compress_seq(z, cmp_k[0], cmp_v[0], cmp_k[1], cmp_v[1], cmp_k[2], cmp_v[2], B, T)
        o_cmp, bias = _cmp_select(z, kcmp, vcmp, ct, B, T, 128)
        o_sel = _sel_attn(z, bias, B, T, 128, 256)
        o_win = _win_attn(z, B, T, 128)
        wl = min(WINDOW, T)
        new_wk, new_wv = kv(COL_KW)[:, T - wl:], kv(COL_VW)[:, T - wl:]
        conv_y, u_tail = _conv_seq(z, *conv, B, T, 256)
        new_conv = u_tail[:, CONV_HALO - (CONV_WIDTH - 1):]
    else:
        n_pool = past['cmp_k'].shape[0]
        as_chunks = lambda pool: pool.reshape(n_pool, CHUNKS_PER_PAGE, CHUNK_W)
        as_pages = lambda pool: pool.reshape(n_pool, PAGE_SIZE, KVW)
        n_ch = -(-(PAST_LEN + T) // SEL_LEN) * SEL_LEN // CMP_STRIDE
        n_sel = -(-(PAST_LEN + T) // SEL_LEN)
        ct = _cmp_to_sel_t(n_ch - 1, n_sel, -(-n_ch // LANES) * LANES, -(-n_sel // 8) * 8)
        o_cmp, bias = _dec_cmp_select(
            page_table, as_chunks(past['cmp_k']), as_chunks(past['cmp_v']),
            _tail_chunks(z, COL_KC, B), _tail_chunks(z, COL_VC, B), z,
            cmp_k[0], cmp_v[0], cmp_k[1], cmp_v[1], cmp_k[2], cmp_v[2], ct, PAST_LEN)
        o_sel = _dec_sel_attn(page_table, as_pages(past['sel_k']), as_pages(past['sel_v']),
                              _tail_page(z, COL_KS, B), _tail_page(z, COL_VS, B), z, bias, PAST_LEN)
        wl = past['win_k'].shape[1]
        o_win, new_wk, new_wv = _dec_win_attn(
            z, past['win_k'].reshape(B, wl, KVW), past['win_v'].reshape(B, wl, KVW),
            _tail_page(z, COL_KW, B), _tail_page(z, COL_VW, B), PAST_LEN)
        new_wk = new_wk.reshape(B, wl, N_KV_HEADS, HEAD_DIM)
        new_wv = new_wv.reshape(B, wl, N_KV_HEADS, HEAD_DIM)
        conv_y, new_conv = _dec_conv(z, past['conv'], *conv)
    merged = _merge(o_cmp, o_sel, o_win, conv_y, z,
                    lp['w_nsa_out_b'], lp['w_conv_out_b'], 256, 512)
    h, hn = _out_proj(merged, x2, lp['w_out_b'], lp['norm_ffn'], 256)
    ia, ib, gt = _peer_route(hn, lp['peer_wq_b'], lp['peer_k1_b'], lp['peer_k2_b'], 256)
    y = _peer_ffn(hn, h, ia, ib, gt, lp['peer_u_b'], lp['peer_v_b'], lp['norm_final'], 256, 4)
    return y.reshape(B, T, D_MODEL), (k_c, v_c, k_s, v_s, new_wk, new_wv, new_conv)


def kernel(x_prompt, x_sample, cache_cmp_k, cache_cmp_v, cache_sel_k, cache_sel_v, cache_win_k, cache_win_v, state_conv, page_table, norm_mix, w_in, cmp_pe_k, cmp_w1_k, cmp_w2_k, cmp_pe_v, cmp_w1_v, cmp_w2_v, w_nsa_out, conv_w, conv_b, conv_ln_g, conv_ln_b, w_conv_out, w_out, norm_ffn, peer_wq, peer_k1, peer_k2, peer_u, peer_v, norm_final):
    assert DEPTH == 1
    bf = lambda w: w.astype(jnp.bfloat16)
    l = 0
    lp = {'norm_mix': norm_mix[l], 'w_in_p': _permute_w_in(w_in[l]),
          'cmp_k': _compress_params(cmp_pe_k[l], cmp_w1_k[l], cmp_w2_k[l]),
          'cmp_v': _compress_params(cmp_pe_v[l], cmp_w1_v[l], cmp_w2_v[l]),
          'w_nsa_out_b': bf(w_nsa_out[l]), 'conv_w': conv_w[l], 'conv_b': conv_b[l],
          'conv_ln_g': conv_ln_g[l], 'conv_ln_b': conv_ln_b[l],
          'w_conv_out_b': bf(w_conv_out[l]), 'w_out_b': bf(w_out[l]), 'norm_ffn': norm_ffn[l],
          'peer_wq_b': bf(peer_wq[l]), 'peer_k1_b': bf(peer_k1[l]), 'peer_k2_b': bf(peer_k2[l]),
          'peer_u_b': bf(peer_u[l]), 'peer_v_b': bf(peer_v[l]), 'norm_final': norm_final}
    past = {'cmp_k': cache_cmp_k[l], 'cmp_v': cache_cmp_v[l], 'sel_k': cache_sel_k[l],
            'sel_v': cache_sel_v[l], 'win_k': cache_win_k[l], 'win_v': cache_win_v[l],
            'conv': state_conv[l]}
    y_prompt, st_p = _layer(x_prompt, lp, None, None, True, 512)
    y_sample, st_s = _layer(x_sample, lp, past, page_table, False, 256)
    p_ck, p_cv, p_sk, p_sv, p_wk, p_wv, p_conv = [a[None] for a in st_p]
    s_ck, s_cv, s_sk, s_sv, s_wk, s_wv, s_conv = [a[None] for a in st_s]
    return (y_prompt, y_sample, p_ck, s_ck, p_cv, s_cv, p_sk, s_sk, p_sv, s_sv,
            p_wk, s_wk, p_wv, s_wv, p_conv, s_conv)
```

```python
import functools

import jax
import jax.numpy as jnp
import numpy as np
from jax import lax
from jax.experimental import pallas as pl
from jax.experimental.pallas import tpu as pltpu

D_MODEL = 2048
BATCH = 2
SEQ = 4096
DEPTH = 1
DEC_BATCH = 32
DEC_SEQ = 8
PAST_LEN = 8192
PAGE_SIZE = 128
N_HEADS = 16
HEAD_DIM = 128
N_KV_HEADS = 4
GROUP = N_HEADS // N_KV_HEADS
CMP_STRIDE = 16
CMP_LEN = 2 * CMP_STRIDE
SEL_LEN = 64
SEL_SHIFT = SEL_LEN.bit_length() - 1
SEL_TOPK = 16
WINDOW = 512
WIN_Q_BLOCK = 128
SEL_Q_BLOCK = 64
FORCE_BONUS = 1e4
CONV_CH = D_MODEL // 2
CONV_WIDTH = 31
PEER_HEADS = 8
PEER_NKEYS = 128
PEER_DKEY = 256
PEER_TOPK = 16
PEER_CHUNK = 128
QW = N_HEADS * HEAD_DIM
KVW = N_KV_HEADS * HEAD_DIM
N_GATES = 3 * N_HEADS
SCALE = HEAD_DIM ** -0.5
EPS = 1e-6
NEG = -1e30

VMEM_LIMIT_BYTES = 48 * 1024 * 1024
VMEM_LIMIT_PEER_BYTES = 58 * 1024 * 1024

GATE_PAD = 256
COL_Q = 0
COL_KC = COL_Q + QW
COL_VC = COL_KC + KVW
COL_KS = COL_VC + KVW
COL_VS = COL_KS + KVW
COL_KW = COL_VS + KVW
COL_VW = COL_KW + KVW
COL_GLU_A = COL_VW + KVW
COL_GLU_B = COL_GLU_A + CONV_CH
COL_GM_A = COL_GLU_B + CONV_CH
COL_GM_B = COL_GM_A + D_MODEL
COL_G = COL_GM_B + D_MODEL
N_COLS = COL_G + GATE_PAD


def _permute_w_in(w_in):
    g0 = QW + 6 * KVW
    g1 = g0 + N_GATES
    parts = [w_in[:, :g0], w_in[:, g1:], w_in[:, g0:g1],
             jnp.zeros((w_in.shape[0], GATE_PAD - N_GATES), w_in.dtype)]
    return jnp.concatenate(parts, axis=1).astype(jnp.bfloat16)


def _proj_in_kernel(x_ref, g_ref, w_ref, o_ref, xn_ref):
    @pl.when(pl.program_id(1) == 0)
    def _():
        x = x_ref[...]
        ms = jnp.mean(x * x, axis=-1, keepdims=True)
        xn_ref[...] = (x * lax.rsqrt(ms + EPS) * g_ref[...]).astype(jnp.bfloat16)

    o_ref[...] = jnp.dot(xn_ref[...], w_ref[...], preferred_element_type=jnp.float32)


def _proj_in(x, gain, w_bf16, tm, tn):
    n, d = x.shape
    nc = w_bf16.shape[1]
    return pl.pallas_call(
        _proj_in_kernel,
        out_shape=jax.ShapeDtypeStruct((n, nc), jnp.float32),
        grid=(n // tm, nc // tn),
        in_specs=[pl.BlockSpec((tm, d), lambda i, j: (i, 0)),
                  pl.BlockSpec((1, d), lambda i, j: (0, 0)),
                  pl.BlockSpec((d, tn), lambda i, j: (0, j))],
        out_specs=pl.BlockSpec((tm, tn), lambda i, j: (i, j)),
        scratch_shapes=[pltpu.VMEM((tm, d), jnp.bfloat16)],
        compiler_params=pltpu.CompilerParams(
            dimension_semantics=("parallel", "arbitrary"),
            vmem_limit_bytes=VMEM_LIMIT_BYTES),
        name="proj_in",
    )(x, gain.reshape(1, d), w_bf16)


_NT = (((1,), (1,)), ((), ()))
MASK_BIAS = -1e9


def _stack_heads(q):
    return jnp.concatenate([q[:, r * HEAD_DIM:(r + 1) * HEAD_DIM] for r in range(GROUP)], axis=0)


def _unstack_heads(o, tq):
    return jnp.concatenate([o[r * tq:(r + 1) * tq] for r in range(GROUP)], axis=1)


def _cmp_select_kernel(q_ref, kc_ref, vc_ref, ct_ref, ocmp_ref, bias_ref, *, tq):
    t0 = pl.program_id(2) * tq
    q4 = _stack_heads(q_ref[...]).astype(jnp.bfloat16)
    kc = kc_ref[0, 0].astype(jnp.bfloat16)
    vc = vc_ref[0, 0].astype(jnp.bfloat16)
    s = lax.dot_general(q4, kc, _NT, preferred_element_type=jnp.float32) * SCALE
    t = t0 + (lax.broadcasted_iota(jnp.int32, s.shape, 0) & (tq - 1))
    i = lax.broadcasted_iota(jnp.int32, s.shape, 1)
    valid = (i * CMP_STRIDE + (CMP_LEN - 1)) <= t
    s = jnp.where(valid, s, NEG)
    m = jnp.max(s, axis=-1, keepdims=True)
    e = jnp.where(valid, jnp.exp(s - m), 0.0)
    p = (e / jnp.maximum(jnp.sum(e, axis=-1, keepdims=True), 1e-30)).astype(jnp.bfloat16)
    ocmp_ref[...] = _unstack_heads(jnp.dot(p, vc, preferred_element_type=jnp.float32), tq)

    ct = ct_ref[...]
    n_sel = ct.shape[0]
    pslc = lax.dot_general(ct, p[0:tq], _NT, preferred_element_type=jnp.float32)
    for r in range(1, GROUP):
        pslc = pslc + lax.dot_general(ct, p[r * tq:(r + 1) * tq], _NT,
                                      preferred_element_type=jnp.float32)
    j = lax.broadcasted_iota(jnp.int32, (n_sel, tq), 0)
    cur = (t0 + lax.broadcasted_iota(jnp.int32, (n_sel, tq), 1)) >> SEL_SHIFT
    forced = (j == 0) | (j == cur) | (j == cur - 1)
    score = jnp.where(j <= cur, pslc + jnp.where(forced, FORCE_BONUS, 0.0), NEG)
    rank = jnp.zeros((n_sel, tq), jnp.float32)
    for jp in range(n_sel):
        sj = score[jp:jp + 1, :]
        rank = rank + jnp.where(sj > score, 1.0, jnp.where((sj == score) & (j > jp), 1.0, 0.0))
    bias = jnp.where((rank < SEL_TOPK) & (j <= cur), 0.0, MASK_BIAS)
    bias = jnp.concatenate([bias, jnp.zeros((HEAD_DIM - n_sel, tq), jnp.float32)], axis=0)
    bias_ref[0, 0] = bias.T.astype(jnp.bfloat16)


def _cmp_select(z, kcmp, vcmp, ct, B, T, tq):
    n_cp = kcmp.shape[2]
    nq = T // tq
    return pl.pallas_call(
        functools.partial(_cmp_select_kernel, tq=tq),
        out_shape=(jax.ShapeDtypeStruct((B * T, QW), jnp.float32),
                   jax.ShapeDtypeStruct((B, N_KV_HEADS, T, HEAD_DIM), jnp.bfloat16)),
        grid=(B, N_KV_HEADS, nq),
        in_specs=[pl.BlockSpec((tq, GROUP * HEAD_DIM), lambda b, g, qi: (b * nq + qi, g)),
                  pl.BlockSpec((1, 1, n_cp, HEAD_DIM), lambda b, g, qi: (b, g, 0, 0)),
                  pl.BlockSpec((1, 1, n_cp, HEAD_DIM), lambda b, g, qi: (b, g, 0, 0)),
                  pl.BlockSpec(ct.shape, lambda b, g, qi: (0, 0))],
        out_specs=(pl.BlockSpec((tq, GROUP * HEAD_DIM), lambda b, g, qi: (b * nq + qi, g)),
                   pl.BlockSpec((1, 1, tq, HEAD_DIM), lambda b, g, qi: (b, g, qi, 0))),
        compiler_params=pltpu.CompilerParams(
            dimension_semantics=("parallel", "parallel", "arbitrary"),
            vmem_limit_bytes=VMEM_LIMIT_BYTES),
        name="cmp_select",
    )(z, kcmp, vcmp, ct)


def _sel_attn_kernel(q_ref, bias_ref, k_ref, v_ref, o_ref, kaug, vt, acc_sc, *, tq, kc):
    qi = pl.program_id(2)
    t0 = qi * tq
    T = k_ref.shape[0]
    nq = GROUP * tq

    @pl.when(qi == 0)
    def _():
        kaug[:, 0:HEAD_DIM] = k_ref[...].astype(jnp.bfloat16)
        blk = lax.broadcasted_iota(jnp.int32, (T, HEAD_DIM), 0) >> SEL_SHIFT
        col = lax.broadcasted_iota(jnp.int32, (T, HEAD_DIM), 1)
        kaug[:, HEAD_DIM:2 * HEAD_DIM] = jnp.where(blk == col, 1.0, 0.0).astype(jnp.bfloat16)
        for c in range(T // kc):
            vt[c] = v_ref[c * kc:(c + 1) * kc, :].T.astype(jnp.bfloat16)

    q = q_ref[...]
    bias = bias_ref[0, 0]
    qa = jnp.concatenate(
        [jnp.concatenate([q[:, r * HEAD_DIM:(r + 1) * HEAD_DIM].astype(jnp.bfloat16), bias], axis=1)
         for r in range(GROUP)], axis=0)
    acc_sc[...] = jnp.zeros(acc_sc.shape, jnp.float32)
    t = t0 + (lax.broadcasted_iota(jnp.int32, (kc, nq), 1) & (tq - 1))
    key = lax.broadcasted_iota(jnp.int32, (kc, nq), 0)

    def step(c, m_old, l_old, causal):
        k0 = pl.multiple_of(c * kc, kc)
        s = lax.dot_general(kaug[pl.ds(k0, kc), :], qa, _NT,
                            preferred_element_type=jnp.float32) * SCALE
        if causal:
            s = jnp.where(k0 + key <= t, s, NEG)
        m_new = jnp.maximum(m_old, jnp.max(s, axis=0, keepdims=True))
        alpha = jnp.exp(m_old - m_new)
        p = jnp.exp(s - m_new)
        l_new = alpha * l_old + jnp.sum(p, axis=0, keepdims=True)
        acc_sc[...] = alpha * acc_sc[...] + jnp.dot(vt[c], p.astype(jnp.bfloat16),
                                                    preferred_element_type=jnp.float32)
        return m_new, l_new

    last = (t0 + tq - 1) // kc
    init = (jnp.full((1, nq), NEG, jnp.float32), jnp.zeros((1, nq), jnp.float32))
    m, l = lax.fori_loop(0, last, lambda c, ml: step(c, ml[0], ml[1], False), init)
    m, l = step(last, m, l, True)
    o_ref[...] = _unstack_heads((acc_sc[...] / jnp.maximum(l, 1e-30)).T, tq)


def _sel_attn(z, bias, B, T, tq, kc):
    nq = T // tq
    ck = COL_KS // HEAD_DIM
    cv = COL_VS // HEAD_DIM
    return pl.pallas_call(
        functools.partial(_sel_attn_kernel, tq=tq, kc=kc),
        out_shape=jax.ShapeDtypeStruct((B * T, QW), jnp.float32),
        grid=(B, N_KV_HEADS, nq),
        in_specs=[pl.BlockSpec((tq, GROUP * HEAD_DIM), lambda b, g, qi: (b * nq + qi, g)),
                  pl.BlockSpec((1, 1, tq, HEAD_DIM), lambda b, g, qi: (b, g, qi, 0)),
                  pl.BlockSpec((T, HEAD_DIM), lambda b, g, qi: (b, ck + g)),
                  pl.BlockSpec((T, HEAD_DIM), lambda b, g, qi: (b, cv + g))],
        out_specs=pl.BlockSpec((tq, GROUP * HEAD_DIM), lambda b, g, qi: (b * nq + qi, g)),
        scratch_shapes=[pltpu.VMEM((T, 2 * HEAD_DIM), jnp.bfloat16),
                        pltpu.VMEM((T // kc, HEAD_DIM, kc), jnp.bfloat16),
                        pltpu.VMEM((HEAD_DIM, GROUP * tq), jnp.float32)],
        compiler_params=pltpu.CompilerParams(
            dimension_semantics=("parallel", "parallel", "arbitrary"),
            vmem_limit_bytes=VMEM_LIMIT_BYTES),
        name="sel_attn",
    )(z, bias, z, z)


def _win_attn_kernel(q_ref, k_ref, v_ref, o_ref, kpad, vpad, *, tq):
    qi = pl.program_id(2)
    t0 = pl.multiple_of(qi * tq, tq)
    T = k_ref.shape[0]
    span = WINDOW + tq

    @pl.when(qi == 0)
    def _():
        kpad[0:WINDOW, :] = jnp.zeros((WINDOW, HEAD_DIM), jnp.bfloat16)
        vpad[0:WINDOW, :] = jnp.zeros((WINDOW, HEAD_DIM), jnp.bfloat16)
        kpad[WINDOW:WINDOW + T, :] = k_ref[...].astype(jnp.bfloat16)
        vpad[WINDOW:WINDOW + T, :] = v_ref[...].astype(jnp.bfloat16)

    q4 = _stack_heads(q_ref[...]).astype(jnp.bfloat16)
    s = lax.dot_general(q4, kpad[pl.ds(t0, span), :], _NT,
                        preferred_element_type=jnp.float32) * SCALE
    t = t0 + (lax.broadcasted_iota(jnp.int32, s.shape, 0) & (tq - 1))
    kpos = t0 - WINDOW + lax.broadcasted_iota(jnp.int32, s.shape, 1)
    d = t - kpos
    valid = (kpos >= 0) & (d >= 0) & (d < WINDOW)
    s = jnp.where(valid, s, NEG)
    m = jnp.max(s, axis=-1, keepdims=True)
    e = jnp.where(valid, jnp.exp(s - m), 0.0)
    p = (e / jnp.maximum(jnp.sum(e, axis=-1, keepdims=True), 1e-30)).astype(jnp.bfloat16)
    o_ref[...] = _unstack_heads(jnp.dot(p, vpad[pl.ds(t0, span), :],
                                        preferred_element_type=jnp.float32), tq)


def _win_attn(z, B, T, tq):
    nq = T // tq
    ck = COL_KW // HEAD_DIM
    cv = COL_VW // HEAD_DIM
    return pl.pallas_call(
        functools.partial(_win_attn_kernel, tq=tq),
        out_shape=jax.ShapeDtypeStruct((B * T, QW), jnp.float32),
        grid=(B, N_KV_HEADS, nq),
        in_specs=[pl.BlockSpec((tq, GROUP * HEAD_DIM), lambda b, g, qi: (b * nq + qi, g)),
                  pl.BlockSpec((T, HEAD_DIM), lambda b, g, qi: (b, ck + g)),
                  pl.BlockSpec((T, HEAD_DIM), lambda b, g, qi: (b, cv + g))],
        out_specs=pl.BlockSpec((tq, GROUP * HEAD_DIM), lambda b, g, qi: (b * nq + qi, g)),
        scratch_shapes=[pltpu.VMEM((T + WINDOW, HEAD_DIM), jnp.bfloat16),
                        pltpu.VMEM((T + WINDOW, HEAD_DIM), jnp.bfloat16)],
        compiler_params=pltpu.CompilerParams(
            dimension_semantics=("parallel", "parallel", "arbitrary"),
            vmem_limit_bytes=VMEM_LIMIT_BYTES),
        name="win_attn",
    )(z, z, z)


PAGES_PER_STEP = 8
CHUNKS_PER_PAGE = PAGE_SIZE // CMP_STRIDE
PAGE_ROWS = PAGE_SIZE * N_KV_HEADS
DEC_ROWS = N_HEADS * DEC_SEQ
assert DEC_ROWS == 128 and DEC_SEQ == 8


def _head_rows(ref, g, n_tokens):
    return ref[pl.ds(g, n_tokens, stride=N_KV_HEADS), :]


def _dec_rows(q):
    return jnp.concatenate([q[:, (g * GROUP + r) * HEAD_DIM:(g * GROUP + r + 1) * HEAD_DIM]
                            for r in range(GROUP) for g in range(N_KV_HEADS)], axis=0)


def _dec_cols(o):
    return jnp.concatenate(
        [o[(r * N_KV_HEADS + g) * DEC_SEQ:(r * N_KV_HEADS + g + 1) * DEC_SEQ]
         for g in range(N_KV_HEADS) for r in range(GROUP)], axis=1)


def _dec_row_ids():
    row = lax.broadcasted_iota(jnp.int32, (DEC_ROWS, 1), 0)
    return (row >> 3) & (N_KV_HEADS - 1), row & (DEC_SEQ - 1)


def _group_scores(qall, keys_of, row_g):
    out = None
    for g in range(N_KV_HEADS):
        s = lax.dot_general(qall, keys_of(g), _NT, preferred_element_type=jnp.float32)
        s = jnp.where(row_g == g, s, 0.0)
        out = s if out is None else out + s
    return out


def _group_values(p, vals_of, row_g):
    out = None
    for g in range(N_KV_HEADS):
        pg = jnp.where(row_g == g, p, 0.0).astype(jnp.bfloat16)
        o = jnp.dot(pg, vals_of(g), preferred_element_type=jnp.float32)
        out = o if out is None else out + o
    return out


def _chunk_hidden(page_refs, w1_ref):
    acc = None
    chunk_rows = CMP_STRIDE * N_KV_HEADS
    for l in range(CMP_STRIDE):
        x = jnp.concatenate(
            [p[pl.ds(l * N_KV_HEADS + g, CHUNKS_PER_PAGE, stride=chunk_rows), :]
             for g in range(N_KV_HEADS) for p in page_refs], axis=0)
        h = jnp.dot(x.astype(jnp.bfloat16), w1_ref[l], preferred_element_type=jnp.float32)
        acc = h if acc is None else acc + h
    return acc


def _dec_cmp_kernel(pt_ref, *refs, past_len, n_sel):
    nps = PAGES_PER_STEP
    kp, vp = refs[0:nps], refs[nps:2 * nps]
    (kt_ref, vt_ref, q_ref, w1k_ref, w1v_ref, ck_ref, cv_ref, w2k_ref, w2v_ref, ct_ref,
     ocmp_ref, bias_ref, hk_sc, hv_sc, score_sc) = refs[2 * nps:]
    j = pl.program_id(1)
    rows = nps * CHUNKS_PER_PAGE

    @pl.when(j == 0)
    def _():
        hk_sc[...] = jnp.zeros(hk_sc.shape, jnp.float32)
        hv_sc[...] = jnp.zeros(hv_sc.shape, jnp.float32)

    for pages, w1_ref, h_sc in ((kp, w1k_ref, hk_sc), (vp, w1v_ref, hv_sc)):
        h = _chunk_hidden(pages, w1_ref)
        for g in range(N_KV_HEADS):
            h_sc[g, pl.ds(pl.multiple_of(j * rows, rows), rows), :] = h[g * rows:(g + 1) * rows]

    @pl.when(j == pl.num_programs(1) - 1)
    def _():
        base = (past_len // PAGE_SIZE) * CHUNKS_PER_PAGE
        for t_ref, w1_ref, h_sc in ((kt_ref, w1k_ref, hk_sc), (vt_ref, w1v_ref, hv_sc)):
            h = _chunk_hidden([t_ref], w1_ref)
            for g in range(N_KV_HEADS):
                h_sc[g, base:base + CHUNKS_PER_PAGE, :] = h[g * CHUNKS_PER_PAGE:(g + 1) * CHUNKS_PER_PAGE]
        ncp = ct_ref.shape[1]
        kc, vc = [], []
        for g in range(N_KV_HEADS):
            for h_sc, c_ref, w2_ref, dst in ((hk_sc, ck_ref, w2k_ref, kc), (hv_sc, cv_ref, w2v_ref, vc)):
                hh = (h_sc[g, 0:ncp, 0:HEAD_DIM] + h_sc[g, 1:ncp + 1, HEAD_DIM:2 * HEAD_DIM]
                      + c_ref[...])
                dst.append(jnp.dot(jax.nn.gelu(hh).astype(jnp.bfloat16), w2_ref[...],
                                   preferred_element_type=jnp.float32).astype(jnp.bfloat16))
        qall = _dec_rows(q_ref[...]).astype(jnp.bfloat16)
        row_g, row_t = _dec_row_ids()
        s = _group_scores(qall, lambda g: kc[g], row_g) * SCALE
        i = lax.broadcasted_iota(jnp.int32, s.shape, 1)
        valid = (i * CMP_STRIDE + (CMP_LEN - 1)) <= (past_len + row_t)
        s = jnp.where(valid, s, NEG)
        m = jnp.max(s, axis=-1, keepdims=True)
        e = jnp.where(valid, jnp.exp(s - m), 0.0)
        p = e / jnp.maximum(jnp.sum(e, axis=-1, keepdims=True), 1e-30)
        ocmp_ref[...] = _dec_cols(_group_values(p, lambda g: vc[g], row_g))

        x = lax.dot_general(ct_ref[...], p.astype(jnp.bfloat16), _NT,
                            preferred_element_type=jnp.float32)
        quarter = DEC_ROWS // GROUP
        pslc = x
        for r in range(1, GROUP):
            pslc = pslc + pltpu.roll(x, r * quarter, 1)
        shape = pslc.shape
        jj = lax.broadcasted_iota(jnp.int32, shape, 0)
        cur = (past_len + (lax.broadcasted_iota(jnp.int32, shape, 1) & (DEC_SEQ - 1))) >> SEL_SHIFT
        forced = (jj == 0) | (jj == cur) | (jj == cur - 1)
        score = jnp.where(jj <= cur, pslc + jnp.where(forced, FORCE_BONUS, 0.0), NEG)
        score_sc[...] = score

        def rank_step(jp, rank):
            sj = score_sc[pl.ds(jp, 1), :]
            return rank + jnp.where(sj > score, 1.0, jnp.where((sj == score) & (jj > jp), 1.0, 0.0))

        rank = lax.fori_loop(0, n_sel, rank_step, jnp.zeros(shape, jnp.float32))
        bias = jnp.where((rank < SEL_TOPK) & (jj <= cur), 0.0, MASK_BIAS)
        pad = jnp.full((bias_ref.shape[2] - shape[0], shape[1]), MASK_BIAS, jnp.float32)
        bias_ref[0] = jnp.concatenate([bias, pad], axis=0).T.astype(jnp.bfloat16)


PAGE_BLOCK = (PAGE_ROWS, HEAD_DIM)


def _page_specs(n):
    return [pl.BlockSpec(PAGE_BLOCK, lambda b, j, pt, i=i: (pt[b, j * PAGES_PER_STEP + i], 0))
            for i in range(n)]


_TAIL_SPEC = pl.BlockSpec(PAGE_BLOCK, lambda b, j, pt: (b, 0))


def _dec_cmp_select(page_table, pool_k, pool_v, tail_k, tail_v, z, w1k, w1v, ck, cv, w2k, w2v, ct,
                    past_len):
    nb = page_table.shape[0]
    n_pages = past_len // PAGE_SIZE
    n_sel, ncp = ct.shape
    n_sel_real = -(-(past_len + DEC_SEQ) // SEL_LEN)
    full2 = lambda a: pl.BlockSpec(a.shape, lambda b, j, pt: (0, 0))
    full3 = lambda a: pl.BlockSpec(a.shape, lambda b, j, pt: (0, 0, 0))
    tail = _TAIL_SPEC
    n_blocks = 2 * HEAD_DIM
    grid_spec = pltpu.PrefetchScalarGridSpec(
        num_scalar_prefetch=1,
        grid=(nb, n_pages // PAGES_PER_STEP),
        in_specs=(_page_specs(PAGES_PER_STEP) + _page_specs(PAGES_PER_STEP)
                  + [tail, tail, pl.BlockSpec((DEC_SEQ, QW), lambda b, j, pt: (b, 0)),
                     full3(w1k), full3(w1v), full2(ck), full2(cv), full2(w2k), full2(w2v), full2(ct)]),
        out_specs=(pl.BlockSpec((DEC_SEQ, QW), lambda b, j, pt: (b, 0)),
                   pl.BlockSpec((1, DEC_ROWS, n_blocks), lambda b, j, pt: (b, 0, 0))),
        scratch_shapes=[pltpu.VMEM((N_KV_HEADS, ncp + 8, 2 * HEAD_DIM), jnp.float32),
                        pltpu.VMEM((N_KV_HEADS, ncp + 8, 2 * HEAD_DIM), jnp.float32),
                        pltpu.VMEM((n_sel, DEC_ROWS), jnp.float32)])
    return pl.pallas_call(
        functools.partial(_dec_cmp_kernel, past_len=past_len, n_sel=n_sel_real),
        out_shape=(jax.ShapeDtypeStruct((nb * DEC_SEQ, QW), jnp.float32),
                   jax.ShapeDtypeStruct((nb, DEC_ROWS, n_blocks), jnp.bfloat16)),
        grid_spec=grid_spec,
        compiler_params=pltpu.CompilerParams(
            dimension_semantics=("parallel", "arbitrary"),
            vmem_limit_bytes=VMEM_LIMIT_BYTES),
        name="dec_cmp_select",
    )(page_table, *([pool_k] * PAGES_PER_STEP), *([pool_v] * PAGES_PER_STEP), tail_k, tail_v, z,
      w1k, w1v, ck, cv, w2k, w2v, ct)


def _dec_sel_kernel(pt_ref, *refs, past_len):
    nps = PAGES_PER_STEP
    kp, vp = refs[0:nps], refs[nps:2 * nps]
    kt_ref, vt_ref, q_ref, bias_ref, o_ref, m_sc, l_sc, acc_sc = refs[2 * nps:]
    j = pl.program_id(1)
    qall = _dec_rows(q_ref[...]).astype(jnp.bfloat16)
    row_g, row_t = _dec_row_ids()
    selcols = bias_ref[0]

    @pl.when(j == 0)
    def _():
        m_sc[...] = jnp.full(m_sc.shape, NEG, jnp.float32)
        l_sc[...] = jnp.zeros(l_sc.shape, jnp.float32)
        acc_sc[...] = jnp.zeros(acc_sc.shape, jnp.float32)

    def update(k_refs, v_refs, first_blk, causal):
        grp = lambda ref, g: _head_rows(ref, g, PAGE_SIZE).astype(jnp.bfloat16)
        s = jnp.concatenate([_group_scores(qall, functools.partial(grp, kr), row_g)
                             for kr in k_refs], axis=1) * SCALE
        nk = s.shape[1]
        n_blocks = selcols.shape[1]
        blk = first_blk + (lax.broadcasted_iota(jnp.int32, (n_blocks, nk), 1) >> SEL_SHIFT)
        onehot = jnp.where(lax.broadcasted_iota(jnp.int32, (n_blocks, nk), 0) == blk, 1.0, 0.0)
        s = s + jnp.dot(selcols, onehot.astype(jnp.bfloat16), preferred_element_type=jnp.float32)
        if causal:
            kpos = first_blk * SEL_LEN + lax.broadcasted_iota(jnp.int32, s.shape, 1)
            s = jnp.where(kpos <= past_len + row_t, s, NEG)
        m_old = m_sc[...]
        m_new = jnp.maximum(m_old, jnp.max(s, axis=-1, keepdims=True))
        alpha = jnp.exp(m_old - m_new)
        p = jnp.exp(s - m_new)
        l_sc[...] = alpha * l_sc[...] + jnp.sum(p, axis=-1, keepdims=True)
        pv = None
        for n, vr in enumerate(v_refs):
            o = _group_values(p[:, n * PAGE_SIZE:(n + 1) * PAGE_SIZE], functools.partial(grp, vr), row_g)
            pv = o if pv is None else pv + o
        acc_sc[...] = alpha * acc_sc[...] + pv
        m_sc[...] = m_new

    update(kp, vp, j * (nps * PAGE_SIZE // SEL_LEN), False)

    @pl.when(j == pl.num_programs(1) - 1)
    def _():
        update([kt_ref], [vt_ref], past_len // SEL_LEN, True)
        o_ref[...] = _dec_cols(acc_sc[...] / jnp.maximum(l_sc[...], 1e-30))


def _dec_sel_attn(page_table, pool_k, pool_v, tail_k, tail_v, z, bias, past_len):
    nb = page_table.shape[0]
    n_pages = past_len // PAGE_SIZE
    tail = _TAIL_SPEC
    grid_spec = pltpu.PrefetchScalarGridSpec(
        num_scalar_prefetch=1,
        grid=(nb, n_pages // PAGES_PER_STEP),
        in_specs=(_page_specs(PAGES_PER_STEP) + _page_specs(PAGES_PER_STEP)
                  + [tail, tail, pl.BlockSpec((DEC_SEQ, QW), lambda b, j, pt: (b, 0)),
                     pl.BlockSpec((1,) + bias.shape[1:], lambda b, j, pt: (b, 0, 0))]),
        out_specs=pl.BlockSpec((DEC_SEQ, QW), lambda b, j, pt: (b, 0)),
        scratch_shapes=[pltpu.VMEM((DEC_ROWS, 1), jnp.float32),
                        pltpu.VMEM((DEC_ROWS, 1), jnp.float32),
                        pltpu.VMEM((DEC_ROWS, HEAD_DIM), jnp.float32)])
    return pl.pallas_call(
        functools.partial(_dec_sel_kernel, past_len=past_len),
        out_shape=jax.ShapeDtypeStruct((nb * DEC_SEQ, QW), jnp.float32),
        grid_spec=grid_spec,
        compiler_params=pltpu.CompilerParams(
            dimension_semantics=("parallel", "arbitrary"),
            vmem_limit_bytes=VMEM_LIMIT_BYTES),
        name="dec_sel_attn",
    )(page_table, *([pool_k] * PAGES_PER_STEP), *([pool_v] * PAGES_PER_STEP), tail_k, tail_v, z, bias)


def _dec_win_kernel(q_ref, wk_ref, wv_ref, kt_ref, vt_ref, o_ref, nwk_ref, nwv_ref, *, past_len):
    wl = wk_ref.shape[0] // N_KV_HEADS
    qall = _dec_rows(q_ref[...]).astype(jnp.bfloat16)
    row_g, row_t = _dec_row_ids()

    def cat(w_ref, t_ref, g):
        return jnp.concatenate([_head_rows(w_ref, g, wl), _head_rows(t_ref, g, PAGE_SIZE)],
                               axis=0).astype(jnp.bfloat16)

    s = _group_scores(qall, functools.partial(cat, wk_ref, kt_ref), row_g) * SCALE
    kpos = past_len - wl + lax.broadcasted_iota(jnp.int32, s.shape, 1)
    d = past_len + row_t - kpos
    valid = (d >= 0) & (d < WINDOW)
    s = jnp.where(valid, s, NEG)
    m = jnp.max(s, axis=-1, keepdims=True)
    e = jnp.where(valid, jnp.exp(s - m), 0.0)
    p = e / jnp.maximum(jnp.sum(e, axis=-1, keepdims=True), 1e-30)
    o_ref[...] = _dec_cols(_group_values(p, functools.partial(cat, wv_ref, vt_ref), row_g))
    keep = (wl - DEC_SEQ) * N_KV_HEADS
    for w_ref, t_ref, n_ref in ((wk_ref, kt_ref, nwk_ref), (wv_ref, vt_ref, nwv_ref)):
        n_ref[0:keep, :] = w_ref[wl * N_KV_HEADS - keep:wl * N_KV_HEADS, :]
        n_ref[keep:wl * N_KV_HEADS, :] = t_ref[0:DEC_SEQ * N_KV_HEADS, :]


def _dec_win_attn(z, win_k, win_v, tail_k, tail_v, nb, past_len):
    cache = pl.BlockSpec((win_k.shape[0] // nb, HEAD_DIM), lambda b: (b, 0))
    tail = pl.BlockSpec(PAGE_BLOCK, lambda b: (b, 0))
    rows = pl.BlockSpec((DEC_SEQ, QW), lambda b: (b, 0))
    return pl.pallas_call(
        functools.partial(_dec_win_kernel, past_len=past_len),
        out_shape=(jax.ShapeDtypeStruct((nb * DEC_SEQ, QW), jnp.float32),
                   jax.ShapeDtypeStruct(win_k.shape, jnp.float32),
                   jax.ShapeDtypeStruct(win_v.shape, jnp.float32)),
        grid=(nb,),
        in_specs=[rows, cache, cache, tail, tail],
        out_specs=(rows, cache, cache),
        compiler_params=pltpu.CompilerParams(
            dimension_semantics=("parallel",),
            vmem_limit_bytes=VMEM_LIMIT_BYTES),
        name="dec_win_attn",
    )(z, win_k, win_v, tail_k, tail_v)


def _compress_kernel(xk_ref, xv_ref, w1k_ref, w1v_ref, ck_ref, cv_ref, w2k_ref, w2v_ref,
                     ko_ref, vo_ref, h_sc):
    nch = xk_ref.shape[0] // CMP_STRIDE
    h_sc[nch:nch + 8, :] = jnp.zeros((8, 2 * HEAD_DIM), jnp.float32)
    for x_ref, w1_ref, c_ref, w2_ref, o_ref in ((xk_ref, w1k_ref, ck_ref, w2k_ref, ko_ref),
                                                (xv_ref, w1v_ref, cv_ref, w2v_ref, vo_ref)):
        acc = None
        for l in range(CMP_STRIDE):
            x = x_ref[pl.ds(l, nch, stride=CMP_STRIDE), :].astype(jnp.bfloat16)
            h = jnp.dot(x, w1_ref[l], preferred_element_type=jnp.float32)
            acc = h if acc is None else acc + h
        h_sc[0:nch, :] = acc
        hh = h_sc[0:nch, 0:HEAD_DIM] + h_sc[1:nch + 1, HEAD_DIM:2 * HEAD_DIM] + c_ref[...]
        o_ref[0, 0] = jnp.dot(jax.nn.gelu(hh).astype(jnp.bfloat16), w2_ref[...],
                              preferred_element_type=jnp.float32)


def _compress_seq(z, w1k, w1v, ck, cv, w2k, w2v, B, T):
    nch = T // CMP_STRIDE
    ck0 = COL_KC // HEAD_DIM
    cv0 = COL_VC // HEAD_DIM
    full2 = lambda a: pl.BlockSpec(a.shape, lambda b, g: (0, 0))
    full3 = lambda a: pl.BlockSpec(a.shape, lambda b, g: (0, 0, 0))
    out = jax.ShapeDtypeStruct((B, N_KV_HEADS, nch, HEAD_DIM), jnp.float32)
    ospec = pl.BlockSpec((1, 1, nch, HEAD_DIM), lambda b, g: (b, g, 0, 0))
    return pl.pallas_call(
        _compress_kernel,
        out_shape=(out, out),
        grid=(B, N_KV_HEADS),
        in_specs=[pl.BlockSpec((T, HEAD_DIM), lambda b, g: (b, ck0 + g)),
                  pl.BlockSpec((T, HEAD_DIM), lambda b, g: (b, cv0 + g)),
                  full3(w1k), full3(w1v), full2(ck), full2(cv), full2(w2k), full2(w2v)],
        out_specs=(ospec, ospec),
        scratch_shapes=[pltpu.VMEM((nch + 8, 2 * HEAD_DIM), jnp.float32)],
        compiler_params=pltpu.CompilerParams(
            dimension_semantics=("parallel", "parallel"),
            vmem_limit_bytes=VMEM_LIMIT_BYTES),
        name="compress_seq",
    )(z, z, w1k, w1v, ck, cv, w2k, w2v)


CONV_HALO = 32
LANES = 128


def _conv_taps(xs, w_ref, first, rows, y_sc):
    for cb in range(CONV_CH // LANES):
        sl = slice(cb * LANES, (cb + 1) * LANES)
        acc = None
        for k in range(CONV_WIDTH):
            term = w_ref[k:k + 1, sl] * xs[first + k:first + k + rows, sl]
            acc = term if acc is None else acc + term
        y_sc[:, sl] = acc


def _conv_post(y, cb_ref, lg_ref, lb_ref):
    y = y + cb_ref[...]
    mu = jnp.mean(y, axis=-1, keepdims=True)
    var = jnp.mean(jnp.square(y - mu), axis=-1, keepdims=True)
    yn = (y - mu) * lax.rsqrt(var + EPS) * lg_ref[...] + lb_ref[...]
    return yn * jax.nn.sigmoid(yn)


def _conv_kernel(a_ref, b_ref, ap_ref, bp_ref, w_ref, cb_ref, lg_ref, lb_ref, y_ref, tail_ref,
                 xs, y_sc, *, tm):
    i = pl.program_id(1)
    prev = ap_ref[...] * jax.nn.sigmoid(bp_ref[...])
    xs[0:CONV_HALO, :] = jnp.where(i == 0, 0.0, prev)
    xs[CONV_HALO:CONV_HALO + tm, :] = a_ref[...] * jax.nn.sigmoid(b_ref[...])
    _conv_taps(xs, w_ref, CONV_HALO - (CONV_WIDTH - 1), tm, y_sc)
    y_ref[...] = _conv_post(y_sc[...], cb_ref, lg_ref, lb_ref).astype(y_ref.dtype)

    @pl.when(i == pl.num_programs(1) - 1)
    def _():
        tail_ref[0] = xs[tm:tm + CONV_HALO, :]


def _conv_seq(z, w, cb, lg, lb, B, T, tm):
    nt = T // tm
    ca = COL_GLU_A // CONV_CH
    cbk = COL_GLU_B // CONV_CH
    r = tm // CONV_HALO
    prev = lambda c: pl.BlockSpec((CONV_HALO, CONV_CH),
                                  lambda b, i: (jnp.maximum((b * nt + i) * r - 1, 0), c))
    cur = lambda c: pl.BlockSpec((tm, CONV_CH), lambda b, i: (b * nt + i, c))
    vec = pl.BlockSpec((1, CONV_CH), lambda b, i: (0, 0))
    return pl.pallas_call(
        functools.partial(_conv_kernel, tm=tm),
        out_shape=(jax.ShapeDtypeStruct((B * T, CONV_CH), jnp.bfloat16),
                   jax.ShapeDtypeStruct((B, CONV_HALO, CONV_CH), jnp.float32)),
        grid=(B, nt),
        in_specs=[cur(ca), cur(cbk), prev(ca), prev(cbk),
                  pl.BlockSpec(w.shape, lambda b, i: (0, 0)), vec, vec, vec],
        out_specs=(pl.BlockSpec((tm, CONV_CH), lambda b, i: (b * nt + i, 0)),
                   pl.BlockSpec((1, CONV_HALO, CONV_CH), lambda b, i: (b, 0, 0))),
        scratch_shapes=[pltpu.VMEM((tm + CONV_HALO, CONV_CH), jnp.float32),
                        pltpu.VMEM((tm, CONV_CH), jnp.float32)],
        compiler_params=pltpu.CompilerParams(
            dimension_semantics=("parallel", "arbitrary"),
            vmem_limit_bytes=VMEM_LIMIT_BYTES),
        name="conv_seq",
    )(z, z, z, z, w, cb.reshape(1, -1), lg.reshape(1, -1), lb.reshape(1, -1))


def _dec_conv_kernel(a_ref, b_ref, st_ref, w_ref, cb_ref, lg_ref, lb_ref, y_ref, ns_ref, xs, y_sc):
    nbuf = CONV_WIDTH - 1
    xs[0:nbuf, :] = st_ref[0]
    xs[nbuf:nbuf + DEC_SEQ, :] = a_ref[...] * jax.nn.sigmoid(b_ref[...])
    _conv_taps(xs, w_ref, 0, DEC_SEQ, y_sc)
    y_ref[...] = _conv_post(y_sc[...], cb_ref, lg_ref, lb_ref)
    ns_ref[0] = xs[DEC_SEQ:DEC_SEQ + nbuf, :]


def _dec_conv(z, state, w, cb, lg, lb):
    nb, nbuf, _ = state.shape
    ca = COL_GLU_A // CONV_CH
    cbk = COL_GLU_B // CONV_CH
    vec = pl.BlockSpec((1, CONV_CH), lambda b: (0, 0))
    st = pl.BlockSpec((1, nbuf, CONV_CH), lambda b: (b, 0, 0))
    return pl.pallas_call(
        _dec_conv_kernel,
        out_shape=(jax.ShapeDtypeStruct((nb * DEC_SEQ, CONV_CH), jnp.float32),
                   jax.ShapeDtypeStruct(state.shape, jnp.float32)),
        grid=(nb,),
        in_specs=[pl.BlockSpec((DEC_SEQ, CONV_CH), lambda b: (b, ca)),
                  pl.BlockSpec((DEC_SEQ, CONV_CH), lambda b: (b, cbk)),
                  st, pl.BlockSpec(w.shape, lambda b: (0, 0)), vec, vec, vec],
        out_specs=(pl.BlockSpec((DEC_SEQ, CONV_CH), lambda b: (b, 0)), st),
        scratch_shapes=[pltpu.VMEM((nbuf + DEC_SEQ + 2, CONV_CH), jnp.float32),
                        pltpu.VMEM((DEC_SEQ, CONV_CH), jnp.float32)],
        compiler_params=pltpu.CompilerParams(
            dimension_semantics=("parallel",),
            vmem_limit_bytes=VMEM_LIMIT_BYTES),
        name="dec_conv",
    )(z, z, state, w, cb.reshape(1, -1), lg.reshape(1, -1), lb.reshape(1, -1))


def _merge_kernel(oc_ref, os_ref, ow_ref, g_ref, convy_ref, ga_ref, gb_ref, wn_ref, wc_ref,
                  o_ref, onsa_ref):
    @pl.when(pl.program_id(1) == 0)
    def _():
        gz = jax.nn.sigmoid(g_ref[...])
        for h in range(N_HEADS):
            sl = slice(h * HEAD_DIM, (h + 1) * HEAD_DIM)
            o = (gz[:, 3 * h:3 * h + 1] * oc_ref[:, sl]
                 + gz[:, 3 * h + 1:3 * h + 2] * os_ref[:, sl]
                 + gz[:, 3 * h + 2:3 * h + 3] * ow_ref[:, sl])
            onsa_ref[:, sl] = o.astype(jnp.bfloat16)

    a = jnp.dot(onsa_ref[...], wn_ref[...], preferred_element_type=jnp.float32)
    b = jnp.dot(convy_ref[...].astype(jnp.bfloat16), wc_ref[...],
                preferred_element_type=jnp.float32)
    o_ref[...] = (jax.nn.sigmoid(ga_ref[...]) * a
                  + jax.nn.sigmoid(gb_ref[...]) * b).astype(o_ref.dtype)


def _merge(o_cmp, o_sel, o_win, conv_y, z, wn_bf16, wc_bf16, tm, tn):
    n = o_cmp.shape[0]
    ja = COL_GM_A // tn
    jb = COL_GM_B // tn
    jg = COL_G // GATE_PAD
    row = lambda i, j: (i, 0)
    return pl.pallas_call(
        _merge_kernel,
        out_shape=jax.ShapeDtypeStruct((n, D_MODEL), jnp.bfloat16),
        grid=(n // tm, D_MODEL // tn),
        in_specs=[pl.BlockSpec((tm, QW), row),
                  pl.BlockSpec((tm, QW), row),
                  pl.BlockSpec((tm, QW), row),
                  pl.BlockSpec((tm, GATE_PAD), lambda i, j: (i, jg)),
                  pl.BlockSpec((tm, CONV_CH), row),
                  pl.BlockSpec((tm, tn), lambda i, j: (i, ja + j)),
                  pl.BlockSpec((tm, tn), lambda i, j: (i, jb + j)),
                  pl.BlockSpec((QW, tn), lambda i, j: (0, j)),
                  pl.BlockSpec((CONV_CH, tn), lambda i, j: (0, j))],
        out_specs=pl.BlockSpec((tm, tn), lambda i, j: (i, j)),
        scratch_shapes=[pltpu.VMEM((tm, QW), jnp.bfloat16)],
        compiler_params=pltpu.CompilerParams(
            dimension_semantics=("parallel", "arbitrary"),
            vmem_limit_bytes=VMEM_LIMIT_BYTES),
        name="merge",
    )(o_cmp, o_sel, o_win, z, conv_y, z, z, wn_bf16, wc_bf16)


def _out_proj_kernel(m_ref, x_ref, w_ref, g_ref, h_ref, hn_ref):
    h = x_ref[...] + jnp.dot(m_ref[...], w_ref[...], preferred_element_type=jnp.float32)
    h_ref[...] = h
    ms = jnp.mean(h * h, axis=-1, keepdims=True)
    hn_ref[...] = (h * lax.rsqrt(ms + EPS) * g_ref[...]).astype(hn_ref.dtype)


def _out_proj(merged, x, w_bf16, gain, tm):
    n = x.shape[0]
    return pl.pallas_call(
        _out_proj_kernel,
        out_shape=(jax.ShapeDtypeStruct((n, D_MODEL), jnp.float32),
                   jax.ShapeDtypeStruct((n, D_MODEL), jnp.bfloat16)),
        grid=(n // tm,),
        in_specs=[pl.BlockSpec((tm, D_MODEL), lambda i: (i, 0)),
                  pl.BlockSpec((tm, D_MODEL), lambda i: (i, 0)),
                  pl.BlockSpec((D_MODEL, D_MODEL), lambda i: (0, 0)),
                  pl.BlockSpec((1, D_MODEL), lambda i: (0, 0))],
        out_specs=(pl.BlockSpec((tm, D_MODEL), lambda i: (i, 0)),
                   pl.BlockSpec((tm, D_MODEL), lambda i: (i, 0))),
        compiler_params=pltpu.CompilerParams(
            dimension_semantics=("parallel",),
            vmem_limit_bytes=VMEM_LIMIT_BYTES),
        name="out_proj",
    )(merged, x, w_bf16, gain.reshape(1, D_MODEL))


PEER_HALF = PEER_DKEY // 2
PEER_SEL = PEER_HEADS * PEER_TOPK
PEER_TOPK_SHIFT = PEER_TOPK.bit_length() - 1


def _topk_chains(s_sc, v_sc, i_sc, k):
    n_chain, r, n = s_sc.shape
    row = lax.broadcasted_iota(jnp.int32, (r, n), 0)

    def body(i, carry):
        for c in range(n_chain):
            s = s_sc[c]
            m = jnp.max(s, axis=0, keepdims=True)
            j = jnp.min(jnp.where(s == m, row, r), axis=0, keepdims=True)
            s_sc[c] = jnp.where(row == j, -jnp.inf, s)
            v_sc[c, pl.ds(i, 1), :] = m
            i_sc[c, pl.ds(i, 1), :] = j
        return carry

    lax.fori_loop(0, k, body, 0)


def _pick_rows(idx, table):
    out = jnp.zeros(idx.shape, table.dtype)
    for a in range(PEER_TOPK):
        out = jnp.where(idx == a, table[a:a + 1, :], out)
    return out


def _peer_route_kernel(hn_ref, wq_ref, k1_ref, k2_ref, ia_ref, ib_ref, gt_ref,
                       s_sc, v_sc, i_sc, c_sc, cv_sc, ci_sc, a_sc, b_sc, g_sc):
    qh = jnp.dot(hn_ref[...], wq_ref[...], preferred_element_type=jnp.float32).astype(jnp.bfloat16)
    for h in range(PEER_HEADS):
        q1 = qh[:, h * PEER_DKEY:h * PEER_DKEY + PEER_HALF]
        q2 = qh[:, h * PEER_DKEY + PEER_HALF:(h + 1) * PEER_DKEY]
        s_sc[2 * h] = lax.dot_general(k1_ref[h], q1, _NT, preferred_element_type=jnp.float32)
        s_sc[2 * h + 1] = lax.dot_general(k2_ref[h], q2, _NT, preferred_element_type=jnp.float32)
    _topk_chains(s_sc, v_sc, i_sc, PEER_TOPK)
    for h in range(PEER_HEADS):
        v1, v2 = v_sc[2 * h], v_sc[2 * h + 1]
        c_sc[h] = jnp.concatenate([v1[a:a + 1, :] + v2 for a in range(PEER_TOPK)], axis=0)
    _topk_chains(c_sc, cv_sc, ci_sc, PEER_TOPK)
    for h in range(PEER_HEADS):
        sc, pos = cv_sc[h], ci_sc[h]
        ia = _pick_rows(pos >> PEER_TOPK_SHIFT, i_sc[2 * h])
        ib = _pick_rows(pos & (PEER_TOPK - 1), i_sc[2 * h + 1])
        e = jnp.exp(sc - sc[0:1, :])
        gate = e / jnp.sum(e, axis=0, keepdims=True)
        rows = slice(h * PEER_TOPK, (h + 1) * PEER_TOPK)
        a_sc[rows, :] = ia.astype(jnp.float32)
        b_sc[rows, :] = ib.astype(jnp.float32)
        g_sc[rows, :] = gate
    ia_ref[...] = a_sc[...].T
    ib_ref[...] = b_sc[...].T
    gt_ref[...] = g_sc[...].T


def _peer_route(hn, wq_bf16, k1_bf16, k2_bf16, tm):
    n, d = hn.shape
    out = jax.ShapeDtypeStruct((n, PEER_SEL), jnp.float32)
    ospec = pl.BlockSpec((tm, PEER_SEL), lambda i: (i, 0))
    return pl.pallas_call(
        _peer_route_kernel,
        out_shape=(out, out, out),
        grid=(n // tm,),
        in_specs=[pl.BlockSpec((tm, d), lambda i: (i, 0)),
                  pl.BlockSpec(wq_bf16.shape, lambda i: (0, 0)),
                  pl.BlockSpec(k1_bf16.shape, lambda i: (0, 0, 0)),
                  pl.BlockSpec(k2_bf16.shape, lambda i: (0, 0, 0))],
        out_specs=(ospec, ospec, ospec),
        scratch_shapes=[pltpu.VMEM((2 * PEER_HEADS, PEER_NKEYS, tm), jnp.float32),
                        pltpu.VMEM((2 * PEER_HEADS, PEER_TOPK, tm), jnp.float32),
                        pltpu.VMEM((2 * PEER_HEADS, PEER_TOPK, tm), jnp.int32),
                        pltpu.VMEM((PEER_HEADS, PEER_TOPK * PEER_TOPK, tm), jnp.float32),
                        pltpu.VMEM((PEER_HEADS, PEER_TOPK, tm), jnp.float32),
                        pltpu.VMEM((PEER_HEADS, PEER_TOPK, tm), jnp.int32)]
        + [pltpu.VMEM((PEER_SEL, tm), jnp.float32)] * 3,
        compiler_params=pltpu.CompilerParams(
            dimension_semantics=("parallel",),
            vmem_limit_bytes=VMEM_LIMIT_BYTES),
        name="peer_route",
    )(hn, wq_bf16, k1_bf16, k2_bf16)


def _peer_ffn_kernel(hn_ref, h_ref, ia_ref, ib_ref, gt_ref, u_ref, v_ref, gf_ref, y_ref,
                     w_sc, acc_ref, *, tm, n_i1):
    c = pl.program_id(1)

    @pl.when(c == 0)
    def _():
        acc_ref[...] = jnp.zeros(acc_ref.shape, jnp.float32)
        sub = lax.broadcasted_iota(jnp.int32, (PEER_NKEYS, PEER_SEL), 0).astype(jnp.float32)

        def token(n, carry):
            a_row = ia_ref[pl.ds(n, 1), :]
            b_row = ib_ref[pl.ds(n, 1), :]
            g_row = gt_ref[pl.ds(n, 1), :]
            g_hi = g_row.astype(jnp.bfloat16).astype(jnp.float32)
            g_lo = g_row - g_hi
            oa = jnp.where(sub == a_row, 1.0, 0.0).astype(jnp.bfloat16)
            hit_b = sub == b_row
            gb_hi = jnp.where(hit_b, g_hi, 0.0).astype(jnp.bfloat16)
            gb_lo = jnp.where(hit_b, g_lo, 0.0).astype(jnp.bfloat16)
            w = lax.dot_general(jnp.concatenate([oa, oa], axis=1),
                                jnp.concatenate([gb_hi, gb_lo], axis=1), _NT,
                                preferred_element_type=jnp.float32)
            w_sc[pl.ds(pl.multiple_of(n * PEER_NKEYS, PEER_NKEYS), PEER_NKEYS), :] = w
            return carry

        lax.fori_loop(0, tm, token, 0, unroll=8)

    act = jax.nn.gelu(lax.dot_general(hn_ref[...], u_ref[...], _NT,
                                      preferred_element_type=jnp.float32))
    parts = []
    for j in range(n_i1):
        wj = w_sc[pl.ds(c * n_i1 + j, tm, stride=PEER_NKEYS), :]
        parts.append((act[:, j * PEER_NKEYS:(j + 1) * PEER_NKEYS] * wj).astype(jnp.bfloat16))
    acc_ref[...] += jnp.dot(jnp.concatenate(parts, axis=1), v_ref[...],
                            preferred_element_type=jnp.float32)

    @pl.when(c == pl.num_programs(1) - 1)
    def _():
        y = h_ref[...] + acc_ref[...]
        ms = jnp.mean(y * y, axis=-1, keepdims=True)
        y_ref[...] = y * lax.rsqrt(ms + EPS) * gf_ref[...]


def _peer_ffn(hn, h, ia, ib, gt, u_bf16, v_bf16, gain_final, tm, n_i1):
    n, d = hn.shape
    ec = n_i1 * PEER_NKEYS
    row = lambda i, c: (i, 0)
    return pl.pallas_call(
        functools.partial(_peer_ffn_kernel, tm=tm, n_i1=n_i1),
        out_shape=jax.ShapeDtypeStruct((n, d), jnp.float32),
        grid=(n // tm, u_bf16.shape[0] // ec),
        in_specs=[pl.BlockSpec((tm, d), row),
                  pl.BlockSpec((tm, d), row),
                  pl.BlockSpec((tm, PEER_SEL), row),
                  pl.BlockSpec((tm, PEER_SEL), row),
                  pl.BlockSpec((tm, PEER_SEL), row),
                  pl.BlockSpec((ec, d), lambda i, c: (c, 0)),
                  pl.BlockSpec((ec, d), lambda i, c: (c, 0)),
                  pl.BlockSpec((1, d), lambda i, c: (0, 0))],
        out_specs=pl.BlockSpec((tm, d), row),
        scratch_shapes=[pltpu.VMEM((tm * PEER_NKEYS, PEER_NKEYS), jnp.float32),
                        pltpu.VMEM((tm, d), jnp.float32)],
        compiler_params=pltpu.CompilerParams(
            dimension_semantics=("parallel", "arbitrary"),
            vmem_limit_bytes=VMEM_LIMIT_PEER_BYTES),
        name="peer_ffn",
    )(hn, h, ia, ib, gt, u_bf16, v_bf16, gain_final.reshape(1, d))


def _rmsnorm(x, g):
    xf = x.astype(jnp.float32)
    y = xf * lax.rsqrt(jnp.mean(xf * xf, axis=-1, keepdims=True) + EPS)
    return (y * g.astype(jnp.float32)).astype(x.dtype)


def _layernorm(x, g, b):
    xf = x.astype(jnp.float32)
    mu = jnp.mean(xf, axis=-1, keepdims=True)
    var = jnp.mean(jnp.square(xf - mu), axis=-1, keepdims=True)
    return ((xf - mu) * lax.rsqrt(var + EPS) * g.astype(jnp.float32) + b.astype(jnp.float32)).astype(x.dtype)


def _masked_probs(s, mask):
    s = jnp.where(mask, s, NEG)
    m = jnp.max(s, axis=-1, keepdims=True)
    e = jnp.where(mask, jnp.exp(s - m), 0.0)
    return e / jnp.maximum(jnp.sum(e, axis=-1, keepdims=True), 1e-30)


def _attn_probs(q, k, mask):
    s = jnp.einsum('...tgrd,...kgd->...grtk', q, k).astype(jnp.float32) * SCALE
    return _masked_probs(s, mask[..., None, None, :, :])


def _attn_out(p, v):
    return jnp.einsum('...grtk,...kgd->...tgrd', p.astype(v.dtype), v)


def _gather_pages(pool, page_table):
    g = pool[page_table]
    return g.reshape(page_table.shape[0], -1, pool.shape[2], pool.shape[3])


def _compress(kv, pe, w1, w2):
    B, L, G, D = kv.shape
    ch = kv.reshape(B, L // CMP_STRIDE, CMP_STRIDE, G, D)
    h_lo = jnp.einsum('bjlgd,ldh->bjgh', ch, w1[:CMP_STRIDE])
    h_hi = jnp.einsum('bjlgd,ldh->bjgh', ch, w1[CMP_STRIDE:])
    h = h_lo[:, :-1] + h_hi[:, 1:] + jnp.einsum('ld,ldh->h', pe, w1)
    return jnp.einsum('bigh,he->bige', jax.nn.gelu(h), w2)


def _cmp_to_sel(n_cmp, n_sel):
    cs = jnp.arange(n_cmp)[:, None] * CMP_STRIDE
    ss = jnp.arange(n_sel)[None, :] * SEL_LEN
    ov = jnp.clip(jnp.minimum(cs + CMP_LEN, ss + SEL_LEN) - jnp.maximum(cs, ss), 0, None)
    return ov.astype(jnp.float32) / CMP_LEN


def _sel_attend(q, kb, vb, idx, valid, qpos):
    B, T, G, R, D = q.shape
    bi = jnp.arange(B)[:, None, None, None]
    gi = jnp.arange(G)[None, :, None, None]
    kg = kb[bi, idx, :, gi].reshape(B, G, T, -1, D)
    vg = vb[bi, idx, :, gi].reshape(B, G, T, -1, D)
    kpos = idx[..., None] * SEL_LEN + jnp.arange(SEL_LEN)
    mask = (valid[..., None] & (kpos <= qpos[:, None, None])).reshape(B, G, T, -1)
    s = jnp.einsum('btgrd,bgtkd->bgrtk', q, kg).astype(jnp.float32) * SCALE
    p = _masked_probs(s, mask[:, :, None])
    return jnp.einsum('bgrtk,bgtkd->btgrd', p.astype(vg.dtype), vg)


def _nsa_cmp_sel(q, k_c, v_c, k_s, v_s, qpos, lp, sweep_queries):
    B, T, G, R, D = q.shape
    L = k_c.shape[1]
    L_pad = -(-L // SEL_LEN) * SEL_LEN
    pad = ((0, 0), (0, L_pad - L), (0, 0), (0, 0))
    k_c, v_c, k_s, v_s = jnp.pad(k_c, pad), jnp.pad(v_c, pad), jnp.pad(k_s, pad), jnp.pad(v_s, pad)
    k_cmp = _compress(k_c, lp['cmp_pe_k'], lp['cmp_w1_k'], lp['cmp_w2_k'])
    v_cmp = _compress(v_c, lp['cmp_pe_v'], lp['cmp_w1_v'], lp['cmp_w2_v'])
    n_cmp = k_cmp.shape[1]
    cmp_end = jnp.arange(n_cmp) * CMP_STRIDE + CMP_LEN - 1
    p_cmp = _attn_probs(q, k_cmp, cmp_end[None, :] <= qpos[:, None])
    o_cmp = _attn_out(p_cmp, v_cmp)
    n_sel = L_pad // SEL_LEN
    p_slc = jnp.einsum('bgrti,ij->bgtj', p_cmp, _cmp_to_sel(n_cmp, n_sel))
    blk = jnp.arange(n_sel)[None, :]
    cur = (qpos // SEL_LEN)[:, None]
    forced = (blk == 0) | (blk == cur) | (blk == cur - 1)
    score = jnp.where(blk <= cur, p_slc + jnp.where(forced, FORCE_BONUS, 0.0), NEG)
    top_s, top_i = lax.top_k(score, min(SEL_TOPK, n_sel))
    valid = top_s > 0.5 * NEG
    kb = k_s.reshape(B, n_sel, SEL_LEN, G, D)
    vb = v_s.reshape(B, n_sel, SEL_LEN, G, D)
    if sweep_queries:
        nq = T // SEL_Q_BLOCK
        xs = (q.reshape(B, nq, SEL_Q_BLOCK, G, R, D).swapaxes(0, 1),
              top_i.reshape(B, G, nq, SEL_Q_BLOCK, -1).transpose(2, 0, 1, 3, 4),
              valid.reshape(B, G, nq, SEL_Q_BLOCK, -1).transpose(2, 0, 1, 3, 4),
              qpos.reshape(nq, SEL_Q_BLOCK))
        o = lax.map(lambda a: _sel_attend(a[0], kb, vb, a[1], a[2], a[3]), xs)
        o_sel = o.swapaxes(0, 1).reshape(B, T, G, R, D)
    else:
        xs = (q[:, None], kb[:, None], vb[:, None], top_i[:, None], valid[:, None])
        o_sel = lax.map(lambda a: _sel_attend(a[0], a[1], a[2], a[3], a[4], qpos)[0], xs)
    return o_cmp, o_sel


def _window_banded(q, k, v):
    B, T, G, R, D = q.shape
    nb = T // WIN_Q_BLOCK
    pad = ((0, 0), (WINDOW, 0), (0, 0), (0, 0))
    kidx = jnp.arange(nb)[:, None] * WIN_Q_BLOCK + jnp.arange(WINDOW + WIN_Q_BLOCK)[None, :]
    kblk = jnp.pad(k, pad)[:, kidx]
    vblk = jnp.pad(v, pad)[:, kidx]
    kpos = kidx - WINDOW
    qpos = jnp.arange(T).reshape(nb, WIN_Q_BLOCK)
    d = qpos[:, :, None] - kpos[:, None, :]
    mask = (kpos[:, None, :] >= 0) & (d >= 0) & (d < WINDOW)
    o = _attn_out(_attn_probs(q.reshape(B, nb, WIN_Q_BLOCK, G, R, D), kblk, mask), vblk)
    return o.reshape(B, T, G, R, D)


def _window_dense(q, k, v, qpos, kpos):
    d = qpos[:, None] - kpos[None, :]
    return _attn_out(_attn_probs(q, k, (d >= 0) & (d < WINDOW)), v)


def _conv_module(u, buf, w_dw, b_dw, ln_g, ln_b):
    xp = jnp.concatenate([buf, u], axis=1)
    y = lax.conv_general_dilated(xp, w_dw[:, None, :], (1,), 'VALID',
                                 dimension_numbers=('NWC', 'WIO', 'NWC'),
                                 feature_group_count=CONV_CH) + b_dw
    return jax.nn.silu(_layernorm(y, ln_g, ln_b)), xp[:, -(CONV_WIDTH - 1):]


def _split_z(z, B, T):
    kv = lambda c: z[:, c:c + KVW].reshape(B, T, N_KV_HEADS, HEAD_DIM)
    q = z[:, COL_Q:COL_Q + QW].reshape(B, T, N_KV_HEADS, GROUP, HEAD_DIM)
    a = z[:, COL_GLU_A:COL_GLU_A + CONV_CH]
    b = z[:, COL_GLU_B:COL_GLU_B + CONV_CH]
    u = (a * jax.nn.sigmoid(b)).reshape(B, T, CONV_CH)
    return q, kv(COL_KC), kv(COL_VC), kv(COL_KS), kv(COL_VS), kv(COL_KW), kv(COL_VW), u


def _cmp_to_sel_t(n_cmp, n_sel, n_cmp_pad, n_sel_pad=None):
    cs = np.arange(n_cmp)[None, :] * CMP_STRIDE
    ss = np.arange(n_sel)[:, None] * SEL_LEN
    ov = np.clip(np.minimum(cs + CMP_LEN, ss + SEL_LEN) - np.maximum(cs, ss), 0, None)
    ct = np.zeros((n_sel_pad or n_sel, n_cmp_pad), np.float32)
    ct[:n_sel, :n_cmp] = ov.astype(np.float32) / CMP_LEN
    return jnp.asarray(ct, jnp.bfloat16)


def _compress_params(pe, w1, w2):
    w1cat = jnp.concatenate([w1[:CMP_STRIDE], w1[CMP_STRIDE:]], axis=-1).astype(jnp.bfloat16)
    c = jnp.einsum('ld,ldh->h', pe, w1).reshape(1, -1)
    return w1cat, c, w2.astype(jnp.bfloat16)


def _tail_page(z, col, nb):
    x = z[:, col:col + KVW].reshape(nb, DEC_SEQ * N_KV_HEADS, HEAD_DIM)
    x = jnp.pad(x, ((0, 0), (0, PAGE_ROWS - DEC_SEQ * N_KV_HEADS), (0, 0)))
    return x.reshape(nb * PAGE_ROWS, HEAD_DIM)


def _pad_cmp(x_cmp, n_pad):
    x = jnp.transpose(x_cmp, (0, 2, 1, 3))
    return jnp.pad(x, ((0, 0), (0, 0), (0, n_pad - x.shape[2]), (0, 0)))


def _layer(x, lp, past, page_table, prompt, tm):
    B, T, _ = x.shape
    n = B * T
    x2 = x.reshape(n, D_MODEL)
    z = _proj_in(x2, lp['norm_mix'], lp['w_in_p'], tm, 768)
    kv = lambda c: z[:, c:c + KVW].reshape(B, T, N_KV_HEADS, HEAD_DIM)
    k_c, v_c, k_s, v_s = kv(COL_KC), kv(COL_VC), kv(COL_KS), kv(COL_VS)
    cmp_k, cmp_v = lp['cmp_k'], lp['cmp_v']
    conv = (lp['conv_w'], lp['conv_b'], lp['conv_ln_g'], lp['conv_ln_b'])
    if prompt:
        n_ch = T // CMP_STRIDE
        ct = _cmp_to_sel_t(n_ch - 1, T // SEL_LEN, n_ch)
        kcmp, vcmp = _compress_seq(z, cmp_k[0], cmp_v[0], cmp_k[1], cmp_v[1], cmp_k[2], cmp_v[2], B, T)
        o_cmp, bias = _cmp_select(z, kcmp, vcmp, ct, B, T, 128)
        o_sel = _sel_attn(z, bias, B, T, 256, 256)
        o_win = _win_attn(z, B, T, 128)
        wl = min(WINDOW, T)
        new_wk, new_wv = kv(COL_KW)[:, T - wl:], kv(COL_VW)[:, T - wl:]
        conv_y, u_tail = _conv_seq(z, *conv, B, T, 256)
        new_conv = u_tail[:, CONV_HALO - (CONV_WIDTH - 1):]
    else:
        as_rows = lambda a: a.reshape(-1, HEAD_DIM)
        n_ch = -(-(PAST_LEN + T) // SEL_LEN) * SEL_LEN // CMP_STRIDE
        n_sel = -(-(PAST_LEN + T) // SEL_LEN)
        ct = _cmp_to_sel_t(n_ch - 1, n_sel, -(-n_ch // LANES) * LANES, -(-n_sel // 8) * 8)
        o_cmp, bias = _dec_cmp_select(
            page_table, as_rows(past['cmp_k']), as_rows(past['cmp_v']),
            _tail_page(z, COL_KC, B), _tail_page(z, COL_VC, B), z,
            cmp_k[0], cmp_v[0], cmp_k[1], cmp_v[1], cmp_k[2], cmp_v[2], ct, PAST_LEN)
        o_sel = _dec_sel_attn(page_table, as_rows(past['sel_k']), as_rows(past['sel_v']),
                              _tail_page(z, COL_KS, B), _tail_page(z, COL_VS, B), z, bias, PAST_LEN)
        wl = past['win_k'].shape[1]
        o_win, new_wk, new_wv = _dec_win_attn(
            z, as_rows(past['win_k']), as_rows(past['win_v']),
            _tail_page(z, COL_KW, B), _tail_page(z, COL_VW, B), B, PAST_LEN)
        new_wk = new_wk.reshape(B, wl, N_KV_HEADS, HEAD_DIM)
        new_wv = new_wv.reshape(B, wl, N_KV_HEADS, HEAD_DIM)
        conv_y, new_conv = _dec_conv(z, past['conv'], *conv)
    merged = _merge(o_cmp, o_sel, o_win, conv_y, z,
                    lp['w_nsa_out_b'], lp['w_conv_out_b'], 256, 512)
    h, hn = _out_proj(merged, x2, lp['w_out_b'], lp['norm_ffn'], 256)
    ia, ib, gt = _peer_route(hn, lp['peer_wq_b'], lp['peer_k1_b'], lp['peer_k2_b'], 256)
    y = _peer_ffn(hn, h, ia, ib, gt, lp['peer_u_b'], lp['peer_v_b'], lp['norm_final'], 256, 8)
    return y.reshape(B, T, D_MODEL), (k_c, v_c, k_s, v_s, new_wk, new_wv, new_conv)


def kernel(x_prompt, x_sample, cache_cmp_k, cache_cmp_v, cache_sel_k, cache_sel_v, cache_win_k, cache_win_v, state_conv, page_table, norm_mix, w_in, cmp_pe_k, cmp_w1_k, cmp_w2_k, cmp_pe_v, cmp_w1_v, cmp_w2_v, w_nsa_out, conv_w, conv_b, conv_ln_g, conv_ln_b, w_conv_out, w_out, norm_ffn, peer_wq, peer_k1, peer_k2, peer_u, peer_v, norm_final):
    assert DEPTH == 1
    bf = lambda w: w.astype(jnp.bfloat16)
    l = 0
    lp = {'norm_mix': norm_mix[l], 'w_in_p': _permute_w_in(w_in[l]),
          'cmp_k': _compress_params(cmp_pe_k[l], cmp_w1_k[l], cmp_w2_k[l]),
          'cmp_v': _compress_params(cmp_pe_v[l], cmp_w1_v[l], cmp_w2_v[l]),
          'w_nsa_out_b': bf(w_nsa_out[l]), 'conv_w': conv_w[l], 'conv_b': conv_b[l],
          'conv_ln_g': conv_ln_g[l], 'conv_ln_b': conv_ln_b[l],
          'w_conv_out_b': bf(w_conv_out[l]), 'w_out_b': bf(w_out[l]), 'norm_ffn': norm_ffn[l],
          'peer_wq_b': bf(peer_wq[l]), 'peer_k1_b': bf(peer_k1[l]), 'peer_k2_b': bf(peer_k2[l]),
          'peer_u_b': bf(peer_u[l]), 'peer_v_b': bf(peer_v[l]), 'norm_final': norm_final}
    past = {'cmp_k': cache_cmp_k[l], 'cmp_v': cache_cmp_v[l], 'sel_k': cache_sel_k[l],
            'sel_v': cache_sel_v[l], 'win_k': cache_win_k[l], 'win_v': cache_win_v[l],
            'conv': state_conv[l]}
    y_prompt, st_p = _layer(x_prompt, lp, None, None, True, 512)
    y_sample, st_s = _layer(x_sample, lp, past, page_table, False, 256)
    p_ck, p_cv, p_sk, p_sv, p_wk, p_wv, p_conv = [a[None] for a in st_p]
    s_ck, s_cv, s_sk, s_sv, s_wk, s_wv, s_conv = [a[None] for a in st_s]
    return (y_prompt, y_sample, p_ck, s_ck, p_cv, s_cv, p_sk, s_sk, p_sv, s_sv,
            p_wk, s_wk, p_wv, s_wv, p_conv, s_conv)
```

```python
import functools

import jax
import jax.numpy as jnp
import numpy as np
from jax import lax
from jax.experimental import pallas as pl
from jax.experimental.pallas import tpu as pltpu

D_MODEL = 2048
BATCH = 2
SEQ = 4096
DEPTH = 1
DEC_BATCH = 32
DEC_SEQ = 8
PAST_LEN = 8192
PAGE_SIZE = 128
N_HEADS = 16
HEAD_DIM = 128
N_KV_HEADS = 4
GROUP = N_HEADS // N_KV_HEADS
CMP_STRIDE = 16
CMP_LEN = 2 * CMP_STRIDE
SEL_LEN = 64
SEL_SHIFT = SEL_LEN.bit_length() - 1
SEL_TOPK = 16
WINDOW = 512
WIN_Q_BLOCK = 128
SEL_Q_BLOCK = 64
FORCE_BONUS = 1e4
CONV_CH = D_MODEL // 2
CONV_WIDTH = 31
PEER_HEADS = 8
PEER_NKEYS = 128
PEER_DKEY = 256
PEER_TOPK = 16
PEER_CHUNK = 128
QW = N_HEADS * HEAD_DIM
KVW = N_KV_HEADS * HEAD_DIM
N_GATES = 3 * N_HEADS
SCALE = HEAD_DIM ** -0.5
EPS = 1e-6
NEG = -1e30

VMEM_LIMIT_BYTES = 48 * 1024 * 1024
VMEM_LIMIT_PEER_BYTES = 58 * 1024 * 1024

GATE_PAD = 256
COL_Q = 0
COL_KC = COL_Q + QW
COL_VC = COL_KC + KVW
COL_KS = COL_VC + KVW
COL_VS = COL_KS + KVW
COL_KW = COL_VS + KVW
COL_VW = COL_KW + KVW
COL_GLU_A = COL_VW + KVW
COL_GLU_B = COL_GLU_A + CONV_CH
COL_GM_A = COL_GLU_B + CONV_CH
COL_GM_B = COL_GM_A + D_MODEL
COL_G = COL_GM_B + D_MODEL
N_COLS = COL_G + GATE_PAD


def _permute_w_in(w_in):
    g0 = QW + 6 * KVW
    g1 = g0 + N_GATES
    parts = [w_in[:, :g0], w_in[:, g1:], w_in[:, g0:g1],
             jnp.zeros((w_in.shape[0], GATE_PAD - N_GATES), w_in.dtype)]
    return jnp.concatenate(parts, axis=1).astype(jnp.bfloat16)


def _proj_in_kernel(x_ref, g_ref, w_ref, o_ref, xn_ref):
    @pl.when(pl.program_id(1) == 0)
    def _():
        x = x_ref[...]
        ms = jnp.mean(x * x, axis=-1, keepdims=True)
        xn_ref[...] = (x * lax.rsqrt(ms + EPS) * g_ref[...]).astype(jnp.bfloat16)

    o_ref[...] = jnp.dot(xn_ref[...], w_ref[...], preferred_element_type=jnp.float32)


def _proj_in(x, gain, w_bf16, tm, tn):
    n, d = x.shape
    nc = w_bf16.shape[1]
    return pl.pallas_call(
        _proj_in_kernel,
        out_shape=jax.ShapeDtypeStruct((n, nc), jnp.float32),
        grid=(n // tm, nc // tn),
        in_specs=[pl.BlockSpec((tm, d), lambda i, j: (i, 0)),
                  pl.BlockSpec((1, d), lambda i, j: (0, 0)),
                  pl.BlockSpec((d, tn), lambda i, j: (0, j))],
        out_specs=pl.BlockSpec((tm, tn), lambda i, j: (i, j)),
        scratch_shapes=[pltpu.VMEM((tm, d), jnp.bfloat16)],
        compiler_params=pltpu.CompilerParams(
            dimension_semantics=("parallel", "arbitrary"),
            vmem_limit_bytes=VMEM_LIMIT_BYTES),
        name="proj_in",
    )(x, gain.reshape(1, d), w_bf16)


_NT = (((1,), (1,)), ((), ()))
MASK_BIAS = -1e9


def _stack_heads(q):
    return jnp.concatenate([q[:, r * HEAD_DIM:(r + 1) * HEAD_DIM] for r in range(GROUP)], axis=0)


def _unstack_heads(o, tq):
    return jnp.concatenate([o[r * tq:(r + 1) * tq] for r in range(GROUP)], axis=1)


def _cmp_select_kernel(q_ref, kc_ref, vc_ref, ct_ref, ocmp_ref, bias_ref, *, tq):
    t0 = pl.program_id(2) * tq
    q4 = _stack_heads(q_ref[...]).astype(jnp.bfloat16)
    kc = kc_ref[0, 0].astype(jnp.bfloat16)
    vc = vc_ref[0, 0].astype(jnp.bfloat16)
    s = lax.dot_general(q4, kc, _NT, preferred_element_type=jnp.float32) * SCALE
    t = t0 + (lax.broadcasted_iota(jnp.int32, s.shape, 0) & (tq - 1))
    i = lax.broadcasted_iota(jnp.int32, s.shape, 1)
    valid = (i * CMP_STRIDE + (CMP_LEN - 1)) <= t
    s = jnp.where(valid, s, NEG)
    m = jnp.max(s, axis=-1, keepdims=True)
    e = jnp.where(valid, jnp.exp(s - m), 0.0)
    p = (e / jnp.maximum(jnp.sum(e, axis=-1, keepdims=True), 1e-30)).astype(jnp.bfloat16)
    ocmp_ref[...] = _unstack_heads(jnp.dot(p, vc, preferred_element_type=jnp.float32), tq)

    ct = ct_ref[...]
    n_sel = ct.shape[0]
    pslc = lax.dot_general(ct, p[0:tq], _NT, preferred_element_type=jnp.float32)
    for r in range(1, GROUP):
        pslc = pslc + lax.dot_general(ct, p[r * tq:(r + 1) * tq], _NT,
                                      preferred_element_type=jnp.float32)
    j = lax.broadcasted_iota(jnp.int32, (n_sel, tq), 0)
    cur = (t0 + lax.broadcasted_iota(jnp.int32, (n_sel, tq), 1)) >> SEL_SHIFT
    forced = (j == 0) | (j == cur) | (j == cur - 1)
    score = jnp.where(j <= cur, pslc + jnp.where(forced, FORCE_BONUS, 0.0), NEG)
    rank = jnp.zeros((n_sel, tq), jnp.float32)
    for jp in range(n_sel):
        sj = score[jp:jp + 1, :]
        rank = rank + jnp.where(sj > score, 1.0, jnp.where((sj == score) & (j > jp), 1.0, 0.0))
    bias = jnp.where((rank < SEL_TOPK) & (j <= cur), 0.0, MASK_BIAS)
    bias = jnp.concatenate([bias, jnp.zeros((HEAD_DIM - n_sel, tq), jnp.float32)], axis=0)
    bias_ref[0, 0] = bias.T.astype(jnp.bfloat16)


def _cmp_select(z, kcmp, vcmp, ct, B, T, tq):
    n_cp = kcmp.shape[2]
    nq = T // tq
    return pl.pallas_call(
        functools.partial(_cmp_select_kernel, tq=tq),
        out_shape=(jax.ShapeDtypeStruct((B * T, QW), jnp.float32),
                   jax.ShapeDtypeStruct((B, N_KV_HEADS, T, HEAD_DIM), jnp.bfloat16)),
        grid=(B, N_KV_HEADS, nq),
        in_specs=[pl.BlockSpec((tq, GROUP * HEAD_DIM), lambda b, g, qi: (b * nq + qi, g)),
                  pl.BlockSpec((1, 1, n_cp, HEAD_DIM), lambda b, g, qi: (b, g, 0, 0)),
                  pl.BlockSpec((1, 1, n_cp, HEAD_DIM), lambda b, g, qi: (b, g, 0, 0)),
                  pl.BlockSpec(ct.shape, lambda b, g, qi: (0, 0))],
        out_specs=(pl.BlockSpec((tq, GROUP * HEAD_DIM), lambda b, g, qi: (b * nq + qi, g)),
                   pl.BlockSpec((1, 1, tq, HEAD_DIM), lambda b, g, qi: (b, g, qi, 0))),
        compiler_params=pltpu.CompilerParams(
            dimension_semantics=("parallel", "parallel", "arbitrary"),
            vmem_limit_bytes=VMEM_LIMIT_BYTES),
        name="cmp_select",
    )(z, kcmp, vcmp, ct)


def _sel_attn_kernel(q_ref, bias_ref, k_ref, v_ref, o_ref, kaug, vt, acc_sc, *, tq, kc):
    qi = pl.program_id(2)
    t0 = qi * tq
    T = k_ref.shape[0]
    nq = GROUP * tq

    @pl.when(qi == 0)
    def _():
        kaug[:, 0:HEAD_DIM] = k_ref[...].astype(jnp.bfloat16)
        blk = lax.broadcasted_iota(jnp.int32, (T, HEAD_DIM), 0) >> SEL_SHIFT
        col = lax.broadcasted_iota(jnp.int32, (T, HEAD_DIM), 1)
        kaug[:, HEAD_DIM:2 * HEAD_DIM] = jnp.where(blk == col, 1.0, 0.0).astype(jnp.bfloat16)
        for c in range(T // kc):
            vt[c] = v_ref[c * kc:(c + 1) * kc, :].T.astype(jnp.bfloat16)

    q = q_ref[...]
    bias = bias_ref[0, 0]
    qa = jnp.concatenate(
        [jnp.concatenate([q[:, r * HEAD_DIM:(r + 1) * HEAD_DIM].astype(jnp.bfloat16), bias], axis=1)
         for r in range(GROUP)], axis=0)
    acc_sc[...] = jnp.zeros(acc_sc.shape, jnp.float32)
    t = t0 + (lax.broadcasted_iota(jnp.int32, (kc, nq), 1) & (tq - 1))
    key = lax.broadcasted_iota(jnp.int32, (kc, nq), 0)

    def step(c, m_old, l_old, causal):
        k0 = pl.multiple_of(c * kc, kc)
        s = lax.dot_general(kaug[pl.ds(k0, kc), :], qa, _NT,
                            preferred_element_type=jnp.float32) * SCALE
        if causal:
            s = jnp.where(k0 + key <= t, s, NEG)
        m_new = jnp.maximum(m_old, jnp.max(s, axis=0, keepdims=True))
        alpha = jnp.exp(m_old - m_new)
        p = jnp.exp(s - m_new)
        l_new = alpha * l_old + jnp.sum(p, axis=0, keepdims=True)
        acc_sc[...] = alpha * acc_sc[...] + jnp.dot(vt[c], p.astype(jnp.bfloat16),
                                                    preferred_element_type=jnp.float32)
        return m_new, l_new

    last = (t0 + tq - 1) // kc
    init = (jnp.full((1, nq), NEG, jnp.float32), jnp.zeros((1, nq), jnp.float32))
    m, l = lax.fori_loop(0, last, lambda c, ml: step(c, ml[0], ml[1], False), init)
    m, l = step(last, m, l, True)
    o_ref[...] = _unstack_heads((acc_sc[...] / jnp.maximum(l, 1e-30)).T, tq)


def _sel_attn(z, bias, B, T, tq, kc):
    nq = T // tq
    ck = COL_KS // HEAD_DIM
    cv = COL_VS // HEAD_DIM
    return pl.pallas_call(
        functools.partial(_sel_attn_kernel, tq=tq, kc=kc),
        out_shape=jax.ShapeDtypeStruct((B * T, QW), jnp.float32),
        grid=(B, N_KV_HEADS, nq),
        in_specs=[pl.BlockSpec((tq, GROUP * HEAD_DIM), lambda b, g, qi: (b * nq + qi, g)),
                  pl.BlockSpec((1, 1, tq, HEAD_DIM), lambda b, g, qi: (b, g, qi, 0)),
                  pl.BlockSpec((T, HEAD_DIM), lambda b, g, qi: (b, ck + g)),
                  pl.BlockSpec((T, HEAD_DIM), lambda b, g, qi: (b, cv + g))],
        out_specs=pl.BlockSpec((tq, GROUP * HEAD_DIM), lambda b, g, qi: (b * nq + qi, g)),
        scratch_shapes=[pltpu.VMEM((T, 2 * HEAD_DIM), jnp.bfloat16),
                        pltpu.VMEM((T // kc, HEAD_DIM, kc), jnp.bfloat16),
                        pltpu.VMEM((HEAD_DIM, GROUP * tq), jnp.float32)],
        compiler_params=pltpu.CompilerParams(
            dimension_semantics=("parallel", "parallel", "arbitrary"),
            vmem_limit_bytes=VMEM_LIMIT_BYTES),
        name="sel_attn",
    )(z, bias, z, z)


def _win_attn_kernel(q_ref, k_ref, v_ref, o_ref, kpad, vt, *, tq):
    qi = pl.program_id(2)
    t0 = pl.multiple_of(qi * tq, tq)
    T = k_ref.shape[0]
    n_pad = WINDOW // tq
    n_span = n_pad + 1
    span = n_span * tq

    @pl.when(qi == 0)
    def _():
        kpad[0:WINDOW, :] = jnp.zeros((WINDOW, HEAD_DIM), jnp.bfloat16)
        kpad[WINDOW:WINDOW + T, :] = k_ref[...].astype(jnp.bfloat16)
        for c in range(n_pad):
            vt[c] = jnp.zeros((HEAD_DIM, tq), jnp.bfloat16)
        for c in range(T // tq):
            vt[n_pad + c] = v_ref[c * tq:(c + 1) * tq, :].T.astype(jnp.bfloat16)

    q4 = _stack_heads(q_ref[...]).astype(jnp.bfloat16)
    s = lax.dot_general(kpad[pl.ds(t0, span), :], q4, _NT,
                        preferred_element_type=jnp.float32) * SCALE
    kpos = t0 - WINDOW + lax.broadcasted_iota(jnp.int32, s.shape, 0)
    t = t0 + (lax.broadcasted_iota(jnp.int32, s.shape, 1) & (tq - 1))
    d = t - kpos
    valid = (kpos >= 0) & (d >= 0) & (d < WINDOW)
    s = jnp.where(valid, s, NEG)
    m = jnp.max(s, axis=0, keepdims=True)
    e = jnp.where(valid, jnp.exp(s - m), 0.0)
    p = (e / jnp.maximum(jnp.sum(e, axis=0, keepdims=True), 1e-30)).astype(jnp.bfloat16)
    o_t = None
    for c in range(n_span):
        part = jnp.dot(vt[qi + c], p[c * tq:(c + 1) * tq], preferred_element_type=jnp.float32)
        o_t = part if o_t is None else o_t + part
    o_ref[...] = _unstack_heads(o_t.T, tq)


def _win_attn(z, B, T, tq):
    nq = T // tq
    ck = COL_KW // HEAD_DIM
    cv = COL_VW // HEAD_DIM
    return pl.pallas_call(
        functools.partial(_win_attn_kernel, tq=tq),
        out_shape=jax.ShapeDtypeStruct((B * T, QW), jnp.float32),
        grid=(B, N_KV_HEADS, nq),
        in_specs=[pl.BlockSpec((tq, GROUP * HEAD_DIM), lambda b, g, qi: (b * nq + qi, g)),
                  pl.BlockSpec((T, HEAD_DIM), lambda b, g, qi: (b, ck + g)),
                  pl.BlockSpec((T, HEAD_DIM), lambda b, g, qi: (b, cv + g))],
        out_specs=pl.BlockSpec((tq, GROUP * HEAD_DIM), lambda b, g, qi: (b * nq + qi, g)),
        scratch_shapes=[pltpu.VMEM((T + WINDOW, HEAD_DIM), jnp.bfloat16),
                        pltpu.VMEM(((T + WINDOW) // tq, HEAD_DIM, tq), jnp.bfloat16)],
        compiler_params=pltpu.CompilerParams(
            dimension_semantics=("parallel", "parallel", "arbitrary"),
            vmem_limit_bytes=VMEM_LIMIT_BYTES),
        name="win_attn",
    )(z, z, z)


PAGES_PER_STEP = 8
CHUNKS_PER_PAGE = PAGE_SIZE // CMP_STRIDE
PAGE_ROWS = PAGE_SIZE * N_KV_HEADS
DEC_ROWS = N_HEADS * DEC_SEQ
assert DEC_ROWS == 128 and DEC_SEQ == 8


def _head_rows(ref, g, n_tokens):
    return ref[pl.ds(g, n_tokens, stride=N_KV_HEADS), :]


def _dec_rows(q):
    return jnp.concatenate([q[:, (g * GROUP + r) * HEAD_DIM:(g * GROUP + r + 1) * HEAD_DIM]
                            for r in range(GROUP) for g in range(N_KV_HEADS)], axis=0)


def _dec_cols(o):
    return jnp.concatenate(
        [o[(r * N_KV_HEADS + g) * DEC_SEQ:(r * N_KV_HEADS + g + 1) * DEC_SEQ]
         for g in range(N_KV_HEADS) for r in range(GROUP)], axis=1)


def _dec_row_ids():
    row = lax.broadcasted_iota(jnp.int32, (DEC_ROWS, 1), 0)
    return (row >> 3) & (N_KV_HEADS - 1), row & (DEC_SEQ - 1)


def _group_scores(qall, keys_of, row_g):
    out = None
    for g in range(N_KV_HEADS):
        s = lax.dot_general(qall, keys_of(g), _NT, preferred_element_type=jnp.float32)
        s = jnp.where(row_g == g, s, 0.0)
        out = s if out is None else out + s
    return out


def _group_values(p, vals_of, row_g):
    out = None
    for g in range(N_KV_HEADS):
        pg = jnp.where(row_g == g, p, 0.0).astype(jnp.bfloat16)
        o = jnp.dot(pg, vals_of(g), preferred_element_type=jnp.float32)
        out = o if out is None else out + o
    return out


TAP_ROWS = 2 * N_KV_HEADS


def _chunk_hidden(page_refs, w1_ref):
    chunk_rows = CMP_STRIDE * N_KV_HEADS
    n_rows = len(page_refs) * CHUNKS_PER_PAGE * TAP_ROWS
    odd = (lax.broadcasted_iota(jnp.int32, (n_rows, 1), 0) & N_KV_HEADS) != 0
    acc = None
    for p in range(CMP_STRIDE // 2):
        x = jnp.concatenate(
            [pg[c * chunk_rows + p * TAP_ROWS:c * chunk_rows + (p + 1) * TAP_ROWS, :]
             for pg in page_refs for c in range(CHUNKS_PER_PAGE)], axis=0).astype(jnp.bfloat16)
        even_tap = jnp.dot(x, w1_ref[2 * p], preferred_element_type=jnp.float32)
        odd_tap = jnp.dot(x, w1_ref[2 * p + 1], preferred_element_type=jnp.float32)
        h = jnp.where(odd, odd_tap, even_tap)
        acc = h if acc is None else acc + h
    return acc


def _hidden_rows(h_sc, g, first_chunk, n):
    even = h_sc[pl.ds(first_chunk * TAP_ROWS + g, n, stride=TAP_ROWS), :]
    odd = h_sc[pl.ds(first_chunk * TAP_ROWS + N_KV_HEADS + g, n, stride=TAP_ROWS), :]
    return even + odd


def _store_hidden(h_lo, h_hi, rows, h):
    h_lo[rows, :] = h[:, 0:HEAD_DIM]
    h_hi[rows, :] = h[:, HEAD_DIM:2 * HEAD_DIM]


def _dec_cmp_kernel(pt_ref, *refs, past_len, n_sel):
    nps = PAGES_PER_STEP
    kp, vp = refs[0:nps], refs[nps:2 * nps]
    (kt_ref, vt_ref, q_ref, w1k_ref, w1v_ref, ck_ref, cv_ref, w2k_ref, w2v_ref, ct_ref,
     ocmp_ref, bias_ref, hk_lo, hk_hi, hv_lo, hv_hi, score_sc) = refs[2 * nps:]
    j = pl.program_id(1)
    rows = nps * CHUNKS_PER_PAGE * TAP_ROWS
    base = (past_len // PAGE_SIZE) * CHUNKS_PER_PAGE
    tail_end = (base + CHUNKS_PER_PAGE) * TAP_ROWS

    step_rows = pl.ds(pl.multiple_of(j * rows, rows), rows)
    for pages, w1_ref, h_lo, h_hi in ((kp, w1k_ref, hk_lo, hk_hi), (vp, w1v_ref, hv_lo, hv_hi)):
        _store_hidden(h_lo, h_hi, step_rows, _chunk_hidden(pages, w1_ref))

    @pl.when(j == pl.num_programs(1) - 1)
    def _():
        for t_ref, w1_ref, h_lo, h_hi in ((kt_ref, w1k_ref, hk_lo, hk_hi),
                                          (vt_ref, w1v_ref, hv_lo, hv_hi)):
            _store_hidden(h_lo, h_hi, slice(base * TAP_ROWS, tail_end), _chunk_hidden([t_ref], w1_ref))
            zeros = jnp.zeros((h_lo.shape[0] - tail_end, HEAD_DIM), jnp.float32)
            h_lo[tail_end:, :] = zeros
            h_hi[tail_end:, :] = zeros
        ncp = ct_ref.shape[1]
        kc, vc = [], []
        for g in range(N_KV_HEADS):
            for h_lo, h_hi, c_ref, w2_ref, dst in ((hk_lo, hk_hi, ck_ref, w2k_ref, kc),
                                                   (hv_lo, hv_hi, cv_ref, w2v_ref, vc)):
                hh = _hidden_rows(h_lo, g, 0, ncp) + _hidden_rows(h_hi, g, 1, ncp) + c_ref[...]
                dst.append(jnp.dot(jax.nn.gelu(hh).astype(jnp.bfloat16), w2_ref[...],
                                   preferred_element_type=jnp.float32).astype(jnp.bfloat16))
        qall = _dec_rows(q_ref[...]).astype(jnp.bfloat16)
        row_g, row_t = _dec_row_ids()
        s = _group_scores(qall, lambda g: kc[g], row_g) * SCALE
        i = lax.broadcasted_iota(jnp.int32, s.shape, 1)
        valid = (i * CMP_STRIDE + (CMP_LEN - 1)) <= (past_len + row_t)
        s = jnp.where(valid, s, NEG)
        m = jnp.max(s, axis=-1, keepdims=True)
        e = jnp.where(valid, jnp.exp(s - m), 0.0)
        p = e / jnp.maximum(jnp.sum(e, axis=-1, keepdims=True), 1e-30)
        ocmp_ref[...] = _dec_cols(_group_values(p, lambda g: vc[g], row_g))

        x = lax.dot_general(ct_ref[...], p.astype(jnp.bfloat16), _NT,
                            preferred_element_type=jnp.float32)
        quarter = DEC_ROWS // GROUP
        pslc = x
        for r in range(1, GROUP):
            pslc = pslc + pltpu.roll(x, r * quarter, 1)
        shape = pslc.shape
        jj = lax.broadcasted_iota(jnp.int32, shape, 0)
        cur = (past_len + (lax.broadcasted_iota(jnp.int32, shape, 1) & (DEC_SEQ - 1))) >> SEL_SHIFT
        forced = (jj == 0) | (jj == cur) | (jj == cur - 1)
        score = jnp.where(jj <= cur, pslc + jnp.where(forced, FORCE_BONUS, 0.0), NEG)
        score_sc[...] = score

        def rank_step(jp, rank):
            sj = score_sc[pl.ds(jp, 1), :]
            return rank + jnp.where(sj > score, 1.0, jnp.where((sj == score) & (jj > jp), 1.0, 0.0))

        rank = lax.fori_loop(0, n_sel, rank_step, jnp.zeros(shape, jnp.float32))
        bias = jnp.where((rank < SEL_TOPK) & (jj <= cur), 0.0, MASK_BIAS)
        pad = jnp.full((bias_ref.shape[2] - shape[0], shape[1]), MASK_BIAS, jnp.float32)
        bias_ref[0] = jnp.concatenate([bias, pad], axis=0).T.astype(jnp.bfloat16)


PAGE_BLOCK = (PAGE_ROWS, HEAD_DIM)


def _page_specs(n):
    return [pl.BlockSpec(PAGE_BLOCK, lambda b, j, pt, i=i: (pt[b, j * PAGES_PER_STEP + i], 0))
            for i in range(n)]


_TAIL_SPEC = pl.BlockSpec(PAGE_BLOCK, lambda b, j, pt: (b, 0))


def _dec_cmp_select(page_table, pool_k, pool_v, tail_k, tail_v, z, w1k, w1v, ck, cv, w2k, w2v, ct,
                    past_len):
    nb = page_table.shape[0]
    n_pages = past_len // PAGE_SIZE
    n_sel, ncp = ct.shape
    n_sel_real = -(-(past_len + DEC_SEQ) // SEL_LEN)
    full2 = lambda a: pl.BlockSpec(a.shape, lambda b, j, pt: (0, 0))
    full3 = lambda a: pl.BlockSpec(a.shape, lambda b, j, pt: (0, 0, 0))
    tail = _TAIL_SPEC
    n_blocks = 2 * HEAD_DIM
    grid_spec = pltpu.PrefetchScalarGridSpec(
        num_scalar_prefetch=1,
        grid=(nb, n_pages // PAGES_PER_STEP),
        in_specs=(_page_specs(PAGES_PER_STEP) + _page_specs(PAGES_PER_STEP)
                  + [tail, tail, pl.BlockSpec((DEC_SEQ, QW), lambda b, j, pt: (b, 0)),
                     full3(w1k), full3(w1v), full2(ck), full2(cv), full2(w2k), full2(w2v), full2(ct)]),
        out_specs=(pl.BlockSpec((DEC_SEQ, QW), lambda b, j, pt: (b, 0)),
                   pl.BlockSpec((1, DEC_ROWS, n_blocks), lambda b, j, pt: (b, 0, 0))),
        scratch_shapes=[pltpu.VMEM(((ncp + 8) * TAP_ROWS, HEAD_DIM), jnp.float32)] * 4
        + [pltpu.VMEM((n_sel, DEC_ROWS), jnp.float32)])
    return pl.pallas_call(
        functools.partial(_dec_cmp_kernel, past_len=past_len, n_sel=n_sel_real),
        out_shape=(jax.ShapeDtypeStruct((nb * DEC_SEQ, QW), jnp.float32),
                   jax.ShapeDtypeStruct((nb, DEC_ROWS, n_blocks), jnp.bfloat16)),
        grid_spec=grid_spec,
        compiler_params=pltpu.CompilerParams(
            dimension_semantics=("parallel", "arbitrary"),
            vmem_limit_bytes=VMEM_LIMIT_BYTES),
        name="dec_cmp_select",
    )(page_table, *([pool_k] * PAGES_PER_STEP), *([pool_v] * PAGES_PER_STEP), tail_k, tail_v, z,
      w1k, w1v, ck, cv, w2k, w2v, ct)


def _dec_sel_kernel(pt_ref, *refs, past_len):
    nps = PAGES_PER_STEP
    kp, vp = refs[0:nps], refs[nps:2 * nps]
    kt_ref, vt_ref, q_ref, bias_ref, o_ref, m_sc, l_sc, acc_sc = refs[2 * nps:]
    j = pl.program_id(1)
    qall = _dec_rows(q_ref[...]).astype(jnp.bfloat16)
    row_g, row_t = _dec_row_ids()
    selcols = bias_ref[0]

    @pl.when(j == 0)
    def _():
        m_sc[...] = jnp.full(m_sc.shape, NEG, jnp.float32)
        l_sc[...] = jnp.zeros(l_sc.shape, jnp.float32)
        acc_sc[...] = jnp.zeros(acc_sc.shape, jnp.float32)

    def update(k_refs, v_refs, first_blk, causal):
        grp = lambda ref, g: _head_rows(ref, g, PAGE_SIZE).astype(jnp.bfloat16)
        s = jnp.concatenate([_group_scores(qall, functools.partial(grp, kr), row_g)
                             for kr in k_refs], axis=1) * SCALE
        nk = s.shape[1]
        n_blocks = selcols.shape[1]
        blk = first_blk + (lax.broadcasted_iota(jnp.int32, (n_blocks, nk), 1) >> SEL_SHIFT)
        onehot = jnp.where(lax.broadcasted_iota(jnp.int32, (n_blocks, nk), 0) == blk, 1.0, 0.0)
        s = s + jnp.dot(selcols, onehot.astype(jnp.bfloat16), preferred_element_type=jnp.float32)
        if causal:
            kpos = first_blk * SEL_LEN + lax.broadcasted_iota(jnp.int32, s.shape, 1)
            s = jnp.where(kpos <= past_len + row_t, s, NEG)
        m_old = m_sc[...]
        m_new = jnp.maximum(m_old, jnp.max(s, axis=-1, keepdims=True))
        alpha = jnp.exp(m_old - m_new)
        p = jnp.exp(s - m_new)
        l_sc[...] = alpha * l_sc[...] + jnp.sum(p, axis=-1, keepdims=True)
        pv = None
        for n, vr in enumerate(v_refs):
            o = _group_values(p[:, n * PAGE_SIZE:(n + 1) * PAGE_SIZE], functools.partial(grp, vr), row_g)
            pv = o if pv is None else pv + o
        acc_sc[...] = alpha * acc_sc[...] + pv
        m_sc[...] = m_new

    update(kp, vp, j * (nps * PAGE_SIZE // SEL_LEN), False)

    @pl.when(j == pl.num_programs(1) - 1)
    def _():
        update([kt_ref], [vt_ref], past_len // SEL_LEN, True)
        o_ref[...] = _dec_cols(acc_sc[...] / jnp.maximum(l_sc[...], 1e-30))


def _dec_sel_attn(page_table, pool_k, pool_v, tail_k, tail_v, z, bias, past_len):
    nb = page_table.shape[0]
    n_pages = past_len // PAGE_SIZE
    tail = _TAIL_SPEC
    grid_spec = pltpu.PrefetchScalarGridSpec(
        num_scalar_prefetch=1,
        grid=(nb, n_pages // PAGES_PER_STEP),
        in_specs=(_page_specs(PAGES_PER_STEP) + _page_specs(PAGES_PER_STEP)
                  + [tail, tail, pl.BlockSpec((DEC_SEQ, QW), lambda b, j, pt: (b, 0)),
                     pl.BlockSpec((1,) + bias.shape[1:], lambda b, j, pt: (b, 0, 0))]),
        out_specs=pl.BlockSpec((DEC_SEQ, QW), lambda b, j, pt: (b, 0)),
        scratch_shapes=[pltpu.VMEM((DEC_ROWS, 1), jnp.float32),
                        pltpu.VMEM((DEC_ROWS, 1), jnp.float32),
                        pltpu.VMEM((DEC_ROWS, HEAD_DIM), jnp.float32)])
    return pl.pallas_call(
        functools.partial(_dec_sel_kernel, past_len=past_len),
        out_shape=jax.ShapeDtypeStruct((nb * DEC_SEQ, QW), jnp.float32),
        grid_spec=grid_spec,
        compiler_params=pltpu.CompilerParams(
            dimension_semantics=("parallel", "arbitrary"),
            vmem_limit_bytes=VMEM_LIMIT_BYTES),
        name="dec_sel_attn",
    )(page_table, *([pool_k] * PAGES_PER_STEP), *([pool_v] * PAGES_PER_STEP), tail_k, tail_v, z, bias)


def _dec_win_kernel(q_ref, wk_ref, wv_ref, kt_ref, vt_ref, o_ref, nwk_ref, nwv_ref, *, past_len):
    wl = wk_ref.shape[0] // N_KV_HEADS
    qall = _dec_rows(q_ref[...]).astype(jnp.bfloat16)
    row_g, row_t = _dec_row_ids()

    def cat(w_ref, t_ref, g):
        return jnp.concatenate([_head_rows(w_ref, g, wl), _head_rows(t_ref, g, PAGE_SIZE)],
                               axis=0).astype(jnp.bfloat16)

    s = _group_scores(qall, functools.partial(cat, wk_ref, kt_ref), row_g) * SCALE
    kpos = past_len - wl + lax.broadcasted_iota(jnp.int32, s.shape, 1)
    d = past_len + row_t - kpos
    valid = (d >= 0) & (d < WINDOW)
    s = jnp.where(valid, s, NEG)
    m = jnp.max(s, axis=-1, keepdims=True)
    e = jnp.where(valid, jnp.exp(s - m), 0.0)
    p = e / jnp.maximum(jnp.sum(e, axis=-1, keepdims=True), 1e-30)
    o_ref[...] = _dec_cols(_group_values(p, functools.partial(cat, wv_ref, vt_ref), row_g))
    keep = (wl - DEC_SEQ) * N_KV_HEADS
    for w_ref, t_ref, n_ref in ((wk_ref, kt_ref, nwk_ref), (wv_ref, vt_ref, nwv_ref)):
        n_ref[0:keep, :] = w_ref[wl * N_KV_HEADS - keep:wl * N_KV_HEADS, :]
        n_ref[keep:wl * N_KV_HEADS, :] = t_ref[0:DEC_SEQ * N_KV_HEADS, :]


def _dec_win_attn(z, win_k, win_v, tail_k, tail_v, nb, past_len):
    cache = pl.BlockSpec((win_k.shape[0] // nb, HEAD_DIM), lambda b: (b, 0))
    tail = pl.BlockSpec(PAGE_BLOCK, lambda b: (b, 0))
    rows = pl.BlockSpec((DEC_SEQ, QW), lambda b: (b, 0))
    return pl.pallas_call(
        functools.partial(_dec_win_kernel, past_len=past_len),
        out_shape=(jax.ShapeDtypeStruct((nb * DEC_SEQ, QW), jnp.float32),
                   jax.ShapeDtypeStruct(win_k.shape, jnp.float32),
                   jax.ShapeDtypeStruct(win_v.shape, jnp.float32)),
        grid=(nb,),
        in_specs=[rows, cache, cache, tail, tail],
        out_specs=(rows, cache, cache),
        compiler_params=pltpu.CompilerParams(
            dimension_semantics=("parallel",),
            vmem_limit_bytes=VMEM_LIMIT_BYTES),
        name="dec_win_attn",
    )(z, win_k, win_v, tail_k, tail_v)


def _compress_kernel(xk_ref, xv_ref, w1k_ref, w1v_ref, ck_ref, cv_ref, w2k_ref, w2v_ref,
                     ko_ref, vo_ref, h_sc):
    nch = xk_ref.shape[0] // CMP_STRIDE
    h_sc[nch:nch + 8, :] = jnp.zeros((8, 2 * HEAD_DIM), jnp.float32)
    for x_ref, w1_ref, c_ref, w2_ref, o_ref in ((xk_ref, w1k_ref, ck_ref, w2k_ref, ko_ref),
                                                (xv_ref, w1v_ref, cv_ref, w2v_ref, vo_ref)):
        acc = None
        for l in range(CMP_STRIDE):
            x = x_ref[pl.ds(l, nch, stride=CMP_STRIDE), :].astype(jnp.bfloat16)
            h = jnp.dot(x, w1_ref[l], preferred_element_type=jnp.float32)
            acc = h if acc is None else acc + h
        h_sc[0:nch, :] = acc
        hh = h_sc[0:nch, 0:HEAD_DIM] + h_sc[1:nch + 1, HEAD_DIM:2 * HEAD_DIM] + c_ref[...]
        o_ref[0, 0] = jnp.dot(jax.nn.gelu(hh).astype(jnp.bfloat16), w2_ref[...],
                              preferred_element_type=jnp.float32)


def _compress_seq(z, w1k, w1v, ck, cv, w2k, w2v, B, T):
    nch = T // CMP_STRIDE
    ck0 = COL_KC // HEAD_DIM
    cv0 = COL_VC // HEAD_DIM
    full2 = lambda a: pl.BlockSpec(a.shape, lambda b, g: (0, 0))
    full3 = lambda a: pl.BlockSpec(a.shape, lambda b, g: (0, 0, 0))
    out = jax.ShapeDtypeStruct((B, N_KV_HEADS, nch, HEAD_DIM), jnp.float32)
    ospec = pl.BlockSpec((1, 1, nch, HEAD_DIM), lambda b, g: (b, g, 0, 0))
    return pl.pallas_call(
        _compress_kernel,
        out_shape=(out, out),
        grid=(B, N_KV_HEADS),
        in_specs=[pl.BlockSpec((T, HEAD_DIM), lambda b, g: (b, ck0 + g)),
                  pl.BlockSpec((T, HEAD_DIM), lambda b, g: (b, cv0 + g)),
                  full3(w1k), full3(w1v), full2(ck), full2(cv), full2(w2k), full2(w2v)],
        out_specs=(ospec, ospec),
        scratch_shapes=[pltpu.VMEM((nch + 8, 2 * HEAD_DIM), jnp.float32)],
        compiler_params=pltpu.CompilerParams(
            dimension_semantics=("parallel", "parallel"),
            vmem_limit_bytes=VMEM_LIMIT_BYTES),
        name="compress_seq",
    )(z, z, w1k, w1v, ck, cv, w2k, w2v)


CONV_HALO = 32
LANES = 128


def _conv_taps(xs, w_ref, first, rows, y_sc):
    for cb in range(CONV_CH // LANES):
        sl = slice(cb * LANES, (cb + 1) * LANES)
        acc = None
        for k in range(CONV_WIDTH):
            term = w_ref[k:k + 1, sl] * xs[first + k:first + k + rows, sl]
            acc = term if acc is None else acc + term
        y_sc[:, sl] = acc


def _conv_post(y, cb_ref, lg_ref, lb_ref):
    y = y + cb_ref[...]
    mu = jnp.mean(y, axis=-1, keepdims=True)
    var = jnp.mean(jnp.square(y - mu), axis=-1, keepdims=True)
    yn = (y - mu) * lax.rsqrt(var + EPS) * lg_ref[...] + lb_ref[...]
    return yn * jax.nn.sigmoid(yn)


def _conv_kernel(a_ref, b_ref, ap_ref, bp_ref, w_ref, cb_ref, lg_ref, lb_ref, y_ref, tail_ref,
                 xs, y_sc, *, tm):
    i = pl.program_id(1)
    prev = ap_ref[...] * jax.nn.sigmoid(bp_ref[...])
    xs[0:CONV_HALO, :] = jnp.where(i == 0, 0.0, prev)
    xs[CONV_HALO:CONV_HALO + tm, :] = a_ref[...] * jax.nn.sigmoid(b_ref[...])
    _conv_taps(xs, w_ref, CONV_HALO - (CONV_WIDTH - 1), tm, y_sc)
    y_ref[...] = _conv_post(y_sc[...], cb_ref, lg_ref, lb_ref).astype(y_ref.dtype)

    @pl.when(i == pl.num_programs(1) - 1)
    def _():
        tail_ref[0] = xs[tm:tm + CONV_HALO, :]


def _conv_seq(z, w, cb, lg, lb, B, T, tm):
    nt = T // tm
    ca = COL_GLU_A // CONV_CH
    cbk = COL_GLU_B // CONV_CH
    r = tm // CONV_HALO
    prev = lambda c: pl.BlockSpec((CONV_HALO, CONV_CH),
                                  lambda b, i: (jnp.maximum((b * nt + i) * r - 1, 0), c))
    cur = lambda c: pl.BlockSpec((tm, CONV_CH), lambda b, i: (b * nt + i, c))
    vec = pl.BlockSpec((1, CONV_CH), lambda b, i: (0, 0))
    return pl.pallas_call(
        functools.partial(_conv_kernel, tm=tm),
        out_shape=(jax.ShapeDtypeStruct((B * T, CONV_CH), jnp.bfloat16),
                   jax.ShapeDtypeStruct((B, CONV_HALO, CONV_CH), jnp.float32)),
        grid=(B, nt),
        in_specs=[cur(ca), cur(cbk), prev(ca), prev(cbk),
                  pl.BlockSpec(w.shape, lambda b, i: (0, 0)), vec, vec, vec],
        out_specs=(pl.BlockSpec((tm, CONV_CH), lambda b, i: (b * nt + i, 0)),
                   pl.BlockSpec((1, CONV_HALO, CONV_CH), lambda b, i: (b, 0, 0))),
        scratch_shapes=[pltpu.VMEM((tm + CONV_HALO, CONV_CH), jnp.float32),
                        pltpu.VMEM((tm, CONV_CH), jnp.float32)],
        compiler_params=pltpu.CompilerParams(
            dimension_semantics=("parallel", "arbitrary"),
            vmem_limit_bytes=VMEM_LIMIT_BYTES),
        name="conv_seq",
    )(z, z, z, z, w, cb.reshape(1, -1), lg.reshape(1, -1), lb.reshape(1, -1))


def _dec_conv_kernel(a_ref, b_ref, st_ref, w_ref, cb_ref, lg_ref, lb_ref, y_ref, ns_ref, xs, y_sc):
    nbuf = CONV_WIDTH - 1
    xs[0:nbuf, :] = st_ref[0]
    xs[nbuf:nbuf + DEC_SEQ, :] = a_ref[...] * jax.nn.sigmoid(b_ref[...])
    _conv_taps(xs, w_ref, 0, DEC_SEQ, y_sc)
    y_ref[...] = _conv_post(y_sc[...], cb_ref, lg_ref, lb_ref)
    ns_ref[0] = xs[DEC_SEQ:DEC_SEQ + nbuf, :]


def _dec_conv(z, state, w, cb, lg, lb):
    nb, nbuf, _ = state.shape
    ca = COL_GLU_A // CONV_CH
    cbk = COL_GLU_B // CONV_CH
    vec = pl.BlockSpec((1, CONV_CH), lambda b: (0, 0))
    st = pl.BlockSpec((1, nbuf, CONV_CH), lambda b: (b, 0, 0))
    return pl.pallas_call(
        _dec_conv_kernel,
        out_shape=(jax.ShapeDtypeStruct((nb * DEC_SEQ, CONV_CH), jnp.float32),
                   jax.ShapeDtypeStruct(state.shape, jnp.float32)),
        grid=(nb,),
        in_specs=[pl.BlockSpec((DEC_SEQ, CONV_CH), lambda b: (b, ca)),
                  pl.BlockSpec((DEC_SEQ, CONV_CH), lambda b: (b, cbk)),
                  st, pl.BlockSpec(w.shape, lambda b: (0, 0)), vec, vec, vec],
        out_specs=(pl.BlockSpec((DEC_SEQ, CONV_CH), lambda b: (b, 0)), st),
        scratch_shapes=[pltpu.VMEM((nbuf + DEC_SEQ + 2, CONV_CH), jnp.float32),
                        pltpu.VMEM((DEC_SEQ, CONV_CH), jnp.float32)],
        compiler_params=pltpu.CompilerParams(
            dimension_semantics=("parallel",),
            vmem_limit_bytes=VMEM_LIMIT_BYTES),
        name="dec_conv",
    )(z, z, state, w, cb.reshape(1, -1), lg.reshape(1, -1), lb.reshape(1, -1))


def _merge_kernel(oc_ref, os_ref, ow_ref, g_ref, convy_ref, ga_ref, gb_ref, wn_ref, wc_ref,
                  o_ref, onsa_ref):
    @pl.when(pl.program_id(1) == 0)
    def _():
        gz = jax.nn.sigmoid(g_ref[...])
        for h in range(N_HEADS):
            sl = slice(h * HEAD_DIM, (h + 1) * HEAD_DIM)
            o = (gz[:, 3 * h:3 * h + 1] * oc_ref[:, sl]
                 + gz[:, 3 * h + 1:3 * h + 2] * os_ref[:, sl]
                 + gz[:, 3 * h + 2:3 * h + 3] * ow_ref[:, sl])
            onsa_ref[:, sl] = o.astype(jnp.bfloat16)

    a = jnp.dot(onsa_ref[...], wn_ref[...], preferred_element_type=jnp.float32)
    b = jnp.dot(convy_ref[...].astype(jnp.bfloat16), wc_ref[...],
                preferred_element_type=jnp.float32)
    o_ref[...] = (jax.nn.sigmoid(ga_ref[...]) * a
                  + jax.nn.sigmoid(gb_ref[...]) * b).astype(o_ref.dtype)


def _merge(o_cmp, o_sel, o_win, conv_y, z, wn_bf16, wc_bf16, tm, tn):
    n = o_cmp.shape[0]
    ja = COL_GM_A // tn
    jb = COL_GM_B // tn
    jg = COL_G // GATE_PAD
    row = lambda i, j: (i, 0)
    return pl.pallas_call(
        _merge_kernel,
        out_shape=jax.ShapeDtypeStruct((n, D_MODEL), jnp.bfloat16),
        grid=(n // tm, D_MODEL // tn),
        in_specs=[pl.BlockSpec((tm, QW), row),
                  pl.BlockSpec((tm, QW), row),
                  pl.BlockSpec((tm, QW), row),
                  pl.BlockSpec((tm, GATE_PAD), lambda i, j: (i, jg)),
                  pl.BlockSpec((tm, CONV_CH), row),
                  pl.BlockSpec((tm, tn), lambda i, j: (i, ja + j)),
                  pl.BlockSpec((tm, tn), lambda i, j: (i, jb + j)),
                  pl.BlockSpec((QW, tn), lambda i, j: (0, j)),
                  pl.BlockSpec((CONV_CH, tn), lambda i, j: (0, j))],
        out_specs=pl.BlockSpec((tm, tn), lambda i, j: (i, j)),
        scratch_shapes=[pltpu.VMEM((tm, QW), jnp.bfloat16)],
        compiler_params=pltpu.CompilerParams(
            dimension_semantics=("parallel", "arbitrary"),
            vmem_limit_bytes=VMEM_LIMIT_BYTES),
        name="merge",
    )(o_cmp, o_sel, o_win, z, conv_y, z, z, wn_bf16, wc_bf16)


def _out_proj_kernel(m_ref, x_ref, w_ref, g_ref, h_ref, hn_ref):
    h = x_ref[...] + jnp.dot(m_ref[...], w_ref[...], preferred_element_type=jnp.float32)
    h_ref[...] = h
    ms = jnp.mean(h * h, axis=-1, keepdims=True)
    hn_ref[...] = (h * lax.rsqrt(ms + EPS) * g_ref[...]).astype(hn_ref.dtype)


def _out_proj(merged, x, w_bf16, gain, tm):
    n = x.shape[0]
    return pl.pallas_call(
        _out_proj_kernel,
        out_shape=(jax.ShapeDtypeStruct((n, D_MODEL), jnp.float32),
                   jax.ShapeDtypeStruct((n, D_MODEL), jnp.bfloat16)),
        grid=(n // tm,),
        in_specs=[pl.BlockSpec((tm, D_MODEL), lambda i: (i, 0)),
                  pl.BlockSpec((tm, D_MODEL), lambda i: (i, 0)),
                  pl.BlockSpec((D_MODEL, D_MODEL), lambda i: (0, 0)),
                  pl.BlockSpec((1, D_MODEL), lambda i: (0, 0))],
        out_specs=(pl.BlockSpec((tm, D_MODEL), lambda i: (i, 0)),
                   pl.BlockSpec((tm, D_MODEL), lambda i: (i, 0))),
        compiler_params=pltpu.CompilerParams(
            dimension_semantics=("parallel",),
            vmem_limit_bytes=VMEM_LIMIT_BYTES),
        name="out_proj",
    )(merged, x, w_bf16, gain.reshape(1, D_MODEL))


PEER_HALF = PEER_DKEY // 2
PEER_SEL = PEER_HEADS * PEER_TOPK
PEER_TOPK_SHIFT = PEER_TOPK.bit_length() - 1
CAND_B = PEER_TOPK // 2
CAND_B_SHIFT = CAND_B.bit_length() - 1
N_CAND = PEER_TOPK + (PEER_TOPK - 1) * CAND_B
GATE_HALF = PEER_NKEYS // 2
GATE_PITCH = GATE_HALF + 8


def _topk_chains(s_sc, v_sc, i_sc, k):
    n_chain, r, n = s_sc.shape
    row = lax.broadcasted_iota(jnp.int32, (r, n), 0)

    def body(i, carry):
        for c in range(n_chain):
            s = s_sc[c]
            m = jnp.max(s, axis=0, keepdims=True)
            j = jnp.min(jnp.where(s == m, row, r), axis=0, keepdims=True)
            s_sc[c] = jnp.where(row == j, -jnp.inf, s)
            v_sc[c, pl.ds(i, 1), :] = m
            i_sc[c, pl.ds(i, 1), :] = j
        return carry

    lax.fori_loop(0, k, body, 0)


def _pick_rows(idx, table):
    out = jnp.zeros(idx.shape, table.dtype)
    for a in range(PEER_TOPK):
        out = jnp.where(idx == a, table[a:a + 1, :], out)
    return out


def _peer_route_kernel(hn_ref, wq_ref, k1_ref, k2_ref, ia_ref, ib_ref, gt_ref,
                       s_sc, v_sc, i_sc, c_sc, cv_sc, ci_sc, a_sc, b_sc, g_sc):
    qh = jnp.dot(hn_ref[...], wq_ref[...], preferred_element_type=jnp.float32).astype(jnp.bfloat16)
    for h in range(PEER_HEADS):
        q1 = qh[:, h * PEER_DKEY:h * PEER_DKEY + PEER_HALF]
        q2 = qh[:, h * PEER_DKEY + PEER_HALF:(h + 1) * PEER_DKEY]
        s_sc[2 * h] = lax.dot_general(k1_ref[h], q1, _NT, preferred_element_type=jnp.float32)
        s_sc[2 * h + 1] = lax.dot_general(k2_ref[h], q2, _NT, preferred_element_type=jnp.float32)
    _topk_chains(s_sc, v_sc, i_sc, PEER_TOPK)
    for h in range(PEER_HEADS):
        v1, v2 = v_sc[2 * h], v_sc[2 * h + 1]
        c_sc[h] = jnp.concatenate([v1[0:1, :] + v2]
                                  + [v1[a:a + 1, :] + v2[0:CAND_B, :] for a in range(1, PEER_TOPK)],
                                  axis=0)
    _topk_chains(c_sc, cv_sc, ci_sc, PEER_TOPK)
    for h in range(PEER_HEADS):
        sc, pos = cv_sc[h], ci_sc[h]
        rest = pos - PEER_TOPK
        first = pos < PEER_TOPK
        ia = _pick_rows(jnp.where(first, 0, 1 + (rest >> CAND_B_SHIFT)), i_sc[2 * h])
        ib = _pick_rows(jnp.where(first, pos, rest & (CAND_B - 1)), i_sc[2 * h + 1])
        e = jnp.exp(sc - sc[0:1, :])
        gate = e / jnp.sum(e, axis=0, keepdims=True)
        rows = slice(h * PEER_TOPK, (h + 1) * PEER_TOPK)
        a_sc[rows, :] = ia.astype(jnp.float32)
        b_sc[rows, :] = ib.astype(jnp.float32)
        g_sc[rows, :] = gate
    ia_ref[...] = a_sc[...].T
    ib_ref[...] = b_sc[...].T
    gt_ref[...] = g_sc[...].T


def _peer_route(hn, wq_bf16, k1_bf16, k2_bf16, tm):
    n, d = hn.shape
    out = jax.ShapeDtypeStruct((n, PEER_SEL), jnp.float32)
    ospec = pl.BlockSpec((tm, PEER_SEL), lambda i: (i, 0))
    return pl.pallas_call(
        _peer_route_kernel,
        out_shape=(out, out, out),
        grid=(n // tm,),
        in_specs=[pl.BlockSpec((tm, d), lambda i: (i, 0)),
                  pl.BlockSpec(wq_bf16.shape, lambda i: (0, 0)),
                  pl.BlockSpec(k1_bf16.shape, lambda i: (0, 0, 0)),
                  pl.BlockSpec(k2_bf16.shape, lambda i: (0, 0, 0))],
        out_specs=(ospec, ospec, ospec),
        scratch_shapes=[pltpu.VMEM((2 * PEER_HEADS, PEER_NKEYS, tm), jnp.float32),
                        pltpu.VMEM((2 * PEER_HEADS, PEER_TOPK, tm), jnp.float32),
                        pltpu.VMEM((2 * PEER_HEADS, PEER_TOPK, tm), jnp.int32),
                        pltpu.VMEM((PEER_HEADS, N_CAND, tm), jnp.float32),
                        pltpu.VMEM((PEER_HEADS, PEER_TOPK, tm), jnp.float32),
                        pltpu.VMEM((PEER_HEADS, PEER_TOPK, tm), jnp.int32)]
        + [pltpu.VMEM((PEER_SEL, tm), jnp.float32)] * 3,
        compiler_params=pltpu.CompilerParams(
            dimension_semantics=("parallel",),
            vmem_limit_bytes=VMEM_LIMIT_BYTES),
        name="peer_route",
    )(hn, wq_bf16, k1_bf16, k2_bf16)


def _peer_ffn_kernel(hn_ref, h_ref, ia_ref, ib_ref, gt_ref, u_ref, v_ref, gf_ref, y_ref,
                     w_sc, acc_ref, *, tm, n_i1):
    c = pl.program_id(1)

    @pl.when(c == 0)
    def _():
        acc_ref[...] = jnp.zeros(acc_ref.shape, jnp.float32)

    steps_per_half = GATE_HALF // n_i1

    @pl.when(c % steps_per_half == 0)
    def _():
        first = (c // steps_per_half) * GATE_HALF
        rows_a = (first + lax.broadcasted_iota(jnp.int32, (GATE_HALF, PEER_SEL), 0)).astype(jnp.float32)
        rows_b = lax.broadcasted_iota(jnp.int32, (PEER_NKEYS, PEER_SEL), 0).astype(jnp.float32)

        def token(n, carry):
            a_row = ia_ref[pl.ds(n, 1), :]
            b_row = ib_ref[pl.ds(n, 1), :]
            g_row = gt_ref[pl.ds(n, 1), :]
            g_hi = g_row.astype(jnp.bfloat16).astype(jnp.float32)
            g_lo = g_row - g_hi
            oa = jnp.where(rows_a == a_row, 1.0, 0.0).astype(jnp.bfloat16)
            hit_b = rows_b == b_row
            gb_hi = jnp.where(hit_b, g_hi, 0.0).astype(jnp.bfloat16)
            gb_lo = jnp.where(hit_b, g_lo, 0.0).astype(jnp.bfloat16)
            w = lax.dot_general(jnp.concatenate([oa, oa], axis=1),
                                jnp.concatenate([gb_hi, gb_lo], axis=1), _NT,
                                preferred_element_type=jnp.float32)
            w_sc[pl.ds(pl.multiple_of(n * GATE_PITCH, 8), GATE_HALF), :] = w
            return carry

        lax.fori_loop(0, tm, token, 0, unroll=8)

    act = jax.nn.gelu(lax.dot_general(hn_ref[...], u_ref[...], _NT,
                                      preferred_element_type=jnp.float32))
    row0 = (c * n_i1) & (GATE_HALF - 1)
    parts = []
    for j in range(n_i1):
        wj = w_sc[pl.ds(row0 + j, tm, stride=GATE_PITCH), :]
        parts.append((act[:, j * PEER_NKEYS:(j + 1) * PEER_NKEYS] * wj).astype(jnp.bfloat16))
    acc_ref[...] += jnp.dot(jnp.concatenate(parts, axis=1), v_ref[...],
                            preferred_element_type=jnp.float32)

    @pl.when(c == pl.num_programs(1) - 1)
    def _():
        y = h_ref[...] + acc_ref[...]
        ms = jnp.mean(y * y, axis=-1, keepdims=True)
        y_ref[...] = y * lax.rsqrt(ms + EPS) * gf_ref[...]


def _peer_ffn(hn, h, ia, ib, gt, u_bf16, v_bf16, gain_final, tm, n_i1):
    n, d = hn.shape
    ec = n_i1 * PEER_NKEYS
    assert GATE_HALF % n_i1 == 0
    row = lambda i, c: (i, 0)
    return pl.pallas_call(
        functools.partial(_peer_ffn_kernel, tm=tm, n_i1=n_i1),
        out_shape=jax.ShapeDtypeStruct((n, d), jnp.float32),
        grid=(n // tm, u_bf16.shape[0] // ec),
        in_specs=[pl.BlockSpec((tm, d), row),
                  pl.BlockSpec((tm, d), row),
                  pl.BlockSpec((tm, PEER_SEL), row),
                  pl.BlockSpec((tm, PEER_SEL), row),
                  pl.BlockSpec((tm, PEER_SEL), row),
                  pl.BlockSpec((ec, d), lambda i, c: (c, 0)),
                  pl.BlockSpec((ec, d), lambda i, c: (c, 0)),
                  pl.BlockSpec((1, d), lambda i, c: (0, 0))],
        out_specs=pl.BlockSpec((tm, d), row),
        scratch_shapes=[pltpu.VMEM((tm * GATE_PITCH, PEER_NKEYS), jnp.float32),
                        pltpu.VMEM((tm, d), jnp.float32)],
        compiler_params=pltpu.CompilerParams(
            dimension_semantics=("parallel", "arbitrary"),
            vmem_limit_bytes=VMEM_LIMIT_PEER_BYTES),
        name="peer_ffn",
    )(hn, h, ia, ib, gt, u_bf16, v_bf16, gain_final.reshape(1, d))


def _rmsnorm(x, g):
    xf = x.astype(jnp.float32)
    y = xf * lax.rsqrt(jnp.mean(xf * xf, axis=-1, keepdims=True) + EPS)
    return (y * g.astype(jnp.float32)).astype(x.dtype)


def _layernorm(x, g, b):
    xf = x.astype(jnp.float32)
    mu = jnp.mean(xf, axis=-1, keepdims=True)
    var = jnp.mean(jnp.square(xf - mu), axis=-1, keepdims=True)
    return ((xf - mu) * lax.rsqrt(var + EPS) * g.astype(jnp.float32) + b.astype(jnp.float32)).astype(x.dtype)


def _masked_probs(s, mask):
    s = jnp.where(mask, s, NEG)
    m = jnp.max(s, axis=-1, keepdims=True)
    e = jnp.where(mask, jnp.exp(s - m), 0.0)
    return e / jnp.maximum(jnp.sum(e, axis=-1, keepdims=True), 1e-30)


def _attn_probs(q, k, mask):
    s = jnp.einsum('...tgrd,...kgd->...grtk', q, k).astype(jnp.float32) * SCALE
    return _masked_probs(s, mask[..., None, None, :, :])


def _attn_out(p, v):
    return jnp.einsum('...grtk,...kgd->...tgrd', p.astype(v.dtype), v)


def _gather_pages(pool, page_table):
    g = pool[page_table]
    return g.reshape(page_table.shape[0], -1, pool.shape[2], pool.shape[3])


def _compress(kv, pe, w1, w2):
    B, L, G, D = kv.shape
    ch = kv.reshape(B, L // CMP_STRIDE, CMP_STRIDE, G, D)
    h_lo = jnp.einsum('bjlgd,ldh->bjgh', ch, w1[:CMP_STRIDE])
    h_hi = jnp.einsum('bjlgd,ldh->bjgh', ch, w1[CMP_STRIDE:])
    h = h_lo[:, :-1] + h_hi[:, 1:] + jnp.einsum('ld,ldh->h', pe, w1)
    return jnp.einsum('bigh,he->bige', jax.nn.gelu(h), w2)


def _cmp_to_sel(n_cmp, n_sel):
    cs = jnp.arange(n_cmp)[:, None] * CMP_STRIDE
    ss = jnp.arange(n_sel)[None, :] * SEL_LEN
    ov = jnp.clip(jnp.minimum(cs + CMP_LEN, ss + SEL_LEN) - jnp.maximum(cs, ss), 0, None)
    return ov.astype(jnp.float32) / CMP_LEN


def _sel_attend(q, kb, vb, idx, valid, qpos):
    B, T, G, R, D = q.shape
    bi = jnp.arange(B)[:, None, None, None]
    gi = jnp.arange(G)[None, :, None, None]
    kg = kb[bi, idx, :, gi].reshape(B, G, T, -1, D)
    vg = vb[bi, idx, :, gi].reshape(B, G, T, -1, D)
    kpos = idx[..., None] * SEL_LEN + jnp.arange(SEL_LEN)
    mask = (valid[..., None] & (kpos <= qpos[:, None, None])).reshape(B, G, T, -1)
    s = jnp.einsum('btgrd,bgtkd->bgrtk', q, kg).astype(jnp.float32) * SCALE
    p = _masked_probs(s, mask[:, :, None])
    return jnp.einsum('bgrtk,bgtkd->btgrd', p.astype(vg.dtype), vg)


def _nsa_cmp_sel(q, k_c, v_c, k_s, v_s, qpos, lp, sweep_queries):
    B, T, G, R, D = q.shape
    L = k_c.shape[1]
    L_pad = -(-L // SEL_LEN) * SEL_LEN
    pad = ((0, 0), (0, L_pad - L), (0, 0), (0, 0))
    k_c, v_c, k_s, v_s = jnp.pad(k_c, pad), jnp.pad(v_c, pad), jnp.pad(k_s, pad), jnp.pad(v_s, pad)
    k_cmp = _compress(k_c, lp['cmp_pe_k'], lp['cmp_w1_k'], lp['cmp_w2_k'])
    v_cmp = _compress(v_c, lp['cmp_pe_v'], lp['cmp_w1_v'], lp['cmp_w2_v'])
    n_cmp = k_cmp.shape[1]
    cmp_end = jnp.arange(n_cmp) * CMP_STRIDE + CMP_LEN - 1
    p_cmp = _attn_probs(q, k_cmp, cmp_end[None, :] <= qpos[:, None])
    o_cmp = _attn_out(p_cmp, v_cmp)
    n_sel = L_pad // SEL_LEN
    p_slc = jnp.einsum('bgrti,ij->bgtj', p_cmp, _cmp_to_sel(n_cmp, n_sel))
    blk = jnp.arange(n_sel)[None, :]
    cur = (qpos // SEL_LEN)[:, None]
    forced = (blk == 0) | (blk == cur) | (blk == cur - 1)
    score = jnp.where(blk <= cur, p_slc + jnp.where(forced, FORCE_BONUS, 0.0), NEG)
    top_s, top_i = lax.top_k(score, min(SEL_TOPK, n_sel))
    valid = top_s > 0.5 * NEG
    kb = k_s.reshape(B, n_sel, SEL_LEN, G, D)
    vb = v_s.reshape(B, n_sel, SEL_LEN, G, D)
    if sweep_queries:
        nq = T // SEL_Q_BLOCK
        xs = (q.reshape(B, nq, SEL_Q_BLOCK, G, R, D).swapaxes(0, 1),
              top_i.reshape(B, G, nq, SEL_Q_BLOCK, -1).transpose(2, 0, 1, 3, 4),
              valid.reshape(B, G, nq, SEL_Q_BLOCK, -1).transpose(2, 0, 1, 3, 4),
              qpos.reshape(nq, SEL_Q_BLOCK))
        o = lax.map(lambda a: _sel_attend(a[0], kb, vb, a[1], a[2], a[3]), xs)
        o_sel = o.swapaxes(0, 1).reshape(B, T, G, R, D)
    else:
        xs = (q[:, None], kb[:, None], vb[:, None], top_i[:, None], valid[:, None])
        o_sel = lax.map(lambda a: _sel_attend(a[0], a[1], a[2], a[3], a[4], qpos)[0], xs)
    return o_cmp, o_sel


def _window_banded(q, k, v):
    B, T, G, R, D = q.shape
    nb = T // WIN_Q_BLOCK
    pad = ((0, 0), (WINDOW, 0), (0, 0), (0, 0))
    kidx = jnp.arange(nb)[:, None] * WIN_Q_BLOCK + jnp.arange(WINDOW + WIN_Q_BLOCK)[None, :]
    kblk = jnp.pad(k, pad)[:, kidx]
    vblk = jnp.pad(v, pad)[:, kidx]
    kpos = kidx - WINDOW
    qpos = jnp.arange(T).reshape(nb, WIN_Q_BLOCK)
    d = qpos[:, :, None] - kpos[:, None, :]
    mask = (kpos[:, None, :] >= 0) & (d >= 0) & (d < WINDOW)
    o = _attn_out(_attn_probs(q.reshape(B, nb, WIN_Q_BLOCK, G, R, D), kblk, mask), vblk)
    return o.reshape(B, T, G, R, D)


def _window_dense(q, k, v, qpos, kpos):
    d = qpos[:, None] - kpos[None, :]
    return _attn_out(_attn_probs(q, k, (d >= 0) & (d < WINDOW)), v)


def _conv_module(u, buf, w_dw, b_dw, ln_g, ln_b):
    xp = jnp.concatenate([buf, u], axis=1)
    y = lax.conv_general_dilated(xp, w_dw[:, None, :], (1,), 'VALID',
                                 dimension_numbers=('NWC', 'WIO', 'NWC'),
                                 feature_group_count=CONV_CH) + b_dw
    return jax.nn.silu(_layernorm(y, ln_g, ln_b)), xp[:, -(CONV_WIDTH - 1):]


def _split_z(z, B, T):
    kv = lambda c: z[:, c:c + KVW].reshape(B, T, N_KV_HEADS, HEAD_DIM)
    q = z[:, COL_Q:COL_Q + QW].reshape(B, T, N_KV_HEADS, GROUP, HEAD_DIM)
    a = z[:, COL_GLU_A:COL_GLU_A + CONV_CH]
    b = z[:, COL_GLU_B:COL_GLU_B + CONV_CH]
    u = (a * jax.nn.sigmoid(b)).reshape(B, T, CONV_CH)
    return q, kv(COL_KC), kv(COL_VC), kv(COL_KS), kv(COL_VS), kv(COL_KW), kv(COL_VW), u


def _cmp_to_sel_t(n_cmp, n_sel, n_cmp_pad, n_sel_pad=None):
    cs = np.arange(n_cmp)[None, :] * CMP_STRIDE
    ss = np.arange(n_sel)[:, None] * SEL_LEN
    ov = np.clip(np.minimum(cs + CMP_LEN, ss + SEL_LEN) - np.maximum(cs, ss), 0, None)
    ct = np.zeros((n_sel_pad or n_sel, n_cmp_pad), np.float32)
    ct[:n_sel, :n_cmp] = ov.astype(np.float32) / CMP_LEN
    return jnp.asarray(ct, jnp.bfloat16)


def _compress_params(pe, w1, w2):
    w1cat = jnp.concatenate([w1[:CMP_STRIDE], w1[CMP_STRIDE:]], axis=-1).astype(jnp.bfloat16)
    c = jnp.einsum('ld,ldh->h', pe, w1).reshape(1, -1)
    return w1cat, c, w2.astype(jnp.bfloat16)


def _tail_page(z, col, nb):
    x = z[:, col:col + KVW].reshape(nb, DEC_SEQ * N_KV_HEADS, HEAD_DIM)
    x = jnp.pad(x, ((0, 0), (0, PAGE_ROWS - DEC_SEQ * N_KV_HEADS), (0, 0)))
    return x.reshape(nb * PAGE_ROWS, HEAD_DIM)


def _pad_cmp(x_cmp, n_pad):
    x = jnp.transpose(x_cmp, (0, 2, 1, 3))
    return jnp.pad(x, ((0, 0), (0, 0), (0, n_pad - x.shape[2]), (0, 0)))


def _layer(x, lp, past, page_table, prompt, tm):
    B, T, _ = x.shape
    n = B * T
    x2 = x.reshape(n, D_MODEL)
    z = _proj_in(x2, lp['norm_mix'], lp['w_in_p'], tm, 768)
    kv = lambda c: z[:, c:c + KVW].reshape(B, T, N_KV_HEADS, HEAD_DIM)
    k_c, v_c, k_s, v_s = kv(COL_KC), kv(COL_VC), kv(COL_KS), kv(COL_VS)
    cmp_k, cmp_v = lp['cmp_k'], lp['cmp_v']
    conv = (lp['conv_w'], lp['conv_b'], lp['conv_ln_g'], lp['conv_ln_b'])
    if prompt:
        n_ch = T // CMP_STRIDE
        ct = _cmp_to_sel_t(n_ch - 1, T // SEL_LEN, n_ch)
        kcmp, vcmp = _compress_seq(z, cmp_k[0], cmp_v[0], cmp_k[1], cmp_v[1], cmp_k[2], cmp_v[2], B, T)
        o_cmp, bias = _cmp_select(z, kcmp, vcmp, ct, B, T, 128)
        o_sel = _sel_attn(z, bias, B, T, 256, 256)
        o_win = _win_attn(z, B, T, 128)
        wl = min(WINDOW, T)
        new_wk, new_wv = kv(COL_KW)[:, T - wl:], kv(COL_VW)[:, T - wl:]
        conv_y, u_tail = _conv_seq(z, *conv, B, T, 256)
        new_conv = u_tail[:, CONV_HALO - (CONV_WIDTH - 1):]
    else:
        as_rows = lambda a: a.reshape(-1, HEAD_DIM)
        n_ch = -(-(PAST_LEN + T) // SEL_LEN) * SEL_LEN // CMP_STRIDE
        n_sel = -(-(PAST_LEN + T) // SEL_LEN)
        ct = _cmp_to_sel_t(n_ch - 1, n_sel, -(-n_ch // LANES) * LANES, -(-n_sel // 8) * 8)
        o_cmp, bias = _dec_cmp_select(
            page_table, as_rows(past['cmp_k']), as_rows(past['cmp_v']),
            _tail_page(z, COL_KC, B), _tail_page(z, COL_VC, B), z,
            cmp_k[0], cmp_v[0], cmp_k[1], cmp_v[1], cmp_k[2], cmp_v[2], ct, PAST_LEN)
        o_sel = _dec_sel_attn(page_table, as_rows(past['sel_k']), as_rows(past['sel_v']),
                              _tail_page(z, COL_KS, B), _tail_page(z, COL_VS, B), z, bias, PAST_LEN)
        wl = past['win_k'].shape[1]
        o_win, new_wk, new_wv = _dec_win_attn(
            z, as_rows(past['win_k']), as_rows(past['win_v']),
            _tail_page(z, COL_KW, B), _tail_page(z, COL_VW, B), B, PAST_LEN)
        new_wk = new_wk.reshape(B, wl, N_KV_HEADS, HEAD_DIM)
        new_wv = new_wv.reshape(B, wl, N_KV_HEADS, HEAD_DIM)
        conv_y, new_conv = _dec_conv(z, past['conv'], *conv)
    merged = _merge(o_cmp, o_sel, o_win, conv_y, z,
                    lp['w_nsa_out_b'], lp['w_conv_out_b'], 256, 512)
    h, hn = _out_proj(merged, x2, lp['w_out_b'], lp['norm_ffn'], 256)
    ia, ib, gt = _peer_route(hn, lp['peer_wq_b'], lp['peer_k1_b'], lp['peer_k2_b'], 256)
    y = _peer_ffn(hn, h, ia, ib, gt, lp['peer_u_b'], lp['peer_v_b'], lp['norm_final'], min(n, 512), 4)
    return y.reshape(B, T, D_MODEL), (k_c, v_c, k_s, v_s, new_wk, new_wv, new_conv)


def kernel(x_prompt, x_sample, cache_cmp_k, cache_cmp_v, cache_sel_k, cache_sel_v, cache_win_k, cache_win_v, state_conv, page_table, norm_mix, w_in, cmp_pe_k, cmp_w1_k, cmp_w2_k, cmp_pe_v, cmp_w1_v, cmp_w2_v, w_nsa_out, conv_w, conv_b, conv_ln_g, conv_ln_b, w_conv_out, w_out, norm_ffn, peer_wq, peer_k1, peer_k2, peer_u, peer_v, norm_final):
    assert DEPTH == 1
    bf = lambda w: w.astype(jnp.bfloat16)
    l = 0
    lp = {'norm_mix': norm_mix[l], 'w_in_p': _permute_w_in(w_in[l]),
          'cmp_k': _compress_params(cmp_pe_k[l], cmp_w1_k[l], cmp_w2_k[l]),
          'cmp_v': _compress_params(cmp_pe_v[l], cmp_w1_v[l], cmp_w2_v[l]),
          'w_nsa_out_b': bf(w_nsa_out[l]), 'conv_w': conv_w[l], 'conv_b': conv_b[l],
          'conv_ln_g': conv_ln_g[l], 'conv_ln_b': conv_ln_b[l],
          'w_conv_out_b': bf(w_conv_out[l]), 'w_out_b': bf(w_out[l]), 'norm_ffn': norm_ffn[l],
          'peer_wq_b': bf(peer_wq[l]), 'peer_k1_b': bf(peer_k1[l]), 'peer_k2_b': bf(peer_k2[l]),
          'peer_u_b': bf(peer_u[l]), 'peer_v_b': bf(peer_v[l]), 'norm_final': norm_final}
    past = {'cmp_k': cache_cmp_k[l], 'cmp_v': cache_cmp_v[l], 'sel_k': cache_sel_k[l],
            'sel_v': cache_sel_v[l], 'win_k': cache_win_k[l], 'win_v': cache_win_v[l],
            'conv': state_conv[l]}
    y_prompt, st_p = _layer(x_prompt, lp, None, None, True, 1024)
    y_sample, st_s = _layer(x_sample, lp, past, page_table, False, 256)
    p_ck, p_cv, p_sk, p_sv, p_wk, p_wv, p_conv = [a[None] for a in st_p]
    s_ck, s_cv, s_sk, s_sv, s_wk, s_wv, s_conv = [a[None] for a in st_s]
    return (y_prompt, y_sample, p_ck, s_ck, p_cv, s_cv, p_sk, s_sk, p_sv, s_sv,
            p_wk, s_wk, p_wv, s_wv, p_conv, s_conv)
```

```python
import functools

import jax
import jax.numpy as jnp
import numpy as np
from jax import lax
from jax.experimental import pallas as pl
from jax.experimental.pallas import tpu as pltpu

D_MODEL = 2048
BATCH = 2
SEQ = 4096
DEPTH = 1
DEC_BATCH = 32
DEC_SEQ = 8
PAST_LEN = 8192
PAGE_SIZE = 128
N_HEADS = 16
HEAD_DIM = 128
N_KV_HEADS = 4
GROUP = N_HEADS // N_KV_HEADS
CMP_STRIDE = 16
CMP_LEN = 2 * CMP_STRIDE
SEL_LEN = 64
SEL_SHIFT = SEL_LEN.bit_length() - 1
SEL_TOPK = 16
WINDOW = 512
WIN_Q_BLOCK = 128
SEL_Q_BLOCK = 64
FORCE_BONUS = 1e4
CONV_CH = D_MODEL // 2
CONV_WIDTH = 31
PEER_HEADS = 8
PEER_NKEYS = 128
PEER_DKEY = 256
PEER_TOPK = 16
PEER_CHUNK = 128
QW = N_HEADS * HEAD_DIM
KVW = N_KV_HEADS * HEAD_DIM
N_GATES = 3 * N_HEADS
SCALE = HEAD_DIM ** -0.5
EPS = 1e-6
NEG = -1e30

VMEM_LIMIT_BYTES = 48 * 1024 * 1024
VMEM_LIMIT_PEER_BYTES = 58 * 1024 * 1024

GATE_PAD = 256
COL_Q = 0
COL_KC = COL_Q + QW
COL_VC = COL_KC + KVW
COL_KS = COL_VC + KVW
COL_VS = COL_KS + KVW
COL_KW = COL_VS + KVW
COL_VW = COL_KW + KVW
COL_GLU_A = COL_VW + KVW
COL_GLU_B = COL_GLU_A + CONV_CH
COL_GM_A = COL_GLU_B + CONV_CH
COL_GM_B = COL_GM_A + D_MODEL
COL_G = COL_GM_B + D_MODEL
N_COLS = COL_G + GATE_PAD


def _permute_w_in(w_in):
    g0 = QW + 6 * KVW
    g1 = g0 + N_GATES
    parts = [w_in[:, :g0], w_in[:, g1:], w_in[:, g0:g1],
             jnp.zeros((w_in.shape[0], GATE_PAD - N_GATES), w_in.dtype)]
    return jnp.concatenate(parts, axis=1).astype(jnp.bfloat16)


def _proj_in_kernel(x_ref, g_ref, w_ref, o_ref, xn_ref):
    @pl.when(pl.program_id(1) == 0)
    def _():
        x = x_ref[...]
        ms = jnp.mean(x * x, axis=-1, keepdims=True)
        xn_ref[...] = (x * lax.rsqrt(ms + EPS) * g_ref[...]).astype(jnp.bfloat16)

    o_ref[...] = jnp.dot(xn_ref[...], w_ref[...], preferred_element_type=jnp.float32)


def _proj_in(x, gain, w_bf16, tm, tn):
    n, d = x.shape
    nc = w_bf16.shape[1]
    return pl.pallas_call(
        _proj_in_kernel,
        out_shape=jax.ShapeDtypeStruct((n, nc), jnp.float32),
        grid=(n // tm, nc // tn),
        in_specs=[pl.BlockSpec((tm, d), lambda i, j: (i, 0)),
                  pl.BlockSpec((1, d), lambda i, j: (0, 0)),
                  pl.BlockSpec((d, tn), lambda i, j: (0, j))],
        out_specs=pl.BlockSpec((tm, tn), lambda i, j: (i, j)),
        scratch_shapes=[pltpu.VMEM((tm, d), jnp.bfloat16)],
        compiler_params=pltpu.CompilerParams(
            dimension_semantics=("parallel", "arbitrary"),
            vmem_limit_bytes=VMEM_LIMIT_BYTES),
        name="proj_in",
    )(x, gain.reshape(1, d), w_bf16)


_NT = (((1,), (1,)), ((), ()))
MASK_BIAS = -1e9


def _stack_heads(q):
    return jnp.concatenate([q[:, r * HEAD_DIM:(r + 1) * HEAD_DIM] for r in range(GROUP)], axis=0)


def _unstack_heads(o, tq):
    return jnp.concatenate([o[r * tq:(r + 1) * tq] for r in range(GROUP)], axis=1)


def _cmp_select_kernel(q_ref, kc_ref, vc_ref, ct_ref, ocmp_ref, bias_ref, *, tq):
    t0 = pl.program_id(2) * tq
    q4 = _stack_heads(q_ref[...]).astype(jnp.bfloat16)
    kc = kc_ref[0, 0].astype(jnp.bfloat16)
    vc = vc_ref[0, 0].astype(jnp.bfloat16)
    s = lax.dot_general(q4, kc, _NT, preferred_element_type=jnp.float32) * SCALE
    t = t0 + (lax.broadcasted_iota(jnp.int32, s.shape, 0) & (tq - 1))
    i = lax.broadcasted_iota(jnp.int32, s.shape, 1)
    valid = (i * CMP_STRIDE + (CMP_LEN - 1)) <= t
    s = jnp.where(valid, s, NEG)
    m = jnp.max(s, axis=-1, keepdims=True)
    e = jnp.where(valid, jnp.exp(s - m), 0.0)
    p = (e / jnp.maximum(jnp.sum(e, axis=-1, keepdims=True), 1e-30)).astype(jnp.bfloat16)
    ocmp_ref[...] = _unstack_heads(jnp.dot(p, vc, preferred_element_type=jnp.float32), tq)

    ct = ct_ref[...]
    n_sel = ct.shape[0]
    pslc = lax.dot_general(ct, p[0:tq], _NT, preferred_element_type=jnp.float32)
    for r in range(1, GROUP):
        pslc = pslc + lax.dot_general(ct, p[r * tq:(r + 1) * tq], _NT,
                                      preferred_element_type=jnp.float32)
    j = lax.broadcasted_iota(jnp.int32, (n_sel, tq), 0)
    cur = (t0 + lax.broadcasted_iota(jnp.int32, (n_sel, tq), 1)) >> SEL_SHIFT
    forced = (j == 0) | (j == cur) | (j == cur - 1)
    score = jnp.where(j <= cur, pslc + jnp.where(forced, FORCE_BONUS, 0.0), NEG)
    rank = jnp.zeros((n_sel, tq), jnp.float32)
    for jp in range(n_sel):
        sj = score[jp:jp + 1, :]
        rank = rank + jnp.where(sj > score, 1.0, jnp.where((sj == score) & (j > jp), 1.0, 0.0))
    bias = jnp.where((rank < SEL_TOPK) & (j <= cur), 0.0, MASK_BIAS)
    bias = jnp.concatenate([bias, jnp.zeros((HEAD_DIM - n_sel, tq), jnp.float32)], axis=0)
    bias_ref[0, 0] = bias.T.astype(jnp.bfloat16)


def _cmp_select(z, kcmp, vcmp, ct, B, T, tq):
    n_cp = kcmp.shape[2]
    nq = T // tq
    return pl.pallas_call(
        functools.partial(_cmp_select_kernel, tq=tq),
        out_shape=(jax.ShapeDtypeStruct((B * T, QW), jnp.float32),
                   jax.ShapeDtypeStruct((B, N_KV_HEADS, T, HEAD_DIM), jnp.bfloat16)),
        grid=(B, N_KV_HEADS, nq),
        in_specs=[pl.BlockSpec((tq, GROUP * HEAD_DIM), lambda b, g, qi: (b * nq + qi, g)),
                  pl.BlockSpec((1, 1, n_cp, HEAD_DIM), lambda b, g, qi: (b, g, 0, 0)),
                  pl.BlockSpec((1, 1, n_cp, HEAD_DIM), lambda b, g, qi: (b, g, 0, 0)),
                  pl.BlockSpec(ct.shape, lambda b, g, qi: (0, 0))],
        out_specs=(pl.BlockSpec((tq, GROUP * HEAD_DIM), lambda b, g, qi: (b * nq + qi, g)),
                   pl.BlockSpec((1, 1, tq, HEAD_DIM), lambda b, g, qi: (b, g, qi, 0))),
        compiler_params=pltpu.CompilerParams(
            dimension_semantics=("parallel", "parallel", "arbitrary"),
            vmem_limit_bytes=VMEM_LIMIT_BYTES),
        name="cmp_select",
    )(z, kcmp, vcmp, ct)


def _sel_attn_kernel(q_ref, bias_ref, k_ref, v_ref, o_ref, kaug, vt, acc_sc, *, tq, kc):
    qi = pl.program_id(2)
    t0 = qi * tq
    T = k_ref.shape[0]
    nq = GROUP * tq

    @pl.when(qi == 0)
    def _():
        kaug[:, 0:HEAD_DIM] = k_ref[...].astype(jnp.bfloat16)
        blk = lax.broadcasted_iota(jnp.int32, (T, HEAD_DIM), 0) >> SEL_SHIFT
        col = lax.broadcasted_iota(jnp.int32, (T, HEAD_DIM), 1)
        kaug[:, HEAD_DIM:2 * HEAD_DIM] = jnp.where(blk == col, 1.0, 0.0).astype(jnp.bfloat16)
        for c in range(T // kc):
            vt[c] = v_ref[c * kc:(c + 1) * kc, :].T.astype(jnp.bfloat16)

    q = q_ref[...]
    bias = bias_ref[0, 0]
    qa = jnp.concatenate(
        [jnp.concatenate([q[:, r * HEAD_DIM:(r + 1) * HEAD_DIM].astype(jnp.bfloat16), bias], axis=1)
         for r in range(GROUP)], axis=0)
    acc_sc[...] = jnp.zeros(acc_sc.shape, jnp.float32)
    t = t0 + (lax.broadcasted_iota(jnp.int32, (kc, nq), 1) & (tq - 1))
    key = lax.broadcasted_iota(jnp.int32, (kc, nq), 0)

    def step(c, m_old, l_old, causal):
        k0 = pl.multiple_of(c * kc, kc)
        s = lax.dot_general(kaug[pl.ds(k0, kc), :], qa, _NT,
                            preferred_element_type=jnp.float32) * SCALE
        if causal:
            s = jnp.where(k0 + key <= t, s, NEG)
        m_new = jnp.maximum(m_old, jnp.max(s, axis=0, keepdims=True))
        alpha = jnp.exp(m_old - m_new)
        p = jnp.exp(s - m_new)
        l_new = alpha * l_old + jnp.sum(p, axis=0, keepdims=True)
        acc_sc[...] = alpha * acc_sc[...] + jnp.dot(vt[c], p.astype(jnp.bfloat16),
                                                    preferred_element_type=jnp.float32)
        return m_new, l_new

    last = (t0 + tq - 1) // kc
    init = (jnp.full((1, nq), NEG, jnp.float32), jnp.zeros((1, nq), jnp.float32))
    def pair(i, ml):
        m, l = step(2 * i, ml[0], ml[1], False)
        return step(2 * i + 1, m, l, False)

    m, l = lax.fori_loop(0, last // 2, pair, init)
    m, l = lax.fori_loop(2 * (last // 2), last, lambda c, ml: step(c, ml[0], ml[1], False), (m, l))
    m, l = step(last, m, l, True)
    o_ref[...] = _unstack_heads((acc_sc[...] / jnp.maximum(l, 1e-30)).T, tq)


def _sel_attn(z, bias, B, T, tq, kc):
    nq = T // tq
    ck = COL_KS // HEAD_DIM
    cv = COL_VS // HEAD_DIM
    return pl.pallas_call(
        functools.partial(_sel_attn_kernel, tq=tq, kc=kc),
        out_shape=jax.ShapeDtypeStruct((B * T, QW), jnp.float32),
        grid=(B, N_KV_HEADS, nq),
        in_specs=[pl.BlockSpec((tq, GROUP * HEAD_DIM), lambda b, g, qi: (b * nq + qi, g)),
                  pl.BlockSpec((1, 1, tq, HEAD_DIM), lambda b, g, qi: (b, g, qi, 0)),
                  pl.BlockSpec((T, HEAD_DIM), lambda b, g, qi: (b, ck + g)),
                  pl.BlockSpec((T, HEAD_DIM), lambda b, g, qi: (b, cv + g))],
        out_specs=pl.BlockSpec((tq, GROUP * HEAD_DIM), lambda b, g, qi: (b * nq + qi, g)),
        scratch_shapes=[pltpu.VMEM((T, 2 * HEAD_DIM), jnp.bfloat16),
                        pltpu.VMEM((T // kc, HEAD_DIM, kc), jnp.bfloat16),
                        pltpu.VMEM((HEAD_DIM, GROUP * tq), jnp.float32)],
        compiler_params=pltpu.CompilerParams(
            dimension_semantics=("parallel", "parallel", "arbitrary"),
            vmem_limit_bytes=VMEM_LIMIT_BYTES),
        name="sel_attn",
    )(z, bias, z, z)


def _win_attn_kernel(q_ref, k_ref, v_ref, o_ref, kpad, vt, *, tq):
    qi = pl.program_id(2)
    t0 = pl.multiple_of(qi * tq, tq)
    T = k_ref.shape[0]
    n_pad = WINDOW // tq
    n_span = n_pad + 1
    span = n_span * tq

    @pl.when(qi == 0)
    def _():
        kpad[0:WINDOW, :] = jnp.zeros((WINDOW, HEAD_DIM), jnp.bfloat16)
        kpad[WINDOW:WINDOW + T, :] = k_ref[...].astype(jnp.bfloat16)
        for c in range(n_pad):
            vt[c] = jnp.zeros((HEAD_DIM, tq), jnp.bfloat16)
        for c in range(T // tq):
            vt[n_pad + c] = v_ref[c * tq:(c + 1) * tq, :].T.astype(jnp.bfloat16)

    q4 = _stack_heads(q_ref[...]).astype(jnp.bfloat16)
    s = lax.dot_general(kpad[pl.ds(t0, span), :], q4, _NT,
                        preferred_element_type=jnp.float32) * SCALE
    kpos = t0 - WINDOW + lax.broadcasted_iota(jnp.int32, s.shape, 0)
    t = t0 + (lax.broadcasted_iota(jnp.int32, s.shape, 1) & (tq - 1))
    d = t - kpos
    valid = (kpos >= 0) & (d >= 0) & (d < WINDOW)
    s = jnp.where(valid, s, NEG)
    m = jnp.max(s, axis=0, keepdims=True)
    e = jnp.where(valid, jnp.exp(s - m), 0.0)
    p = (e / jnp.maximum(jnp.sum(e, axis=0, keepdims=True), 1e-30)).astype(jnp.bfloat16)
    o_t = None
    for c in range(n_span):
        part = jnp.dot(vt[qi + c], p[c * tq:(c + 1) * tq], preferred_element_type=jnp.float32)
        o_t = part if o_t is None else o_t + part
    o_ref[...] = _unstack_heads(o_t.T, tq)


def _win_attn(z, B, T, tq):
    nq = T // tq
    ck = COL_KW // HEAD_DIM
    cv = COL_VW // HEAD_DIM
    return pl.pallas_call(
        functools.partial(_win_attn_kernel, tq=tq),
        out_shape=jax.ShapeDtypeStruct((B * T, QW), jnp.float32),
        grid=(B, N_KV_HEADS, nq),
        in_specs=[pl.BlockSpec((tq, GROUP * HEAD_DIM), lambda b, g, qi: (b * nq + qi, g)),
                  pl.BlockSpec((T, HEAD_DIM), lambda b, g, qi: (b, ck + g)),
                  pl.BlockSpec((T, HEAD_DIM), lambda b, g, qi: (b, cv + g))],
        out_specs=pl.BlockSpec((tq, GROUP * HEAD_DIM), lambda b, g, qi: (b * nq + qi, g)),
        scratch_shapes=[pltpu.VMEM((T + WINDOW, HEAD_DIM), jnp.bfloat16),
                        pltpu.VMEM(((T + WINDOW) // tq, HEAD_DIM, tq), jnp.bfloat16)],
        compiler_params=pltpu.CompilerParams(
            dimension_semantics=("parallel", "parallel", "arbitrary"),
            vmem_limit_bytes=VMEM_LIMIT_BYTES),
        name="win_attn",
    )(z, z, z)


PAGES_PER_STEP = 16
CHUNKS_PER_PAGE = PAGE_SIZE // CMP_STRIDE
PAGE_ROWS = PAGE_SIZE * N_KV_HEADS
DEC_ROWS = N_HEADS * DEC_SEQ
assert DEC_ROWS == 128 and DEC_SEQ == 8


def _head_rows(ref, g, n_tokens):
    return ref[pl.ds(g, n_tokens, stride=N_KV_HEADS), :]


def _dec_rows(q):
    return jnp.concatenate([q[:, (g * GROUP + r) * HEAD_DIM:(g * GROUP + r + 1) * HEAD_DIM]
                            for r in range(GROUP) for g in range(N_KV_HEADS)], axis=0)


def _dec_cols(o):
    return jnp.concatenate(
        [o[(r * N_KV_HEADS + g) * DEC_SEQ:(r * N_KV_HEADS + g + 1) * DEC_SEQ]
         for g in range(N_KV_HEADS) for r in range(GROUP)], axis=1)


def _dec_row_ids():
    row = lax.broadcasted_iota(jnp.int32, (DEC_ROWS, 1), 0)
    return (row >> 3) & (N_KV_HEADS - 1), row & (DEC_SEQ - 1)


def _group_scores(qall, keys_of, row_g):
    out = None
    for g in range(N_KV_HEADS):
        s = lax.dot_general(qall, keys_of(g), _NT, preferred_element_type=jnp.float32)
        s = jnp.where(row_g == g, s, 0.0)
        out = s if out is None else out + s
    return out


def _group_values(p, vals_of, row_g):
    out = None
    for g in range(N_KV_HEADS):
        pg = jnp.where(row_g == g, p, 0.0).astype(jnp.bfloat16)
        o = jnp.dot(pg, vals_of(g), preferred_element_type=jnp.float32)
        out = o if out is None else out + o
    return out


TAP_ROWS = 2 * N_KV_HEADS


def _chunk_hidden(page_refs, w1_ref):
    chunk_rows = CMP_STRIDE * N_KV_HEADS
    n_rows = len(page_refs) * CHUNKS_PER_PAGE * TAP_ROWS
    odd = (lax.broadcasted_iota(jnp.int32, (n_rows, 1), 0) & N_KV_HEADS) != 0
    acc = None
    for p in range(CMP_STRIDE // 2):
        x = jnp.concatenate(
            [pg[c * chunk_rows + p * TAP_ROWS:c * chunk_rows + (p + 1) * TAP_ROWS, :]
             for pg in page_refs for c in range(CHUNKS_PER_PAGE)], axis=0).astype(jnp.bfloat16)
        even_tap = jnp.dot(x, w1_ref[2 * p], preferred_element_type=jnp.float32)
        odd_tap = jnp.dot(x, w1_ref[2 * p + 1], preferred_element_type=jnp.float32)
        h = jnp.where(odd, odd_tap, even_tap)
        acc = h if acc is None else acc + h
    return acc


def _hidden_rows(h_sc, g, first_chunk, n):
    even = h_sc[pl.ds(first_chunk * TAP_ROWS + g, n, stride=TAP_ROWS), :]
    odd = h_sc[pl.ds(first_chunk * TAP_ROWS + N_KV_HEADS + g, n, stride=TAP_ROWS), :]
    return even + odd


def _store_hidden(h_lo, h_hi, rows, h):
    h_lo[rows, :] = h[:, 0:HEAD_DIM]
    h_hi[rows, :] = h[:, HEAD_DIM:2 * HEAD_DIM]


def _dec_cmp_kernel(pt_ref, *refs, past_len, n_sel):
    nps = PAGES_PER_STEP
    kp, vp = refs[0:nps], refs[nps:2 * nps]
    (kt_ref, vt_ref, q_ref, w1k_ref, w1v_ref, ck_ref, cv_ref, w2k_ref, w2v_ref, ct_ref,
     ocmp_ref, bias_ref, hk_lo, hk_hi, hv_lo, hv_hi, score_sc) = refs[2 * nps:]
    j = pl.program_id(1)
    rows = nps * CHUNKS_PER_PAGE * TAP_ROWS
    base = (past_len // PAGE_SIZE) * CHUNKS_PER_PAGE
    tail_end = (base + CHUNKS_PER_PAGE) * TAP_ROWS

    step_rows = pl.ds(pl.multiple_of(j * rows, rows), rows)
    for pages, w1_ref, h_lo, h_hi in ((kp, w1k_ref, hk_lo, hk_hi), (vp, w1v_ref, hv_lo, hv_hi)):
        _store_hidden(h_lo, h_hi, step_rows, _chunk_hidden(pages, w1_ref))

    @pl.when(j == pl.num_programs(1) - 1)
    def _():
        for t_ref, w1_ref, h_lo, h_hi in ((kt_ref, w1k_ref, hk_lo, hk_hi),
                                          (vt_ref, w1v_ref, hv_lo, hv_hi)):
            _store_hidden(h_lo, h_hi, slice(base * TAP_ROWS, tail_end), _chunk_hidden([t_ref], w1_ref))
            zeros = jnp.zeros((h_lo.shape[0] - tail_end, HEAD_DIM), jnp.float32)
            h_lo[tail_end:, :] = zeros
            h_hi[tail_end:, :] = zeros
        ncp = ct_ref.shape[1]
        kc, vc = [], []
        for g in range(N_KV_HEADS):
            for h_lo, h_hi, c_ref, w2_ref, dst in ((hk_lo, hk_hi, ck_ref, w2k_ref, kc),
                                                   (hv_lo, hv_hi, cv_ref, w2v_ref, vc)):
                hh = _hidden_rows(h_lo, g, 0, ncp) + _hidden_rows(h_hi, g, 1, ncp) + c_ref[...]
                dst.append(jnp.dot(jax.nn.gelu(hh).astype(jnp.bfloat16), w2_ref[...],
                                   preferred_element_type=jnp.float32).astype(jnp.bfloat16))
        qall = _dec_rows(q_ref[...]).astype(jnp.bfloat16)
        row_g, row_t = _dec_row_ids()
        s = _group_scores(qall, lambda g: kc[g], row_g) * SCALE
        i = lax.broadcasted_iota(jnp.int32, s.shape, 1)
        valid = (i * CMP_STRIDE + (CMP_LEN - 1)) <= (past_len + row_t)
        s = jnp.where(valid, s, NEG)
        m = jnp.max(s, axis=-1, keepdims=True)
        e = jnp.where(valid, jnp.exp(s - m), 0.0)
        p = e / jnp.maximum(jnp.sum(e, axis=-1, keepdims=True), 1e-30)
        ocmp_ref[...] = _dec_cols(_group_values(p, lambda g: vc[g], row_g))

        x = lax.dot_general(ct_ref[...], p.astype(jnp.bfloat16), _NT,
                            preferred_element_type=jnp.float32)
        quarter = DEC_ROWS // GROUP
        pslc = x
        for r in range(1, GROUP):
            pslc = pslc + pltpu.roll(x, r * quarter, 1)
        shape = pslc.shape
        jj = lax.broadcasted_iota(jnp.int32, shape, 0)
        cur = (past_len + (lax.broadcasted_iota(jnp.int32, shape, 1) & (DEC_SEQ - 1))) >> SEL_SHIFT
        forced = (jj == 0) | (jj == cur) | (jj == cur - 1)
        score = jnp.where(jj <= cur, pslc + jnp.where(forced, FORCE_BONUS, 0.0), NEG)
        score_sc[...] = score

        def rank_step(jp, rank):
            sj = score_sc[pl.ds(jp, 1), :]
            return rank + jnp.where(sj > score, 1.0, jnp.where((sj == score) & (jj > jp), 1.0, 0.0))

        rank = lax.fori_loop(0, n_sel, rank_step, jnp.zeros(shape, jnp.float32))
        bias = jnp.where((rank < SEL_TOPK) & (jj <= cur), 0.0, MASK_BIAS)
        pad = jnp.full((bias_ref.shape[2] - shape[0], shape[1]), MASK_BIAS, jnp.float32)
        bias_ref[0] = jnp.concatenate([bias, pad], axis=0).T.astype(jnp.bfloat16)


PAGE_BLOCK = (PAGE_ROWS, HEAD_DIM)


def _page_specs(n):
    return [pl.BlockSpec(PAGE_BLOCK, lambda b, j, pt, i=i: (pt[b, j * PAGES_PER_STEP + i], 0))
            for i in range(n)]


_TAIL_SPEC = pl.BlockSpec(PAGE_BLOCK, lambda b, j, pt: (b, 0))


def _dec_cmp_select(page_table, pool_k, pool_v, tail_k, tail_v, z, w1k, w1v, ck, cv, w2k, w2v, ct,
                    past_len):
    nb = page_table.shape[0]
    n_pages = past_len // PAGE_SIZE
    n_sel, ncp = ct.shape
    n_sel_real = -(-(past_len + DEC_SEQ) // SEL_LEN)
    full2 = lambda a: pl.BlockSpec(a.shape, lambda b, j, pt: (0, 0))
    full3 = lambda a: pl.BlockSpec(a.shape, lambda b, j, pt: (0, 0, 0))
    tail = _TAIL_SPEC
    n_blocks = 2 * HEAD_DIM
    grid_spec = pltpu.PrefetchScalarGridSpec(
        num_scalar_prefetch=1,
        grid=(nb, n_pages // PAGES_PER_STEP),
        in_specs=(_page_specs(PAGES_PER_STEP) + _page_specs(PAGES_PER_STEP)
                  + [tail, tail, pl.BlockSpec((DEC_SEQ, QW), lambda b, j, pt: (b, 0)),
                     full3(w1k), full3(w1v), full2(ck), full2(cv), full2(w2k), full2(w2v), full2(ct)]),
        out_specs=(pl.BlockSpec((DEC_SEQ, QW), lambda b, j, pt: (b, 0)),
                   pl.BlockSpec((1, DEC_ROWS, n_blocks), lambda b, j, pt: (b, 0, 0))),
        scratch_shapes=[pltpu.VMEM(((ncp + 8) * TAP_ROWS, HEAD_DIM), jnp.float32)] * 4
        + [pltpu.VMEM((n_sel, DEC_ROWS), jnp.float32)])
    return pl.pallas_call(
        functools.partial(_dec_cmp_kernel, past_len=past_len, n_sel=n_sel_real),
        out_shape=(jax.ShapeDtypeStruct((nb * DEC_SEQ, QW), jnp.float32),
                   jax.ShapeDtypeStruct((nb, DEC_ROWS, n_blocks), jnp.bfloat16)),
        grid_spec=grid_spec,
        compiler_params=pltpu.CompilerParams(
            dimension_semantics=("parallel", "arbitrary"),
            vmem_limit_bytes=VMEM_LIMIT_BYTES),
        name="dec_cmp_select",
    )(page_table, *([pool_k] * PAGES_PER_STEP), *([pool_v] * PAGES_PER_STEP), tail_k, tail_v, z,
      w1k, w1v, ck, cv, w2k, w2v, ct)


def _dec_sel_kernel(pt_ref, *refs, past_len):
    nps = PAGES_PER_STEP
    kp, vp = refs[0:nps], refs[nps:2 * nps]
    kt_ref, vt_ref, q_ref, bias_ref, o_ref, m_sc, l_sc, acc_sc = refs[2 * nps:]
    j = pl.program_id(1)
    qall = _dec_rows(q_ref[...]).astype(jnp.bfloat16)
    row_g, row_t = _dec_row_ids()
    selcols = bias_ref[0]

    @pl.when(j == 0)
    def _():
        m_sc[...] = jnp.full(m_sc.shape, NEG, jnp.float32)
        l_sc[...] = jnp.zeros(l_sc.shape, jnp.float32)
        acc_sc[...] = jnp.zeros(acc_sc.shape, jnp.float32)

    def update(k_refs, v_refs, first_blk, causal):
        grp = lambda ref, g: _head_rows(ref, g, PAGE_SIZE).astype(jnp.bfloat16)
        s = jnp.concatenate([_group_scores(qall, functools.partial(grp, kr), row_g)
                             for kr in k_refs], axis=1) * SCALE
        nk = s.shape[1]
        n_blocks = selcols.shape[1]
        blk = first_blk + (lax.broadcasted_iota(jnp.int32, (n_blocks, nk), 1) >> SEL_SHIFT)
        onehot = jnp.where(lax.broadcasted_iota(jnp.int32, (n_blocks, nk), 0) == blk, 1.0, 0.0)
        s = s + jnp.dot(selcols, onehot.astype(jnp.bfloat16), preferred_element_type=jnp.float32)
        if causal:
            kpos = first_blk * SEL_LEN + lax.broadcasted_iota(jnp.int32, s.shape, 1)
            s = jnp.where(kpos <= past_len + row_t, s, NEG)
        m_old = m_sc[...]
        m_new = jnp.maximum(m_old, jnp.max(s, axis=-1, keepdims=True))
        alpha = jnp.exp(m_old - m_new)
        p = jnp.exp(s - m_new)
        l_sc[...] = alpha * l_sc[...] + jnp.sum(p, axis=-1, keepdims=True)
        pv = None
        for n, vr in enumerate(v_refs):
            o = _group_values(p[:, n * PAGE_SIZE:(n + 1) * PAGE_SIZE], functools.partial(grp, vr), row_g)
            pv = o if pv is None else pv + o
        acc_sc[...] = alpha * acc_sc[...] + pv
        m_sc[...] = m_new

    update(kp, vp, j * (nps * PAGE_SIZE // SEL_LEN), False)

    @pl.when(j == pl.num_programs(1) - 1)
    def _():
        update([kt_ref], [vt_ref], past_len // SEL_LEN, True)
        o_ref[...] = _dec_cols(acc_sc[...] / jnp.maximum(l_sc[...], 1e-30))


def _dec_sel_attn(page_table, pool_k, pool_v, tail_k, tail_v, z, bias, past_len):
    nb = page_table.shape[0]
    n_pages = past_len // PAGE_SIZE
    tail = _TAIL_SPEC
    grid_spec = pltpu.PrefetchScalarGridSpec(
        num_scalar_prefetch=1,
        grid=(nb, n_pages // PAGES_PER_STEP),
        in_specs=(_page_specs(PAGES_PER_STEP) + _page_specs(PAGES_PER_STEP)
                  + [tail, tail, pl.BlockSpec((DEC_SEQ, QW), lambda b, j, pt: (b, 0)),
                     pl.BlockSpec((1,) + bias.shape[1:], lambda b, j, pt: (b, 0, 0))]),
        out_specs=pl.BlockSpec((DEC_SEQ, QW), lambda b, j, pt: (b, 0)),
        scratch_shapes=[pltpu.VMEM((DEC_ROWS, 1), jnp.float32),
                        pltpu.VMEM((DEC_ROWS, 1), jnp.float32),
                        pltpu.VMEM((DEC_ROWS, HEAD_DIM), jnp.float32)])
    return pl.pallas_call(
        functools.partial(_dec_sel_kernel, past_len=past_len),
        out_shape=jax.ShapeDtypeStruct((nb * DEC_SEQ, QW), jnp.float32),
        grid_spec=grid_spec,
        compiler_params=pltpu.CompilerParams(
            dimension_semantics=("parallel", "arbitrary"),
            vmem_limit_bytes=VMEM_LIMIT_BYTES),
        name="dec_sel_attn",
    )(page_table, *([pool_k] * PAGES_PER_STEP), *([pool_v] * PAGES_PER_STEP), tail_k, tail_v, z, bias)


def _dec_win_kernel(q_ref, wk_ref, wv_ref, kt_ref, vt_ref, o_ref, nwk_ref, nwv_ref, *, past_len):
    wl = wk_ref.shape[0] // N_KV_HEADS
    qall = _dec_rows(q_ref[...]).astype(jnp.bfloat16)
    row_g, row_t = _dec_row_ids()

    def cat(w_ref, t_ref, g):
        return jnp.concatenate([_head_rows(w_ref, g, wl), _head_rows(t_ref, g, PAGE_SIZE)],
                               axis=0).astype(jnp.bfloat16)

    s = _group_scores(qall, functools.partial(cat, wk_ref, kt_ref), row_g) * SCALE
    kpos = past_len - wl + lax.broadcasted_iota(jnp.int32, s.shape, 1)
    d = past_len + row_t - kpos
    valid = (d >= 0) & (d < WINDOW)
    s = jnp.where(valid, s, NEG)
    m = jnp.max(s, axis=-1, keepdims=True)
    e = jnp.where(valid, jnp.exp(s - m), 0.0)
    p = e / jnp.maximum(jnp.sum(e, axis=-1, keepdims=True), 1e-30)
    o_ref[...] = _dec_cols(_group_values(p, functools.partial(cat, wv_ref, vt_ref), row_g))
    keep = (wl - DEC_SEQ) * N_KV_HEADS
    for w_ref, t_ref, n_ref in ((wk_ref, kt_ref, nwk_ref), (wv_ref, vt_ref, nwv_ref)):
        n_ref[0:keep, :] = w_ref[wl * N_KV_HEADS - keep:wl * N_KV_HEADS, :]
        n_ref[keep:wl * N_KV_HEADS, :] = t_ref[0:DEC_SEQ * N_KV_HEADS, :]


def _dec_win_attn(z, win_k, win_v, tail_k, tail_v, nb, past_len):
    cache = pl.BlockSpec((win_k.shape[0] // nb, HEAD_DIM), lambda b: (b, 0))
    tail = pl.BlockSpec(PAGE_BLOCK, lambda b: (b, 0))
    rows = pl.BlockSpec((DEC_SEQ, QW), lambda b: (b, 0))
    return pl.pallas_call(
        functools.partial(_dec_win_kernel, past_len=past_len),
        out_shape=(jax.ShapeDtypeStruct((nb * DEC_SEQ, QW), jnp.float32),
                   jax.ShapeDtypeStruct(win_k.shape, jnp.float32),
                   jax.ShapeDtypeStruct(win_v.shape, jnp.float32)),
        grid=(nb,),
        in_specs=[rows, cache, cache, tail, tail],
        out_specs=(rows, cache, cache),
        compiler_params=pltpu.CompilerParams(
            dimension_semantics=("parallel",),
            vmem_limit_bytes=VMEM_LIMIT_BYTES),
        name="dec_win_attn",
    )(z, win_k, win_v, tail_k, tail_v)


def _compress_kernel(xk_ref, xv_ref, w1k_ref, w1v_ref, ck_ref, cv_ref, w2k_ref, w2v_ref,
                     ko_ref, vo_ref, h_sc):
    nch = xk_ref.shape[0] // CMP_STRIDE
    h_sc[nch:nch + 8, :] = jnp.zeros((8, 2 * HEAD_DIM), jnp.float32)
    for x_ref, w1_ref, c_ref, w2_ref, o_ref in ((xk_ref, w1k_ref, ck_ref, w2k_ref, ko_ref),
                                                (xv_ref, w1v_ref, cv_ref, w2v_ref, vo_ref)):
        acc = None
        for l in range(CMP_STRIDE):
            x = x_ref[pl.ds(l, nch, stride=CMP_STRIDE), :].astype(jnp.bfloat16)
            h = jnp.dot(x, w1_ref[l], preferred_element_type=jnp.float32)
            acc = h if acc is None else acc + h
        h_sc[0:nch, :] = acc
        hh = h_sc[0:nch, 0:HEAD_DIM] + h_sc[1:nch + 1, HEAD_DIM:2 * HEAD_DIM] + c_ref[...]
        o_ref[0, 0] = jnp.dot(jax.nn.gelu(hh).astype(jnp.bfloat16), w2_ref[...],
                              preferred_element_type=jnp.float32)


def _compress_seq(z, w1k, w1v, ck, cv, w2k, w2v, B, T):
    nch = T // CMP_STRIDE
    ck0 = COL_KC // HEAD_DIM
    cv0 = COL_VC // HEAD_DIM
    full2 = lambda a: pl.BlockSpec(a.shape, lambda b, g: (0, 0))
    full3 = lambda a: pl.BlockSpec(a.shape, lambda b, g: (0, 0, 0))
    out = jax.ShapeDtypeStruct((B, N_KV_HEADS, nch, HEAD_DIM), jnp.float32)
    ospec = pl.BlockSpec((1, 1, nch, HEAD_DIM), lambda b, g: (b, g, 0, 0))
    return pl.pallas_call(
        _compress_kernel,
        out_shape=(out, out),
        grid=(B, N_KV_HEADS),
        in_specs=[pl.BlockSpec((T, HEAD_DIM), lambda b, g: (b, ck0 + g)),
                  pl.BlockSpec((T, HEAD_DIM), lambda b, g: (b, cv0 + g)),
                  full3(w1k), full3(w1v), full2(ck), full2(cv), full2(w2k), full2(w2v)],
        out_specs=(ospec, ospec),
        scratch_shapes=[pltpu.VMEM((nch + 8, 2 * HEAD_DIM), jnp.float32)],
        compiler_params=pltpu.CompilerParams(
            dimension_semantics=("parallel", "parallel"),
            vmem_limit_bytes=VMEM_LIMIT_BYTES),
        name="compress_seq",
    )(z, z, w1k, w1v, ck, cv, w2k, w2v)


CONV_HALO = 32
LANES = 128


def _conv_taps(xs, w_ref, first, rows, y_sc):
    for cb in range(CONV_CH // LANES):
        sl = slice(cb * LANES, (cb + 1) * LANES)
        acc = None
        for k in range(CONV_WIDTH):
            term = w_ref[k:k + 1, sl] * xs[first + k:first + k + rows, sl]
            acc = term if acc is None else acc + term
        y_sc[:, sl] = acc


def _conv_post(y, cb_ref, lg_ref, lb_ref):
    y = y + cb_ref[...]
    mu = jnp.mean(y, axis=-1, keepdims=True)
    var = jnp.mean(jnp.square(y - mu), axis=-1, keepdims=True)
    yn = (y - mu) * lax.rsqrt(var + EPS) * lg_ref[...] + lb_ref[...]
    return yn * jax.nn.sigmoid(yn)


def _conv_kernel(a_ref, b_ref, ap_ref, bp_ref, w_ref, cb_ref, lg_ref, lb_ref, y_ref, tail_ref,
                 xs, y_sc, *, tm):
    i = pl.program_id(1)
    prev = ap_ref[...] * jax.nn.sigmoid(bp_ref[...])
    xs[0:CONV_HALO, :] = jnp.where(i == 0, 0.0, prev)
    xs[CONV_HALO:CONV_HALO + tm, :] = a_ref[...] * jax.nn.sigmoid(b_ref[...])
    _conv_taps(xs, w_ref, CONV_HALO - (CONV_WIDTH - 1), tm, y_sc)
    y_ref[...] = _conv_post(y_sc[...], cb_ref, lg_ref, lb_ref).astype(y_ref.dtype)

    @pl.when(i == pl.num_programs(1) - 1)
    def _():
        tail_ref[0] = xs[tm:tm + CONV_HALO, :]


def _conv_seq(z, w, cb, lg, lb, B, T, tm):
    nt = T // tm
    ca = COL_GLU_A // CONV_CH
    cbk = COL_GLU_B // CONV_CH
    r = tm // CONV_HALO
    prev = lambda c: pl.BlockSpec((CONV_HALO, CONV_CH),
                                  lambda b, i: (jnp.maximum((b * nt + i) * r - 1, 0), c))
    cur = lambda c: pl.BlockSpec((tm, CONV_CH), lambda b, i: (b * nt + i, c))
    vec = pl.BlockSpec((1, CONV_CH), lambda b, i: (0, 0))
    return pl.pallas_call(
        functools.partial(_conv_kernel, tm=tm),
        out_shape=(jax.ShapeDtypeStruct((B * T, CONV_CH), jnp.bfloat16),
                   jax.ShapeDtypeStruct((B, CONV_HALO, CONV_CH), jnp.float32)),
        grid=(B, nt),
        in_specs=[cur(ca), cur(cbk), prev(ca), prev(cbk),
                  pl.BlockSpec(w.shape, lambda b, i: (0, 0)), vec, vec, vec],
        out_specs=(pl.BlockSpec((tm, CONV_CH), lambda b, i: (b * nt + i, 0)),
                   pl.BlockSpec((1, CONV_HALO, CONV_CH), lambda b, i: (b, 0, 0))),
        scratch_shapes=[pltpu.VMEM((tm + CONV_HALO, CONV_CH), jnp.float32),
                        pltpu.VMEM((tm, CONV_CH), jnp.float32)],
        compiler_params=pltpu.CompilerParams(
            dimension_semantics=("parallel", "arbitrary"),
            vmem_limit_bytes=VMEM_LIMIT_BYTES),
        name="conv_seq",
    )(z, z, z, z, w, cb.reshape(1, -1), lg.reshape(1, -1), lb.reshape(1, -1))


def _dec_conv_kernel(a_ref, b_ref, st_ref, w_ref, cb_ref, lg_ref, lb_ref, y_ref, ns_ref, xs, y_sc):
    nbuf = CONV_WIDTH - 1
    xs[0:nbuf, :] = st_ref[0]
    xs[nbuf:nbuf + DEC_SEQ, :] = a_ref[...] * jax.nn.sigmoid(b_ref[...])
    _conv_taps(xs, w_ref, 0, DEC_SEQ, y_sc)
    y_ref[...] = _conv_post(y_sc[...], cb_ref, lg_ref, lb_ref)
    ns_ref[0] = xs[DEC_SEQ:DEC_SEQ + nbuf, :]


def _dec_conv(z, state, w, cb, lg, lb):
    nb, nbuf, _ = state.shape
    ca = COL_GLU_A // CONV_CH
    cbk = COL_GLU_B // CONV_CH
    vec = pl.BlockSpec((1, CONV_CH), lambda b: (0, 0))
    st = pl.BlockSpec((1, nbuf, CONV_CH), lambda b: (b, 0, 0))
    return pl.pallas_call(
        _dec_conv_kernel,
        out_shape=(jax.ShapeDtypeStruct((nb * DEC_SEQ, CONV_CH), jnp.float32),
                   jax.ShapeDtypeStruct(state.shape, jnp.float32)),
        grid=(nb,),
        in_specs=[pl.BlockSpec((DEC_SEQ, CONV_CH), lambda b: (b, ca)),
                  pl.BlockSpec((DEC_SEQ, CONV_CH), lambda b: (b, cbk)),
                  st, pl.BlockSpec(w.shape, lambda b: (0, 0)), vec, vec, vec],
        out_specs=(pl.BlockSpec((DEC_SEQ, CONV_CH), lambda b: (b, 0)), st),
        scratch_shapes=[pltpu.VMEM((nbuf + DEC_SEQ + 2, CONV_CH), jnp.float32),
                        pltpu.VMEM((DEC_SEQ, CONV_CH), jnp.float32)],
        compiler_params=pltpu.CompilerParams(
            dimension_semantics=("parallel",),
            vmem_limit_bytes=VMEM_LIMIT_BYTES),
        name="dec_conv",
    )(z, z, state, w, cb.reshape(1, -1), lg.reshape(1, -1), lb.reshape(1, -1))


def _merge_kernel(oc_ref, os_ref, ow_ref, g_ref, convy_ref, ga_ref, gb_ref, wn_ref, wc_ref,
                  o_ref, onsa_ref):
    @pl.when(pl.program_id(1) == 0)
    def _():
        gz = jax.nn.sigmoid(g_ref[...])
        for h in range(N_HEADS):
            sl = slice(h * HEAD_DIM, (h + 1) * HEAD_DIM)
            o = (gz[:, 3 * h:3 * h + 1] * oc_ref[:, sl]
                 + gz[:, 3 * h + 1:3 * h + 2] * os_ref[:, sl]
                 + gz[:, 3 * h + 2:3 * h + 3] * ow_ref[:, sl])
            onsa_ref[:, sl] = o.astype(jnp.bfloat16)

    a = jnp.dot(onsa_ref[...], wn_ref[...], preferred_element_type=jnp.float32)
    b = jnp.dot(convy_ref[...].astype(jnp.bfloat16), wc_ref[...],
                preferred_element_type=jnp.float32)
    o_ref[...] = (jax.nn.sigmoid(ga_ref[...]) * a
                  + jax.nn.sigmoid(gb_ref[...]) * b).astype(o_ref.dtype)


def _merge(o_cmp, o_sel, o_win, conv_y, z, wn_bf16, wc_bf16, tm, tn):
    n = o_cmp.shape[0]
    ja = COL_GM_A // tn
    jb = COL_GM_B // tn
    jg = COL_G // GATE_PAD
    row = lambda i, j: (i, 0)
    return pl.pallas_call(
        _merge_kernel,
        out_shape=jax.ShapeDtypeStruct((n, D_MODEL), jnp.bfloat16),
        grid=(n // tm, D_MODEL // tn),
        in_specs=[pl.BlockSpec((tm, QW), row),
                  pl.BlockSpec((tm, QW), row),
                  pl.BlockSpec((tm, QW), row),
                  pl.BlockSpec((tm, GATE_PAD), lambda i, j: (i, jg)),
                  pl.BlockSpec((tm, CONV_CH), row),
                  pl.BlockSpec((tm, tn), lambda i, j: (i, ja + j)),
                  pl.BlockSpec((tm, tn), lambda i, j: (i, jb + j)),
                  pl.BlockSpec((QW, tn), lambda i, j: (0, j)),
                  pl.BlockSpec((CONV_CH, tn), lambda i, j: (0, j))],
        out_specs=pl.BlockSpec((tm, tn), lambda i, j: (i, j)),
        scratch_shapes=[pltpu.VMEM((tm, QW), jnp.bfloat16)],
        compiler_params=pltpu.CompilerParams(
            dimension_semantics=("parallel", "arbitrary"),
            vmem_limit_bytes=VMEM_LIMIT_BYTES),
        name="merge",
    )(o_cmp, o_sel, o_win, z, conv_y, z, z, wn_bf16, wc_bf16)


def _out_proj_kernel(m_ref, x_ref, w_ref, g_ref, h_ref, hn_ref):
    h = x_ref[...] + jnp.dot(m_ref[...], w_ref[...], preferred_element_type=jnp.float32)
    h_ref[...] = h
    ms = jnp.mean(h * h, axis=-1, keepdims=True)
    hn_ref[...] = (h * lax.rsqrt(ms + EPS) * g_ref[...]).astype(hn_ref.dtype)


def _out_proj(merged, x, w_bf16, gain, tm):
    n = x.shape[0]
    return pl.pallas_call(
        _out_proj_kernel,
        out_shape=(jax.ShapeDtypeStruct((n, D_MODEL), jnp.float32),
                   jax.ShapeDtypeStruct((n, D_MODEL), jnp.bfloat16)),
        grid=(n // tm,),
        in_specs=[pl.BlockSpec((tm, D_MODEL), lambda i: (i, 0)),
                  pl.BlockSpec((tm, D_MODEL), lambda i: (i, 0)),
                  pl.BlockSpec((D_MODEL, D_MODEL), lambda i: (0, 0)),
                  pl.BlockSpec((1, D_MODEL), lambda i: (0, 0))],
        out_specs=(pl.BlockSpec((tm, D_MODEL), lambda i: (i, 0)),
                   pl.BlockSpec((tm, D_MODEL), lambda i: (i, 0))),
        compiler_params=pltpu.CompilerParams(
            dimension_semantics=("parallel",),
            vmem_limit_bytes=VMEM_LIMIT_BYTES),
        name="out_proj",
    )(merged, x, w_bf16, gain.reshape(1, D_MODEL))


PEER_HALF = PEER_DKEY // 2
PEER_SEL = PEER_HEADS * PEER_TOPK
PEER_TOPK_SHIFT = PEER_TOPK.bit_length() - 1
CAND_B = PEER_TOPK // 2
CAND_B_SHIFT = CAND_B.bit_length() - 1
N_CAND = PEER_TOPK + (PEER_TOPK - 1) * CAND_B
GATE_HALF = PEER_NKEYS // 2
GATE_PITCH = GATE_HALF + 8


def _topk_chains(s_sc, v_sc, i_sc, k):
    n_chain, r, n = s_sc.shape
    row = lax.broadcasted_iota(jnp.int32, (r, n), 0)

    def body(i, carry):
        for c in range(n_chain):
            s = s_sc[c]
            m = jnp.max(s, axis=0, keepdims=True)
            j = jnp.min(jnp.where(s == m, row, r), axis=0, keepdims=True)
            s_sc[c] = jnp.where(row == j, -jnp.inf, s)
            v_sc[c, pl.ds(i, 1), :] = m
            i_sc[c, pl.ds(i, 1), :] = j
        return carry

    lax.fori_loop(0, k, body, 0)


def _pick_rows(idx, table):
    out = jnp.zeros(idx.shape, table.dtype)
    for a in range(PEER_TOPK):
        out = jnp.where(idx == a, table[a:a + 1, :], out)
    return out


def _peer_route_kernel(hn_ref, wq_ref, k1_ref, k2_ref, ia_ref, ib_ref, gt_ref,
                       s_sc, v_sc, i_sc, c_sc, cv_sc, ci_sc, a_sc, b_sc, g_sc):
    qh = jnp.dot(hn_ref[...], wq_ref[...], preferred_element_type=jnp.float32).astype(jnp.bfloat16)
    for h in range(PEER_HEADS):
        q1 = qh[:, h * PEER_DKEY:h * PEER_DKEY + PEER_HALF]
        q2 = qh[:, h * PEER_DKEY + PEER_HALF:(h + 1) * PEER_DKEY]
        s_sc[2 * h] = lax.dot_general(k1_ref[h], q1, _NT, preferred_element_type=jnp.float32)
        s_sc[2 * h + 1] = lax.dot_general(k2_ref[h], q2, _NT, preferred_element_type=jnp.float32)
    _topk_chains(s_sc, v_sc, i_sc, PEER_TOPK)
    for h in range(PEER_HEADS):
        v1, v2 = v_sc[2 * h], v_sc[2 * h + 1]
        c_sc[h] = jnp.concatenate([v1[0:1, :] + v2]
                                  + [v1[a:a + 1, :] + v2[0:CAND_B, :] for a in range(1, PEER_TOPK)],
                                  axis=0)
    _topk_chains(c_sc, cv_sc, ci_sc, PEER_TOPK)
    for h in range(PEER_HEADS):
        sc, pos = cv_sc[h], ci_sc[h]
        rest = pos - PEER_TOPK
        first = pos < PEER_TOPK
        ia = _pick_rows(jnp.where(first, 0, 1 + (rest >> CAND_B_SHIFT)), i_sc[2 * h])
        ib = _pick_rows(jnp.where(first, pos, rest & (CAND_B - 1)), i_sc[2 * h + 1])
        e = jnp.exp(sc - sc[0:1, :])
        gate = e / jnp.sum(e, axis=0, keepdims=True)
        rows = slice(h * PEER_TOPK, (h + 1) * PEER_TOPK)
        a_sc[rows, :] = ia.astype(jnp.float32)
        b_sc[rows, :] = ib.astype(jnp.float32)
        g_sc[rows, :] = gate
    ia_ref[...] = a_sc[...].T
    ib_ref[...] = b_sc[...].T
    gt_ref[...] = g_sc[...].T


def _peer_route(hn, wq_bf16, k1_bf16, k2_bf16, tm):
    n, d = hn.shape
    out = jax.ShapeDtypeStruct((n, PEER_SEL), jnp.float32)
    ospec = pl.BlockSpec((tm, PEER_SEL), lambda i: (i, 0))
    return pl.pallas_call(
        _peer_route_kernel,
        out_shape=(out, out, out),
        grid=(n // tm,),
        in_specs=[pl.BlockSpec((tm, d), lambda i: (i, 0)),
                  pl.BlockSpec(wq_bf16.shape, lambda i: (0, 0)),
                  pl.BlockSpec(k1_bf16.shape, lambda i: (0, 0, 0)),
                  pl.BlockSpec(k2_bf16.shape, lambda i: (0, 0, 0))],
        out_specs=(ospec, ospec, ospec),
        scratch_shapes=[pltpu.VMEM((2 * PEER_HEADS, PEER_NKEYS, tm), jnp.float32),
                        pltpu.VMEM((2 * PEER_HEADS, PEER_TOPK, tm), jnp.float32),
                        pltpu.VMEM((2 * PEER_HEADS, PEER_TOPK, tm), jnp.int32),
                        pltpu.VMEM((PEER_HEADS, N_CAND, tm), jnp.float32),
                        pltpu.VMEM((PEER_HEADS, PEER_TOPK, tm), jnp.float32),
                        pltpu.VMEM((PEER_HEADS, PEER_TOPK, tm), jnp.int32)]
        + [pltpu.VMEM((PEER_SEL, tm), jnp.float32)] * 3,
        compiler_params=pltpu.CompilerParams(
            dimension_semantics=("parallel",),
            vmem_limit_bytes=VMEM_LIMIT_BYTES),
        name="peer_route",
    )(hn, wq_bf16, k1_bf16, k2_bf16)


def _peer_ffn_kernel(hn_ref, ia_ref, ib_ref, gt_ref, u_ref, v_ref, o_ref, w_sc, *, tm, n_i1):
    c = pl.program_id(1)

    @pl.when(c == 0)
    def _():
        o_ref[...] = jnp.zeros(o_ref.shape, jnp.float32)

    steps_per_half = GATE_HALF // n_i1

    @pl.when(c % steps_per_half == 0)
    def _():
        first = (c // steps_per_half) * GATE_HALF
        bf = lambda x: x.astype(jnp.float32).astype(jnp.bfloat16)
        rows_a = bf(first + lax.broadcasted_iota(jnp.int32, (GATE_HALF, PEER_SEL), 0))
        rows_b = bf(lax.broadcasted_iota(jnp.int32, (PEER_NKEYS, PEER_SEL), 0))
        one = jnp.ones((1, PEER_SEL), jnp.bfloat16)
        zero = jnp.zeros((1, PEER_SEL), jnp.bfloat16)

        def token(n, carry):
            a_row = ia_ref[pl.ds(n, 1), :].astype(jnp.bfloat16)
            b_row = ib_ref[pl.ds(n, 1), :].astype(jnp.bfloat16)
            g_row = gt_ref[pl.ds(n, 1), :]
            g_hi = g_row.astype(jnp.bfloat16)
            g_lo = (g_row - g_hi.astype(jnp.float32)).astype(jnp.bfloat16)
            oa = jnp.where(rows_a == a_row, one, zero)
            hit_b = rows_b == b_row
            gb_hi = jnp.where(hit_b, g_hi, zero)
            gb_lo = jnp.where(hit_b, g_lo, zero)
            w = lax.dot_general(jnp.concatenate([oa, oa], axis=1),
                                jnp.concatenate([gb_hi, gb_lo], axis=1), _NT,
                                preferred_element_type=jnp.float32)
            w_sc[pl.ds(pl.multiple_of(n * GATE_PITCH, 8), GATE_HALF), :] = w
            return carry

        lax.fori_loop(0, tm, token, 0, unroll=8)

    act = jax.nn.gelu(lax.dot_general(hn_ref[...], u_ref[...], _NT,
                                      preferred_element_type=jnp.float32))
    row0 = (c * n_i1) & (GATE_HALF - 1)
    parts = []
    for j in range(n_i1):
        wj = w_sc[pl.ds(row0 + j, tm, stride=GATE_PITCH), :]
        parts.append((act[:, j * PEER_NKEYS:(j + 1) * PEER_NKEYS] * wj).astype(jnp.bfloat16))
    o_ref[...] += jnp.dot(jnp.concatenate(parts, axis=1), v_ref[...],
                          preferred_element_type=jnp.float32)


def _peer_ffn(hn, ia, ib, gt, u_bf16, v_bf16, tm, n_i1):
    n, d = hn.shape
    ec = n_i1 * PEER_NKEYS
    assert GATE_HALF % n_i1 == 0
    row = lambda i, c: (i, 0)
    return pl.pallas_call(
        functools.partial(_peer_ffn_kernel, tm=tm, n_i1=n_i1),
        out_shape=jax.ShapeDtypeStruct((n, d), jnp.float32),
        grid=(n // tm, u_bf16.shape[0] // ec),
        in_specs=[pl.BlockSpec((tm, d), row),
                  pl.BlockSpec((tm, PEER_SEL), row),
                  pl.BlockSpec((tm, PEER_SEL), row),
                  pl.BlockSpec((tm, PEER_SEL), row),
                  pl.BlockSpec((ec, d), lambda i, c: (c, 0)),
                  pl.BlockSpec((ec, d), lambda i, c: (c, 0))],
        out_specs=pl.BlockSpec((tm, d), row),
        scratch_shapes=[pltpu.VMEM((tm * GATE_PITCH, PEER_NKEYS), jnp.float32)],
        compiler_params=pltpu.CompilerParams(
            dimension_semantics=("parallel", "arbitrary"),
            vmem_limit_bytes=VMEM_LIMIT_PEER_BYTES),
        name="peer_ffn",
    )(hn, ia, ib, gt, u_bf16, v_bf16)


def _residual_norm_kernel(h_ref, f_ref, g_ref, y_ref):
    y = h_ref[...] + f_ref[...]
    ms = jnp.mean(y * y, axis=-1, keepdims=True)
    y_ref[...] = y * lax.rsqrt(ms + EPS) * g_ref[...]


def _residual_norm(h, f, gain, tm):
    n, d = h.shape
    row = pl.BlockSpec((tm, d), lambda i: (i, 0))
    return pl.pallas_call(
        _residual_norm_kernel,
        out_shape=jax.ShapeDtypeStruct((n, d), jnp.float32),
        grid=(n // tm,),
        in_specs=[row, row, pl.BlockSpec((1, d), lambda i: (0, 0))],
        out_specs=row,
        compiler_params=pltpu.CompilerParams(
            dimension_semantics=("parallel",),
            vmem_limit_bytes=VMEM_LIMIT_BYTES),
        name="residual_norm",
    )(h, f, gain.reshape(1, d))


def _rmsnorm(x, g):
    xf = x.astype(jnp.float32)
    y = xf * lax.rsqrt(jnp.mean(xf * xf, axis=-1, keepdims=True) + EPS)
    return (y * g.astype(jnp.float32)).astype(x.dtype)


def _layernorm(x, g, b):
    xf = x.astype(jnp.float32)
    mu = jnp.mean(xf, axis=-1, keepdims=True)
    var = jnp.mean(jnp.square(xf - mu), axis=-1, keepdims=True)
    return ((xf - mu) * lax.rsqrt(var + EPS) * g.astype(jnp.float32) + b.astype(jnp.float32)).astype(x.dtype)


def _masked_probs(s, mask):
    s = jnp.where(mask, s, NEG)
    m = jnp.max(s, axis=-1, keepdims=True)
    e = jnp.where(mask, jnp.exp(s - m), 0.0)
    return e / jnp.maximum(jnp.sum(e, axis=-1, keepdims=True), 1e-30)


def _attn_probs(q, k, mask):
    s = jnp.einsum('...tgrd,...kgd->...grtk', q, k).astype(jnp.float32) * SCALE
    return _masked_probs(s, mask[..., None, None, :, :])


def _attn_out(p, v):
    return jnp.einsum('...grtk,...kgd->...tgrd', p.astype(v.dtype), v)


def _gather_pages(pool, page_table):
    g = pool[page_table]
    return g.reshape(page_table.shape[0], -1, pool.shape[2], pool.shape[3])


def _compress(kv, pe, w1, w2):
    B, L, G, D = kv.shape
    ch = kv.reshape(B, L // CMP_STRIDE, CMP_STRIDE, G, D)
    h_lo = jnp.einsum('bjlgd,ldh->bjgh', ch, w1[:CMP_STRIDE])
    h_hi = jnp.einsum('bjlgd,ldh->bjgh', ch, w1[CMP_STRIDE:])
    h = h_lo[:, :-1] + h_hi[:, 1:] + jnp.einsum('ld,ldh->h', pe, w1)
    return jnp.einsum('bigh,he->bige', jax.nn.gelu(h), w2)


def _cmp_to_sel(n_cmp, n_sel):
    cs = jnp.arange(n_cmp)[:, None] * CMP_STRIDE
    ss = jnp.arange(n_sel)[None, :] * SEL_LEN
    ov = jnp.clip(jnp.minimum(cs + CMP_LEN, ss + SEL_LEN) - jnp.maximum(cs, ss), 0, None)
    return ov.astype(jnp.float32) / CMP_LEN


def _sel_attend(q, kb, vb, idx, valid, qpos):
    B, T, G, R, D = q.shape
    bi = jnp.arange(B)[:, None, None, None]
    gi = jnp.arange(G)[None, :, None, None]
    kg = kb[bi, idx, :, gi].reshape(B, G, T, -1, D)
    vg = vb[bi, idx, :, gi].reshape(B, G, T, -1, D)
    kpos = idx[..., None] * SEL_LEN + jnp.arange(SEL_LEN)
    mask = (valid[..., None] & (kpos <= qpos[:, None, None])).reshape(B, G, T, -1)
    s = jnp.einsum('btgrd,bgtkd->bgrtk', q, kg).astype(jnp.float32) * SCALE
    p = _masked_probs(s, mask[:, :, None])
    return jnp.einsum('bgrtk,bgtkd->btgrd', p.astype(vg.dtype), vg)


def _nsa_cmp_sel(q, k_c, v_c, k_s, v_s, qpos, lp, sweep_queries):
    B, T, G, R, D = q.shape
    L = k_c.shape[1]
    L_pad = -(-L // SEL_LEN) * SEL_LEN
    pad = ((0, 0), (0, L_pad - L), (0, 0), (0, 0))
    k_c, v_c, k_s, v_s = jnp.pad(k_c, pad), jnp.pad(v_c, pad), jnp.pad(k_s, pad), jnp.pad(v_s, pad)
    k_cmp = _compress(k_c, lp['cmp_pe_k'], lp['cmp_w1_k'], lp['cmp_w2_k'])
    v_cmp = _compress(v_c, lp['cmp_pe_v'], lp['cmp_w1_v'], lp['cmp_w2_v'])
    n_cmp = k_cmp.shape[1]
    cmp_end = jnp.arange(n_cmp) * CMP_STRIDE + CMP_LEN - 1
    p_cmp = _attn_probs(q, k_cmp, cmp_end[None, :] <= qpos[:, None])
    o_cmp = _attn_out(p_cmp, v_cmp)
    n_sel = L_pad // SEL_LEN
    p_slc = jnp.einsum('bgrti,ij->bgtj', p_cmp, _cmp_to_sel(n_cmp, n_sel))
    blk = jnp.arange(n_sel)[None, :]
    cur = (qpos // SEL_LEN)[:, None]
    forced = (blk == 0) | (blk == cur) | (blk == cur - 1)
    score = jnp.where(blk <= cur, p_slc + jnp.where(forced, FORCE_BONUS, 0.0), NEG)
    top_s, top_i = lax.top_k(score, min(SEL_TOPK, n_sel))
    valid = top_s > 0.5 * NEG
    kb = k_s.reshape(B, n_sel, SEL_LEN, G, D)
    vb = v_s.reshape(B, n_sel, SEL_LEN, G, D)
    if sweep_queries:
        nq = T // SEL_Q_BLOCK
        xs = (q.reshape(B, nq, SEL_Q_BLOCK, G, R, D).swapaxes(0, 1),
              top_i.reshape(B, G, nq, SEL_Q_BLOCK, -1).transpose(2, 0, 1, 3, 4),
              valid.reshape(B, G, nq, SEL_Q_BLOCK, -1).transpose(2, 0, 1, 3, 4),
              qpos.reshape(nq, SEL_Q_BLOCK))
        o = lax.map(lambda a: _sel_attend(a[0], kb, vb, a[1], a[2], a[3]), xs)
        o_sel = o.swapaxes(0, 1).reshape(B, T, G, R, D)
    else:
        xs = (q[:, None], kb[:, None], vb[:, None], top_i[:, None], valid[:, None])
        o_sel = lax.map(lambda a: _sel_attend(a[0], a[1], a[2], a[3], a[4], qpos)[0], xs)
    return o_cmp, o_sel


def _window_banded(q, k, v):
    B, T, G, R, D = q.shape
    nb = T // WIN_Q_BLOCK
    pad = ((0, 0), (WINDOW, 0), (0, 0), (0, 0))
    kidx = jnp.arange(nb)[:, None] * WIN_Q_BLOCK + jnp.arange(WINDOW + WIN_Q_BLOCK)[None, :]
    kblk = jnp.pad(k, pad)[:, kidx]
    vblk = jnp.pad(v, pad)[:, kidx]
    kpos = kidx - WINDOW
    qpos = jnp.arange(T).reshape(nb, WIN_Q_BLOCK)
    d = qpos[:, :, None] - kpos[:, None, :]
    mask = (kpos[:, None, :] >= 0) & (d >= 0) & (d < WINDOW)
    o = _attn_out(_attn_probs(q.reshape(B, nb, WIN_Q_BLOCK, G, R, D), kblk, mask), vblk)
    return o.reshape(B, T, G, R, D)


def _window_dense(q, k, v, qpos, kpos):
    d = qpos[:, None] - kpos[None, :]
    return _attn_out(_attn_probs(q, k, (d >= 0) & (d < WINDOW)), v)


def _conv_module(u, buf, w_dw, b_dw, ln_g, ln_b):
    xp = jnp.concatenate([buf, u], axis=1)
    y = lax.conv_general_dilated(xp, w_dw[:, None, :], (1,), 'VALID',
                                 dimension_numbers=('NWC', 'WIO', 'NWC'),
                                 feature_group_count=CONV_CH) + b_dw
    return jax.nn.silu(_layernorm(y, ln_g, ln_b)), xp[:, -(CONV_WIDTH - 1):]


def _split_z(z, B, T):
    kv = lambda c: z[:, c:c + KVW].reshape(B, T, N_KV_HEADS, HEAD_DIM)
    q = z[:, COL_Q:COL_Q + QW].reshape(B, T, N_KV_HEADS, GROUP, HEAD_DIM)
    a = z[:, COL_GLU_A:COL_GLU_A + CONV_CH]
    b = z[:, COL_GLU_B:COL_GLU_B + CONV_CH]
    u = (a * jax.nn.sigmoid(b)).reshape(B, T, CONV_CH)
    return q, kv(COL_KC), kv(COL_VC), kv(COL_KS), kv(COL_VS), kv(COL_KW), kv(COL_VW), u


def _cmp_to_sel_t(n_cmp, n_sel, n_cmp_pad, n_sel_pad=None):
    cs = np.arange(n_cmp)[None, :] * CMP_STRIDE
    ss = np.arange(n_sel)[:, None] * SEL_LEN
    ov = np.clip(np.minimum(cs + CMP_LEN, ss + SEL_LEN) - np.maximum(cs, ss), 0, None)
    ct = np.zeros((n_sel_pad or n_sel, n_cmp_pad), np.float32)
    ct[:n_sel, :n_cmp] = ov.astype(np.float32) / CMP_LEN
    return jnp.asarray(ct, jnp.bfloat16)


def _compress_params(pe, w1, w2):
    w1cat = jnp.concatenate([w1[:CMP_STRIDE], w1[CMP_STRIDE:]], axis=-1).astype(jnp.bfloat16)
    c = jnp.einsum('ld,ldh->h', pe, w1).reshape(1, -1)
    return w1cat, c, w2.astype(jnp.bfloat16)


def _tail_page(z, col, nb):
    x = z[:, col:col + KVW].reshape(nb, DEC_SEQ * N_KV_HEADS, HEAD_DIM)
    x = jnp.pad(x, ((0, 0), (0, PAGE_ROWS - DEC_SEQ * N_KV_HEADS), (0, 0)))
    return x.reshape(nb * PAGE_ROWS, HEAD_DIM)


def _pad_cmp(x_cmp, n_pad):
    x = jnp.transpose(x_cmp, (0, 2, 1, 3))
    return jnp.pad(x, ((0, 0), (0, 0), (0, n_pad - x.shape[2]), (0, 0)))


def _layer(x, lp, past, page_table, prompt, tm):
    B, T, _ = x.shape
    n = B * T
    x2 = x.reshape(n, D_MODEL)
    z = _proj_in(x2, lp['norm_mix'], lp['w_in_p'], tm, 768)
    kv = lambda c: z[:, c:c + KVW].reshape(B, T, N_KV_HEADS, HEAD_DIM)
    k_c, v_c, k_s, v_s = kv(COL_KC), kv(COL_VC), kv(COL_KS), kv(COL_VS)
    cmp_k, cmp_v = lp['cmp_k'], lp['cmp_v']
    conv = (lp['conv_w'], lp['conv_b'], lp['conv_ln_g'], lp['conv_ln_b'])
    if prompt:
        n_ch = T // CMP_STRIDE
        ct = _cmp_to_sel_t(n_ch - 1, T // SEL_LEN, n_ch)
        kcmp, vcmp = _compress_seq(z, cmp_k[0], cmp_v[0], cmp_k[1], cmp_v[1], cmp_k[2], cmp_v[2], B, T)
        o_cmp, bias = _cmp_select(z, kcmp, vcmp, ct, B, T, 128)
        o_sel = _sel_attn(z, bias, B, T, 256, 256)
        o_win = _win_attn(z, B, T, 128)
        wl = min(WINDOW, T)
        new_wk, new_wv = kv(COL_KW)[:, T - wl:], kv(COL_VW)[:, T - wl:]
        conv_y, u_tail = _conv_seq(z, *conv, B, T, 256)
        new_conv = u_tail[:, CONV_HALO - (CONV_WIDTH - 1):]
    else:
        as_rows = lambda a: a.reshape(-1, HEAD_DIM)
        n_ch = -(-(PAST_LEN + T) // SEL_LEN) * SEL_LEN // CMP_STRIDE
        n_sel = -(-(PAST_LEN + T) // SEL_LEN)
        ct = _cmp_to_sel_t(n_ch - 1, n_sel, -(-n_ch // LANES) * LANES, -(-n_sel // 8) * 8)
        o_cmp, bias = _dec_cmp_select(
            page_table, as_rows(past['cmp_k']), as_rows(past['cmp_v']),
            _tail_page(z, COL_KC, B), _tail_page(z, COL_VC, B), z,
            cmp_k[0], cmp_v[0], cmp_k[1], cmp_v[1], cmp_k[2], cmp_v[2], ct, PAST_LEN)
        o_sel = _dec_sel_attn(page_table, as_rows(past['sel_k']), as_rows(past['sel_v']),
                              _tail_page(z, COL_KS, B), _tail_page(z, COL_VS, B), z, bias, PAST_LEN)
        wl = past['win_k'].shape[1]
        o_win, new_wk, new_wv = _dec_win_attn(
            z, as_rows(past['win_k']), as_rows(past['win_v']),
            _tail_page(z, COL_KW, B), _tail_page(z, COL_VW, B), B, PAST_LEN)
        new_wk = new_wk.reshape(B, wl, N_KV_HEADS, HEAD_DIM)
        new_wv = new_wv.reshape(B, wl, N_KV_HEADS, HEAD_DIM)
        conv_y, new_conv = _dec_conv(z, past['conv'], *conv)
    merged = _merge(o_cmp, o_sel, o_win, conv_y, z,
                    lp['w_nsa_out_b'], lp['w_conv_out_b'], 256, 512)
    h, hn = _out_proj(merged, x2, lp['w_out_b'], lp['norm_ffn'], 256)
    ia, ib, gt = _peer_route(hn, lp['peer_wq_b'], lp['peer_k1_b'], lp['peer_k2_b'], 256)
    ffn = _peer_ffn(hn, ia, ib, gt, lp['peer_u_b'], lp['peer_v_b'], min(n, 512), 8)
    y = _residual_norm(h, ffn, lp['norm_final'], 256)
    return y.reshape(B, T, D_MODEL), (k_c, v_c, k_s, v_s, new_wk, new_wv, new_conv)


def kernel(x_prompt, x_sample, cache_cmp_k, cache_cmp_v, cache_sel_k, cache_sel_v, cache_win_k, cache_win_v, state_conv, page_table, norm_mix, w_in, cmp_pe_k, cmp_w1_k, cmp_w2_k, cmp_pe_v, cmp_w1_v, cmp_w2_v, w_nsa_out, conv_w, conv_b, conv_ln_g, conv_ln_b, w_conv_out, w_out, norm_ffn, peer_wq, peer_k1, peer_k2, peer_u, peer_v, norm_final):
    assert DEPTH == 1
    bf = lambda w: w.astype(jnp.bfloat16)
    l = 0
    lp = {'norm_mix': norm_mix[l], 'w_in_p': _permute_w_in(w_in[l]),
          'cmp_k': _compress_params(cmp_pe_k[l], cmp_w1_k[l], cmp_w2_k[l]),
          'cmp_v': _compress_params(cmp_pe_v[l], cmp_w1_v[l], cmp_w2_v[l]),
          'w_nsa_out_b': bf(w_nsa_out[l]), 'conv_w': conv_w[l], 'conv_b': conv_b[l],
          'conv_ln_g': conv_ln_g[l], 'conv_ln_b': conv_ln_b[l],
          'w_conv_out_b': bf(w_conv_out[l]), 'w_out_b': bf(w_out[l]), 'norm_ffn': norm_ffn[l],
          'peer_wq_b': bf(peer_wq[l]), 'peer_k1_b': bf(peer_k1[l]), 'peer_k2_b': bf(peer_k2[l]),
          'peer_u_b': bf(peer_u[l]), 'peer_v_b': bf(peer_v[l]), 'norm_final': norm_final}
    past = {'cmp_k': cache_cmp_k[l], 'cmp_v': cache_cmp_v[l], 'sel_k': cache_sel_k[l],
            'sel_v': cache_sel_v[l], 'win_k': cache_win_k[l], 'win_v': cache_win_v[l],
            'conv': state_conv[l]}
    y_prompt, st_p = _layer(x_prompt, lp, None, None, True, 1024)
    y_sample, st_s = _layer(x_sample, lp, past, page_table, False, 256)
    p_ck, p_cv, p_sk, p_sv, p_wk, p_wv, p_conv = [a[None] for a in st_p]
    s_ck, s_cv, s_sk, s_sv, s_wk, s_wv, s_conv = [a[None] for a in st_s]
    return (y_prompt, y_sample, p_ck, s_ck, p_cv, s_cv, p_sk, s_sk, p_sv, s_sv,
            p_wk, s_wk, p_wv, s_wv, p_conv, s_conv)
```

```python
import functools

import jax
import jax.numpy as jnp
import numpy as np
from jax import lax
from jax.experimental import pallas as pl
from jax.experimental.pallas import tpu as pltpu

D_MODEL = 2048
BATCH = 2
SEQ = 4096
DEPTH = 1
DEC_BATCH = 32
DEC_SEQ = 8
PAST_LEN = 8192
PAGE_SIZE = 128
N_HEADS = 16
HEAD_DIM = 128
N_KV_HEADS = 4
GROUP = N_HEADS // N_KV_HEADS
CMP_STRIDE = 16
CMP_LEN = 2 * CMP_STRIDE
SEL_LEN = 64
SEL_SHIFT = SEL_LEN.bit_length() - 1
SEL_TOPK = 16
WINDOW = 512
WIN_Q_BLOCK = 128
SEL_Q_BLOCK = 64
FORCE_BONUS = 1e4
CONV_CH = D_MODEL // 2
CONV_WIDTH = 31
PEER_HEADS = 8
PEER_NKEYS = 128
PEER_DKEY = 256
PEER_TOPK = 16
PEER_CHUNK = 128
QW = N_HEADS * HEAD_DIM
KVW = N_KV_HEADS * HEAD_DIM
N_GATES = 3 * N_HEADS
SCALE = HEAD_DIM ** -0.5
EPS = 1e-6
NEG = -1e30

VMEM_LIMIT_BYTES = 48 * 1024 * 1024
VMEM_LIMIT_PEER_BYTES = 58 * 1024 * 1024

GATE_PAD = 256
COL_Q = 0
COL_KC = COL_Q + QW
COL_VC = COL_KC + KVW
COL_KS = COL_VC + KVW
COL_VS = COL_KS + KVW
COL_KW = COL_VS + KVW
COL_VW = COL_KW + KVW
COL_GLU_A = COL_VW + KVW
COL_GLU_B = COL_GLU_A + CONV_CH
COL_GM_A = COL_GLU_B + CONV_CH
COL_GM_B = COL_GM_A + D_MODEL
COL_G = COL_GM_B + D_MODEL
N_COLS = COL_G + GATE_PAD


def _permute_w_in(w_in):
    g0 = QW + 6 * KVW
    g1 = g0 + N_GATES
    parts = [w_in[:, :g0], w_in[:, g1:], w_in[:, g0:g1],
             jnp.zeros((w_in.shape[0], GATE_PAD - N_GATES), w_in.dtype)]
    return jnp.concatenate(parts, axis=1).astype(jnp.bfloat16)


def _proj_in_kernel(x_ref, g_ref, w_ref, o_ref, xn_ref):
    @pl.when(pl.program_id(1) == 0)
    def _():
        x = x_ref[...]
        ms = jnp.mean(x * x, axis=-1, keepdims=True)
        xn_ref[...] = (x * lax.rsqrt(ms + EPS) * g_ref[...]).astype(jnp.bfloat16)

    o_ref[...] = jnp.dot(xn_ref[...], w_ref[...], preferred_element_type=jnp.float32)


def _proj_in(x, gain, w_bf16, tm, tn):
    n, d = x.shape
    nc = w_bf16.shape[1]
    return pl.pallas_call(
        _proj_in_kernel,
        out_shape=jax.ShapeDtypeStruct((n, nc), jnp.float32),
        grid=(n // tm, nc // tn),
        in_specs=[pl.BlockSpec((tm, d), lambda i, j: (i, 0)),
                  pl.BlockSpec((1, d), lambda i, j: (0, 0)),
                  pl.BlockSpec((d, tn), lambda i, j: (0, j))],
        out_specs=pl.BlockSpec((tm, tn), lambda i, j: (i, j)),
        scratch_shapes=[pltpu.VMEM((tm, d), jnp.bfloat16)],
        compiler_params=pltpu.CompilerParams(
            dimension_semantics=("parallel", "arbitrary"),
            vmem_limit_bytes=VMEM_LIMIT_BYTES),
        name="proj_in",
    )(x, gain.reshape(1, d), w_bf16)


_NT = (((1,), (1,)), ((), ()))
MASK_BIAS = -1e9


def _stack_heads(q):
    return jnp.concatenate([q[:, r * HEAD_DIM:(r + 1) * HEAD_DIM] for r in range(GROUP)], axis=0)


def _unstack_heads(o, tq):
    return jnp.concatenate([o[r * tq:(r + 1) * tq] for r in range(GROUP)], axis=1)


def _cmp_select_kernel(q_ref, kc_ref, vc_ref, ct_ref, ocmp_ref, bias_ref, *, tq):
    t0 = pl.program_id(2) * tq
    q4 = _stack_heads(q_ref[...]).astype(jnp.bfloat16)
    kc = kc_ref[0, 0].astype(jnp.bfloat16)
    vc = vc_ref[0, 0].astype(jnp.bfloat16)
    s = lax.dot_general(q4, kc, _NT, preferred_element_type=jnp.float32) * SCALE
    t = t0 + (lax.broadcasted_iota(jnp.int32, s.shape, 0) & (tq - 1))
    i = lax.broadcasted_iota(jnp.int32, s.shape, 1)
    valid = (i * CMP_STRIDE + (CMP_LEN - 1)) <= t
    s = jnp.where(valid, s, NEG)
    m = jnp.max(s, axis=-1, keepdims=True)
    e = jnp.where(valid, jnp.exp(s - m), 0.0)
    p = (e / jnp.maximum(jnp.sum(e, axis=-1, keepdims=True), 1e-30)).astype(jnp.bfloat16)
    ocmp_ref[...] = _unstack_heads(jnp.dot(p, vc, preferred_element_type=jnp.float32), tq)

    ct = ct_ref[...]
    n_sel = ct.shape[0]
    pslc = lax.dot_general(ct, p[0:tq], _NT, preferred_element_type=jnp.float32)
    for r in range(1, GROUP):
        pslc = pslc + lax.dot_general(ct, p[r * tq:(r + 1) * tq], _NT,
                                      preferred_element_type=jnp.float32)
    j = lax.broadcasted_iota(jnp.int32, (n_sel, tq), 0)
    cur = (t0 + lax.broadcasted_iota(jnp.int32, (n_sel, tq), 1)) >> SEL_SHIFT
    forced = (j == 0) | (j == cur) | (j == cur - 1)
    score = jnp.where(j <= cur, pslc + jnp.where(forced, FORCE_BONUS, 0.0), NEG)
    rank = jnp.zeros((n_sel, tq), jnp.float32)
    for jp in range(n_sel):
        sj = score[jp:jp + 1, :]
        rank = rank + jnp.where(sj > score, 1.0, jnp.where((sj == score) & (j > jp), 1.0, 0.0))
    bias = jnp.where((rank < SEL_TOPK) & (j <= cur), 0.0, MASK_BIAS)
    bias = jnp.concatenate([bias, jnp.zeros((HEAD_DIM - n_sel, tq), jnp.float32)], axis=0)
    bias_ref[0, 0] = bias.T.astype(jnp.bfloat16)


def _cmp_select(z, kcmp, vcmp, ct, B, T, tq):
    n_cp = kcmp.shape[2]
    nq = T // tq
    return pl.pallas_call(
        functools.partial(_cmp_select_kernel, tq=tq),
        out_shape=(jax.ShapeDtypeStruct((B * T, QW), jnp.float32),
                   jax.ShapeDtypeStruct((B, N_KV_HEADS, T, HEAD_DIM), jnp.bfloat16)),
        grid=(B, N_KV_HEADS, nq),
        in_specs=[pl.BlockSpec((tq, GROUP * HEAD_DIM), lambda b, g, qi: (b * nq + qi, g)),
                  pl.BlockSpec((1, 1, n_cp, HEAD_DIM), lambda b, g, qi: (b, g, 0, 0)),
                  pl.BlockSpec((1, 1, n_cp, HEAD_DIM), lambda b, g, qi: (b, g, 0, 0)),
                  pl.BlockSpec(ct.shape, lambda b, g, qi: (0, 0))],
        out_specs=(pl.BlockSpec((tq, GROUP * HEAD_DIM), lambda b, g, qi: (b * nq + qi, g)),
                   pl.BlockSpec((1, 1, tq, HEAD_DIM), lambda b, g, qi: (b, g, qi, 0))),
        compiler_params=pltpu.CompilerParams(
            dimension_semantics=("parallel", "parallel", "arbitrary"),
            vmem_limit_bytes=VMEM_LIMIT_BYTES),
        name="cmp_select",
    )(z, kcmp, vcmp, ct)


def _sel_attn_kernel(q_ref, bias_ref, k_ref, v_ref, o_ref, kaug, vt, acc_sc, *, tq, kc):
    qi = pl.program_id(2)
    t0 = qi * tq
    T = k_ref.shape[0]
    nq = GROUP * tq

    @pl.when(qi == 0)
    def _():
        kaug[:, 0:HEAD_DIM] = k_ref[...].astype(jnp.bfloat16)
        blk = lax.broadcasted_iota(jnp.int32, (T, HEAD_DIM), 0) >> SEL_SHIFT
        col = lax.broadcasted_iota(jnp.int32, (T, HEAD_DIM), 1)
        kaug[:, HEAD_DIM:2 * HEAD_DIM] = jnp.where(blk == col, 1.0, 0.0).astype(jnp.bfloat16)
        for c in range(T // kc):
            vt[c] = v_ref[c * kc:(c + 1) * kc, :].T.astype(jnp.bfloat16)

    q = q_ref[...]
    bias = bias_ref[0, 0]
    qa = jnp.concatenate(
        [jnp.concatenate([q[:, r * HEAD_DIM:(r + 1) * HEAD_DIM].astype(jnp.bfloat16), bias], axis=1)
         for r in range(GROUP)], axis=0)
    acc_sc[...] = jnp.zeros(acc_sc.shape, jnp.float32)
    t = t0 + (lax.broadcasted_iota(jnp.int32, (kc, nq), 1) & (tq - 1))
    key = lax.broadcasted_iota(jnp.int32, (kc, nq), 0)

    def step(c, m_old, l_old, causal):
        k0 = pl.multiple_of(c * kc, kc)
        s = lax.dot_general(kaug[pl.ds(k0, kc), :], qa, _NT,
                            preferred_element_type=jnp.float32) * SCALE
        if causal:
            s = jnp.where(k0 + key <= t, s, NEG)
        m_new = jnp.maximum(m_old, jnp.max(s, axis=0, keepdims=True))
        alpha = jnp.exp(m_old - m_new)
        p = jnp.exp(s - m_new)
        l_new = alpha * l_old + jnp.sum(p, axis=0, keepdims=True)
        acc_sc[...] = alpha * acc_sc[...] + jnp.dot(vt[c], p.astype(jnp.bfloat16),
                                                    preferred_element_type=jnp.float32)
        return m_new, l_new

    last = (t0 + tq - 1) // kc
    init = (jnp.full((1, nq), NEG, jnp.float32), jnp.zeros((1, nq), jnp.float32))
    def pair(i, ml):
        m, l = step(2 * i, ml[0], ml[1], False)
        return step(2 * i + 1, m, l, False)

    m, l = lax.fori_loop(0, last // 2, pair, init)
    m, l = lax.fori_loop(2 * (last // 2), last, lambda c, ml: step(c, ml[0], ml[1], False), (m, l))
    m, l = step(last, m, l, True)
    o_ref[...] = _unstack_heads((acc_sc[...] / jnp.maximum(l, 1e-30)).T, tq)


def _sel_attn(z, bias, B, T, tq, kc):
    nq = T // tq
    ck = COL_KS // HEAD_DIM
    cv = COL_VS // HEAD_DIM
    return pl.pallas_call(
        functools.partial(_sel_attn_kernel, tq=tq, kc=kc),
        out_shape=jax.ShapeDtypeStruct((B * T, QW), jnp.float32),
        grid=(B, N_KV_HEADS, nq),
        in_specs=[pl.BlockSpec((tq, GROUP * HEAD_DIM), lambda b, g, qi: (b * nq + qi, g)),
                  pl.BlockSpec((1, 1, tq, HEAD_DIM), lambda b, g, qi: (b, g, qi, 0)),
                  pl.BlockSpec((T, HEAD_DIM), lambda b, g, qi: (b, ck + g)),
                  pl.BlockSpec((T, HEAD_DIM), lambda b, g, qi: (b, cv + g))],
        out_specs=pl.BlockSpec((tq, GROUP * HEAD_DIM), lambda b, g, qi: (b * nq + qi, g)),
        scratch_shapes=[pltpu.VMEM((T, 2 * HEAD_DIM), jnp.bfloat16),
                        pltpu.VMEM((T // kc, HEAD_DIM, kc), jnp.bfloat16),
                        pltpu.VMEM((HEAD_DIM, GROUP * tq), jnp.float32)],
        compiler_params=pltpu.CompilerParams(
            dimension_semantics=("parallel", "parallel", "arbitrary"),
            vmem_limit_bytes=VMEM_LIMIT_BYTES),
        name="sel_attn",
    )(z, bias, z, z)


def _win_attn_kernel(q_ref, k_ref, v_ref, o_ref, kpad, vt, *, tq):
    qi = pl.program_id(2)
    t0 = pl.multiple_of(qi * tq, tq)
    T = k_ref.shape[0]
    n_pad = WINDOW // tq
    n_span = n_pad + 1
    span = n_span * tq

    @pl.when(qi == 0)
    def _():
        kpad[0:WINDOW, :] = jnp.zeros((WINDOW, HEAD_DIM), jnp.bfloat16)
        kpad[WINDOW:WINDOW + T, :] = k_ref[...].astype(jnp.bfloat16)
        for c in range(n_pad):
            vt[c] = jnp.zeros((HEAD_DIM, tq), jnp.bfloat16)
        for c in range(T // tq):
            vt[n_pad + c] = v_ref[c * tq:(c + 1) * tq, :].T.astype(jnp.bfloat16)

    q4 = _stack_heads(q_ref[...]).astype(jnp.bfloat16)
    s = lax.dot_general(kpad[pl.ds(t0, span), :], q4, _NT,
                        preferred_element_type=jnp.float32) * SCALE
    kpos = t0 - WINDOW + lax.broadcasted_iota(jnp.int32, s.shape, 0)
    t = t0 + (lax.broadcasted_iota(jnp.int32, s.shape, 1) & (tq - 1))
    d = t - kpos
    valid = (kpos >= 0) & (d >= 0) & (d < WINDOW)
    s = jnp.where(valid, s, NEG)
    m = jnp.max(s, axis=0, keepdims=True)
    e = jnp.where(valid, jnp.exp(s - m), 0.0)
    p = (e / jnp.maximum(jnp.sum(e, axis=0, keepdims=True), 1e-30)).astype(jnp.bfloat16)
    o_t = None
    for c in range(n_span):
        part = jnp.dot(vt[qi + c], p[c * tq:(c + 1) * tq], preferred_element_type=jnp.float32)
        o_t = part if o_t is None else o_t + part
    o_ref[...] = _unstack_heads(o_t.T, tq)


def _win_attn(z, B, T, tq):
    nq = T // tq
    ck = COL_KW // HEAD_DIM
    cv = COL_VW // HEAD_DIM
    return pl.pallas_call(
        functools.partial(_win_attn_kernel, tq=tq),
        out_shape=jax.ShapeDtypeStruct((B * T, QW), jnp.float32),
        grid=(B, N_KV_HEADS, nq),
        in_specs=[pl.BlockSpec((tq, GROUP * HEAD_DIM), lambda b, g, qi: (b * nq + qi, g)),
                  pl.BlockSpec((T, HEAD_DIM), lambda b, g, qi: (b, ck + g)),
                  pl.BlockSpec((T, HEAD_DIM), lambda b, g, qi: (b, cv + g))],
        out_specs=pl.BlockSpec((tq, GROUP * HEAD_DIM), lambda b, g, qi: (b * nq + qi, g)),
        scratch_shapes=[pltpu.VMEM((T + WINDOW, HEAD_DIM), jnp.bfloat16),
                        pltpu.VMEM(((T + WINDOW) // tq, HEAD_DIM, tq), jnp.bfloat16)],
        compiler_params=pltpu.CompilerParams(
            dimension_semantics=("parallel", "parallel", "arbitrary"),
            vmem_limit_bytes=VMEM_LIMIT_BYTES),
        name="win_attn",
    )(z, z, z)


PAGES_PER_STEP = 16
CHUNKS_PER_PAGE = PAGE_SIZE // CMP_STRIDE
PAGE_ROWS = PAGE_SIZE * N_KV_HEADS
DEC_ROWS = N_HEADS * DEC_SEQ
assert DEC_ROWS == 128 and DEC_SEQ == 8


def _head_rows(ref, g, n_tokens):
    return ref[pl.ds(g, n_tokens, stride=N_KV_HEADS), :]


def _dec_rows(q):
    return jnp.concatenate([q[:, (g * GROUP + r) * HEAD_DIM:(g * GROUP + r + 1) * HEAD_DIM]
                            for r in range(GROUP) for g in range(N_KV_HEADS)], axis=0)


def _dec_cols(o):
    return jnp.concatenate(
        [o[(r * N_KV_HEADS + g) * DEC_SEQ:(r * N_KV_HEADS + g + 1) * DEC_SEQ]
         for g in range(N_KV_HEADS) for r in range(GROUP)], axis=1)


def _dec_row_ids():
    row = lax.broadcasted_iota(jnp.int32, (DEC_ROWS, 1), 0)
    return (row >> 3) & (N_KV_HEADS - 1), row & (DEC_SEQ - 1)


def _group_scores(qall, keys_of, row_g):
    out = None
    for g in range(N_KV_HEADS):
        s = lax.dot_general(qall, keys_of(g), _NT, preferred_element_type=jnp.float32)
        s = jnp.where(row_g == g, s, 0.0)
        out = s if out is None else out + s
    return out


def _group_values(p, vals_of, row_g):
    out = None
    for g in range(N_KV_HEADS):
        pg = jnp.where(row_g == g, p, 0.0).astype(jnp.bfloat16)
        o = jnp.dot(pg, vals_of(g), preferred_element_type=jnp.float32)
        out = o if out is None else out + o
    return out


TAP_ROWS = 2 * N_KV_HEADS


def _chunk_hidden(page_refs, w1_ref):
    chunk_rows = CMP_STRIDE * N_KV_HEADS
    n_rows = len(page_refs) * CHUNKS_PER_PAGE * TAP_ROWS
    odd = (lax.broadcasted_iota(jnp.int32, (n_rows, 1), 0) & N_KV_HEADS) != 0
    pieces = []
    for p in range(CMP_STRIDE // 2):
        x = jnp.concatenate(
            [pg[c * chunk_rows + p * TAP_ROWS:c * chunk_rows + (p + 1) * TAP_ROWS, :]
             for pg in page_refs for c in range(CHUNKS_PER_PAGE)], axis=0)
        pieces.append(jnp.where(odd, 0.0, x).astype(jnp.bfloat16))
        pieces.append(jnp.where(odd, x, 0.0).astype(jnp.bfloat16))
    w1 = w1_ref[...].reshape(CMP_STRIDE * HEAD_DIM, 2 * HEAD_DIM)
    return jnp.dot(jnp.concatenate(pieces, axis=1), w1, preferred_element_type=jnp.float32)


def _hidden_rows(h_sc, g, first_chunk, n):
    even = h_sc[pl.ds(first_chunk * TAP_ROWS + g, n, stride=TAP_ROWS), :]
    odd = h_sc[pl.ds(first_chunk * TAP_ROWS + N_KV_HEADS + g, n, stride=TAP_ROWS), :]
    return even + odd


def _store_hidden(h_lo, h_hi, rows, h):
    h_lo[rows, :] = h[:, 0:HEAD_DIM]
    h_hi[rows, :] = h[:, HEAD_DIM:2 * HEAD_DIM]


def _dec_cmp_kernel(pt_ref, *refs, past_len, n_sel):
    nps = PAGES_PER_STEP
    kp, vp = refs[0:nps], refs[nps:2 * nps]
    (kt_ref, vt_ref, q_ref, w1k_ref, w1v_ref, ck_ref, cv_ref, w2k_ref, w2v_ref, ct_ref,
     ocmp_ref, bias_ref, hk_lo, hk_hi, hv_lo, hv_hi, score_sc) = refs[2 * nps:]
    j = pl.program_id(1)
    rows = nps * CHUNKS_PER_PAGE * TAP_ROWS
    base = (past_len // PAGE_SIZE) * CHUNKS_PER_PAGE
    tail_end = (base + CHUNKS_PER_PAGE) * TAP_ROWS

    step_rows = pl.ds(pl.multiple_of(j * rows, rows), rows)
    for pages, w1_ref, h_lo, h_hi in ((kp, w1k_ref, hk_lo, hk_hi), (vp, w1v_ref, hv_lo, hv_hi)):
        _store_hidden(h_lo, h_hi, step_rows, _chunk_hidden(pages, w1_ref))

    @pl.when(j == pl.num_programs(1) - 1)
    def _():
        for t_ref, w1_ref, h_lo, h_hi in ((kt_ref, w1k_ref, hk_lo, hk_hi),
                                          (vt_ref, w1v_ref, hv_lo, hv_hi)):
            _store_hidden(h_lo, h_hi, slice(base * TAP_ROWS, tail_end), _chunk_hidden([t_ref], w1_ref))
            zeros = jnp.zeros((h_lo.shape[0] - tail_end, HEAD_DIM), jnp.float32)
            h_lo[tail_end:, :] = zeros
            h_hi[tail_end:, :] = zeros
        ncp = ct_ref.shape[1]
        kc, vc = [], []
        for g in range(N_KV_HEADS):
            for h_lo, h_hi, c_ref, w2_ref, dst in ((hk_lo, hk_hi, ck_ref, w2k_ref, kc),
                                                   (hv_lo, hv_hi, cv_ref, w2v_ref, vc)):
                hh = _hidden_rows(h_lo, g, 0, ncp) + _hidden_rows(h_hi, g, 1, ncp) + c_ref[...]
                dst.append(jnp.dot(jax.nn.gelu(hh).astype(jnp.bfloat16), w2_ref[...],
                                   preferred_element_type=jnp.float32).astype(jnp.bfloat16))
        qall = _dec_rows(q_ref[...]).astype(jnp.bfloat16)
        row_g, row_t = _dec_row_ids()
        s = _group_scores(qall, lambda g: kc[g], row_g) * SCALE
        i = lax.broadcasted_iota(jnp.int32, s.shape, 1)
        valid = (i * CMP_STRIDE + (CMP_LEN - 1)) <= (past_len + row_t)
        s = jnp.where(valid, s, NEG)
        m = jnp.max(s, axis=-1, keepdims=True)
        e = jnp.where(valid, jnp.exp(s - m), 0.0)
        p = e / jnp.maximum(jnp.sum(e, axis=-1, keepdims=True), 1e-30)
        ocmp_ref[...] = _dec_cols(_group_values(p, lambda g: vc[g], row_g))

        x = lax.dot_general(ct_ref[...], p.astype(jnp.bfloat16), _NT,
                            preferred_element_type=jnp.float32)
        quarter = DEC_ROWS // GROUP
        pslc = x
        for r in range(1, GROUP):
            pslc = pslc + pltpu.roll(x, r * quarter, 1)
        shape = pslc.shape
        jj = lax.broadcasted_iota(jnp.int32, shape, 0)
        cur = (past_len + (lax.broadcasted_iota(jnp.int32, shape, 1) & (DEC_SEQ - 1))) >> SEL_SHIFT
        forced = (jj == 0) | (jj == cur) | (jj == cur - 1)
        score = jnp.where(jj <= cur, pslc + jnp.where(forced, FORCE_BONUS, 0.0), NEG)
        score_sc[...] = score

        def rank_step(jp, rank):
            sj = score_sc[pl.ds(jp, 1), :]
            return rank + jnp.where(sj > score, 1.0, jnp.where((sj == score) & (jj > jp), 1.0, 0.0))

        rank = lax.fori_loop(0, n_sel, rank_step, jnp.zeros(shape, jnp.float32))
        bias = jnp.where((rank < SEL_TOPK) & (jj <= cur), 0.0, MASK_BIAS)
        pad = jnp.full((bias_ref.shape[2] - shape[0], shape[1]), MASK_BIAS, jnp.float32)
        bias_ref[0] = jnp.concatenate([bias, pad], axis=0).T.astype(jnp.bfloat16)


PAGE_BLOCK = (PAGE_ROWS, HEAD_DIM)


def _page_specs(n):
    return [pl.BlockSpec(PAGE_BLOCK, lambda b, j, pt, i=i: (pt[b, j * PAGES_PER_STEP + i], 0))
            for i in range(n)]


_TAIL_SPEC = pl.BlockSpec(PAGE_BLOCK, lambda b, j, pt: (b, 0))


def _dec_cmp_select(page_table, pool_k, pool_v, tail_k, tail_v, z, w1k, w1v, ck, cv, w2k, w2v, ct,
                    past_len):
    nb = page_table.shape[0]
    n_pages = past_len // PAGE_SIZE
    n_sel, ncp = ct.shape
    n_sel_real = -(-(past_len + DEC_SEQ) // SEL_LEN)
    full2 = lambda a: pl.BlockSpec(a.shape, lambda b, j, pt: (0, 0))
    full3 = lambda a: pl.BlockSpec(a.shape, lambda b, j, pt: (0, 0, 0))
    tail = _TAIL_SPEC
    n_blocks = 2 * HEAD_DIM
    grid_spec = pltpu.PrefetchScalarGridSpec(
        num_scalar_prefetch=1,
        grid=(nb, n_pages // PAGES_PER_STEP),
        in_specs=(_page_specs(PAGES_PER_STEP) + _page_specs(PAGES_PER_STEP)
                  + [tail, tail, pl.BlockSpec((DEC_SEQ, QW), lambda b, j, pt: (b, 0)),
                     full3(w1k), full3(w1v), full2(ck), full2(cv), full2(w2k), full2(w2v), full2(ct)]),
        out_specs=(pl.BlockSpec((DEC_SEQ, QW), lambda b, j, pt: (b, 0)),
                   pl.BlockSpec((1, DEC_ROWS, n_blocks), lambda b, j, pt: (b, 0, 0))),
        scratch_shapes=[pltpu.VMEM(((ncp + 8) * TAP_ROWS, HEAD_DIM), jnp.float32)] * 4
        + [pltpu.VMEM((n_sel, DEC_ROWS), jnp.float32)])
    return pl.pallas_call(
        functools.partial(_dec_cmp_kernel, past_len=past_len, n_sel=n_sel_real),
        out_shape=(jax.ShapeDtypeStruct((nb * DEC_SEQ, QW), jnp.float32),
                   jax.ShapeDtypeStruct((nb, DEC_ROWS, n_blocks), jnp.bfloat16)),
        grid_spec=grid_spec,
        compiler_params=pltpu.CompilerParams(
            dimension_semantics=("parallel", "arbitrary"),
            vmem_limit_bytes=VMEM_LIMIT_BYTES),
        name="dec_cmp_select",
    )(page_table, *([pool_k] * PAGES_PER_STEP), *([pool_v] * PAGES_PER_STEP), tail_k, tail_v, z,
      w1k, w1v, ck, cv, w2k, w2v, ct)


def _dec_sel_kernel(pt_ref, *refs, past_len):
    nps = PAGES_PER_STEP
    kp, vp = refs[0:nps], refs[nps:2 * nps]
    kt_ref, vt_ref, q_ref, bias_ref, o_ref, m_sc, l_sc, acc_sc = refs[2 * nps:]
    j = pl.program_id(1)
    qall = _dec_rows(q_ref[...]).astype(jnp.bfloat16)
    row_g, row_t = _dec_row_ids()
    selcols = bias_ref[0]

    @pl.when(j == 0)
    def _():
        m_sc[...] = jnp.full(m_sc.shape, NEG, jnp.float32)
        l_sc[...] = jnp.zeros(l_sc.shape, jnp.float32)
        acc_sc[...] = jnp.zeros(acc_sc.shape, jnp.float32)

    def update(k_refs, v_refs, first_blk, causal):
        grp = lambda ref, g: _head_rows(ref, g, PAGE_SIZE).astype(jnp.bfloat16)
        s = jnp.concatenate([_group_scores(qall, functools.partial(grp, kr), row_g)
                             for kr in k_refs], axis=1) * SCALE
        nk = s.shape[1]
        n_blocks = selcols.shape[1]
        blk = first_blk + (lax.broadcasted_iota(jnp.int32, (n_blocks, nk), 1) >> SEL_SHIFT)
        onehot = jnp.where(lax.broadcasted_iota(jnp.int32, (n_blocks, nk), 0) == blk, 1.0, 0.0)
        s = s + jnp.dot(selcols, onehot.astype(jnp.bfloat16), preferred_element_type=jnp.float32)
        if causal:
            kpos = first_blk * SEL_LEN + lax.broadcasted_iota(jnp.int32, s.shape, 1)
            s = jnp.where(kpos <= past_len + row_t, s, NEG)
        m_old = m_sc[...]
        m_new = jnp.maximum(m_old, jnp.max(s, axis=-1, keepdims=True))
        alpha = jnp.exp(m_old - m_new)
        p = jnp.exp(s - m_new)
        l_sc[...] = alpha * l_sc[...] + jnp.sum(p, axis=-1, keepdims=True)
        pv = None
        for n, vr in enumerate(v_refs):
            o = _group_values(p[:, n * PAGE_SIZE:(n + 1) * PAGE_SIZE], functools.partial(grp, vr), row_g)
            pv = o if pv is None else pv + o
        acc_sc[...] = alpha * acc_sc[...] + pv
        m_sc[...] = m_new

    update(kp, vp, j * (nps * PAGE_SIZE // SEL_LEN), False)

    @pl.when(j == pl.num_programs(1) - 1)
    def _():
        update([kt_ref], [vt_ref], past_len // SEL_LEN, True)
        o_ref[...] = _dec_cols(acc_sc[...] / jnp.maximum(l_sc[...], 1e-30))


def _dec_sel_attn(page_table, pool_k, pool_v, tail_k, tail_v, z, bias, past_len):
    nb = page_table.shape[0]
    n_pages = past_len // PAGE_SIZE
    tail = _TAIL_SPEC
    grid_spec = pltpu.PrefetchScalarGridSpec(
        num_scalar_prefetch=1,
        grid=(nb, n_pages // PAGES_PER_STEP),
        in_specs=(_page_specs(PAGES_PER_STEP) + _page_specs(PAGES_PER_STEP)
                  + [tail, tail, pl.BlockSpec((DEC_SEQ, QW), lambda b, j, pt: (b, 0)),
                     pl.BlockSpec((1,) + bias.shape[1:], lambda b, j, pt: (b, 0, 0))]),
        out_specs=pl.BlockSpec((DEC_SEQ, QW), lambda b, j, pt: (b, 0)),
        scratch_shapes=[pltpu.VMEM((DEC_ROWS, 1), jnp.float32),
                        pltpu.VMEM((DEC_ROWS, 1), jnp.float32),
                        pltpu.VMEM((DEC_ROWS, HEAD_DIM), jnp.float32)])
    return pl.pallas_call(
        functools.partial(_dec_sel_kernel, past_len=past_len),
        out_shape=jax.ShapeDtypeStruct((nb * DEC_SEQ, QW), jnp.float32),
        grid_spec=grid_spec,
        compiler_params=pltpu.CompilerParams(
            dimension_semantics=("parallel", "arbitrary"),
            vmem_limit_bytes=VMEM_LIMIT_BYTES),
        name="dec_sel_attn",
    )(page_table, *([pool_k] * PAGES_PER_STEP), *([pool_v] * PAGES_PER_STEP), tail_k, tail_v, z, bias)


def _dec_win_kernel(q_ref, wk_ref, wv_ref, kt_ref, vt_ref, o_ref, nwk_ref, nwv_ref, *, past_len):
    wl = wk_ref.shape[0] // N_KV_HEADS
    qall = _dec_rows(q_ref[...]).astype(jnp.bfloat16)
    row_g, row_t = _dec_row_ids()

    def cat(w_ref, t_ref, g):
        return jnp.concatenate([_head_rows(w_ref, g, wl), _head_rows(t_ref, g, PAGE_SIZE)],
                               axis=0).astype(jnp.bfloat16)

    s = _group_scores(qall, functools.partial(cat, wk_ref, kt_ref), row_g) * SCALE
    kpos = past_len - wl + lax.broadcasted_iota(jnp.int32, s.shape, 1)
    d = past_len + row_t - kpos
    valid = (d >= 0) & (d < WINDOW)
    s = jnp.where(valid, s, NEG)
    m = jnp.max(s, axis=-1, keepdims=True)
    e = jnp.where(valid, jnp.exp(s - m), 0.0)
    p = e / jnp.maximum(jnp.sum(e, axis=-1, keepdims=True), 1e-30)
    o_ref[...] = _dec_cols(_group_values(p, functools.partial(cat, wv_ref, vt_ref), row_g))
    keep = (wl - DEC_SEQ) * N_KV_HEADS
    for w_ref, t_ref, n_ref in ((wk_ref, kt_ref, nwk_ref), (wv_ref, vt_ref, nwv_ref)):
        n_ref[0:keep, :] = w_ref[wl * N_KV_HEADS - keep:wl * N_KV_HEADS, :]
        n_ref[keep:wl * N_KV_HEADS, :] = t_ref[0:DEC_SEQ * N_KV_HEADS, :]


def _dec_win_attn(z, win_k, win_v, tail_k, tail_v, nb, past_len):
    cache = pl.BlockSpec((win_k.shape[0] // nb, HEAD_DIM), lambda b: (b, 0))
    tail = pl.BlockSpec(PAGE_BLOCK, lambda b: (b, 0))
    rows = pl.BlockSpec((DEC_SEQ, QW), lambda b: (b, 0))
    return pl.pallas_call(
        functools.partial(_dec_win_kernel, past_len=past_len),
        out_shape=(jax.ShapeDtypeStruct((nb * DEC_SEQ, QW), jnp.float32),
                   jax.ShapeDtypeStruct(win_k.shape, jnp.float32),
                   jax.ShapeDtypeStruct(win_v.shape, jnp.float32)),
        grid=(nb,),
        in_specs=[rows, cache, cache, tail, tail],
        out_specs=(rows, cache, cache),
        compiler_params=pltpu.CompilerParams(
            dimension_semantics=("parallel",),
            vmem_limit_bytes=VMEM_LIMIT_BYTES),
        name="dec_win_attn",
    )(z, win_k, win_v, tail_k, tail_v)


def _compress_kernel(xk_ref, xv_ref, w1k_ref, w1v_ref, ck_ref, cv_ref, w2k_ref, w2v_ref,
                     ko_ref, vo_ref, h_sc):
    nch = xk_ref.shape[0] // CMP_STRIDE
    h_sc[nch:nch + 8, :] = jnp.zeros((8, 2 * HEAD_DIM), jnp.float32)
    for x_ref, w1_ref, c_ref, w2_ref, o_ref in ((xk_ref, w1k_ref, ck_ref, w2k_ref, ko_ref),
                                                (xv_ref, w1v_ref, cv_ref, w2v_ref, vo_ref)):
        acc = None
        for l in range(CMP_STRIDE):
            x = x_ref[pl.ds(l, nch, stride=CMP_STRIDE), :].astype(jnp.bfloat16)
            h = jnp.dot(x, w1_ref[l], preferred_element_type=jnp.float32)
            acc = h if acc is None else acc + h
        h_sc[0:nch, :] = acc
        hh = h_sc[0:nch, 0:HEAD_DIM] + h_sc[1:nch + 1, HEAD_DIM:2 * HEAD_DIM] + c_ref[...]
        o_ref[0, 0] = jnp.dot(jax.nn.gelu(hh).astype(jnp.bfloat16), w2_ref[...],
                              preferred_element_type=jnp.float32)


def _compress_seq(z, w1k, w1v, ck, cv, w2k, w2v, B, T):
    nch = T // CMP_STRIDE
    ck0 = COL_KC // HEAD_DIM
    cv0 = COL_VC // HEAD_DIM
    full2 = lambda a: pl.BlockSpec(a.shape, lambda b, g: (0, 0))
    full3 = lambda a: pl.BlockSpec(a.shape, lambda b, g: (0, 0, 0))
    out = jax.ShapeDtypeStruct((B, N_KV_HEADS, nch, HEAD_DIM), jnp.float32)
    ospec = pl.BlockSpec((1, 1, nch, HEAD_DIM), lambda b, g: (b, g, 0, 0))
    return pl.pallas_call(
        _compress_kernel,
        out_shape=(out, out),
        grid=(B, N_KV_HEADS),
        in_specs=[pl.BlockSpec((T, HEAD_DIM), lambda b, g: (b, ck0 + g)),
                  pl.BlockSpec((T, HEAD_DIM), lambda b, g: (b, cv0 + g)),
                  full3(w1k), full3(w1v), full2(ck), full2(cv), full2(w2k), full2(w2v)],
        out_specs=(ospec, ospec),
        scratch_shapes=[pltpu.VMEM((nch + 8, 2 * HEAD_DIM), jnp.float32)],
        compiler_params=pltpu.CompilerParams(
            dimension_semantics=("parallel", "parallel"),
            vmem_limit_bytes=VMEM_LIMIT_BYTES),
        name="compress_seq",
    )(z, z, w1k, w1v, ck, cv, w2k, w2v)


CONV_HALO = 32
LANES = 128


def _conv_taps(xs, w_ref, first, rows, y_sc):
    for cb in range(CONV_CH // LANES):
        sl = slice(cb * LANES, (cb + 1) * LANES)
        acc = None
        for k in range(CONV_WIDTH):
            term = w_ref[k:k + 1, sl] * xs[first + k:first + k + rows, sl]
            acc = term if acc is None else acc + term
        y_sc[:, sl] = acc


def _conv_post(y, cb_ref, lg_ref, lb_ref):
    y = y + cb_ref[...]
    mu = jnp.mean(y, axis=-1, keepdims=True)
    var = jnp.mean(jnp.square(y - mu), axis=-1, keepdims=True)
    yn = (y - mu) * lax.rsqrt(var + EPS) * lg_ref[...] + lb_ref[...]
    return yn * jax.nn.sigmoid(yn)


def _conv_kernel(a_ref, b_ref, ap_ref, bp_ref, w_ref, cb_ref, lg_ref, lb_ref, y_ref, tail_ref,
                 xs, y_sc, *, tm):
    i = pl.program_id(1)
    prev = ap_ref[...] * jax.nn.sigmoid(bp_ref[...])
    xs[0:CONV_HALO, :] = jnp.where(i == 0, 0.0, prev)
    xs[CONV_HALO:CONV_HALO + tm, :] = a_ref[...] * jax.nn.sigmoid(b_ref[...])
    _conv_taps(xs, w_ref, CONV_HALO - (CONV_WIDTH - 1), tm, y_sc)
    y_ref[...] = _conv_post(y_sc[...], cb_ref, lg_ref, lb_ref).astype(y_ref.dtype)

    @pl.when(i == pl.num_programs(1) - 1)
    def _():
        tail_ref[0] = xs[tm:tm + CONV_HALO, :]


def _conv_seq(z, w, cb, lg, lb, B, T, tm):
    nt = T // tm
    ca = COL_GLU_A // CONV_CH
    cbk = COL_GLU_B // CONV_CH
    r = tm // CONV_HALO
    prev = lambda c: pl.BlockSpec((CONV_HALO, CONV_CH),
                                  lambda b, i: (jnp.maximum((b * nt + i) * r - 1, 0), c))
    cur = lambda c: pl.BlockSpec((tm, CONV_CH), lambda b, i: (b * nt + i, c))
    vec = pl.BlockSpec((1, CONV_CH), lambda b, i: (0, 0))
    return pl.pallas_call(
        functools.partial(_conv_kernel, tm=tm),
        out_shape=(jax.ShapeDtypeStruct((B * T, CONV_CH), jnp.bfloat16),
                   jax.ShapeDtypeStruct((B, CONV_HALO, CONV_CH), jnp.float32)),
        grid=(B, nt),
        in_specs=[cur(ca), cur(cbk), prev(ca), prev(cbk),
                  pl.BlockSpec(w.shape, lambda b, i: (0, 0)), vec, vec, vec],
        out_specs=(pl.BlockSpec((tm, CONV_CH), lambda b, i: (b * nt + i, 0)),
                   pl.BlockSpec((1, CONV_HALO, CONV_CH), lambda b, i: (b, 0, 0))),
        scratch_shapes=[pltpu.VMEM((tm + CONV_HALO, CONV_CH), jnp.float32),
                        pltpu.VMEM((tm, CONV_CH), jnp.float32)],
        compiler_params=pltpu.CompilerParams(
            dimension_semantics=("parallel", "arbitrary"),
            vmem_limit_bytes=VMEM_LIMIT_BYTES),
        name="conv_seq",
    )(z, z, z, z, w, cb.reshape(1, -1), lg.reshape(1, -1), lb.reshape(1, -1))


def _dec_conv_kernel(a_ref, b_ref, st_ref, w_ref, cb_ref, lg_ref, lb_ref, y_ref, ns_ref, xs, y_sc):
    nbuf = CONV_WIDTH - 1
    xs[0:nbuf, :] = st_ref[0]
    xs[nbuf:nbuf + DEC_SEQ, :] = a_ref[...] * jax.nn.sigmoid(b_ref[...])
    _conv_taps(xs, w_ref, 0, DEC_SEQ, y_sc)
    y_ref[...] = _conv_post(y_sc[...], cb_ref, lg_ref, lb_ref)
    ns_ref[0] = xs[DEC_SEQ:DEC_SEQ + nbuf, :]


def _dec_conv(z, state, w, cb, lg, lb):
    nb, nbuf, _ = state.shape
    ca = COL_GLU_A // CONV_CH
    cbk = COL_GLU_B // CONV_CH
    vec = pl.BlockSpec((1, CONV_CH), lambda b: (0, 0))
    st = pl.BlockSpec((1, nbuf, CONV_CH), lambda b: (b, 0, 0))
    return pl.pallas_call(
        _dec_conv_kernel,
        out_shape=(jax.ShapeDtypeStruct((nb * DEC_SEQ, CONV_CH), jnp.float32),
                   jax.ShapeDtypeStruct(state.shape, jnp.float32)),
        grid=(nb,),
        in_specs=[pl.BlockSpec((DEC_SEQ, CONV_CH), lambda b: (b, ca)),
                  pl.BlockSpec((DEC_SEQ, CONV_CH), lambda b: (b, cbk)),
                  st, pl.BlockSpec(w.shape, lambda b: (0, 0)), vec, vec, vec],
        out_specs=(pl.BlockSpec((DEC_SEQ, CONV_CH), lambda b: (b, 0)), st),
        scratch_shapes=[pltpu.VMEM((nbuf + DEC_SEQ + 2, CONV_CH), jnp.float32),
                        pltpu.VMEM((DEC_SEQ, CONV_CH), jnp.float32)],
        compiler_params=pltpu.CompilerParams(
            dimension_semantics=("parallel",),
            vmem_limit_bytes=VMEM_LIMIT_BYTES),
        name="dec_conv",
    )(z, z, state, w, cb.reshape(1, -1), lg.reshape(1, -1), lb.reshape(1, -1))


def _merge_kernel(oc_ref, os_ref, ow_ref, g_ref, convy_ref, ga_ref, gb_ref, wn_ref, wc_ref,
                  o_ref, onsa_ref):
    @pl.when(pl.program_id(1) == 0)
    def _():
        gz = jax.nn.sigmoid(g_ref[...])
        for h in range(N_HEADS):
            sl = slice(h * HEAD_DIM, (h + 1) * HEAD_DIM)
            o = (gz[:, 3 * h:3 * h + 1] * oc_ref[:, sl]
                 + gz[:, 3 * h + 1:3 * h + 2] * os_ref[:, sl]
                 + gz[:, 3 * h + 2:3 * h + 3] * ow_ref[:, sl])
            onsa_ref[:, sl] = o.astype(jnp.bfloat16)

    a = jnp.dot(onsa_ref[...], wn_ref[...], preferred_element_type=jnp.float32)
    b = jnp.dot(convy_ref[...].astype(jnp.bfloat16), wc_ref[...],
                preferred_element_type=jnp.float32)
    o_ref[...] = (jax.nn.sigmoid(ga_ref[...]) * a
                  + jax.nn.sigmoid(gb_ref[...]) * b).astype(o_ref.dtype)


def _merge(o_cmp, o_sel, o_win, conv_y, z, wn_bf16, wc_bf16, tm, tn):
    n = o_cmp.shape[0]
    ja = COL_GM_A // tn
    jb = COL_GM_B // tn
    jg = COL_G // GATE_PAD
    row = lambda i, j: (i, 0)
    return pl.pallas_call(
        _merge_kernel,
        out_shape=jax.ShapeDtypeStruct((n, D_MODEL), jnp.bfloat16),
        grid=(n // tm, D_MODEL // tn),
        in_specs=[pl.BlockSpec((tm, QW), row),
                  pl.BlockSpec((tm, QW), row),
                  pl.BlockSpec((tm, QW), row),
                  pl.BlockSpec((tm, GATE_PAD), lambda i, j: (i, jg)),
                  pl.BlockSpec((tm, CONV_CH), row),
                  pl.BlockSpec((tm, tn), lambda i, j: (i, ja + j)),
                  pl.BlockSpec((tm, tn), lambda i, j: (i, jb + j)),
                  pl.BlockSpec((QW, tn), lambda i, j: (0, j)),
                  pl.BlockSpec((CONV_CH, tn), lambda i, j: (0, j))],
        out_specs=pl.BlockSpec((tm, tn), lambda i, j: (i, j)),
        scratch_shapes=[pltpu.VMEM((tm, QW), jnp.bfloat16)],
        compiler_params=pltpu.CompilerParams(
            dimension_semantics=("parallel", "arbitrary"),
            vmem_limit_bytes=VMEM_LIMIT_BYTES),
        name="merge",
    )(o_cmp, o_sel, o_win, z, conv_y, z, z, wn_bf16, wc_bf16)


def _out_proj_kernel(m_ref, x_ref, w_ref, g_ref, h_ref, hn_ref):
    h = x_ref[...] + jnp.dot(m_ref[...], w_ref[...], preferred_element_type=jnp.float32)
    h_ref[...] = h
    ms = jnp.mean(h * h, axis=-1, keepdims=True)
    hn_ref[...] = (h * lax.rsqrt(ms + EPS) * g_ref[...]).astype(hn_ref.dtype)


def _out_proj(merged, x, w_bf16, gain, tm):
    n = x.shape[0]
    return pl.pallas_call(
        _out_proj_kernel,
        out_shape=(jax.ShapeDtypeStruct((n, D_MODEL), jnp.float32),
                   jax.ShapeDtypeStruct((n, D_MODEL), jnp.bfloat16)),
        grid=(n // tm,),
        in_specs=[pl.BlockSpec((tm, D_MODEL), lambda i: (i, 0)),
                  pl.BlockSpec((tm, D_MODEL), lambda i: (i, 0)),
                  pl.BlockSpec((D_MODEL, D_MODEL), lambda i: (0, 0)),
                  pl.BlockSpec((1, D_MODEL), lambda i: (0, 0))],
        out_specs=(pl.BlockSpec((tm, D_MODEL), lambda i: (i, 0)),
                   pl.BlockSpec((tm, D_MODEL), lambda i: (i, 0))),
        compiler_params=pltpu.CompilerParams(
            dimension_semantics=("parallel",),
            vmem_limit_bytes=VMEM_LIMIT_BYTES),
        name="out_proj",
    )(merged, x, w_bf16, gain.reshape(1, D_MODEL))


PEER_HALF = PEER_DKEY // 2
PEER_SEL = PEER_HEADS * PEER_TOPK
PEER_TOPK_SHIFT = PEER_TOPK.bit_length() - 1
CAND_B = PEER_TOPK // 2
CAND_B_SHIFT = CAND_B.bit_length() - 1
N_CAND = PEER_TOPK + (PEER_TOPK - 1) * CAND_B
GATE_HALF = PEER_NKEYS // 2
GATE_PITCH = GATE_HALF + 8


def _topk_chains(s_sc, v_sc, i_sc, k):
    n_chain, r, n = s_sc.shape
    row = lax.broadcasted_iota(jnp.int32, (r, n), 0)

    def body(i, carry):
        for c in range(n_chain):
            s = s_sc[c]
            m = jnp.max(s, axis=0, keepdims=True)
            j = jnp.min(jnp.where(s == m, row, r), axis=0, keepdims=True)
            s_sc[c] = jnp.where(row == j, -jnp.inf, s)
            v_sc[c, pl.ds(i, 1), :] = m
            i_sc[c, pl.ds(i, 1), :] = j
        return carry

    lax.fori_loop(0, k, body, 0)


def _pick_rows(idx, table):
    out = jnp.zeros(idx.shape, table.dtype)
    for a in range(PEER_TOPK):
        out = jnp.where(idx == a, table[a:a + 1, :], out)
    return out


def _peer_route_kernel(hn_ref, wq_ref, k1_ref, k2_ref, ia_ref, ib_ref, gt_ref,
                       s_sc, v_sc, i_sc, c_sc, cv_sc, ci_sc, a_sc, b_sc, g_sc):
    qh = jnp.dot(hn_ref[...], wq_ref[...], preferred_element_type=jnp.float32).astype(jnp.bfloat16)
    for h in range(PEER_HEADS):
        q1 = qh[:, h * PEER_DKEY:h * PEER_DKEY + PEER_HALF]
        q2 = qh[:, h * PEER_DKEY + PEER_HALF:(h + 1) * PEER_DKEY]
        s_sc[2 * h] = lax.dot_general(k1_ref[h], q1, _NT, preferred_element_type=jnp.float32)
        s_sc[2 * h + 1] = lax.dot_general(k2_ref[h], q2, _NT, preferred_element_type=jnp.float32)
    _topk_chains(s_sc, v_sc, i_sc, PEER_TOPK)
    for h in range(PEER_HEADS):
        v1, v2 = v_sc[2 * h], v_sc[2 * h + 1]
        c_sc[h] = jnp.concatenate([v1[0:1, :] + v2]
                                  + [v1[a:a + 1, :] + v2[0:CAND_B, :] for a in range(1, PEER_TOPK)],
                                  axis=0)
    _topk_chains(c_sc, cv_sc, ci_sc, PEER_TOPK)
    for h in range(PEER_HEADS):
        sc, pos = cv_sc[h], ci_sc[h]
        rest = pos - PEER_TOPK
        first = pos < PEER_TOPK
        ia = _pick_rows(jnp.where(first, 0, 1 + (rest >> CAND_B_SHIFT)), i_sc[2 * h])
        ib = _pick_rows(jnp.where(first, pos, rest & (CAND_B - 1)), i_sc[2 * h + 1])
        e = jnp.exp(sc - sc[0:1, :])
        gate = e / jnp.sum(e, axis=0, keepdims=True)
        rows = slice(h * PEER_TOPK, (h + 1) * PEER_TOPK)
        a_sc[rows, :] = ia.astype(jnp.float32)
        b_sc[rows, :] = ib.astype(jnp.float32)
        g_sc[rows, :] = gate
    ia_ref[...] = a_sc[...].T
    ib_ref[...] = b_sc[...].T
    gt_ref[...] = g_sc[...].T


def _peer_route(hn, wq_bf16, k1_bf16, k2_bf16, tm):
    n, d = hn.shape
    out = jax.ShapeDtypeStruct((n, PEER_SEL), jnp.float32)
    ospec = pl.BlockSpec((tm, PEER_SEL), lambda i: (i, 0))
    return pl.pallas_call(
        _peer_route_kernel,
        out_shape=(out, out, out),
        grid=(n // tm,),
        in_specs=[pl.BlockSpec((tm, d), lambda i: (i, 0)),
                  pl.BlockSpec(wq_bf16.shape, lambda i: (0, 0)),
                  pl.BlockSpec(k1_bf16.shape, lambda i: (0, 0, 0)),
                  pl.BlockSpec(k2_bf16.shape, lambda i: (0, 0, 0))],
        out_specs=(ospec, ospec, ospec),
        scratch_shapes=[pltpu.VMEM((2 * PEER_HEADS, PEER_NKEYS, tm), jnp.float32),
                        pltpu.VMEM((2 * PEER_HEADS, PEER_TOPK, tm), jnp.float32),
                        pltpu.VMEM((2 * PEER_HEADS, PEER_TOPK, tm), jnp.int32),
                        pltpu.VMEM((PEER_HEADS, N_CAND, tm), jnp.float32),
                        pltpu.VMEM((PEER_HEADS, PEER_TOPK, tm), jnp.float32),
                        pltpu.VMEM((PEER_HEADS, PEER_TOPK, tm), jnp.int32)]
        + [pltpu.VMEM((PEER_SEL, tm), jnp.float32)] * 3,
        compiler_params=pltpu.CompilerParams(
            dimension_semantics=("parallel",),
            vmem_limit_bytes=VMEM_LIMIT_BYTES),
        name="peer_route",
    )(hn, wq_bf16, k1_bf16, k2_bf16)


def _peer_ffn_kernel(hn_ref, ia_ref, ib_ref, gt_ref, u_ref, v_ref, o_ref, w_sc, *, tm, n_i1):
    c = pl.program_id(1)

    @pl.when(c == 0)
    def _():
        o_ref[...] = jnp.zeros(o_ref.shape, jnp.float32)

    steps_per_half = GATE_HALF // n_i1

    @pl.when(c % steps_per_half == 0)
    def _():
        first = (c // steps_per_half) * GATE_HALF
        bf = lambda x: x.astype(jnp.float32).astype(jnp.bfloat16)
        rows_a = bf(first + lax.broadcasted_iota(jnp.int32, (GATE_HALF, PEER_SEL), 0))
        rows_b = bf(lax.broadcasted_iota(jnp.int32, (PEER_NKEYS, PEER_SEL), 0))
        one = jnp.ones((1, PEER_SEL), jnp.bfloat16)
        zero = jnp.zeros((1, PEER_SEL), jnp.bfloat16)

        def token(n, carry):
            a_row = ia_ref[pl.ds(n, 1), :].astype(jnp.bfloat16)
            b_row = ib_ref[pl.ds(n, 1), :].astype(jnp.bfloat16)
            g_row = gt_ref[pl.ds(n, 1), :]
            g_hi = g_row.astype(jnp.bfloat16)
            g_lo = (g_row - g_hi.astype(jnp.float32)).astype(jnp.bfloat16)
            oa = jnp.where(rows_a == a_row, one, zero)
            hit_b = rows_b == b_row
            gb_hi = jnp.where(hit_b, g_hi, zero)
            gb_lo = jnp.where(hit_b, g_lo, zero)
            w = lax.dot_general(jnp.concatenate([oa, oa], axis=1),
                                jnp.concatenate([gb_hi, gb_lo], axis=1), _NT,
                                preferred_element_type=jnp.float32)
            w_sc[pl.ds(pl.multiple_of(n * GATE_PITCH, 8), GATE_HALF), :] = w
            return carry

        lax.fori_loop(0, tm, token, 0, unroll=8)

    act = jax.nn.gelu(lax.dot_general(hn_ref[...], u_ref[...], _NT,
                                      preferred_element_type=jnp.float32))
    row0 = (c * n_i1) & (GATE_HALF - 1)
    parts = []
    for j in range(n_i1):
        wj = w_sc[pl.ds(row0 + j, tm, stride=GATE_PITCH), :]
        parts.append((act[:, j * PEER_NKEYS:(j + 1) * PEER_NKEYS] * wj).astype(jnp.bfloat16))
    o_ref[...] += jnp.dot(jnp.concatenate(parts, axis=1), v_ref[...],
                          preferred_element_type=jnp.float32)


def _peer_ffn(hn, ia, ib, gt, u_bf16, v_bf16, tm, n_i1):
    n, d = hn.shape
    ec = n_i1 * PEER_NKEYS
    assert GATE_HALF % n_i1 == 0
    row = lambda i, c: (i, 0)
    return pl.pallas_call(
        functools.partial(_peer_ffn_kernel, tm=tm, n_i1=n_i1),
        out_shape=jax.ShapeDtypeStruct((n, d), jnp.float32),
        grid=(n // tm, u_bf16.shape[0] // ec),
        in_specs=[pl.BlockSpec((tm, d), row),
                  pl.BlockSpec((tm, PEER_SEL), row),
                  pl.BlockSpec((tm, PEER_SEL), row),
                  pl.BlockSpec((tm, PEER_SEL), row),
                  pl.BlockSpec((ec, d), lambda i, c: (c, 0)),
                  pl.BlockSpec((ec, d), lambda i, c: (c, 0))],
        out_specs=pl.BlockSpec((tm, d), row),
        scratch_shapes=[pltpu.VMEM((tm * GATE_PITCH, PEER_NKEYS), jnp.float32)],
        compiler_params=pltpu.CompilerParams(
            dimension_semantics=("parallel", "arbitrary"),
            vmem_limit_bytes=VMEM_LIMIT_PEER_BYTES),
        name="peer_ffn",
    )(hn, ia, ib, gt, u_bf16, v_bf16)


def _residual_norm_kernel(h_ref, f_ref, g_ref, y_ref):
    y = h_ref[...] + f_ref[...]
    ms = jnp.mean(y * y, axis=-1, keepdims=True)
    y_ref[...] = y * lax.rsqrt(ms + EPS) * g_ref[...]


def _residual_norm(h, f, gain, tm):
    n, d = h.shape
    row = pl.BlockSpec((tm, d), lambda i: (i, 0))
    return pl.pallas_call(
        _residual_norm_kernel,
        out_shape=jax.ShapeDtypeStruct((n, d), jnp.float32),
        grid=(n // tm,),
        in_specs=[row, row, pl.BlockSpec((1, d), lambda i: (0, 0))],
        out_specs=row,
        compiler_params=pltpu.CompilerParams(
            dimension_semantics=("parallel",),
            vmem_limit_bytes=VMEM_LIMIT_BYTES),
        name="residual_norm",
    )(h, f, gain.reshape(1, d))


def _rmsnorm(x, g):
    xf = x.astype(jnp.float32)
    y = xf * lax.rsqrt(jnp.mean(xf * xf, axis=-1, keepdims=True) + EPS)
    return (y * g.astype(jnp.float32)).astype(x.dtype)


def _layernorm(x, g, b):
    xf = x.astype(jnp.float32)
    mu = jnp.mean(xf, axis=-1, keepdims=True)
    var = jnp.mean(jnp.square(xf - mu), axis=-1, keepdims=True)
    return ((xf - mu) * lax.rsqrt(var + EPS) * g.astype(jnp.float32) + b.astype(jnp.float32)).astype(x.dtype)


def _masked_probs(s, mask):
    s = jnp.where(mask, s, NEG)
    m = jnp.max(s, axis=-1, keepdims=True)
    e = jnp.where(mask, jnp.exp(s - m), 0.0)
    return e / jnp.maximum(jnp.sum(e, axis=-1, keepdims=True), 1e-30)


def _attn_probs(q, k, mask):
    s = jnp.einsum('...tgrd,...kgd->...grtk', q, k).astype(jnp.float32) * SCALE
    return _masked_probs(s, mask[..., None, None, :, :])


def _attn_out(p, v):
    return jnp.einsum('...grtk,...kgd->...tgrd', p.astype(v.dtype), v)


def _gather_pages(pool, page_table):
    g = pool[page_table]
    return g.reshape(page_table.shape[0], -1, pool.shape[2], pool.shape[3])


def _compress(kv, pe, w1, w2):
    B, L, G, D = kv.shape
    ch = kv.reshape(B, L // CMP_STRIDE, CMP_STRIDE, G, D)
    h_lo = jnp.einsum('bjlgd,ldh->bjgh', ch, w1[:CMP_STRIDE])
    h_hi = jnp.einsum('bjlgd,ldh->bjgh', ch, w1[CMP_STRIDE:])
    h = h_lo[:, :-1] + h_hi[:, 1:] + jnp.einsum('ld,ldh->h', pe, w1)
    return jnp.einsum('bigh,he->bige', jax.nn.gelu(h), w2)


def _cmp_to_sel(n_cmp, n_sel):
    cs = jnp.arange(n_cmp)[:, None] * CMP_STRIDE
    ss = jnp.arange(n_sel)[None, :] * SEL_LEN
    ov = jnp.clip(jnp.minimum(cs + CMP_LEN, ss + SEL_LEN) - jnp.maximum(cs, ss), 0, None)
    return ov.astype(jnp.float32) / CMP_LEN


def _sel_attend(q, kb, vb, idx, valid, qpos):
    B, T, G, R, D = q.shape
    bi = jnp.arange(B)[:, None, None, None]
    gi = jnp.arange(G)[None, :, None, None]
    kg = kb[bi, idx, :, gi].reshape(B, G, T, -1, D)
    vg = vb[bi, idx, :, gi].reshape(B, G, T, -1, D)
    kpos = idx[..., None] * SEL_LEN + jnp.arange(SEL_LEN)
    mask = (valid[..., None] & (kpos <= qpos[:, None, None])).reshape(B, G, T, -1)
    s = jnp.einsum('btgrd,bgtkd->bgrtk', q, kg).astype(jnp.float32) * SCALE
    p = _masked_probs(s, mask[:, :, None])
    return jnp.einsum('bgrtk,bgtkd->btgrd', p.astype(vg.dtype), vg)


def _nsa_cmp_sel(q, k_c, v_c, k_s, v_s, qpos, lp, sweep_queries):
    B, T, G, R, D = q.shape
    L = k_c.shape[1]
    L_pad = -(-L // SEL_LEN) * SEL_LEN
    pad = ((0, 0), (0, L_pad - L), (0, 0), (0, 0))
    k_c, v_c, k_s, v_s = jnp.pad(k_c, pad), jnp.pad(v_c, pad), jnp.pad(k_s, pad), jnp.pad(v_s, pad)
    k_cmp = _compress(k_c, lp['cmp_pe_k'], lp['cmp_w1_k'], lp['cmp_w2_k'])
    v_cmp = _compress(v_c, lp['cmp_pe_v'], lp['cmp_w1_v'], lp['cmp_w2_v'])
    n_cmp = k_cmp.shape[1]
    cmp_end = jnp.arange(n_cmp) * CMP_STRIDE + CMP_LEN - 1
    p_cmp = _attn_probs(q, k_cmp, cmp_end[None, :] <= qpos[:, None])
    o_cmp = _attn_out(p_cmp, v_cmp)
    n_sel = L_pad // SEL_LEN
    p_slc = jnp.einsum('bgrti,ij->bgtj', p_cmp, _cmp_to_sel(n_cmp, n_sel))
    blk = jnp.arange(n_sel)[None, :]
    cur = (qpos // SEL_LEN)[:, None]
    forced = (blk == 0) | (blk == cur) | (blk == cur - 1)
    score = jnp.where(blk <= cur, p_slc + jnp.where(forced, FORCE_BONUS, 0.0), NEG)
    top_s, top_i = lax.top_k(score, min(SEL_TOPK, n_sel))
    valid = top_s > 0.5 * NEG
    kb = k_s.reshape(B, n_sel, SEL_LEN, G, D)
    vb = v_s.reshape(B, n_sel, SEL_LEN, G, D)
    if sweep_queries:
        nq = T // SEL_Q_BLOCK
        xs = (q.reshape(B, nq, SEL_Q_BLOCK, G, R, D).swapaxes(0, 1),
              top_i.reshape(B, G, nq, SEL_Q_BLOCK, -1).transpose(2, 0, 1, 3, 4),
              valid.reshape(B, G, nq, SEL_Q_BLOCK, -1).transpose(2, 0, 1, 3, 4),
              qpos.reshape(nq, SEL_Q_BLOCK))
        o = lax.map(lambda a: _sel_attend(a[0], kb, vb, a[1], a[2], a[3]), xs)
        o_sel = o.swapaxes(0, 1).reshape(B, T, G, R, D)
    else:
        xs = (q[:, None], kb[:, None], vb[:, None], top_i[:, None], valid[:, None])
        o_sel = lax.map(lambda a: _sel_attend(a[0], a[1], a[2], a[3], a[4], qpos)[0], xs)
    return o_cmp, o_sel


def _window_banded(q, k, v):
    B, T, G, R, D = q.shape
    nb = T // WIN_Q_BLOCK
    pad = ((0, 0), (WINDOW, 0), (0, 0), (0, 0))
    kidx = jnp.arange(nb)[:, None] * WIN_Q_BLOCK + jnp.arange(WINDOW + WIN_Q_BLOCK)[None, :]
    kblk = jnp.pad(k, pad)[:, kidx]
    vblk = jnp.pad(v, pad)[:, kidx]
    kpos = kidx - WINDOW
    qpos = jnp.arange(T).reshape(nb, WIN_Q_BLOCK)
    d = qpos[:, :, None] - kpos[:, None, :]
    mask = (kpos[:, None, :] >= 0) & (d >= 0) & (d < WINDOW)
    o = _attn_out(_attn_probs(q.reshape(B, nb, WIN_Q_BLOCK, G, R, D), kblk, mask), vblk)
    return o.reshape(B, T, G, R, D)


def _window_dense(q, k, v, qpos, kpos):
    d = qpos[:, None] - kpos[None, :]
    return _attn_out(_attn_probs(q, k, (d >= 0) & (d < WINDOW)), v)


def _conv_module(u, buf, w_dw, b_dw, ln_g, ln_b):
    xp = jnp.concatenate([buf, u], axis=1)
    y = lax.conv_general_dilated(xp, w_dw[:, None, :], (1,), 'VALID',
                                 dimension_numbers=('NWC', 'WIO', 'NWC'),
                                 feature_group_count=CONV_CH) + b_dw
    return jax.nn.silu(_layernorm(y, ln_g, ln_b)), xp[:, -(CONV_WIDTH - 1):]


def _split_z(z, B, T):
    kv = lambda c: z[:, c:c + KVW].reshape(B, T, N_KV_HEADS, HEAD_DIM)
    q = z[:, COL_Q:COL_Q + QW].reshape(B, T, N_KV_HEADS, GROUP, HEAD_DIM)
    a = z[:, COL_GLU_A:COL_GLU_A + CONV_CH]
    b = z[:, COL_GLU_B:COL_GLU_B + CONV_CH]
    u = (a * jax.nn.sigmoid(b)).reshape(B, T, CONV_CH)
    return q, kv(COL_KC), kv(COL_VC), kv(COL_KS), kv(COL_VS), kv(COL_KW), kv(COL_VW), u


def _cmp_to_sel_t(n_cmp, n_sel, n_cmp_pad, n_sel_pad=None):
    cs = np.arange(n_cmp)[None, :] * CMP_STRIDE
    ss = np.arange(n_sel)[:, None] * SEL_LEN
    ov = np.clip(np.minimum(cs + CMP_LEN, ss + SEL_LEN) - np.maximum(cs, ss), 0, None)
    ct = np.zeros((n_sel_pad or n_sel, n_cmp_pad), np.float32)
    ct[:n_sel, :n_cmp] = ov.astype(np.float32) / CMP_LEN
    return jnp.asarray(ct, jnp.bfloat16)


def _compress_params(pe, w1, w2):
    w1cat = jnp.concatenate([w1[:CMP_STRIDE], w1[CMP_STRIDE:]], axis=-1).astype(jnp.bfloat16)
    c = jnp.einsum('ld,ldh->h', pe, w1).reshape(1, -1)
    return w1cat, c, w2.astype(jnp.bfloat16)


def _tail_page(z, col, nb):
    x = z[:, col:col + KVW].reshape(nb, DEC_SEQ * N_KV_HEADS, HEAD_DIM)
    x = jnp.pad(x, ((0, 0), (0, PAGE_ROWS - DEC_SEQ * N_KV_HEADS), (0, 0)))
    return x.reshape(nb * PAGE_ROWS, HEAD_DIM)


def _pad_cmp(x_cmp, n_pad):
    x = jnp.transpose(x_cmp, (0, 2, 1, 3))
    return jnp.pad(x, ((0, 0), (0, 0), (0, n_pad - x.shape[2]), (0, 0)))


def _layer(x, lp, past, page_table, prompt, tm):
    B, T, _ = x.shape
    n = B * T
    x2 = x.reshape(n, D_MODEL)
    z = _proj_in(x2, lp['norm_mix'], lp['w_in_p'], tm, 768)
    kv = lambda c: z[:, c:c + KVW].reshape(B, T, N_KV_HEADS, HEAD_DIM)
    k_c, v_c, k_s, v_s = kv(COL_KC), kv(COL_VC), kv(COL_KS), kv(COL_VS)
    cmp_k, cmp_v = lp['cmp_k'], lp['cmp_v']
    conv = (lp['conv_w'], lp['conv_b'], lp['conv_ln_g'], lp['conv_ln_b'])
    if prompt:
        n_ch = T // CMP_STRIDE
        ct = _cmp_to_sel_t(n_ch - 1, T // SEL_LEN, n_ch)
        kcmp, vcmp = _compress_seq(z, cmp_k[0], cmp_v[0], cmp_k[1], cmp_v[1], cmp_k[2], cmp_v[2], B, T)
        o_cmp, bias = _cmp_select(z, kcmp, vcmp, ct, B, T, 128)
        o_sel = _sel_attn(z, bias, B, T, 256, 256)
        o_win = _win_attn(z, B, T, 128)
        wl = min(WINDOW, T)
        new_wk, new_wv = kv(COL_KW)[:, T - wl:], kv(COL_VW)[:, T - wl:]
        conv_y, u_tail = _conv_seq(z, *conv, B, T, 256)
        new_conv = u_tail[:, CONV_HALO - (CONV_WIDTH - 1):]
    else:
        as_rows = lambda a: a.reshape(-1, HEAD_DIM)
        n_ch = -(-(PAST_LEN + T) // SEL_LEN) * SEL_LEN // CMP_STRIDE
        n_sel = -(-(PAST_LEN + T) // SEL_LEN)
        ct = _cmp_to_sel_t(n_ch - 1, n_sel, -(-n_ch // LANES) * LANES, -(-n_sel // 8) * 8)
        o_cmp, bias = _dec_cmp_select(
            page_table, as_rows(past['cmp_k']), as_rows(past['cmp_v']),
            _tail_page(z, COL_KC, B), _tail_page(z, COL_VC, B), z,
            cmp_k[0], cmp_v[0], cmp_k[1], cmp_v[1], cmp_k[2], cmp_v[2], ct, PAST_LEN)
        o_sel = _dec_sel_attn(page_table, as_rows(past['sel_k']), as_rows(past['sel_v']),
                              _tail_page(z, COL_KS, B), _tail_page(z, COL_VS, B), z, bias, PAST_LEN)
        wl = past['win_k'].shape[1]
        o_win, new_wk, new_wv = _dec_win_attn(
            z, as_rows(past['win_k']), as_rows(past['win_v']),
            _tail_page(z, COL_KW, B), _tail_page(z, COL_VW, B), B, PAST_LEN)
        new_wk = new_wk.reshape(B, wl, N_KV_HEADS, HEAD_DIM)
        new_wv = new_wv.reshape(B, wl, N_KV_HEADS, HEAD_DIM)
        conv_y, new_conv = _dec_conv(z, past['conv'], *conv)
    merged = _merge(o_cmp, o_sel, o_win, conv_y, z,
                    lp['w_nsa_out_b'], lp['w_conv_out_b'], 256, 512)
    h, hn = _out_proj(merged, x2, lp['w_out_b'], lp['norm_ffn'], 256)
    ia, ib, gt = _peer_route(hn, lp['peer_wq_b'], lp['peer_k1_b'], lp['peer_k2_b'], 256)
    ffn = _peer_ffn(hn, ia, ib, gt, lp['peer_u_b'], lp['peer_v_b'], min(n, 512), 8)
    y = _residual_norm(h, ffn, lp['norm_final'], 256)
    return y.reshape(B, T, D_MODEL), (k_c, v_c, k_s, v_s, new_wk, new_wv, new_conv)


def kernel(x_prompt, x_sample, cache_cmp_k, cache_cmp_v, cache_sel_k, cache_sel_v, cache_win_k, cache_win_v, state_conv, page_table, norm_mix, w_in, cmp_pe_k, cmp_w1_k, cmp_w2_k, cmp_pe_v, cmp_w1_v, cmp_w2_v, w_nsa_out, conv_w, conv_b, conv_ln_g, conv_ln_b, w_conv_out, w_out, norm_ffn, peer_wq, peer_k1, peer_k2, peer_u, peer_v, norm_final):
    assert DEPTH == 1
    bf = lambda w: w.astype(jnp.bfloat16)
    l = 0
    lp = {'norm_mix': norm_mix[l], 'w_in_p': _permute_w_in(w_in[l]),
          'cmp_k': _compress_params(cmp_pe_k[l], cmp_w1_k[l], cmp_w2_k[l]),
          'cmp_v': _compress_params(cmp_pe_v[l], cmp_w1_v[l], cmp_w2_v[l]),
          'w_nsa_out_b': bf(w_nsa_out[l]), 'conv_w': conv_w[l], 'conv_b': conv_b[l],
          'conv_ln_g': conv_ln_g[l], 'conv_ln_b': conv_ln_b[l],
          'w_conv_out_b': bf(w_conv_out[l]), 'w_out_b': bf(w_out[l]), 'norm_ffn': norm_ffn[l],
          'peer_wq_b': bf(peer_wq[l]), 'peer_k1_b': bf(peer_k1[l]), 'peer_k2_b': bf(peer_k2[l]),
          'peer_u_b': bf(peer_u[l]), 'peer_v_b': bf(peer_v[l]), 'norm_final': norm_final}
    past = {'cmp_k': cache_cmp_k[l], 'cmp_v': cache_cmp_v[l], 'sel_k': cache_sel_k[l],
            'sel_v': cache_sel_v[l], 'win_k': cache_win_k[l], 'win_v': cache_win_v[l],
            'conv': state_conv[l]}
    y_prompt, st_p = _layer(x_prompt, lp, None, None, True, 1024)
    y_sample, st_s = _layer(x_sample, lp, past, page_table, False, 256)
    p_ck, p_cv, p_sk, p_sv, p_wk, p_wv, p_conv = [a[None] for a in st_p]
    s_ck, s_cv, s_sk, s_sv, s_wk, s_wv, s_conv = [a[None] for a in st_s]
    return (y_prompt, y_sample, p_ck, s_ck, p_cv, s_cv, p_sk, s_sk, p_sv, s_sv,
            p_wk, s_wk, p_wv, s_wv, p_conv, s_conv)
```

```python
import functools

import jax
import jax.numpy as jnp
import numpy as np
from jax import lax
from jax.experimental import pallas as pl
from jax.experimental.pallas import tpu as pltpu

D_MODEL = 2048
BATCH = 2
SEQ = 4096
DEPTH = 1
DEC_BATCH = 32
DEC_SEQ = 8
PAST_LEN = 8192
PAGE_SIZE = 128
N_HEADS = 16
HEAD_DIM = 128
N_KV_HEADS = 4
GROUP = N_HEADS // N_KV_HEADS
CMP_STRIDE = 16
CMP_LEN = 2 * CMP_STRIDE
SEL_LEN = 64
SEL_SHIFT = SEL_LEN.bit_length() - 1
SEL_TOPK = 16
WINDOW = 512
WIN_Q_BLOCK = 128
SEL_Q_BLOCK = 64
FORCE_BONUS = 1e4
CONV_CH = D_MODEL // 2
CONV_WIDTH = 31
PEER_HEADS = 8
PEER_NKEYS = 128
PEER_DKEY = 256
PEER_TOPK = 16
PEER_CHUNK = 128
QW = N_HEADS * HEAD_DIM
KVW = N_KV_HEADS * HEAD_DIM
N_GATES = 3 * N_HEADS
SCALE = HEAD_DIM ** -0.5
EPS = 1e-6
NEG = -1e30

VMEM_LIMIT_BYTES = 48 * 1024 * 1024
VMEM_LIMIT_PEER_BYTES = 58 * 1024 * 1024

GATE_PAD = 256
COL_Q = 0
COL_KC = COL_Q + QW
COL_VC = COL_KC + KVW
COL_KS = COL_VC + KVW
COL_VS = COL_KS + KVW
COL_KW = COL_VS + KVW
COL_VW = COL_KW + KVW
COL_GLU_A = COL_VW + KVW
COL_GLU_B = COL_GLU_A + CONV_CH
COL_GM_A = COL_GLU_B + CONV_CH
COL_GM_B = COL_GM_A + D_MODEL
COL_G = COL_GM_B + D_MODEL
N_COLS = COL_G + GATE_PAD


def _permute_w_in(w_in):
    g0 = QW + 6 * KVW
    g1 = g0 + N_GATES
    parts = [w_in[:, :g0], w_in[:, g1:], w_in[:, g0:g1],
             jnp.zeros((w_in.shape[0], GATE_PAD - N_GATES), w_in.dtype)]
    return jnp.concatenate(parts, axis=1).astype(jnp.bfloat16)


def _proj_in_kernel(x_ref, g_ref, w_ref, o_ref, xn_ref):
    @pl.when(pl.program_id(1) == 0)
    def _():
        x = x_ref[...]
        ms = jnp.mean(x * x, axis=-1, keepdims=True)
        xn_ref[...] = (x * lax.rsqrt(ms + EPS) * g_ref[...]).astype(jnp.bfloat16)

    o_ref[...] = jnp.dot(xn_ref[...], w_ref[...], preferred_element_type=jnp.float32)


def _proj_in(x, gain, w_bf16, tm, tn):
    n, d = x.shape
    nc = w_bf16.shape[1]
    return pl.pallas_call(
        _proj_in_kernel,
        out_shape=jax.ShapeDtypeStruct((n, nc), jnp.float32),
        grid=(n // tm, nc // tn),
        in_specs=[pl.BlockSpec((tm, d), lambda i, j: (i, 0)),
                  pl.BlockSpec((1, d), lambda i, j: (0, 0)),
                  pl.BlockSpec((d, tn), lambda i, j: (0, j))],
        out_specs=pl.BlockSpec((tm, tn), lambda i, j: (i, j)),
        scratch_shapes=[pltpu.VMEM((tm, d), jnp.bfloat16)],
        compiler_params=pltpu.CompilerParams(
            dimension_semantics=("parallel", "arbitrary"),
            vmem_limit_bytes=VMEM_LIMIT_BYTES),
        name="proj_in",
    )(x, gain.reshape(1, d), w_bf16)


_NT = (((1,), (1,)), ((), ()))
BRANCH_DTYPE = jnp.bfloat16
MASK_BIAS = -1e9


def _stack_heads(q):
    return jnp.concatenate([q[:, r * HEAD_DIM:(r + 1) * HEAD_DIM] for r in range(GROUP)], axis=0)


def _unstack_heads(o, tq):
    return jnp.concatenate([o[r * tq:(r + 1) * tq] for r in range(GROUP)], axis=1)


def _cmp_select_kernel(q_ref, kc_ref, vc_ref, ct_ref, ocmp_ref, bias_ref, *, tq):
    t0 = pl.program_id(2) * tq
    q4 = _stack_heads(q_ref[...]).astype(jnp.bfloat16)
    kc = kc_ref[0, 0].astype(jnp.bfloat16)
    vc = vc_ref[0, 0].astype(jnp.bfloat16)
    s = lax.dot_general(q4, kc, _NT, preferred_element_type=jnp.float32) * SCALE
    t = t0 + (lax.broadcasted_iota(jnp.int32, s.shape, 0) & (tq - 1))
    i = lax.broadcasted_iota(jnp.int32, s.shape, 1)
    valid = (i * CMP_STRIDE + (CMP_LEN - 1)) <= t
    s = jnp.where(valid, s, NEG)
    m = jnp.max(s, axis=-1, keepdims=True)
    e = jnp.where(valid, jnp.exp(s - m), 0.0)
    p = (e / jnp.maximum(jnp.sum(e, axis=-1, keepdims=True), 1e-30)).astype(jnp.bfloat16)
    ocmp_ref[...] = _unstack_heads(jnp.dot(p, vc, preferred_element_type=jnp.float32),
                                   tq).astype(ocmp_ref.dtype)

    ct = ct_ref[...]
    n_sel = ct.shape[0]
    pslc = lax.dot_general(ct, p[0:tq], _NT, preferred_element_type=jnp.float32)
    for r in range(1, GROUP):
        pslc = pslc + lax.dot_general(ct, p[r * tq:(r + 1) * tq], _NT,
                                      preferred_element_type=jnp.float32)
    j = lax.broadcasted_iota(jnp.int32, (n_sel, tq), 0)
    cur = (t0 + lax.broadcasted_iota(jnp.int32, (n_sel, tq), 1)) >> SEL_SHIFT
    forced = (j == 0) | (j == cur) | (j == cur - 1)
    score = jnp.where(j <= cur, pslc + jnp.where(forced, FORCE_BONUS, 0.0), NEG)
    rank = jnp.zeros((n_sel, tq), jnp.float32)
    for jp in range(n_sel):
        sj = score[jp:jp + 1, :]
        rank = rank + jnp.where(sj > score, 1.0, jnp.where((sj == score) & (j > jp), 1.0, 0.0))
    bias = jnp.where((rank < SEL_TOPK) & (j <= cur), 0.0, MASK_BIAS)
    bias = jnp.concatenate([bias, jnp.zeros((HEAD_DIM - n_sel, tq), jnp.float32)], axis=0)
    bias_ref[0, 0] = bias.T.astype(jnp.bfloat16)


def _cmp_select(z, kcmp, vcmp, ct, B, T, tq):
    n_cp = kcmp.shape[2]
    nq = T // tq
    return pl.pallas_call(
        functools.partial(_cmp_select_kernel, tq=tq),
        out_shape=(jax.ShapeDtypeStruct((B * T, QW), BRANCH_DTYPE),
                   jax.ShapeDtypeStruct((B, N_KV_HEADS, T, HEAD_DIM), jnp.bfloat16)),
        grid=(B, N_KV_HEADS, nq),
        in_specs=[pl.BlockSpec((tq, GROUP * HEAD_DIM), lambda b, g, qi: (b * nq + qi, g)),
                  pl.BlockSpec((1, 1, n_cp, HEAD_DIM), lambda b, g, qi: (b, g, 0, 0)),
                  pl.BlockSpec((1, 1, n_cp, HEAD_DIM), lambda b, g, qi: (b, g, 0, 0)),
                  pl.BlockSpec(ct.shape, lambda b, g, qi: (0, 0))],
        out_specs=(pl.BlockSpec((tq, GROUP * HEAD_DIM), lambda b, g, qi: (b * nq + qi, g)),
                   pl.BlockSpec((1, 1, tq, HEAD_DIM), lambda b, g, qi: (b, g, qi, 0))),
        compiler_params=pltpu.CompilerParams(
            dimension_semantics=("parallel", "parallel", "arbitrary"),
            vmem_limit_bytes=VMEM_LIMIT_BYTES),
        name="cmp_select",
    )(z, kcmp, vcmp, ct)


def _sel_attn_kernel(q_ref, bias_ref, k_ref, v_ref, o_ref, kaug, vt, acc_sc, *, tq, kc):
    qi = pl.program_id(2)
    t0 = qi * tq
    T = k_ref.shape[0]
    nq = GROUP * tq

    @pl.when(qi == 0)
    def _():
        kaug[:, 0:HEAD_DIM] = k_ref[...].astype(jnp.bfloat16)
        blk = lax.broadcasted_iota(jnp.int32, (T, HEAD_DIM), 0) >> SEL_SHIFT
        col = lax.broadcasted_iota(jnp.int32, (T, HEAD_DIM), 1)
        kaug[:, HEAD_DIM:2 * HEAD_DIM] = jnp.where(blk == col, 1.0, 0.0).astype(jnp.bfloat16)
        for c in range(T // kc):
            vt[c] = v_ref[c * kc:(c + 1) * kc, :].T.astype(jnp.bfloat16)

    q = q_ref[...]
    bias = bias_ref[0, 0]
    qa = jnp.concatenate(
        [jnp.concatenate([q[:, r * HEAD_DIM:(r + 1) * HEAD_DIM].astype(jnp.bfloat16), bias], axis=1)
         for r in range(GROUP)], axis=0)
    acc_sc[...] = jnp.zeros(acc_sc.shape, jnp.float32)
    t = t0 + (lax.broadcasted_iota(jnp.int32, (kc, nq), 1) & (tq - 1))
    key = lax.broadcasted_iota(jnp.int32, (kc, nq), 0)

    def step(c, m_old, l_old, causal):
        k0 = pl.multiple_of(c * kc, kc)
        s = lax.dot_general(kaug[pl.ds(k0, kc), :], qa, _NT,
                            preferred_element_type=jnp.float32) * SCALE
        if causal:
            s = jnp.where(k0 + key <= t, s, NEG)
        m_new = jnp.maximum(m_old, jnp.max(s, axis=0, keepdims=True))
        alpha = jnp.exp(m_old - m_new)
        p = jnp.exp(s - m_new)
        l_new = alpha * l_old + jnp.sum(p, axis=0, keepdims=True)
        acc_sc[...] = alpha * acc_sc[...] + jnp.dot(vt[c], p.astype(jnp.bfloat16),
                                                    preferred_element_type=jnp.float32)
        return m_new, l_new

    last = (t0 + tq - 1) // kc
    init = (jnp.full((1, nq), NEG, jnp.float32), jnp.zeros((1, nq), jnp.float32))
    def pair(i, ml):
        m, l = step(2 * i, ml[0], ml[1], False)
        return step(2 * i + 1, m, l, False)

    m, l = lax.fori_loop(0, last // 2, pair, init)
    m, l = lax.fori_loop(2 * (last // 2), last, lambda c, ml: step(c, ml[0], ml[1], False), (m, l))
    m, l = step(last, m, l, True)
    o_ref[...] = _unstack_heads((acc_sc[...] / jnp.maximum(l, 1e-30)).T, tq).astype(o_ref.dtype)


def _sel_attn(z, bias, B, T, tq, kc):
    nq = T // tq
    ck = COL_KS // HEAD_DIM
    cv = COL_VS // HEAD_DIM
    return pl.pallas_call(
        functools.partial(_sel_attn_kernel, tq=tq, kc=kc),
        out_shape=jax.ShapeDtypeStruct((B * T, QW), BRANCH_DTYPE),
        grid=(B, N_KV_HEADS, nq),
        in_specs=[pl.BlockSpec((tq, GROUP * HEAD_DIM), lambda b, g, qi: (b * nq + qi, g)),
                  pl.BlockSpec((1, 1, tq, HEAD_DIM), lambda b, g, qi: (b, g, qi, 0)),
                  pl.BlockSpec((T, HEAD_DIM), lambda b, g, qi: (b, ck + g)),
                  pl.BlockSpec((T, HEAD_DIM), lambda b, g, qi: (b, cv + g))],
        out_specs=pl.BlockSpec((tq, GROUP * HEAD_DIM), lambda b, g, qi: (b * nq + qi, g)),
        scratch_shapes=[pltpu.VMEM((T, 2 * HEAD_DIM), jnp.bfloat16),
                        pltpu.VMEM((T // kc, HEAD_DIM, kc), jnp.bfloat16),
                        pltpu.VMEM((HEAD_DIM, GROUP * tq), jnp.float32)],
        compiler_params=pltpu.CompilerParams(
            dimension_semantics=("parallel", "parallel", "arbitrary"),
            vmem_limit_bytes=VMEM_LIMIT_BYTES),
        name="sel_attn",
    )(z, bias, z, z)


def _win_attn_kernel(q_ref, k_ref, v_ref, o_ref, kpad, vt, *, tq):
    qi = pl.program_id(2)
    t0 = pl.multiple_of(qi * tq, tq)
    T = k_ref.shape[0]
    n_pad = WINDOW // tq
    n_span = n_pad + 1
    span = n_span * tq

    @pl.when(qi == 0)
    def _():
        kpad[0:WINDOW, :] = jnp.zeros((WINDOW, HEAD_DIM), jnp.bfloat16)
        kpad[WINDOW:WINDOW + T, :] = k_ref[...].astype(jnp.bfloat16)
        for c in range(n_pad):
            vt[c] = jnp.zeros((HEAD_DIM, tq), jnp.bfloat16)
        for c in range(T // tq):
            vt[n_pad + c] = v_ref[c * tq:(c + 1) * tq, :].T.astype(jnp.bfloat16)

    q4 = _stack_heads(q_ref[...]).astype(jnp.bfloat16)
    s = lax.dot_general(kpad[pl.ds(t0, span), :], q4, _NT,
                        preferred_element_type=jnp.float32) * SCALE
    kpos = t0 - WINDOW + lax.broadcasted_iota(jnp.int32, s.shape, 0)
    t = t0 + (lax.broadcasted_iota(jnp.int32, s.shape, 1) & (tq - 1))
    d = t - kpos
    valid = (kpos >= 0) & (d >= 0) & (d < WINDOW)
    s = jnp.where(valid, s, NEG)
    m = jnp.max(s, axis=0, keepdims=True)
    e = jnp.where(valid, jnp.exp(s - m), 0.0)
    p = (e / jnp.maximum(jnp.sum(e, axis=0, keepdims=True), 1e-30)).astype(jnp.bfloat16)
    o_t = None
    for c in range(n_span):
        part = jnp.dot(vt[qi + c], p[c * tq:(c + 1) * tq], preferred_element_type=jnp.float32)
        o_t = part if o_t is None else o_t + part
    o_ref[...] = _unstack_heads(o_t.T, tq).astype(o_ref.dtype)


def _win_attn(z, B, T, tq):
    nq = T // tq
    ck = COL_KW // HEAD_DIM
    cv = COL_VW // HEAD_DIM
    return pl.pallas_call(
        functools.partial(_win_attn_kernel, tq=tq),
        out_shape=jax.ShapeDtypeStruct((B * T, QW), BRANCH_DTYPE),
        grid=(B, N_KV_HEADS, nq),
        in_specs=[pl.BlockSpec((tq, GROUP * HEAD_DIM), lambda b, g, qi: (b * nq + qi, g)),
                  pl.BlockSpec((T, HEAD_DIM), lambda b, g, qi: (b, ck + g)),
                  pl.BlockSpec((T, HEAD_DIM), lambda b, g, qi: (b, cv + g))],
        out_specs=pl.BlockSpec((tq, GROUP * HEAD_DIM), lambda b, g, qi: (b * nq + qi, g)),
        scratch_shapes=[pltpu.VMEM((T + WINDOW, HEAD_DIM), jnp.bfloat16),
                        pltpu.VMEM(((T + WINDOW) // tq, HEAD_DIM, tq), jnp.bfloat16)],
        compiler_params=pltpu.CompilerParams(
            dimension_semantics=("parallel", "parallel", "arbitrary"),
            vmem_limit_bytes=VMEM_LIMIT_BYTES),
        name="win_attn",
    )(z, z, z)


PAGES_PER_STEP = 16
CHUNKS_PER_PAGE = PAGE_SIZE // CMP_STRIDE
PAGE_ROWS = PAGE_SIZE * N_KV_HEADS
DEC_ROWS = N_HEADS * DEC_SEQ
assert DEC_ROWS == 128 and DEC_SEQ == 8


def _head_rows(ref, g, n_tokens):
    return ref[pl.ds(g, n_tokens, stride=N_KV_HEADS), :]


def _dec_rows(q):
    return jnp.concatenate([q[:, (g * GROUP + r) * HEAD_DIM:(g * GROUP + r + 1) * HEAD_DIM]
                            for r in range(GROUP) for g in range(N_KV_HEADS)], axis=0)


def _dec_cols(o):
    return jnp.concatenate(
        [o[(r * N_KV_HEADS + g) * DEC_SEQ:(r * N_KV_HEADS + g + 1) * DEC_SEQ]
         for g in range(N_KV_HEADS) for r in range(GROUP)], axis=1)


def _dec_row_ids():
    row = lax.broadcasted_iota(jnp.int32, (DEC_ROWS, 1), 0)
    return (row >> 3) & (N_KV_HEADS - 1), row & (DEC_SEQ - 1)


def _group_scores(qall, keys_of, row_g):
    out = None
    for g in range(N_KV_HEADS):
        s = lax.dot_general(qall, keys_of(g), _NT, preferred_element_type=jnp.float32)
        s = jnp.where(row_g == g, s, 0.0)
        out = s if out is None else out + s
    return out


def _group_values(p, vals_of, row_g):
    out = None
    for g in range(N_KV_HEADS):
        pg = jnp.where(row_g == g, p, 0.0).astype(jnp.bfloat16)
        o = jnp.dot(pg, vals_of(g), preferred_element_type=jnp.float32)
        out = o if out is None else out + o
    return out


TAP_ROWS = 2 * N_KV_HEADS


def _chunk_hidden(page_refs, w1_ref):
    chunk_rows = CMP_STRIDE * N_KV_HEADS
    n_rows = len(page_refs) * CHUNKS_PER_PAGE * TAP_ROWS
    odd = (lax.broadcasted_iota(jnp.int32, (n_rows, 1), 0) & N_KV_HEADS) != 0
    pieces = []
    for p in range(CMP_STRIDE // 2):
        x = jnp.concatenate(
            [pg[c * chunk_rows + p * TAP_ROWS:c * chunk_rows + (p + 1) * TAP_ROWS, :]
             for pg in page_refs for c in range(CHUNKS_PER_PAGE)], axis=0)
        pieces.append(jnp.where(odd, 0.0, x).astype(jnp.bfloat16))
        pieces.append(jnp.where(odd, x, 0.0).astype(jnp.bfloat16))
    w1 = w1_ref[...].reshape(CMP_STRIDE * HEAD_DIM, 2 * HEAD_DIM)
    return jnp.dot(jnp.concatenate(pieces, axis=1), w1, preferred_element_type=jnp.float32)


def _hidden_rows(h_sc, g, first_chunk, n):
    even = h_sc[pl.ds(first_chunk * TAP_ROWS + g, n, stride=TAP_ROWS), :]
    odd = h_sc[pl.ds(first_chunk * TAP_ROWS + N_KV_HEADS + g, n, stride=TAP_ROWS), :]
    return even + odd


def _store_hidden(h_lo, h_hi, rows, h):
    h_lo[rows, :] = h[:, 0:HEAD_DIM]
    h_hi[rows, :] = h[:, HEAD_DIM:2 * HEAD_DIM]


def _dec_cmp_kernel(pt_ref, *refs, past_len, n_sel):
    nps = PAGES_PER_STEP
    kp, vp = refs[0:nps], refs[nps:2 * nps]
    (kt_ref, vt_ref, q_ref, w1k_ref, w1v_ref, ck_ref, cv_ref, w2k_ref, w2v_ref, ct_ref,
     ocmp_ref, bias_ref, hk_lo, hk_hi, hv_lo, hv_hi, score_sc) = refs[2 * nps:]
    j = pl.program_id(1)
    rows = nps * CHUNKS_PER_PAGE * TAP_ROWS
    base = (past_len // PAGE_SIZE) * CHUNKS_PER_PAGE
    tail_end = (base + CHUNKS_PER_PAGE) * TAP_ROWS

    step_rows = pl.ds(pl.multiple_of(j * rows, rows), rows)
    for pages, w1_ref, h_lo, h_hi in ((kp, w1k_ref, hk_lo, hk_hi), (vp, w1v_ref, hv_lo, hv_hi)):
        _store_hidden(h_lo, h_hi, step_rows, _chunk_hidden(pages, w1_ref))

    @pl.when(j == pl.num_programs(1) - 1)
    def _():
        for t_ref, w1_ref, h_lo, h_hi in ((kt_ref, w1k_ref, hk_lo, hk_hi),
                                          (vt_ref, w1v_ref, hv_lo, hv_hi)):
            _store_hidden(h_lo, h_hi, slice(base * TAP_ROWS, tail_end), _chunk_hidden([t_ref], w1_ref))
            zeros = jnp.zeros((h_lo.shape[0] - tail_end, HEAD_DIM), jnp.float32)
            h_lo[tail_end:, :] = zeros
            h_hi[tail_end:, :] = zeros
        ncp = ct_ref.shape[1]
        kc, vc = [], []
        for g in range(N_KV_HEADS):
            for h_lo, h_hi, c_ref, w2_ref, dst in ((hk_lo, hk_hi, ck_ref, w2k_ref, kc),
                                                   (hv_lo, hv_hi, cv_ref, w2v_ref, vc)):
                hh = _hidden_rows(h_lo, g, 0, ncp) + _hidden_rows(h_hi, g, 1, ncp) + c_ref[...]
                dst.append(jnp.dot(jax.nn.gelu(hh).astype(jnp.bfloat16), w2_ref[...],
                                   preferred_element_type=jnp.float32).astype(jnp.bfloat16))
        qall = _dec_rows(q_ref[...]).astype(jnp.bfloat16)
        row_g, row_t = _dec_row_ids()
        s = _group_scores(qall, lambda g: kc[g], row_g) * SCALE
        i = lax.broadcasted_iota(jnp.int32, s.shape, 1)
        valid = (i * CMP_STRIDE + (CMP_LEN - 1)) <= (past_len + row_t)
        s = jnp.where(valid, s, NEG)
        m = jnp.max(s, axis=-1, keepdims=True)
        e = jnp.where(valid, jnp.exp(s - m), 0.0)
        p = e / jnp.maximum(jnp.sum(e, axis=-1, keepdims=True), 1e-30)
        ocmp_ref[...] = _dec_cols(_group_values(p, lambda g: vc[g], row_g))

        x = lax.dot_general(ct_ref[...], p.astype(jnp.bfloat16), _NT,
                            preferred_element_type=jnp.float32)
        quarter = DEC_ROWS // GROUP
        pslc = x
        for r in range(1, GROUP):
            pslc = pslc + pltpu.roll(x, r * quarter, 1)
        shape = pslc.shape
        jj = lax.broadcasted_iota(jnp.int32, shape, 0)
        cur = (past_len + (lax.broadcasted_iota(jnp.int32, shape, 1) & (DEC_SEQ - 1))) >> SEL_SHIFT
        forced = (jj == 0) | (jj == cur) | (jj == cur - 1)
        score = jnp.where(jj <= cur, pslc + jnp.where(forced, FORCE_BONUS, 0.0), NEG)
        score_sc[...] = score

        def rank_step(jp, rank):
            sj = score_sc[pl.ds(jp, 1), :]
            return rank + jnp.where(sj > score, 1.0, jnp.where((sj == score) & (jj > jp), 1.0, 0.0))

        rank = lax.fori_loop(0, n_sel, rank_step, jnp.zeros(shape, jnp.float32))
        bias = jnp.where((rank < SEL_TOPK) & (jj <= cur), 0.0, MASK_BIAS)
        pad = jnp.full((bias_ref.shape[2] - shape[0], shape[1]), MASK_BIAS, jnp.float32)
        bias_ref[0] = jnp.concatenate([bias, pad], axis=0).T.astype(jnp.bfloat16)


PAGE_BLOCK = (PAGE_ROWS, HEAD_DIM)


def _page_specs(n):
    return [pl.BlockSpec(PAGE_BLOCK, lambda b, j, pt, i=i: (pt[b, j * PAGES_PER_STEP + i], 0))
            for i in range(n)]


_TAIL_SPEC = pl.BlockSpec(PAGE_BLOCK, lambda b, j, pt: (b, 0))


def _dec_cmp_select(page_table, pool_k, pool_v, tail_k, tail_v, z, w1k, w1v, ck, cv, w2k, w2v, ct,
                    past_len):
    nb = page_table.shape[0]
    n_pages = past_len // PAGE_SIZE
    n_sel, ncp = ct.shape
    n_sel_real = -(-(past_len + DEC_SEQ) // SEL_LEN)
    full2 = lambda a: pl.BlockSpec(a.shape, lambda b, j, pt: (0, 0))
    full3 = lambda a: pl.BlockSpec(a.shape, lambda b, j, pt: (0, 0, 0))
    tail = _TAIL_SPEC
    n_blocks = 2 * HEAD_DIM
    grid_spec = pltpu.PrefetchScalarGridSpec(
        num_scalar_prefetch=1,
        grid=(nb, n_pages // PAGES_PER_STEP),
        in_specs=(_page_specs(PAGES_PER_STEP) + _page_specs(PAGES_PER_STEP)
                  + [tail, tail, pl.BlockSpec((DEC_SEQ, QW), lambda b, j, pt: (b, 0)),
                     full3(w1k), full3(w1v), full2(ck), full2(cv), full2(w2k), full2(w2v), full2(ct)]),
        out_specs=(pl.BlockSpec((DEC_SEQ, QW), lambda b, j, pt: (b, 0)),
                   pl.BlockSpec((1, DEC_ROWS, n_blocks), lambda b, j, pt: (b, 0, 0))),
        scratch_shapes=[pltpu.VMEM(((ncp + 8) * TAP_ROWS, HEAD_DIM), jnp.float32)] * 4
        + [pltpu.VMEM((n_sel, DEC_ROWS), jnp.float32)])
    return pl.pallas_call(
        functools.partial(_dec_cmp_kernel, past_len=past_len, n_sel=n_sel_real),
        out_shape=(jax.ShapeDtypeStruct((nb * DEC_SEQ, QW), jnp.float32),
                   jax.ShapeDtypeStruct((nb, DEC_ROWS, n_blocks), jnp.bfloat16)),
        grid_spec=grid_spec,
        compiler_params=pltpu.CompilerParams(
            dimension_semantics=("parallel", "arbitrary"),
            vmem_limit_bytes=VMEM_LIMIT_BYTES),
        name="dec_cmp_select",
    )(page_table, *([pool_k] * PAGES_PER_STEP), *([pool_v] * PAGES_PER_STEP), tail_k, tail_v, z,
      w1k, w1v, ck, cv, w2k, w2v, ct)


def _dec_sel_kernel(pt_ref, *refs, past_len):
    nps = PAGES_PER_STEP
    kp, vp = refs[0:nps], refs[nps:2 * nps]
    kt_ref, vt_ref, q_ref, bias_ref, o_ref, m_sc, l_sc, acc_sc = refs[2 * nps:]
    j = pl.program_id(1)
    qall = _dec_rows(q_ref[...]).astype(jnp.bfloat16)
    row_g, row_t = _dec_row_ids()
    selcols = bias_ref[0]

    @pl.when(j == 0)
    def _():
        m_sc[...] = jnp.full(m_sc.shape, NEG, jnp.float32)
        l_sc[...] = jnp.zeros(l_sc.shape, jnp.float32)
        acc_sc[...] = jnp.zeros(acc_sc.shape, jnp.float32)

    def update(k_refs, v_refs, first_blk, causal):
        grp = lambda ref, g: _head_rows(ref, g, PAGE_SIZE).astype(jnp.bfloat16)
        s = jnp.concatenate([_group_scores(qall, functools.partial(grp, kr), row_g)
                             for kr in k_refs], axis=1) * SCALE
        nk = s.shape[1]
        n_blocks = selcols.shape[1]
        blk = first_blk + (lax.broadcasted_iota(jnp.int32, (n_blocks, nk), 1) >> SEL_SHIFT)
        onehot = jnp.where(lax.broadcasted_iota(jnp.int32, (n_blocks, nk), 0) == blk, 1.0, 0.0)
        s = s + jnp.dot(selcols, onehot.astype(jnp.bfloat16), preferred_element_type=jnp.float32)
        if causal:
            kpos = first_blk * SEL_LEN + lax.broadcasted_iota(jnp.int32, s.shape, 1)
            s = jnp.where(kpos <= past_len + row_t, s, NEG)
        m_old = m_sc[...]
        m_new = jnp.maximum(m_old, jnp.max(s, axis=-1, keepdims=True))
        alpha = jnp.exp(m_old - m_new)
        p = jnp.exp(s - m_new)
        l_sc[...] = alpha * l_sc[...] + jnp.sum(p, axis=-1, keepdims=True)
        pv = None
        for n, vr in enumerate(v_refs):
            o = _group_values(p[:, n * PAGE_SIZE:(n + 1) * PAGE_SIZE], functools.partial(grp, vr), row_g)
            pv = o if pv is None else pv + o
        acc_sc[...] = alpha * acc_sc[...] + pv
        m_sc[...] = m_new

    update(kp, vp, j * (nps * PAGE_SIZE // SEL_LEN), False)

    @pl.when(j == pl.num_programs(1) - 1)
    def _():
        update([kt_ref], [vt_ref], past_len // SEL_LEN, True)
        o_ref[...] = _dec_cols(acc_sc[...] / jnp.maximum(l_sc[...], 1e-30))


def _dec_sel_attn(page_table, pool_k, pool_v, tail_k, tail_v, z, bias, past_len):
    nb = page_table.shape[0]
    n_pages = past_len // PAGE_SIZE
    tail = _TAIL_SPEC
    grid_spec = pltpu.PrefetchScalarGridSpec(
        num_scalar_prefetch=1,
        grid=(nb, n_pages // PAGES_PER_STEP),
        in_specs=(_page_specs(PAGES_PER_STEP) + _page_specs(PAGES_PER_STEP)
                  + [tail, tail, pl.BlockSpec((DEC_SEQ, QW), lambda b, j, pt: (b, 0)),
                     pl.BlockSpec((1,) + bias.shape[1:], lambda b, j, pt: (b, 0, 0))]),
        out_specs=pl.BlockSpec((DEC_SEQ, QW), lambda b, j, pt: (b, 0)),
        scratch_shapes=[pltpu.VMEM((DEC_ROWS, 1), jnp.float32),
                        pltpu.VMEM((DEC_ROWS, 1), jnp.float32),
                        pltpu.VMEM((DEC_ROWS, HEAD_DIM), jnp.float32)])
    return pl.pallas_call(
        functools.partial(_dec_sel_kernel, past_len=past_len),
        out_shape=jax.ShapeDtypeStruct((nb * DEC_SEQ, QW), jnp.float32),
        grid_spec=grid_spec,
        compiler_params=pltpu.CompilerParams(
            dimension_semantics=("parallel", "arbitrary"),
            vmem_limit_bytes=VMEM_LIMIT_BYTES),
        name="dec_sel_attn",
    )(page_table, *([pool_k] * PAGES_PER_STEP), *([pool_v] * PAGES_PER_STEP), tail_k, tail_v, z, bias)


def _dec_win_kernel(q_ref, wk_ref, wv_ref, kt_ref, vt_ref, o_ref, nwk_ref, nwv_ref, *, past_len):
    wl = wk_ref.shape[0] // N_KV_HEADS
    qall = _dec_rows(q_ref[...]).astype(jnp.bfloat16)
    row_g, row_t = _dec_row_ids()

    def cat(w_ref, t_ref, g):
        return jnp.concatenate([_head_rows(w_ref, g, wl), _head_rows(t_ref, g, PAGE_SIZE)],
                               axis=0).astype(jnp.bfloat16)

    s = _group_scores(qall, functools.partial(cat, wk_ref, kt_ref), row_g) * SCALE
    kpos = past_len - wl + lax.broadcasted_iota(jnp.int32, s.shape, 1)
    d = past_len + row_t - kpos
    valid = (d >= 0) & (d < WINDOW)
    s = jnp.where(valid, s, NEG)
    m = jnp.max(s, axis=-1, keepdims=True)
    e = jnp.where(valid, jnp.exp(s - m), 0.0)
    p = e / jnp.maximum(jnp.sum(e, axis=-1, keepdims=True), 1e-30)
    o_ref[...] = _dec_cols(_group_values(p, functools.partial(cat, wv_ref, vt_ref), row_g))
    keep = (wl - DEC_SEQ) * N_KV_HEADS
    for w_ref, t_ref, n_ref in ((wk_ref, kt_ref, nwk_ref), (wv_ref, vt_ref, nwv_ref)):
        n_ref[0:keep, :] = w_ref[wl * N_KV_HEADS - keep:wl * N_KV_HEADS, :]
        n_ref[keep:wl * N_KV_HEADS, :] = t_ref[0:DEC_SEQ * N_KV_HEADS, :]


def _dec_win_attn(z, win_k, win_v, tail_k, tail_v, nb, past_len):
    cache = pl.BlockSpec((win_k.shape[0] // nb, HEAD_DIM), lambda b: (b, 0))
    tail = pl.BlockSpec(PAGE_BLOCK, lambda b: (b, 0))
    rows = pl.BlockSpec((DEC_SEQ, QW), lambda b: (b, 0))
    return pl.pallas_call(
        functools.partial(_dec_win_kernel, past_len=past_len),
        out_shape=(jax.ShapeDtypeStruct((nb * DEC_SEQ, QW), jnp.float32),
                   jax.ShapeDtypeStruct(win_k.shape, jnp.float32),
                   jax.ShapeDtypeStruct(win_v.shape, jnp.float32)),
        grid=(nb,),
        in_specs=[rows, cache, cache, tail, tail],
        out_specs=(rows, cache, cache),
        compiler_params=pltpu.CompilerParams(
            dimension_semantics=("parallel",),
            vmem_limit_bytes=VMEM_LIMIT_BYTES),
        name="dec_win_attn",
    )(z, win_k, win_v, tail_k, tail_v)


def _compress_kernel(xk_ref, xv_ref, w1k_ref, w1v_ref, ck_ref, cv_ref, w2k_ref, w2v_ref,
                     ko_ref, vo_ref, h_sc):
    nch = xk_ref.shape[0] // CMP_STRIDE
    h_sc[nch:nch + 8, :] = jnp.zeros((8, 2 * HEAD_DIM), jnp.float32)
    for x_ref, w1_ref, c_ref, w2_ref, o_ref in ((xk_ref, w1k_ref, ck_ref, w2k_ref, ko_ref),
                                                (xv_ref, w1v_ref, cv_ref, w2v_ref, vo_ref)):
        acc = None
        for l in range(CMP_STRIDE):
            x = x_ref[pl.ds(l, nch, stride=CMP_STRIDE), :].astype(jnp.bfloat16)
            h = jnp.dot(x, w1_ref[l], preferred_element_type=jnp.float32)
            acc = h if acc is None else acc + h
        h_sc[0:nch, :] = acc
        hh = h_sc[0:nch, 0:HEAD_DIM] + h_sc[1:nch + 1, HEAD_DIM:2 * HEAD_DIM] + c_ref[...]
        o_ref[0, 0] = jnp.dot(jax.nn.gelu(hh).astype(jnp.bfloat16), w2_ref[...],
                              preferred_element_type=jnp.float32)


def _compress_seq(z, w1k, w1v, ck, cv, w2k, w2v, B, T):
    nch = T // CMP_STRIDE
    ck0 = COL_KC // HEAD_DIM
    cv0 = COL_VC // HEAD_DIM
    full2 = lambda a: pl.BlockSpec(a.shape, lambda b, g: (0, 0))
    full3 = lambda a: pl.BlockSpec(a.shape, lambda b, g: (0, 0, 0))
    out = jax.ShapeDtypeStruct((B, N_KV_HEADS, nch, HEAD_DIM), jnp.float32)
    ospec = pl.BlockSpec((1, 1, nch, HEAD_DIM), lambda b, g: (b, g, 0, 0))
    return pl.pallas_call(
        _compress_kernel,
        out_shape=(out, out),
        grid=(B, N_KV_HEADS),
        in_specs=[pl.BlockSpec((T, HEAD_DIM), lambda b, g: (b, ck0 + g)),
                  pl.BlockSpec((T, HEAD_DIM), lambda b, g: (b, cv0 + g)),
                  full3(w1k), full3(w1v), full2(ck), full2(cv), full2(w2k), full2(w2v)],
        out_specs=(ospec, ospec),
        scratch_shapes=[pltpu.VMEM((nch + 8, 2 * HEAD_DIM), jnp.float32)],
        compiler_params=pltpu.CompilerParams(
            dimension_semantics=("parallel", "parallel"),
            vmem_limit_bytes=VMEM_LIMIT_BYTES),
        name="compress_seq",
    )(z, z, w1k, w1v, ck, cv, w2k, w2v)


CONV_HALO = 32
LANES = 128


def _conv_taps(xs, w_ref, first, rows, y_sc):
    for cb in range(CONV_CH // LANES):
        sl = slice(cb * LANES, (cb + 1) * LANES)
        acc = None
        for k in range(CONV_WIDTH):
            term = w_ref[k:k + 1, sl] * xs[first + k:first + k + rows, sl]
            acc = term if acc is None else acc + term
        y_sc[:, sl] = acc


def _conv_post(y, cb_ref, lg_ref, lb_ref):
    y = y + cb_ref[...]
    mu = jnp.mean(y, axis=-1, keepdims=True)
    var = jnp.mean(jnp.square(y - mu), axis=-1, keepdims=True)
    yn = (y - mu) * lax.rsqrt(var + EPS) * lg_ref[...] + lb_ref[...]
    return yn * jax.nn.sigmoid(yn)


def _conv_kernel(a_ref, b_ref, ap_ref, bp_ref, w_ref, cb_ref, lg_ref, lb_ref, y_ref, tail_ref,
                 xs, y_sc, *, tm):
    i = pl.program_id(1)
    prev = ap_ref[...] * jax.nn.sigmoid(bp_ref[...])
    xs[0:CONV_HALO, :] = jnp.where(i == 0, 0.0, prev)
    xs[CONV_HALO:CONV_HALO + tm, :] = a_ref[...] * jax.nn.sigmoid(b_ref[...])
    _conv_taps(xs, w_ref, CONV_HALO - (CONV_WIDTH - 1), tm, y_sc)
    y_ref[...] = _conv_post(y_sc[...], cb_ref, lg_ref, lb_ref).astype(y_ref.dtype)

    @pl.when(i == pl.num_programs(1) - 1)
    def _():
        tail_ref[0] = xs[tm:tm + CONV_HALO, :]


def _conv_seq(z, w, cb, lg, lb, B, T, tm):
    nt = T // tm
    ca = COL_GLU_A // CONV_CH
    cbk = COL_GLU_B // CONV_CH
    r = tm // CONV_HALO
    prev = lambda c: pl.BlockSpec((CONV_HALO, CONV_CH),
                                  lambda b, i: (jnp.maximum((b * nt + i) * r - 1, 0), c))
    cur = lambda c: pl.BlockSpec((tm, CONV_CH), lambda b, i: (b * nt + i, c))
    vec = pl.BlockSpec((1, CONV_CH), lambda b, i: (0, 0))
    return pl.pallas_call(
        functools.partial(_conv_kernel, tm=tm),
        out_shape=(jax.ShapeDtypeStruct((B * T, CONV_CH), jnp.bfloat16),
                   jax.ShapeDtypeStruct((B, CONV_HALO, CONV_CH), jnp.float32)),
        grid=(B, nt),
        in_specs=[cur(ca), cur(cbk), prev(ca), prev(cbk),
                  pl.BlockSpec(w.shape, lambda b, i: (0, 0)), vec, vec, vec],
        out_specs=(pl.BlockSpec((tm, CONV_CH), lambda b, i: (b * nt + i, 0)),
                   pl.BlockSpec((1, CONV_HALO, CONV_CH), lambda b, i: (b, 0, 0))),
        scratch_shapes=[pltpu.VMEM((tm + CONV_HALO, CONV_CH), jnp.float32),
                        pltpu.VMEM((tm, CONV_CH), jnp.float32)],
        compiler_params=pltpu.CompilerParams(
            dimension_semantics=("parallel", "arbitrary"),
            vmem_limit_bytes=VMEM_LIMIT_BYTES),
        name="conv_seq",
    )(z, z, z, z, w, cb.reshape(1, -1), lg.reshape(1, -1), lb.reshape(1, -1))


def _dec_conv_kernel(a_ref, b_ref, st_ref, w_ref, cb_ref, lg_ref, lb_ref, y_ref, ns_ref, xs, y_sc):
    nbuf = CONV_WIDTH - 1
    xs[0:nbuf, :] = st_ref[0]
    xs[nbuf:nbuf + DEC_SEQ, :] = a_ref[...] * jax.nn.sigmoid(b_ref[...])
    _conv_taps(xs, w_ref, 0, DEC_SEQ, y_sc)
    y_ref[...] = _conv_post(y_sc[...], cb_ref, lg_ref, lb_ref)
    ns_ref[0] = xs[DEC_SEQ:DEC_SEQ + nbuf, :]


def _dec_conv(z, state, w, cb, lg, lb):
    nb, nbuf, _ = state.shape
    ca = COL_GLU_A // CONV_CH
    cbk = COL_GLU_B // CONV_CH
    vec = pl.BlockSpec((1, CONV_CH), lambda b: (0, 0))
    st = pl.BlockSpec((1, nbuf, CONV_CH), lambda b: (b, 0, 0))
    return pl.pallas_call(
        _dec_conv_kernel,
        out_shape=(jax.ShapeDtypeStruct((nb * DEC_SEQ, CONV_CH), jnp.float32),
                   jax.ShapeDtypeStruct(state.shape, jnp.float32)),
        grid=(nb,),
        in_specs=[pl.BlockSpec((DEC_SEQ, CONV_CH), lambda b: (b, ca)),
                  pl.BlockSpec((DEC_SEQ, CONV_CH), lambda b: (b, cbk)),
                  st, pl.BlockSpec(w.shape, lambda b: (0, 0)), vec, vec, vec],
        out_specs=(pl.BlockSpec((DEC_SEQ, CONV_CH), lambda b: (b, 0)), st),
        scratch_shapes=[pltpu.VMEM((nbuf + DEC_SEQ + 2, CONV_CH), jnp.float32),
                        pltpu.VMEM((DEC_SEQ, CONV_CH), jnp.float32)],
        compiler_params=pltpu.CompilerParams(
            dimension_semantics=("parallel",),
            vmem_limit_bytes=VMEM_LIMIT_BYTES),
        name="dec_conv",
    )(z, z, state, w, cb.reshape(1, -1), lg.reshape(1, -1), lb.reshape(1, -1))


def _merge_kernel(oc_ref, os_ref, ow_ref, g_ref, convy_ref, ga_ref, gb_ref, wn_ref, wc_ref,
                  o_ref, onsa_ref):
    @pl.when(pl.program_id(1) == 0)
    def _():
        gz = jax.nn.sigmoid(g_ref[...])
        for h in range(N_HEADS):
            sl = slice(h * HEAD_DIM, (h + 1) * HEAD_DIM)
            o = (gz[:, 3 * h:3 * h + 1] * oc_ref[:, sl]
                 + gz[:, 3 * h + 1:3 * h + 2] * os_ref[:, sl]
                 + gz[:, 3 * h + 2:3 * h + 3] * ow_ref[:, sl])
            onsa_ref[:, sl] = o.astype(jnp.bfloat16)

    a = jnp.dot(onsa_ref[...], wn_ref[...], preferred_element_type=jnp.float32)
    b = jnp.dot(convy_ref[...].astype(jnp.bfloat16), wc_ref[...],
                preferred_element_type=jnp.float32)
    o_ref[...] = (jax.nn.sigmoid(ga_ref[...]) * a
                  + jax.nn.sigmoid(gb_ref[...]) * b).astype(o_ref.dtype)


def _merge(o_cmp, o_sel, o_win, conv_y, z, wn_bf16, wc_bf16, tm, tn):
    n = o_cmp.shape[0]
    ja = COL_GM_A // tn
    jb = COL_GM_B // tn
    jg = COL_G // GATE_PAD
    row = lambda i, j: (i, 0)
    return pl.pallas_call(
        _merge_kernel,
        out_shape=jax.ShapeDtypeStruct((n, D_MODEL), jnp.bfloat16),
        grid=(n // tm, D_MODEL // tn),
        in_specs=[pl.BlockSpec((tm, QW), row),
                  pl.BlockSpec((tm, QW), row),
                  pl.BlockSpec((tm, QW), row),
                  pl.BlockSpec((tm, GATE_PAD), lambda i, j: (i, jg)),
                  pl.BlockSpec((tm, CONV_CH), row),
                  pl.BlockSpec((tm, tn), lambda i, j: (i, ja + j)),
                  pl.BlockSpec((tm, tn), lambda i, j: (i, jb + j)),
                  pl.BlockSpec((QW, tn), lambda i, j: (0, j)),
                  pl.BlockSpec((CONV_CH, tn), lambda i, j: (0, j))],
        out_specs=pl.BlockSpec((tm, tn), lambda i, j: (i, j)),
        scratch_shapes=[pltpu.VMEM((tm, QW), jnp.bfloat16)],
        compiler_params=pltpu.CompilerParams(
            dimension_semantics=("parallel", "arbitrary"),
            vmem_limit_bytes=VMEM_LIMIT_BYTES),
        name="merge",
    )(o_cmp, o_sel, o_win, z, conv_y, z, z, wn_bf16, wc_bf16)


def _out_proj_kernel(m_ref, x_ref, w_ref, g_ref, h_ref, hn_ref):
    h = x_ref[...] + jnp.dot(m_ref[...], w_ref[...], preferred_element_type=jnp.float32)
    h_ref[...] = h
    ms = jnp.mean(h * h, axis=-1, keepdims=True)
    hn_ref[...] = (h * lax.rsqrt(ms + EPS) * g_ref[...]).astype(hn_ref.dtype)


def _out_proj(merged, x, w_bf16, gain, tm):
    n = x.shape[0]
    return pl.pallas_call(
        _out_proj_kernel,
        out_shape=(jax.ShapeDtypeStruct((n, D_MODEL), jnp.float32),
                   jax.ShapeDtypeStruct((n, D_MODEL), jnp.bfloat16)),
        grid=(n // tm,),
        in_specs=[pl.BlockSpec((tm, D_MODEL), lambda i: (i, 0)),
                  pl.BlockSpec((tm, D_MODEL), lambda i: (i, 0)),
                  pl.BlockSpec((D_MODEL, D_MODEL), lambda i: (0, 0)),
                  pl.BlockSpec((1, D_MODEL), lambda i: (0, 0))],
        out_specs=(pl.BlockSpec((tm, D_MODEL), lambda i: (i, 0)),
                   pl.BlockSpec((tm, D_MODEL), lambda i: (i, 0))),
        compiler_params=pltpu.CompilerParams(
            dimension_semantics=("parallel",),
            vmem_limit_bytes=VMEM_LIMIT_BYTES),
        name="out_proj",
    )(merged, x, w_bf16, gain.reshape(1, D_MODEL))


PEER_HALF = PEER_DKEY // 2
PEER_SEL = PEER_HEADS * PEER_TOPK
PEER_TOPK_SHIFT = PEER_TOPK.bit_length() - 1
CAND_B = PEER_TOPK // 2
CAND_B_SHIFT = CAND_B.bit_length() - 1
N_CAND = PEER_TOPK + (PEER_TOPK - 1) * CAND_B
GATE_HALF = PEER_NKEYS // 2
GATE_PITCH = GATE_HALF + 8


def _topk_chains(s_sc, v_sc, i_sc, k):
    n_chain, r, n = s_sc.shape
    row = lax.broadcasted_iota(jnp.int32, (r, n), 0)

    def body(i, carry):
        for c in range(n_chain):
            s = s_sc[c]
            m = jnp.max(s, axis=0, keepdims=True)
            j = jnp.min(jnp.where(s == m, row, r), axis=0, keepdims=True)
            s_sc[c] = jnp.where(row == j, -jnp.inf, s)
            v_sc[c, pl.ds(i, 1), :] = m
            i_sc[c, pl.ds(i, 1), :] = j
        return carry

    lax.fori_loop(0, k, body, 0)


def _pick_rows(idx, table):
    out = jnp.zeros(idx.shape, table.dtype)
    for a in range(PEER_TOPK):
        out = jnp.where(idx == a, table[a:a + 1, :], out)
    return out


def _peer_route_kernel(hn_ref, wq_ref, k1_ref, k2_ref, ia_ref, ib_ref, gt_ref,
                       s_sc, v_sc, i_sc, c_sc, cv_sc, ci_sc, a_sc, b_sc, g_sc):
    qh = jnp.dot(hn_ref[...], wq_ref[...], preferred_element_type=jnp.float32).astype(jnp.bfloat16)
    for h in range(PEER_HEADS):
        q1 = qh[:, h * PEER_DKEY:h * PEER_DKEY + PEER_HALF]
        q2 = qh[:, h * PEER_DKEY + PEER_HALF:(h + 1) * PEER_DKEY]
        s_sc[2 * h] = lax.dot_general(k1_ref[h], q1, _NT, preferred_element_type=jnp.float32)
        s_sc[2 * h + 1] = lax.dot_general(k2_ref[h], q2, _NT, preferred_element_type=jnp.float32)
    _topk_chains(s_sc, v_sc, i_sc, PEER_TOPK)
    for h in range(PEER_HEADS):
        v1, v2 = v_sc[2 * h], v_sc[2 * h + 1]
        c_sc[h] = jnp.concatenate([v1[0:1, :] + v2]
                                  + [v1[a:a + 1, :] + v2[0:CAND_B, :] for a in range(1, PEER_TOPK)],
                                  axis=0)
    _topk_chains(c_sc, cv_sc, ci_sc, PEER_TOPK)
    for h in range(PEER_HEADS):
        sc, pos = cv_sc[h], ci_sc[h]
        rest = pos - PEER_TOPK
        first = pos < PEER_TOPK
        ia = _pick_rows(jnp.where(first, 0, 1 + (rest >> CAND_B_SHIFT)), i_sc[2 * h])
        ib = _pick_rows(jnp.where(first, pos, rest & (CAND_B - 1)), i_sc[2 * h + 1])
        e = jnp.exp(sc - sc[0:1, :])
        gate = e / jnp.sum(e, axis=0, keepdims=True)
        rows = slice(h * PEER_TOPK, (h + 1) * PEER_TOPK)
        a_sc[rows, :] = ia.astype(jnp.float32)
        b_sc[rows, :] = ib.astype(jnp.float32)
        g_sc[rows, :] = gate
    ia_ref[...] = a_sc[...].T
    ib_ref[...] = b_sc[...].T
    gt_ref[...] = g_sc[...].T


def _peer_route(hn, wq_bf16, k1_bf16, k2_bf16, tm):
    n, d = hn.shape
    out = jax.ShapeDtypeStruct((n, PEER_SEL), jnp.float32)
    ospec = pl.BlockSpec((tm, PEER_SEL), lambda i: (i, 0))
    return pl.pallas_call(
        _peer_route_kernel,
        out_shape=(out, out, out),
        grid=(n // tm,),
        in_specs=[pl.BlockSpec((tm, d), lambda i: (i, 0)),
                  pl.BlockSpec(wq_bf16.shape, lambda i: (0, 0)),
                  pl.BlockSpec(k1_bf16.shape, lambda i: (0, 0, 0)),
                  pl.BlockSpec(k2_bf16.shape, lambda i: (0, 0, 0))],
        out_specs=(ospec, ospec, ospec),
        scratch_shapes=[pltpu.VMEM((2 * PEER_HEADS, PEER_NKEYS, tm), jnp.float32),
                        pltpu.VMEM((2 * PEER_HEADS, PEER_TOPK, tm), jnp.float32),
                        pltpu.VMEM((2 * PEER_HEADS, PEER_TOPK, tm), jnp.int32),
                        pltpu.VMEM((PEER_HEADS, N_CAND, tm), jnp.float32),
                        pltpu.VMEM((PEER_HEADS, PEER_TOPK, tm), jnp.float32),
                        pltpu.VMEM((PEER_HEADS, PEER_TOPK, tm), jnp.int32)]
        + [pltpu.VMEM((PEER_SEL, tm), jnp.float32)] * 3,
        compiler_params=pltpu.CompilerParams(
            dimension_semantics=("parallel",),
            vmem_limit_bytes=VMEM_LIMIT_BYTES),
        name="peer_route",
    )(hn, wq_bf16, k1_bf16, k2_bf16)


def _peer_ffn_kernel(hn_ref, ia_ref, ib_ref, gt_ref, u_ref, v_ref, o_ref, w_sc, *, tm, n_i1):
    c = pl.program_id(1)

    @pl.when(c == 0)
    def _():
        o_ref[...] = jnp.zeros(o_ref.shape, jnp.float32)

    steps_per_half = GATE_HALF // n_i1

    @pl.when(c % steps_per_half == 0)
    def _():
        first = (c // steps_per_half) * GATE_HALF
        bf = lambda x: x.astype(jnp.float32).astype(jnp.bfloat16)
        rows_a = bf(first + lax.broadcasted_iota(jnp.int32, (GATE_HALF, PEER_SEL), 0))
        rows_b = bf(lax.broadcasted_iota(jnp.int32, (PEER_NKEYS, PEER_SEL), 0))
        one = jnp.ones((1, PEER_SEL), jnp.bfloat16)
        zero = jnp.zeros((1, PEER_SEL), jnp.bfloat16)

        def token(n, carry):
            a_row = ia_ref[pl.ds(n, 1), :].astype(jnp.bfloat16)
            b_row = ib_ref[pl.ds(n, 1), :].astype(jnp.bfloat16)
            g_row = gt_ref[pl.ds(n, 1), :]
            g_hi = g_row.astype(jnp.bfloat16)
            g_lo = (g_row - g_hi.astype(jnp.float32)).astype(jnp.bfloat16)
            oa = jnp.where(rows_a == a_row, one, zero)
            hit_b = rows_b == b_row
            gb_hi = jnp.where(hit_b, g_hi, zero)
            gb_lo = jnp.where(hit_b, g_lo, zero)
            w = lax.dot_general(jnp.concatenate([oa, oa], axis=1),
                                jnp.concatenate([gb_hi, gb_lo], axis=1), _NT,
                                preferred_element_type=jnp.float32)
            w_sc[pl.ds(pl.multiple_of(n * GATE_PITCH, 8), GATE_HALF), :] = w
            return carry

        lax.fori_loop(0, tm, token, 0, unroll=8)

    act = jax.nn.gelu(lax.dot_general(hn_ref[...], u_ref[...], _NT,
                                      preferred_element_type=jnp.float32))
    row0 = (c * n_i1) & (GATE_HALF - 1)
    parts = []
    for j in range(n_i1):
        wj = w_sc[pl.ds(row0 + j, tm, stride=GATE_PITCH), :]
        parts.append((act[:, j * PEER_NKEYS:(j + 1) * PEER_NKEYS] * wj).astype(jnp.bfloat16))
    o_ref[...] += jnp.dot(jnp.concatenate(parts, axis=1), v_ref[...],
                          preferred_element_type=jnp.float32)


def _peer_ffn(hn, ia, ib, gt, u_bf16, v_bf16, tm, n_i1):
    n, d = hn.shape
    ec = n_i1 * PEER_NKEYS
    assert GATE_HALF % n_i1 == 0
    row = lambda i, c: (i, 0)
    return pl.pallas_call(
        functools.partial(_peer_ffn_kernel, tm=tm, n_i1=n_i1),
        out_shape=jax.ShapeDtypeStruct((n, d), jnp.float32),
        grid=(n // tm, u_bf16.shape[0] // ec),
        in_specs=[pl.BlockSpec((tm, d), row),
                  pl.BlockSpec((tm, PEER_SEL), row),
                  pl.BlockSpec((tm, PEER_SEL), row),
                  pl.BlockSpec((tm, PEER_SEL), row),
                  pl.BlockSpec((ec, d), lambda i, c: (c, 0)),
                  pl.BlockSpec((ec, d), lambda i, c: (c, 0))],
        out_specs=pl.BlockSpec((tm, d), row),
        scratch_shapes=[pltpu.VMEM((tm * GATE_PITCH, PEER_NKEYS), jnp.float32)],
        compiler_params=pltpu.CompilerParams(
            dimension_semantics=("parallel", "arbitrary"),
            vmem_limit_bytes=VMEM_LIMIT_PEER_BYTES),
        name="peer_ffn",
    )(hn, ia, ib, gt, u_bf16, v_bf16)


def _residual_norm_kernel(h_ref, f_ref, g_ref, y_ref):
    y = h_ref[...] + f_ref[...]
    ms = jnp.mean(y * y, axis=-1, keepdims=True)
    y_ref[...] = y * lax.rsqrt(ms + EPS) * g_ref[...]


def _residual_norm(h, f, gain, tm):
    n, d = h.shape
    row = pl.BlockSpec((tm, d), lambda i: (i, 0))
    return pl.pallas_call(
        _residual_norm_kernel,
        out_shape=jax.ShapeDtypeStruct((n, d), jnp.float32),
        grid=(n // tm,),
        in_specs=[row, row, pl.BlockSpec((1, d), lambda i: (0, 0))],
        out_specs=row,
        compiler_params=pltpu.CompilerParams(
            dimension_semantics=("parallel",),
            vmem_limit_bytes=VMEM_LIMIT_BYTES),
        name="residual_norm",
    )(h, f, gain.reshape(1, d))


def _rmsnorm(x, g):
    xf = x.astype(jnp.float32)
    y = xf * lax.rsqrt(jnp.mean(xf * xf, axis=-1, keepdims=True) + EPS)
    return (y * g.astype(jnp.float32)).astype(x.dtype)


def _layernorm(x, g, b):
    xf = x.astype(jnp.float32)
    mu = jnp.mean(xf, axis=-1, keepdims=True)
    var = jnp.mean(jnp.square(xf - mu), axis=-1, keepdims=True)
    return ((xf - mu) * lax.rsqrt(var + EPS) * g.astype(jnp.float32) + b.astype(jnp.float32)).astype(x.dtype)


def _masked_probs(s, mask):
    s = jnp.where(mask, s, NEG)
    m = jnp.max(s, axis=-1, keepdims=True)
    e = jnp.where(mask, jnp.exp(s - m), 0.0)
    return e / jnp.maximum(jnp.sum(e, axis=-1, keepdims=True), 1e-30)


def _attn_probs(q, k, mask):
    s = jnp.einsum('...tgrd,...kgd->...grtk', q, k).astype(jnp.float32) * SCALE
    return _masked_probs(s, mask[..., None, None, :, :])


def _attn_out(p, v):
    return jnp.einsum('...grtk,...kgd->...tgrd', p.astype(v.dtype), v)


def _gather_pages(pool, page_table):
    g = pool[page_table]
    return g.reshape(page_table.shape[0], -1, pool.shape[2], pool.shape[3])


def _compress(kv, pe, w1, w2):
    B, L, G, D = kv.shape
    ch = kv.reshape(B, L // CMP_STRIDE, CMP_STRIDE, G, D)
    h_lo = jnp.einsum('bjlgd,ldh->bjgh', ch, w1[:CMP_STRIDE])
    h_hi = jnp.einsum('bjlgd,ldh->bjgh', ch, w1[CMP_STRIDE:])
    h = h_lo[:, :-1] + h_hi[:, 1:] + jnp.einsum('ld,ldh->h', pe, w1)
    return jnp.einsum('bigh,he->bige', jax.nn.gelu(h), w2)


def _cmp_to_sel(n_cmp, n_sel):
    cs = jnp.arange(n_cmp)[:, None] * CMP_STRIDE
    ss = jnp.arange(n_sel)[None, :] * SEL_LEN
    ov = jnp.clip(jnp.minimum(cs + CMP_LEN, ss + SEL_LEN) - jnp.maximum(cs, ss), 0, None)
    return ov.astype(jnp.float32) / CMP_LEN


def _sel_attend(q, kb, vb, idx, valid, qpos):
    B, T, G, R, D = q.shape
    bi = jnp.arange(B)[:, None, None, None]
    gi = jnp.arange(G)[None, :, None, None]
    kg = kb[bi, idx, :, gi].reshape(B, G, T, -1, D)
    vg = vb[bi, idx, :, gi].reshape(B, G, T, -1, D)
    kpos = idx[..., None] * SEL_LEN + jnp.arange(SEL_LEN)
    mask = (valid[..., None] & (kpos <= qpos[:, None, None])).reshape(B, G, T, -1)
    s = jnp.einsum('btgrd,bgtkd->bgrtk', q, kg).astype(jnp.float32) * SCALE
    p = _masked_probs(s, mask[:, :, None])
    return jnp.einsum('bgrtk,bgtkd->btgrd', p.astype(vg.dtype), vg)


def _nsa_cmp_sel(q, k_c, v_c, k_s, v_s, qpos, lp, sweep_queries):
    B, T, G, R, D = q.shape
    L = k_c.shape[1]
    L_pad = -(-L // SEL_LEN) * SEL_LEN
    pad = ((0, 0), (0, L_pad - L), (0, 0), (0, 0))
    k_c, v_c, k_s, v_s = jnp.pad(k_c, pad), jnp.pad(v_c, pad), jnp.pad(k_s, pad), jnp.pad(v_s, pad)
    k_cmp = _compress(k_c, lp['cmp_pe_k'], lp['cmp_w1_k'], lp['cmp_w2_k'])
    v_cmp = _compress(v_c, lp['cmp_pe_v'], lp['cmp_w1_v'], lp['cmp_w2_v'])
    n_cmp = k_cmp.shape[1]
    cmp_end = jnp.arange(n_cmp) * CMP_STRIDE + CMP_LEN - 1
    p_cmp = _attn_probs(q, k_cmp, cmp_end[None, :] <= qpos[:, None])
    o_cmp = _attn_out(p_cmp, v_cmp)
    n_sel = L_pad // SEL_LEN
    p_slc = jnp.einsum('bgrti,ij->bgtj', p_cmp, _cmp_to_sel(n_cmp, n_sel))
    blk = jnp.arange(n_sel)[None, :]
    cur = (qpos // SEL_LEN)[:, None]
    forced = (blk == 0) | (blk == cur) | (blk == cur - 1)
    score = jnp.where(blk <= cur, p_slc + jnp.where(forced, FORCE_BONUS, 0.0), NEG)
    top_s, top_i = lax.top_k(score, min(SEL_TOPK, n_sel))
    valid = top_s > 0.5 * NEG
    kb = k_s.reshape(B, n_sel, SEL_LEN, G, D)
    vb = v_s.reshape(B, n_sel, SEL_LEN, G, D)
    if sweep_queries:
        nq = T // SEL_Q_BLOCK
        xs = (q.reshape(B, nq, SEL_Q_BLOCK, G, R, D).swapaxes(0, 1),
              top_i.reshape(B, G, nq, SEL_Q_BLOCK, -1).transpose(2, 0, 1, 3, 4),
              valid.reshape(B, G, nq, SEL_Q_BLOCK, -1).transpose(2, 0, 1, 3, 4),
              qpos.reshape(nq, SEL_Q_BLOCK))
        o = lax.map(lambda a: _sel_attend(a[0], kb, vb, a[1], a[2], a[3]), xs)
        o_sel = o.swapaxes(0, 1).reshape(B, T, G, R, D)
    else:
        xs = (q[:, None], kb[:, None], vb[:, None], top_i[:, None], valid[:, None])
        o_sel = lax.map(lambda a: _sel_attend(a[0], a[1], a[2], a[3], a[4], qpos)[0], xs)
    return o_cmp, o_sel


def _window_banded(q, k, v):
    B, T, G, R, D = q.shape
    nb = T // WIN_Q_BLOCK
    pad = ((0, 0), (WINDOW, 0), (0, 0), (0, 0))
    kidx = jnp.arange(nb)[:, None] * WIN_Q_BLOCK + jnp.arange(WINDOW + WIN_Q_BLOCK)[None, :]
    kblk = jnp.pad(k, pad)[:, kidx]
    vblk = jnp.pad(v, pad)[:, kidx]
    kpos = kidx - WINDOW
    qpos = jnp.arange(T).reshape(nb, WIN_Q_BLOCK)
    d = qpos[:, :, None] - kpos[:, None, :]
    mask = (kpos[:, None, :] >= 0) & (d >= 0) & (d < WINDOW)
    o = _attn_out(_attn_probs(q.reshape(B, nb, WIN_Q_BLOCK, G, R, D), kblk, mask), vblk)
    return o.reshape(B, T, G, R, D)


def _window_dense(q, k, v, qpos, kpos):
    d = qpos[:, None] - kpos[None, :]
    return _attn_out(_attn_probs(q, k, (d >= 0) & (d < WINDOW)), v)


def _conv_module(u, buf, w_dw, b_dw, ln_g, ln_b):
    xp = jnp.concatenate([buf, u], axis=1)
    y = lax.conv_general_dilated(xp, w_dw[:, None, :], (1,), 'VALID',
                                 dimension_numbers=('NWC', 'WIO', 'NWC'),
                                 feature_group_count=CONV_CH) + b_dw
    return jax.nn.silu(_layernorm(y, ln_g, ln_b)), xp[:, -(CONV_WIDTH - 1):]


def _split_z(z, B, T):
    kv = lambda c: z[:, c:c + KVW].reshape(B, T, N_KV_HEADS, HEAD_DIM)
    q = z[:, COL_Q:COL_Q + QW].reshape(B, T, N_KV_HEADS, GROUP, HEAD_DIM)
    a = z[:, COL_GLU_A:COL_GLU_A + CONV_CH]
    b = z[:, COL_GLU_B:COL_GLU_B + CONV_CH]
    u = (a * jax.nn.sigmoid(b)).reshape(B, T, CONV_CH)
    return q, kv(COL_KC), kv(COL_VC), kv(COL_KS), kv(COL_VS), kv(COL_KW), kv(COL_VW), u


def _cmp_to_sel_t(n_cmp, n_sel, n_cmp_pad, n_sel_pad=None):
    cs = np.arange(n_cmp)[None, :] * CMP_STRIDE
    ss = np.arange(n_sel)[:, None] * SEL_LEN
    ov = np.clip(np.minimum(cs + CMP_LEN, ss + SEL_LEN) - np.maximum(cs, ss), 0, None)
    ct = np.zeros((n_sel_pad or n_sel, n_cmp_pad), np.float32)
    ct[:n_sel, :n_cmp] = ov.astype(np.float32) / CMP_LEN
    return jnp.asarray(ct, jnp.bfloat16)


def _compress_params(pe, w1, w2):
    w1cat = jnp.concatenate([w1[:CMP_STRIDE], w1[CMP_STRIDE:]], axis=-1).astype(jnp.bfloat16)
    c = jnp.einsum('ld,ldh->h', pe, w1).reshape(1, -1)
    return w1cat, c, w2.astype(jnp.bfloat16)


def _tail_page(z, col, nb):
    x = z[:, col:col + KVW].reshape(nb, DEC_SEQ * N_KV_HEADS, HEAD_DIM)
    x = jnp.pad(x, ((0, 0), (0, PAGE_ROWS - DEC_SEQ * N_KV_HEADS), (0, 0)))
    return x.reshape(nb * PAGE_ROWS, HEAD_DIM)


def _pad_cmp(x_cmp, n_pad):
    x = jnp.transpose(x_cmp, (0, 2, 1, 3))
    return jnp.pad(x, ((0, 0), (0, 0), (0, n_pad - x.shape[2]), (0, 0)))


def _layer(x, lp, past, page_table, prompt, tm):
    B, T, _ = x.shape
    n = B * T
    x2 = x.reshape(n, D_MODEL)
    z = _proj_in(x2, lp['norm_mix'], lp['w_in_p'], tm, 768)
    kv = lambda c: z[:, c:c + KVW].reshape(B, T, N_KV_HEADS, HEAD_DIM)
    k_c, v_c, k_s, v_s = kv(COL_KC), kv(COL_VC), kv(COL_KS), kv(COL_VS)
    cmp_k, cmp_v = lp['cmp_k'], lp['cmp_v']
    conv = (lp['conv_w'], lp['conv_b'], lp['conv_ln_g'], lp['conv_ln_b'])
    if prompt:
        n_ch = T // CMP_STRIDE
        ct = _cmp_to_sel_t(n_ch - 1, T // SEL_LEN, n_ch)
        kcmp, vcmp = _compress_seq(z, cmp_k[0], cmp_v[0], cmp_k[1], cmp_v[1], cmp_k[2], cmp_v[2], B, T)
        o_cmp, bias = _cmp_select(z, kcmp, vcmp, ct, B, T, 128)
        o_sel = _sel_attn(z, bias, B, T, 256, 256)
        o_win = _win_attn(z, B, T, 128)
        wl = min(WINDOW, T)
        new_wk, new_wv = kv(COL_KW)[:, T - wl:], kv(COL_VW)[:, T - wl:]
        conv_y, u_tail = _conv_seq(z, *conv, B, T, 256)
        new_conv = u_tail[:, CONV_HALO - (CONV_WIDTH - 1):]
    else:
        as_rows = lambda a: a.reshape(-1, HEAD_DIM)
        n_ch = -(-(PAST_LEN + T) // SEL_LEN) * SEL_LEN // CMP_STRIDE
        n_sel = -(-(PAST_LEN + T) // SEL_LEN)
        ct = _cmp_to_sel_t(n_ch - 1, n_sel, -(-n_ch // LANES) * LANES, -(-n_sel // 8) * 8)
        o_cmp, bias = _dec_cmp_select(
            page_table, as_rows(past['cmp_k']), as_rows(past['cmp_v']),
            _tail_page(z, COL_KC, B), _tail_page(z, COL_VC, B), z,
            cmp_k[0], cmp_v[0], cmp_k[1], cmp_v[1], cmp_k[2], cmp_v[2], ct, PAST_LEN)
        o_sel = _dec_sel_attn(page_table, as_rows(past['sel_k']), as_rows(past['sel_v']),
                              _tail_page(z, COL_KS, B), _tail_page(z, COL_VS, B), z, bias, PAST_LEN)
        wl = past['win_k'].shape[1]
        o_win, new_wk, new_wv = _dec_win_attn(
            z, as_rows(past['win_k']), as_rows(past['win_v']),
            _tail_page(z, COL_KW, B), _tail_page(z, COL_VW, B), B, PAST_LEN)
        new_wk = new_wk.reshape(B, wl, N_KV_HEADS, HEAD_DIM)
        new_wv = new_wv.reshape(B, wl, N_KV_HEADS, HEAD_DIM)
        conv_y, new_conv = _dec_conv(z, past['conv'], *conv)
    merged = _merge(o_cmp, o_sel, o_win, conv_y, z,
                    lp['w_nsa_out_b'], lp['w_conv_out_b'], 256, 512)
    h, hn = _out_proj(merged, x2, lp['w_out_b'], lp['norm_ffn'], 256)
    ia, ib, gt = _peer_route(hn, lp['peer_wq_b'], lp['peer_k1_b'], lp['peer_k2_b'], 256)
    ffn = _peer_ffn(hn, ia, ib, gt, lp['peer_u_b'], lp['peer_v_b'], min(n, 512), 8)
    y = _residual_norm(h, ffn, lp['norm_final'], 256)
    return y.reshape(B, T, D_MODEL), (k_c, v_c, k_s, v_s, new_wk, new_wv, new_conv)


def kernel(x_prompt, x_sample, cache_cmp_k, cache_cmp_v, cache_sel_k, cache_sel_v, cache_win_k, cache_win_v, state_conv, page_table, norm_mix, w_in, cmp_pe_k, cmp_w1_k, cmp_w2_k, cmp_pe_v, cmp_w1_v, cmp_w2_v, w_nsa_out, conv_w, conv_b, conv_ln_g, conv_ln_b, w_conv_out, w_out, norm_ffn, peer_wq, peer_k1, peer_k2, peer_u, peer_v, norm_final):
    assert DEPTH == 1
    bf = lambda w: w.astype(jnp.bfloat16)
    l = 0
    lp = {'norm_mix': norm_mix[l], 'w_in_p': _permute_w_in(w_in[l]),
          'cmp_k': _compress_params(cmp_pe_k[l], cmp_w1_k[l], cmp_w2_k[l]),
          'cmp_v': _compress_params(cmp_pe_v[l], cmp_w1_v[l], cmp_w2_v[l]),
          'w_nsa_out_b': bf(w_nsa_out[l]), 'conv_w': conv_w[l], 'conv_b': conv_b[l],
          'conv_ln_g': conv_ln_g[l], 'conv_ln_b': conv_ln_b[l],
          'w_conv_out_b': bf(w_conv_out[l]), 'w_out_b': bf(w_out[l]), 'norm_ffn': norm_ffn[l],
          'peer_wq_b': bf(peer_wq[l]), 'peer_k1_b': bf(peer_k1[l]), 'peer_k2_b': bf(peer_k2[l]),
          'peer_u_b': bf(peer_u[l]), 'peer_v_b': bf(peer_v[l]), 'norm_final': norm_final}
    past = {'cmp_k': cache_cmp_k[l], 'cmp_v': cache_cmp_v[l], 'sel_k': cache_sel_k[l],
            'sel_v': cache_sel_v[l], 'win_k': cache_win_k[l], 'win_v': cache_win_v[l],
            'conv': state_conv[l]}
    y_prompt, st_p = _layer(x_prompt, lp, None, None, True, 1024)
    y_sample, st_s = _layer(x_sample, lp, past, page_table, False, 256)
    p_ck, p_cv, p_sk, p_sv, p_wk, p_wv, p_conv = [a[None] for a in st_p]
    s_ck, s_cv, s_sk, s_sv, s_wk, s_wv, s_conv = [a[None] for a in st_s]
    return (y_prompt, y_sample, p_ck, s_ck, p_cv, s_cv, p_sk, s_sk, p_sv, s_sv,
            p_wk, s_wk, p_wv, s_wv, p_conv, s_conv)
```

```python
import functools

import jax
import jax.numpy as jnp
import numpy as np
from jax import lax
from jax.experimental import pallas as pl
from jax.experimental.pallas import tpu as pltpu

D_MODEL = 2048
BATCH = 2
SEQ = 4096
DEPTH = 1
DEC_BATCH = 32
DEC_SEQ = 8
PAST_LEN = 8192
PAGE_SIZE = 128
N_HEADS = 16
HEAD_DIM = 128
N_KV_HEADS = 4
GROUP = N_HEADS // N_KV_HEADS
CMP_STRIDE = 16
CMP_LEN = 2 * CMP_STRIDE
SEL_LEN = 64
SEL_SHIFT = SEL_LEN.bit_length() - 1
SEL_TOPK = 16
WINDOW = 512
WIN_Q_BLOCK = 128
SEL_Q_BLOCK = 64
FORCE_BONUS = 1e4
CONV_CH = D_MODEL // 2
CONV_WIDTH = 31
PEER_HEADS = 8
PEER_NKEYS = 128
PEER_DKEY = 256
PEER_TOPK = 16
PEER_CHUNK = 128
QW = N_HEADS * HEAD_DIM
KVW = N_KV_HEADS * HEAD_DIM
N_GATES = 3 * N_HEADS
SCALE = HEAD_DIM ** -0.5
EPS = 1e-6
NEG = -1e30

VMEM_LIMIT_BYTES = 48 * 1024 * 1024
VMEM_LIMIT_PEER_BYTES = 58 * 1024 * 1024

GATE_PAD = 256
COL_Q = 0
COL_KC = COL_Q + QW
COL_VC = COL_KC + KVW
COL_KS = COL_VC + KVW
COL_VS = COL_KS + KVW
COL_KW = COL_VS + KVW
COL_VW = COL_KW + KVW
COL_GLU_A = COL_VW + KVW
COL_GLU_B = COL_GLU_A + CONV_CH
COL_GM_A = COL_GLU_B + CONV_CH
COL_GM_B = COL_GM_A + D_MODEL
COL_G = COL_GM_B + D_MODEL
N_COLS = COL_G + GATE_PAD


def _permute_w_in(w_in):
    g0 = QW + 6 * KVW
    g1 = g0 + N_GATES
    parts = [w_in[:, :g0], w_in[:, g1:], w_in[:, g0:g1],
             jnp.zeros((w_in.shape[0], GATE_PAD - N_GATES), w_in.dtype)]
    return jnp.concatenate(parts, axis=1).astype(jnp.bfloat16)


def _proj_in_kernel(x_ref, g_ref, w_ref, o_ref, xn_ref):
    @pl.when(pl.program_id(1) == 0)
    def _():
        x = x_ref[...]
        ms = jnp.mean(x * x, axis=-1, keepdims=True)
        xn_ref[...] = (x * lax.rsqrt(ms + EPS) * g_ref[...]).astype(jnp.bfloat16)

    o_ref[...] = jnp.dot(xn_ref[...], w_ref[...], preferred_element_type=jnp.float32)


def _proj_in(x, gain, w_bf16, tm, tn):
    n, d = x.shape
    nc = w_bf16.shape[1]
    return pl.pallas_call(
        _proj_in_kernel,
        out_shape=jax.ShapeDtypeStruct((n, nc), jnp.float32),
        grid=(n // tm, nc // tn),
        in_specs=[pl.BlockSpec((tm, d), lambda i, j: (i, 0)),
                  pl.BlockSpec((1, d), lambda i, j: (0, 0)),
                  pl.BlockSpec((d, tn), lambda i, j: (0, j))],
        out_specs=pl.BlockSpec((tm, tn), lambda i, j: (i, j)),
        scratch_shapes=[pltpu.VMEM((tm, d), jnp.bfloat16)],
        compiler_params=pltpu.CompilerParams(
            dimension_semantics=("parallel", "arbitrary"),
            vmem_limit_bytes=VMEM_LIMIT_BYTES),
        name="proj_in",
    )(x, gain.reshape(1, d), w_bf16)


_NT = (((1,), (1,)), ((), ()))
BRANCH_DTYPE = jnp.bfloat16
MASK_BIAS = -1e9


def _stack_heads(q):
    return jnp.concatenate([q[:, r * HEAD_DIM:(r + 1) * HEAD_DIM] for r in range(GROUP)], axis=0)


def _unstack_heads(o, tq):
    return jnp.concatenate([o[r * tq:(r + 1) * tq] for r in range(GROUP)], axis=1)


def _cmp_select_kernel(q_ref, kc_ref, vc_ref, ct_ref, ocmp_ref, bias_ref, *, tq):
    t0 = pl.program_id(2) * tq
    q4 = _stack_heads(q_ref[...]).astype(jnp.bfloat16)
    kc = kc_ref[0, 0].astype(jnp.bfloat16)
    vc = vc_ref[0, 0].astype(jnp.bfloat16)
    s = lax.dot_general(q4, kc, _NT, preferred_element_type=jnp.float32) * SCALE
    t = t0 + (lax.broadcasted_iota(jnp.int32, s.shape, 0) & (tq - 1))
    i = lax.broadcasted_iota(jnp.int32, s.shape, 1)
    valid = (i * CMP_STRIDE + (CMP_LEN - 1)) <= t
    s = jnp.where(valid, s, NEG)
    m = jnp.max(s, axis=-1, keepdims=True)
    e = jnp.where(valid, jnp.exp(s - m), 0.0)
    p = (e / jnp.maximum(jnp.sum(e, axis=-1, keepdims=True), 1e-30)).astype(jnp.bfloat16)
    ocmp_ref[...] = _unstack_heads(jnp.dot(p, vc, preferred_element_type=jnp.float32),
                                   tq).astype(ocmp_ref.dtype)

    ct = ct_ref[...]
    n_sel = ct.shape[0]
    pslc = lax.dot_general(ct, p[0:tq], _NT, preferred_element_type=jnp.float32)
    for r in range(1, GROUP):
        pslc = pslc + lax.dot_general(ct, p[r * tq:(r + 1) * tq], _NT,
                                      preferred_element_type=jnp.float32)
    j = lax.broadcasted_iota(jnp.int32, (n_sel, tq), 0)
    cur = (t0 + lax.broadcasted_iota(jnp.int32, (n_sel, tq), 1)) >> SEL_SHIFT
    forced = (j == 0) | (j == cur) | (j == cur - 1)
    score = jnp.where(j <= cur, pslc + jnp.where(forced, FORCE_BONUS, 0.0), NEG)
    rank = jnp.zeros((n_sel, tq), jnp.float32)
    for jp in range(n_sel):
        sj = score[jp:jp + 1, :]
        rank = rank + jnp.where(sj > score, 1.0, jnp.where((sj == score) & (j > jp), 1.0, 0.0))
    bias = jnp.where((rank < SEL_TOPK) & (j <= cur), 0.0, MASK_BIAS)
    bias = jnp.concatenate([bias, jnp.zeros((HEAD_DIM - n_sel, tq), jnp.float32)], axis=0)
    bias_ref[0, 0] = bias.T.astype(jnp.bfloat16)


def _cmp_select(z, kcmp, vcmp, ct, B, T, tq):
    n_cp = kcmp.shape[2]
    nq = T // tq
    return pl.pallas_call(
        functools.partial(_cmp_select_kernel, tq=tq),
        out_shape=(jax.ShapeDtypeStruct((B * T, QW), BRANCH_DTYPE),
                   jax.ShapeDtypeStruct((B, N_KV_HEADS, T, HEAD_DIM), jnp.bfloat16)),
        grid=(B, N_KV_HEADS, nq),
        in_specs=[pl.BlockSpec((tq, GROUP * HEAD_DIM), lambda b, g, qi: (b * nq + qi, g)),
                  pl.BlockSpec((1, 1, n_cp, HEAD_DIM), lambda b, g, qi: (b, g, 0, 0)),
                  pl.BlockSpec((1, 1, n_cp, HEAD_DIM), lambda b, g, qi: (b, g, 0, 0)),
                  pl.BlockSpec(ct.shape, lambda b, g, qi: (0, 0))],
        out_specs=(pl.BlockSpec((tq, GROUP * HEAD_DIM), lambda b, g, qi: (b * nq + qi, g)),
                   pl.BlockSpec((1, 1, tq, HEAD_DIM), lambda b, g, qi: (b, g, qi, 0))),
        compiler_params=pltpu.CompilerParams(
            dimension_semantics=("parallel", "parallel", "arbitrary"),
            vmem_limit_bytes=VMEM_LIMIT_BYTES),
        name="cmp_select",
    )(z, kcmp, vcmp, ct)


def _sel_attn_kernel(q_ref, bias_ref, k_ref, v_ref, o_ref, kaug, vt, acc_sc, *, tq, kc):
    qi = pl.program_id(2)
    t0 = qi * tq
    T = k_ref.shape[0]
    nq = GROUP * tq

    @pl.when(qi == 0)
    def _():
        kaug[:, 0:HEAD_DIM] = k_ref[...].astype(jnp.bfloat16)
        blk = lax.broadcasted_iota(jnp.int32, (T, HEAD_DIM), 0) >> SEL_SHIFT
        col = lax.broadcasted_iota(jnp.int32, (T, HEAD_DIM), 1)
        kaug[:, HEAD_DIM:2 * HEAD_DIM] = jnp.where(blk == col, 1.0, 0.0).astype(jnp.bfloat16)
        for c in range(T // kc):
            vt[c] = v_ref[c * kc:(c + 1) * kc, :].T.astype(jnp.bfloat16)

    q = q_ref[...]
    bias = bias_ref[0, 0]
    qa = jnp.concatenate(
        [jnp.concatenate([q[:, r * HEAD_DIM:(r + 1) * HEAD_DIM].astype(jnp.bfloat16), bias], axis=1)
         for r in range(GROUP)], axis=0)
    acc_sc[...] = jnp.zeros(acc_sc.shape, jnp.float32)
    t = t0 + (lax.broadcasted_iota(jnp.int32, (kc, nq), 1) & (tq - 1))
    key = lax.broadcasted_iota(jnp.int32, (kc, nq), 0)

    def step(c, m_old, l_old, causal):
        k0 = pl.multiple_of(c * kc, kc)
        s = lax.dot_general(kaug[pl.ds(k0, kc), :], qa, _NT,
                            preferred_element_type=jnp.float32) * SCALE
        if causal:
            s = jnp.where(k0 + key <= t, s, NEG)
        m_new = jnp.maximum(m_old, jnp.max(s, axis=0, keepdims=True))
        alpha = jnp.exp(m_old - m_new)
        p = jnp.exp(s - m_new)
        l_new = alpha * l_old + jnp.sum(p, axis=0, keepdims=True)
        acc_sc[...] = alpha * acc_sc[...] + jnp.dot(vt[c], p.astype(jnp.bfloat16),
                                                    preferred_element_type=jnp.float32)
        return m_new, l_new

    last = (t0 + tq - 1) // kc
    init = (jnp.full((1, nq), NEG, jnp.float32), jnp.zeros((1, nq), jnp.float32))
    def pair(i, ml):
        m, l = step(2 * i, ml[0], ml[1], False)
        return step(2 * i + 1, m, l, False)

    m, l = lax.fori_loop(0, last // 2, pair, init)
    m, l = lax.fori_loop(2 * (last // 2), last, lambda c, ml: step(c, ml[0], ml[1], False), (m, l))
    m, l = step(last, m, l, True)
    o_ref[...] = _unstack_heads((acc_sc[...] / jnp.maximum(l, 1e-30)).T, tq).astype(o_ref.dtype)


def _sel_attn(z, bias, B, T, tq, kc):
    nq = T // tq
    ck = COL_KS // HEAD_DIM
    cv = COL_VS // HEAD_DIM
    return pl.pallas_call(
        functools.partial(_sel_attn_kernel, tq=tq, kc=kc),
        out_shape=jax.ShapeDtypeStruct((B * T, QW), BRANCH_DTYPE),
        grid=(B, N_KV_HEADS, nq),
        in_specs=[pl.BlockSpec((tq, GROUP * HEAD_DIM), lambda b, g, qi: (b * nq + qi, g)),
                  pl.BlockSpec((1, 1, tq, HEAD_DIM), lambda b, g, qi: (b, g, qi, 0)),
                  pl.BlockSpec((T, HEAD_DIM), lambda b, g, qi: (b, ck + g)),
                  pl.BlockSpec((T, HEAD_DIM), lambda b, g, qi: (b, cv + g))],
        out_specs=pl.BlockSpec((tq, GROUP * HEAD_DIM), lambda b, g, qi: (b * nq + qi, g)),
        scratch_shapes=[pltpu.VMEM((T, 2 * HEAD_DIM), jnp.bfloat16),
                        pltpu.VMEM((T // kc, HEAD_DIM, kc), jnp.bfloat16),
                        pltpu.VMEM((HEAD_DIM, GROUP * tq), jnp.float32)],
        compiler_params=pltpu.CompilerParams(
            dimension_semantics=("parallel", "parallel", "arbitrary"),
            vmem_limit_bytes=VMEM_LIMIT_BYTES),
        name="sel_attn",
    )(z, bias, z, z)


def _win_attn_kernel(q_ref, k_ref, v_ref, o_ref, kpad, vt, *, tq):
    qi = pl.program_id(2)
    t0 = pl.multiple_of(qi * tq, tq)
    T = k_ref.shape[0]
    n_pad = WINDOW // tq
    n_span = n_pad + 1
    span = n_span * tq

    @pl.when(qi == 0)
    def _():
        kpad[0:WINDOW, :] = jnp.zeros((WINDOW, HEAD_DIM), jnp.bfloat16)
        kpad[WINDOW:WINDOW + T, :] = k_ref[...].astype(jnp.bfloat16)
        for c in range(n_pad):
            vt[c] = jnp.zeros((HEAD_DIM, tq), jnp.bfloat16)
        for c in range(T // tq):
            vt[n_pad + c] = v_ref[c * tq:(c + 1) * tq, :].T.astype(jnp.bfloat16)

    q4 = _stack_heads(q_ref[...]).astype(jnp.bfloat16)
    s = lax.dot_general(kpad[pl.ds(t0, span), :], q4, _NT,
                        preferred_element_type=jnp.float32) * SCALE
    kpos = t0 - WINDOW + lax.broadcasted_iota(jnp.int32, s.shape, 0)
    t = t0 + (lax.broadcasted_iota(jnp.int32, s.shape, 1) & (tq - 1))
    d = t - kpos
    valid = (kpos >= 0) & (d >= 0) & (d < WINDOW)
    s = jnp.where(valid, s, NEG)
    m = jnp.max(s, axis=0, keepdims=True)
    e = jnp.where(valid, jnp.exp(s - m), 0.0)
    p = (e / jnp.maximum(jnp.sum(e, axis=0, keepdims=True), 1e-30)).astype(jnp.bfloat16)
    o_t = None
    for c in range(n_span):
        part = jnp.dot(vt[qi + c], p[c * tq:(c + 1) * tq], preferred_element_type=jnp.float32)
        o_t = part if o_t is None else o_t + part
    o_ref[...] = _unstack_heads(o_t.T, tq).astype(o_ref.dtype)


def _win_attn(z, B, T, tq):
    nq = T // tq
    ck = COL_KW // HEAD_DIM
    cv = COL_VW // HEAD_DIM
    return pl.pallas_call(
        functools.partial(_win_attn_kernel, tq=tq),
        out_shape=jax.ShapeDtypeStruct((B * T, QW), BRANCH_DTYPE),
        grid=(B, N_KV_HEADS, nq),
        in_specs=[pl.BlockSpec((tq, GROUP * HEAD_DIM), lambda b, g, qi: (b * nq + qi, g)),
                  pl.BlockSpec((T, HEAD_DIM), lambda b, g, qi: (b, ck + g)),
                  pl.BlockSpec((T, HEAD_DIM), lambda b, g, qi: (b, cv + g))],
        out_specs=pl.BlockSpec((tq, GROUP * HEAD_DIM), lambda b, g, qi: (b * nq + qi, g)),
        scratch_shapes=[pltpu.VMEM((T + WINDOW, HEAD_DIM), jnp.bfloat16),
                        pltpu.VMEM(((T + WINDOW) // tq, HEAD_DIM, tq), jnp.bfloat16)],
        compiler_params=pltpu.CompilerParams(
            dimension_semantics=("parallel", "parallel", "arbitrary"),
            vmem_limit_bytes=VMEM_LIMIT_BYTES),
        name="win_attn",
    )(z, z, z)


PAGES_PER_STEP = 16
CHUNKS_PER_PAGE = PAGE_SIZE // CMP_STRIDE
PAGE_ROWS = PAGE_SIZE * N_KV_HEADS
DEC_ROWS = N_HEADS * DEC_SEQ
assert DEC_ROWS == 128 and DEC_SEQ == 8


def _head_rows(ref, g, n_tokens):
    return ref[pl.ds(g, n_tokens, stride=N_KV_HEADS), :]


def _dec_rows(q):
    return jnp.concatenate([q[:, (g * GROUP + r) * HEAD_DIM:(g * GROUP + r + 1) * HEAD_DIM]
                            for r in range(GROUP) for g in range(N_KV_HEADS)], axis=0)


def _dec_cols(o):
    return jnp.concatenate(
        [o[(r * N_KV_HEADS + g) * DEC_SEQ:(r * N_KV_HEADS + g + 1) * DEC_SEQ]
         for g in range(N_KV_HEADS) for r in range(GROUP)], axis=1)


def _dec_row_ids():
    row = lax.broadcasted_iota(jnp.int32, (DEC_ROWS, 1), 0)
    return (row >> 3) & (N_KV_HEADS - 1), row & (DEC_SEQ - 1)


def _group_scores(qall, keys_of, row_g):
    out = None
    for g in range(N_KV_HEADS):
        s = lax.dot_general(qall, keys_of(g), _NT, preferred_element_type=jnp.float32)
        s = jnp.where(row_g == g, s, 0.0)
        out = s if out is None else out + s
    return out


def _group_values(p, vals_of, row_g):
    out = None
    for g in range(N_KV_HEADS):
        pg = jnp.where(row_g == g, p, 0.0).astype(jnp.bfloat16)
        o = jnp.dot(pg, vals_of(g), preferred_element_type=jnp.float32)
        out = o if out is None else out + o
    return out


TAP_ROWS = 2 * N_KV_HEADS


def _chunk_hidden(page_refs, w1_ref):
    chunk_rows = CMP_STRIDE * N_KV_HEADS
    n_rows = len(page_refs) * CHUNKS_PER_PAGE * TAP_ROWS
    odd = (lax.broadcasted_iota(jnp.int32, (n_rows, 1), 0) & N_KV_HEADS) != 0
    pieces = []
    for p in range(CMP_STRIDE // 2):
        x = jnp.concatenate(
            [pg[c * chunk_rows + p * TAP_ROWS:c * chunk_rows + (p + 1) * TAP_ROWS, :]
             for pg in page_refs for c in range(CHUNKS_PER_PAGE)], axis=0)
        pieces.append(jnp.where(odd, 0.0, x).astype(jnp.bfloat16))
        pieces.append(jnp.where(odd, x, 0.0).astype(jnp.bfloat16))
    w1 = w1_ref[...].reshape(CMP_STRIDE * HEAD_DIM, 2 * HEAD_DIM)
    return jnp.dot(jnp.concatenate(pieces, axis=1), w1, preferred_element_type=jnp.float32)


def _hidden_rows(h_sc, g, first_chunk, n):
    even = h_sc[pl.ds(first_chunk * TAP_ROWS + g, n, stride=TAP_ROWS), :]
    odd = h_sc[pl.ds(first_chunk * TAP_ROWS + N_KV_HEADS + g, n, stride=TAP_ROWS), :]
    return even + odd


def _store_hidden(h_lo, h_hi, rows, h):
    h_lo[rows, :] = h[:, 0:HEAD_DIM]
    h_hi[rows, :] = h[:, HEAD_DIM:2 * HEAD_DIM]


def _dec_cmp_kernel(pt_ref, *refs, past_len, n_sel):
    nps = PAGES_PER_STEP
    kp, vp = refs[0:nps], refs[nps:2 * nps]
    (kt_ref, vt_ref, q_ref, w1k_ref, w1v_ref, ck_ref, cv_ref, w2k_ref, w2v_ref, ct_ref,
     ocmp_ref, bias_ref, hk_lo, hk_hi, hv_lo, hv_hi, score_sc) = refs[2 * nps:]
    j = pl.program_id(1)
    rows = nps * CHUNKS_PER_PAGE * TAP_ROWS
    base = (past_len // PAGE_SIZE) * CHUNKS_PER_PAGE
    tail_end = (base + CHUNKS_PER_PAGE) * TAP_ROWS

    step_rows = pl.ds(pl.multiple_of(j * rows, rows), rows)
    for pages, w1_ref, h_lo, h_hi in ((kp, w1k_ref, hk_lo, hk_hi), (vp, w1v_ref, hv_lo, hv_hi)):
        _store_hidden(h_lo, h_hi, step_rows, _chunk_hidden(pages, w1_ref))

    @pl.when(j == pl.num_programs(1) - 1)
    def _():
        for t_ref, w1_ref, h_lo, h_hi in ((kt_ref, w1k_ref, hk_lo, hk_hi),
                                          (vt_ref, w1v_ref, hv_lo, hv_hi)):
            _store_hidden(h_lo, h_hi, slice(base * TAP_ROWS, tail_end), _chunk_hidden([t_ref], w1_ref))
            zeros = jnp.zeros((h_lo.shape[0] - tail_end, HEAD_DIM), jnp.float32)
            h_lo[tail_end:, :] = zeros
            h_hi[tail_end:, :] = zeros
        ncp = ct_ref.shape[1]
        kc, vc = [], []
        for g in range(N_KV_HEADS):
            for h_lo, h_hi, c_ref, w2_ref, dst in ((hk_lo, hk_hi, ck_ref, w2k_ref, kc),
                                                   (hv_lo, hv_hi, cv_ref, w2v_ref, vc)):
                hh = _hidden_rows(h_lo, g, 0, ncp) + _hidden_rows(h_hi, g, 1, ncp) + c_ref[...]
                dst.append(jnp.dot(jax.nn.gelu(hh).astype(jnp.bfloat16), w2_ref[...],
                                   preferred_element_type=jnp.float32).astype(jnp.bfloat16))
        qall = _dec_rows(q_ref[...]).astype(jnp.bfloat16)
        row_g, row_t = _dec_row_ids()
        s = _group_scores(qall, lambda g: kc[g], row_g) * SCALE
        i = lax.broadcasted_iota(jnp.int32, s.shape, 1)
        valid = (i * CMP_STRIDE + (CMP_LEN - 1)) <= (past_len + row_t)
        s = jnp.where(valid, s, NEG)
        m = jnp.max(s, axis=-1, keepdims=True)
        e = jnp.where(valid, jnp.exp(s - m), 0.0)
        p = e / jnp.maximum(jnp.sum(e, axis=-1, keepdims=True), 1e-30)
        ocmp_ref[...] = _dec_cols(_group_values(p, lambda g: vc[g], row_g))

        x = lax.dot_general(ct_ref[...], p.astype(jnp.bfloat16), _NT,
                            preferred_element_type=jnp.float32)
        quarter = DEC_ROWS // GROUP
        pslc = x
        for r in range(1, GROUP):
            pslc = pslc + pltpu.roll(x, r * quarter, 1)
        shape = pslc.shape
        jj = lax.broadcasted_iota(jnp.int32, shape, 0)
        cur = (past_len + (lax.broadcasted_iota(jnp.int32, shape, 1) & (DEC_SEQ - 1))) >> SEL_SHIFT
        forced = (jj == 0) | (jj == cur) | (jj == cur - 1)
        score = jnp.where(jj <= cur, pslc + jnp.where(forced, FORCE_BONUS, 0.0), NEG)
        score_sc[...] = score

        def rank_step(jp, rank):
            sj = score_sc[pl.ds(jp, 1), :]
            return rank + jnp.where(sj > score, 1.0, jnp.where((sj == score) & (jj > jp), 1.0, 0.0))

        rank = lax.fori_loop(0, n_sel, rank_step, jnp.zeros(shape, jnp.float32))
        bias = jnp.where((rank < SEL_TOPK) & (jj <= cur), 0.0, MASK_BIAS)
        pad = jnp.full((bias_ref.shape[2] - shape[0], shape[1]), MASK_BIAS, jnp.float32)
        bias_ref[0] = jnp.concatenate([bias, pad], axis=0).T.astype(jnp.bfloat16)


PAGE_BLOCK = (PAGE_ROWS, HEAD_DIM)


def _page_specs(n):
    return [pl.BlockSpec(PAGE_BLOCK, lambda b, j, pt, i=i: (pt[b, j * PAGES_PER_STEP + i], 0))
            for i in range(n)]


_TAIL_SPEC = pl.BlockSpec(PAGE_BLOCK, lambda b, j, pt: (b, 0))


def _dec_cmp_select(page_table, pool_k, pool_v, tail_k, tail_v, z, w1k, w1v, ck, cv, w2k, w2v, ct,
                    past_len):
    nb = page_table.shape[0]
    n_pages = past_len // PAGE_SIZE
    n_sel, ncp = ct.shape
    n_sel_real = -(-(past_len + DEC_SEQ) // SEL_LEN)
    full2 = lambda a: pl.BlockSpec(a.shape, lambda b, j, pt: (0, 0))
    full3 = lambda a: pl.BlockSpec(a.shape, lambda b, j, pt: (0, 0, 0))
    tail = _TAIL_SPEC
    n_blocks = 2 * HEAD_DIM
    grid_spec = pltpu.PrefetchScalarGridSpec(
        num_scalar_prefetch=1,
        grid=(nb, n_pages // PAGES_PER_STEP),
        in_specs=(_page_specs(PAGES_PER_STEP) + _page_specs(PAGES_PER_STEP)
                  + [tail, tail, pl.BlockSpec((DEC_SEQ, QW), lambda b, j, pt: (b, 0)),
                     full3(w1k), full3(w1v), full2(ck), full2(cv), full2(w2k), full2(w2v), full2(ct)]),
        out_specs=(pl.BlockSpec((DEC_SEQ, QW), lambda b, j, pt: (b, 0)),
                   pl.BlockSpec((1, DEC_ROWS, n_blocks), lambda b, j, pt: (b, 0, 0))),
        scratch_shapes=[pltpu.VMEM(((ncp + 8) * TAP_ROWS, HEAD_DIM), jnp.float32)] * 4
        + [pltpu.VMEM((n_sel, DEC_ROWS), jnp.float32)])
    return pl.pallas_call(
        functools.partial(_dec_cmp_kernel, past_len=past_len, n_sel=n_sel_real),
        out_shape=(jax.ShapeDtypeStruct((nb * DEC_SEQ, QW), jnp.float32),
                   jax.ShapeDtypeStruct((nb, DEC_ROWS, n_blocks), jnp.bfloat16)),
        grid_spec=grid_spec,
        compiler_params=pltpu.CompilerParams(
            dimension_semantics=("parallel", "arbitrary"),
            vmem_limit_bytes=VMEM_LIMIT_BYTES),
        name="dec_cmp_select",
    )(page_table, *([pool_k] * PAGES_PER_STEP), *([pool_v] * PAGES_PER_STEP), tail_k, tail_v, z,
      w1k, w1v, ck, cv, w2k, w2v, ct)


def _dec_sel_kernel(pt_ref, *refs, past_len):
    nps = PAGES_PER_STEP
    kp, vp = refs[0:nps], refs[nps:2 * nps]
    kt_ref, vt_ref, q_ref, bias_ref, o_ref, m_sc, l_sc, acc_sc = refs[2 * nps:]
    j = pl.program_id(1)
    qall = _dec_rows(q_ref[...]).astype(jnp.bfloat16)
    row_g, row_t = _dec_row_ids()
    selcols = bias_ref[0]

    @pl.when(j == 0)
    def _():
        m_sc[...] = jnp.full(m_sc.shape, NEG, jnp.float32)
        l_sc[...] = jnp.zeros(l_sc.shape, jnp.float32)
        acc_sc[...] = jnp.zeros(acc_sc.shape, jnp.float32)

    def update(k_refs, v_refs, first_blk, causal):
        grp = lambda ref, g: _head_rows(ref, g, PAGE_SIZE).astype(jnp.bfloat16)
        s = jnp.concatenate([_group_scores(qall, functools.partial(grp, kr), row_g)
                             for kr in k_refs], axis=1) * SCALE
        nk = s.shape[1]
        n_blocks = selcols.shape[1]
        blk = first_blk + (lax.broadcasted_iota(jnp.int32, (n_blocks, nk), 1) >> SEL_SHIFT)
        onehot = jnp.where(lax.broadcasted_iota(jnp.int32, (n_blocks, nk), 0) == blk, 1.0, 0.0)
        s = s + jnp.dot(selcols, onehot.astype(jnp.bfloat16), preferred_element_type=jnp.float32)
        if causal:
            kpos = first_blk * SEL_LEN + lax.broadcasted_iota(jnp.int32, s.shape, 1)
            s = jnp.where(kpos <= past_len + row_t, s, NEG)
        m_old = m_sc[...]
        m_new = jnp.maximum(m_old, jnp.max(s, axis=-1, keepdims=True))
        alpha = jnp.exp(m_old - m_new)
        p = jnp.exp(s - m_new)
        l_sc[...] = alpha * l_sc[...] + jnp.sum(p, axis=-1, keepdims=True)
        pv = None
        for n, vr in enumerate(v_refs):
            o = _group_values(p[:, n * PAGE_SIZE:(n + 1) * PAGE_SIZE], functools.partial(grp, vr), row_g)
            pv = o if pv is None else pv + o
        acc_sc[...] = alpha * acc_sc[...] + pv
        m_sc[...] = m_new

    update(kp, vp, j * (nps * PAGE_SIZE // SEL_LEN), False)

    @pl.when(j == pl.num_programs(1) - 1)
    def _():
        update([kt_ref], [vt_ref], past_len // SEL_LEN, True)
        o_ref[...] = _dec_cols(acc_sc[...] / jnp.maximum(l_sc[...], 1e-30))


def _dec_sel_attn(page_table, pool_k, pool_v, tail_k, tail_v, z, bias, past_len):
    nb = page_table.shape[0]
    n_pages = past_len // PAGE_SIZE
    tail = _TAIL_SPEC
    grid_spec = pltpu.PrefetchScalarGridSpec(
        num_scalar_prefetch=1,
        grid=(nb, n_pages // PAGES_PER_STEP),
        in_specs=(_page_specs(PAGES_PER_STEP) + _page_specs(PAGES_PER_STEP)
                  + [tail, tail, pl.BlockSpec((DEC_SEQ, QW), lambda b, j, pt: (b, 0)),
                     pl.BlockSpec((1,) + bias.shape[1:], lambda b, j, pt: (b, 0, 0))]),
        out_specs=pl.BlockSpec((DEC_SEQ, QW), lambda b, j, pt: (b, 0)),
        scratch_shapes=[pltpu.VMEM((DEC_ROWS, 1), jnp.float32),
                        pltpu.VMEM((DEC_ROWS, 1), jnp.float32),
                        pltpu.VMEM((DEC_ROWS, HEAD_DIM), jnp.float32)])
    return pl.pallas_call(
        functools.partial(_dec_sel_kernel, past_len=past_len),
        out_shape=jax.ShapeDtypeStruct((nb * DEC_SEQ, QW), jnp.float32),
        grid_spec=grid_spec,
        compiler_params=pltpu.CompilerParams(
            dimension_semantics=("parallel", "arbitrary"),
            vmem_limit_bytes=VMEM_LIMIT_BYTES),
        name="dec_sel_attn",
    )(page_table, *([pool_k] * PAGES_PER_STEP), *([pool_v] * PAGES_PER_STEP), tail_k, tail_v, z, bias)


def _dec_win_kernel(q_ref, wk_ref, wv_ref, kt_ref, vt_ref, o_ref, nwk_ref, nwv_ref, *, past_len):
    wl = wk_ref.shape[0] // N_KV_HEADS
    qall = _dec_rows(q_ref[...]).astype(jnp.bfloat16)
    row_g, row_t = _dec_row_ids()

    def cat(w_ref, t_ref, g):
        return jnp.concatenate([_head_rows(w_ref, g, wl), _head_rows(t_ref, g, PAGE_SIZE)],
                               axis=0).astype(jnp.bfloat16)

    s = _group_scores(qall, functools.partial(cat, wk_ref, kt_ref), row_g) * SCALE
    kpos = past_len - wl + lax.broadcasted_iota(jnp.int32, s.shape, 1)
    d = past_len + row_t - kpos
    valid = (d >= 0) & (d < WINDOW)
    s = jnp.where(valid, s, NEG)
    m = jnp.max(s, axis=-1, keepdims=True)
    e = jnp.where(valid, jnp.exp(s - m), 0.0)
    p = e / jnp.maximum(jnp.sum(e, axis=-1, keepdims=True), 1e-30)
    o_ref[...] = _dec_cols(_group_values(p, functools.partial(cat, wv_ref, vt_ref), row_g))
    keep = (wl - DEC_SEQ) * N_KV_HEADS
    for w_ref, t_ref, n_ref in ((wk_ref, kt_ref, nwk_ref), (wv_ref, vt_ref, nwv_ref)):
        n_ref[0:keep, :] = w_ref[wl * N_KV_HEADS - keep:wl * N_KV_HEADS, :]
        n_ref[keep:wl * N_KV_HEADS, :] = t_ref[0:DEC_SEQ * N_KV_HEADS, :]


def _dec_win_attn(z, win_k, win_v, tail_k, tail_v, nb, past_len):
    cache = pl.BlockSpec((win_k.shape[0] // nb, HEAD_DIM), lambda b: (b, 0))
    tail = pl.BlockSpec(PAGE_BLOCK, lambda b: (b, 0))
    rows = pl.BlockSpec((DEC_SEQ, QW), lambda b: (b, 0))
    return pl.pallas_call(
        functools.partial(_dec_win_kernel, past_len=past_len),
        out_shape=(jax.ShapeDtypeStruct((nb * DEC_SEQ, QW), jnp.float32),
                   jax.ShapeDtypeStruct(win_k.shape, jnp.float32),
                   jax.ShapeDtypeStruct(win_v.shape, jnp.float32)),
        grid=(nb,),
        in_specs=[rows, cache, cache, tail, tail],
        out_specs=(rows, cache, cache),
        compiler_params=pltpu.CompilerParams(
            dimension_semantics=("parallel",),
            vmem_limit_bytes=VMEM_LIMIT_BYTES),
        name="dec_win_attn",
    )(z, win_k, win_v, tail_k, tail_v)


def _compress_kernel(xk_ref, xv_ref, w1k_ref, w1v_ref, ck_ref, cv_ref, w2k_ref, w2v_ref,
                     ko_ref, vo_ref, h_sc):
    nch = xk_ref.shape[0] // CMP_STRIDE
    h_sc[nch:nch + 8, :] = jnp.zeros((8, 2 * HEAD_DIM), jnp.float32)
    for x_ref, w1_ref, c_ref, w2_ref, o_ref in ((xk_ref, w1k_ref, ck_ref, w2k_ref, ko_ref),
                                                (xv_ref, w1v_ref, cv_ref, w2v_ref, vo_ref)):
        acc = None
        for l in range(CMP_STRIDE):
            x = x_ref[pl.ds(l, nch, stride=CMP_STRIDE), :].astype(jnp.bfloat16)
            h = jnp.dot(x, w1_ref[l], preferred_element_type=jnp.float32)
            acc = h if acc is None else acc + h
        h_sc[0:nch, :] = acc
        hh = h_sc[0:nch, 0:HEAD_DIM] + h_sc[1:nch + 1, HEAD_DIM:2 * HEAD_DIM] + c_ref[...]
        o_ref[0, 0] = jnp.dot(jax.nn.gelu(hh).astype(jnp.bfloat16), w2_ref[...],
                              preferred_element_type=jnp.float32)


def _compress_seq(z, w1k, w1v, ck, cv, w2k, w2v, B, T):
    nch = T // CMP_STRIDE
    ck0 = COL_KC // HEAD_DIM
    cv0 = COL_VC // HEAD_DIM
    full2 = lambda a: pl.BlockSpec(a.shape, lambda b, g: (0, 0))
    full3 = lambda a: pl.BlockSpec(a.shape, lambda b, g: (0, 0, 0))
    out = jax.ShapeDtypeStruct((B, N_KV_HEADS, nch, HEAD_DIM), jnp.float32)
    ospec = pl.BlockSpec((1, 1, nch, HEAD_DIM), lambda b, g: (b, g, 0, 0))
    return pl.pallas_call(
        _compress_kernel,
        out_shape=(out, out),
        grid=(B, N_KV_HEADS),
        in_specs=[pl.BlockSpec((T, HEAD_DIM), lambda b, g: (b, ck0 + g)),
                  pl.BlockSpec((T, HEAD_DIM), lambda b, g: (b, cv0 + g)),
                  full3(w1k), full3(w1v), full2(ck), full2(cv), full2(w2k), full2(w2v)],
        out_specs=(ospec, ospec),
        scratch_shapes=[pltpu.VMEM((nch + 8, 2 * HEAD_DIM), jnp.float32)],
        compiler_params=pltpu.CompilerParams(
            dimension_semantics=("parallel", "parallel"),
            vmem_limit_bytes=VMEM_LIMIT_BYTES),
        name="compress_seq",
    )(z, z, w1k, w1v, ck, cv, w2k, w2v)


CONV_HALO = 32
LANES = 128


def _conv_taps(xs, w_ref, first, rows, y_sc):
    for cb in range(CONV_CH // LANES):
        sl = slice(cb * LANES, (cb + 1) * LANES)
        acc = None
        for k in range(CONV_WIDTH):
            term = w_ref[k:k + 1, sl] * xs[first + k:first + k + rows, sl]
            acc = term if acc is None else acc + term
        y_sc[:, sl] = acc


def _conv_post(y, cb_ref, lg_ref, lb_ref):
    y = y + cb_ref[...]
    mu = jnp.mean(y, axis=-1, keepdims=True)
    var = jnp.mean(jnp.square(y - mu), axis=-1, keepdims=True)
    yn = (y - mu) * lax.rsqrt(var + EPS) * lg_ref[...] + lb_ref[...]
    return yn * jax.nn.sigmoid(yn)


def _conv_kernel(a_ref, b_ref, ap_ref, bp_ref, w_ref, cb_ref, lg_ref, lb_ref, y_ref, tail_ref,
                 xs, y_sc, *, tm):
    i = pl.program_id(1)
    prev = ap_ref[...] * jax.nn.sigmoid(bp_ref[...])
    xs[0:CONV_HALO, :] = jnp.where(i == 0, 0.0, prev)
    xs[CONV_HALO:CONV_HALO + tm, :] = a_ref[...] * jax.nn.sigmoid(b_ref[...])
    _conv_taps(xs, w_ref, CONV_HALO - (CONV_WIDTH - 1), tm, y_sc)
    y_ref[...] = _conv_post(y_sc[...], cb_ref, lg_ref, lb_ref).astype(y_ref.dtype)

    @pl.when(i == pl.num_programs(1) - 1)
    def _():
        tail_ref[0] = xs[tm:tm + CONV_HALO, :]


def _conv_seq(z, w, cb, lg, lb, B, T, tm):
    nt = T // tm
    ca = COL_GLU_A // CONV_CH
    cbk = COL_GLU_B // CONV_CH
    r = tm // CONV_HALO
    prev = lambda c: pl.BlockSpec((CONV_HALO, CONV_CH),
                                  lambda b, i: (jnp.maximum((b * nt + i) * r - 1, 0), c))
    cur = lambda c: pl.BlockSpec((tm, CONV_CH), lambda b, i: (b * nt + i, c))
    vec = pl.BlockSpec((1, CONV_CH), lambda b, i: (0, 0))
    return pl.pallas_call(
        functools.partial(_conv_kernel, tm=tm),
        out_shape=(jax.ShapeDtypeStruct((B * T, CONV_CH), jnp.bfloat16),
                   jax.ShapeDtypeStruct((B, CONV_HALO, CONV_CH), jnp.float32)),
        grid=(B, nt),
        in_specs=[cur(ca), cur(cbk), prev(ca), prev(cbk),
                  pl.BlockSpec(w.shape, lambda b, i: (0, 0)), vec, vec, vec],
        out_specs=(pl.BlockSpec((tm, CONV_CH), lambda b, i: (b * nt + i, 0)),
                   pl.BlockSpec((1, CONV_HALO, CONV_CH), lambda b, i: (b, 0, 0))),
        scratch_shapes=[pltpu.VMEM((tm + CONV_HALO, CONV_CH), jnp.float32),
                        pltpu.VMEM((tm, CONV_CH), jnp.float32)],
        compiler_params=pltpu.CompilerParams(
            dimension_semantics=("parallel", "arbitrary"),
            vmem_limit_bytes=VMEM_LIMIT_BYTES),
        name="conv_seq",
    )(z, z, z, z, w, cb.reshape(1, -1), lg.reshape(1, -1), lb.reshape(1, -1))


def _dec_conv_kernel(a_ref, b_ref, st_ref, w_ref, cb_ref, lg_ref, lb_ref, y_ref, ns_ref, xs, y_sc):
    nbuf = CONV_WIDTH - 1
    xs[0:nbuf, :] = st_ref[0]
    xs[nbuf:nbuf + DEC_SEQ, :] = a_ref[...] * jax.nn.sigmoid(b_ref[...])
    _conv_taps(xs, w_ref, 0, DEC_SEQ, y_sc)
    y_ref[...] = _conv_post(y_sc[...], cb_ref, lg_ref, lb_ref)
    ns_ref[0] = xs[DEC_SEQ:DEC_SEQ + nbuf, :]


def _dec_conv(z, state, w, cb, lg, lb):
    nb, nbuf, _ = state.shape
    ca = COL_GLU_A // CONV_CH
    cbk = COL_GLU_B // CONV_CH
    vec = pl.BlockSpec((1, CONV_CH), lambda b: (0, 0))
    st = pl.BlockSpec((1, nbuf, CONV_CH), lambda b: (b, 0, 0))
    return pl.pallas_call(
        _dec_conv_kernel,
        out_shape=(jax.ShapeDtypeStruct((nb * DEC_SEQ, CONV_CH), jnp.float32),
                   jax.ShapeDtypeStruct(state.shape, jnp.float32)),
        grid=(nb,),
        in_specs=[pl.BlockSpec((DEC_SEQ, CONV_CH), lambda b: (b, ca)),
                  pl.BlockSpec((DEC_SEQ, CONV_CH), lambda b: (b, cbk)),
                  st, pl.BlockSpec(w.shape, lambda b: (0, 0)), vec, vec, vec],
        out_specs=(pl.BlockSpec((DEC_SEQ, CONV_CH), lambda b: (b, 0)), st),
        scratch_shapes=[pltpu.VMEM((nbuf + DEC_SEQ + 2, CONV_CH), jnp.float32),
                        pltpu.VMEM((DEC_SEQ, CONV_CH), jnp.float32)],
        compiler_params=pltpu.CompilerParams(
            dimension_semantics=("parallel",),
            vmem_limit_bytes=VMEM_LIMIT_BYTES),
        name="dec_conv",
    )(z, z, state, w, cb.reshape(1, -1), lg.reshape(1, -1), lb.reshape(1, -1))


def _merge_kernel(oc_ref, os_ref, ow_ref, g_ref, convy_ref, ga_ref, gb_ref, wn_ref, wc_ref,
                  o_ref, onsa_ref):
    @pl.when(pl.program_id(1) == 0)
    def _():
        gz = jax.nn.sigmoid(g_ref[...])
        for h in range(N_HEADS):
            sl = slice(h * HEAD_DIM, (h + 1) * HEAD_DIM)
            o = (gz[:, 3 * h:3 * h + 1] * oc_ref[:, sl]
                 + gz[:, 3 * h + 1:3 * h + 2] * os_ref[:, sl]
                 + gz[:, 3 * h + 2:3 * h + 3] * ow_ref[:, sl])
            onsa_ref[:, sl] = o.astype(jnp.bfloat16)

    a = jnp.dot(onsa_ref[...], wn_ref[...], preferred_element_type=jnp.float32)
    b = jnp.dot(convy_ref[...].astype(jnp.bfloat16), wc_ref[...],
                preferred_element_type=jnp.float32)
    o_ref[...] = (jax.nn.sigmoid(ga_ref[...]) * a
                  + jax.nn.sigmoid(gb_ref[...]) * b).astype(o_ref.dtype)


def _merge(o_cmp, o_sel, o_win, conv_y, z, wn_bf16, wc_bf16, tm, tn):
    n = o_cmp.shape[0]
    ja = COL_GM_A // tn
    jb = COL_GM_B // tn
    jg = COL_G // GATE_PAD
    row = lambda i, j: (i, 0)
    return pl.pallas_call(
        _merge_kernel,
        out_shape=jax.ShapeDtypeStruct((n, D_MODEL), jnp.bfloat16),
        grid=(n // tm, D_MODEL // tn),
        in_specs=[pl.BlockSpec((tm, QW), row),
                  pl.BlockSpec((tm, QW), row),
                  pl.BlockSpec((tm, QW), row),
                  pl.BlockSpec((tm, GATE_PAD), lambda i, j: (i, jg)),
                  pl.BlockSpec((tm, CONV_CH), row),
                  pl.BlockSpec((tm, tn), lambda i, j: (i, ja + j)),
                  pl.BlockSpec((tm, tn), lambda i, j: (i, jb + j)),
                  pl.BlockSpec((QW, tn), lambda i, j: (0, j)),
                  pl.BlockSpec((CONV_CH, tn), lambda i, j: (0, j))],
        out_specs=pl.BlockSpec((tm, tn), lambda i, j: (i, j)),
        scratch_shapes=[pltpu.VMEM((tm, QW), jnp.bfloat16)],
        compiler_params=pltpu.CompilerParams(
            dimension_semantics=("parallel", "arbitrary"),
            vmem_limit_bytes=VMEM_LIMIT_BYTES),
        name="merge",
    )(o_cmp, o_sel, o_win, z, conv_y, z, z, wn_bf16, wc_bf16)


def _out_proj_kernel(m_ref, x_ref, w_ref, g_ref, h_ref, hn_ref):
    h = x_ref[...] + jnp.dot(m_ref[...], w_ref[...], preferred_element_type=jnp.float32)
    h_ref[...] = h
    ms = jnp.mean(h * h, axis=-1, keepdims=True)
    hn_ref[...] = (h * lax.rsqrt(ms + EPS) * g_ref[...]).astype(hn_ref.dtype)


def _out_proj(merged, x, w_bf16, gain, tm):
    n = x.shape[0]
    return pl.pallas_call(
        _out_proj_kernel,
        out_shape=(jax.ShapeDtypeStruct((n, D_MODEL), jnp.float32),
                   jax.ShapeDtypeStruct((n, D_MODEL), jnp.bfloat16)),
        grid=(n // tm,),
        in_specs=[pl.BlockSpec((tm, D_MODEL), lambda i: (i, 0)),
                  pl.BlockSpec((tm, D_MODEL), lambda i: (i, 0)),
                  pl.BlockSpec((D_MODEL, D_MODEL), lambda i: (0, 0)),
                  pl.BlockSpec((1, D_MODEL), lambda i: (0, 0))],
        out_specs=(pl.BlockSpec((tm, D_MODEL), lambda i: (i, 0)),
                   pl.BlockSpec((tm, D_MODEL), lambda i: (i, 0))),
        compiler_params=pltpu.CompilerParams(
            dimension_semantics=("parallel",),
            vmem_limit_bytes=VMEM_LIMIT_BYTES),
        name="out_proj",
    )(merged, x, w_bf16, gain.reshape(1, D_MODEL))


PEER_HALF = PEER_DKEY // 2
PEER_SEL = PEER_HEADS * PEER_TOPK
PEER_TOPK_SHIFT = PEER_TOPK.bit_length() - 1
CAND_B = PEER_TOPK // 2
CAND_B_SHIFT = CAND_B.bit_length() - 1
N_CAND = PEER_TOPK + (PEER_TOPK - 1) * CAND_B
GATE_HALF = PEER_NKEYS // 2
GATE_PITCH = GATE_HALF + 8


def _topk_chains(s_sc, v_sc, i_sc, k):
    n_chain, r, n = s_sc.shape
    row = lax.broadcasted_iota(jnp.int32, (r, n), 0)

    def body(i, carry):
        for c in range(n_chain):
            s = s_sc[c]
            m = jnp.max(s, axis=0, keepdims=True)
            j = jnp.min(jnp.where(s == m, row, r), axis=0, keepdims=True)
            s_sc[c] = jnp.where(row == j, -jnp.inf, s)
            v_sc[c, pl.ds(i, 1), :] = m
            i_sc[c, pl.ds(i, 1), :] = j
        return carry

    lax.fori_loop(0, k, body, 0)


def _pick_rows(idx, table):
    out = jnp.zeros(idx.shape, table.dtype)
    for a in range(PEER_TOPK):
        out = jnp.where(idx == a, table[a:a + 1, :], out)
    return out


def _peer_route_kernel(hn_ref, wq_ref, k1_ref, k2_ref, ia_ref, ib_ref, gt_ref,
                       s_sc, v_sc, i_sc, c_sc, cv_sc, ci_sc, a_sc, b_sc, g_sc):
    qh = jnp.dot(hn_ref[...], wq_ref[...], preferred_element_type=jnp.float32).astype(jnp.bfloat16)
    for h in range(PEER_HEADS):
        q1 = qh[:, h * PEER_DKEY:h * PEER_DKEY + PEER_HALF]
        q2 = qh[:, h * PEER_DKEY + PEER_HALF:(h + 1) * PEER_DKEY]
        s_sc[2 * h] = lax.dot_general(k1_ref[h], q1, _NT, preferred_element_type=jnp.float32)
        s_sc[2 * h + 1] = lax.dot_general(k2_ref[h], q2, _NT, preferred_element_type=jnp.float32)
    _topk_chains(s_sc, v_sc, i_sc, PEER_TOPK)
    for h in range(PEER_HEADS):
        v1, v2 = v_sc[2 * h], v_sc[2 * h + 1]
        c_sc[h] = jnp.concatenate([v1[0:1, :] + v2]
                                  + [v1[a:a + 1, :] + v2[0:CAND_B, :] for a in range(1, PEER_TOPK)],
                                  axis=0)
    _topk_chains(c_sc, cv_sc, ci_sc, PEER_TOPK)
    for h in range(PEER_HEADS):
        sc, pos = cv_sc[h], ci_sc[h]
        rest = pos - PEER_TOPK
        first = pos < PEER_TOPK
        ia = _pick_rows(jnp.where(first, 0, 1 + (rest >> CAND_B_SHIFT)), i_sc[2 * h])
        ib = _pick_rows(jnp.where(first, pos, rest & (CAND_B - 1)), i_sc[2 * h + 1])
        e = jnp.exp(sc - sc[0:1, :])
        gate = e / jnp.sum(e, axis=0, keepdims=True)
        rows = slice(h * PEER_TOPK, (h + 1) * PEER_TOPK)
        a_sc[rows, :] = ia.astype(jnp.float32)
        b_sc[rows, :] = ib.astype(jnp.float32)
        g_sc[rows, :] = gate
    ia_ref[...] = a_sc[...].T
    ib_ref[...] = b_sc[...].T
    gt_ref[...] = g_sc[...].T


def _peer_route(hn, wq_bf16, k1_bf16, k2_bf16, tm):
    n, d = hn.shape
    out = jax.ShapeDtypeStruct((n, PEER_SEL), jnp.float32)
    ospec = pl.BlockSpec((tm, PEER_SEL), lambda i: (i, 0))
    return pl.pallas_call(
        _peer_route_kernel,
        out_shape=(out, out, out),
        grid=(n // tm,),
        in_specs=[pl.BlockSpec((tm, d), lambda i: (i, 0)),
                  pl.BlockSpec(wq_bf16.shape, lambda i: (0, 0)),
                  pl.BlockSpec(k1_bf16.shape, lambda i: (0, 0, 0)),
                  pl.BlockSpec(k2_bf16.shape, lambda i: (0, 0, 0))],
        out_specs=(ospec, ospec, ospec),
        scratch_shapes=[pltpu.VMEM((2 * PEER_HEADS, PEER_NKEYS, tm), jnp.float32),
                        pltpu.VMEM((2 * PEER_HEADS, PEER_TOPK, tm), jnp.float32),
                        pltpu.VMEM((2 * PEER_HEADS, PEER_TOPK, tm), jnp.int32),
                        pltpu.VMEM((PEER_HEADS, N_CAND, tm), jnp.float32),
                        pltpu.VMEM((PEER_HEADS, PEER_TOPK, tm), jnp.float32),
                        pltpu.VMEM((PEER_HEADS, PEER_TOPK, tm), jnp.int32)]
        + [pltpu.VMEM((PEER_SEL, tm), jnp.float32)] * 3,
        compiler_params=pltpu.CompilerParams(
            dimension_semantics=("parallel",),
            vmem_limit_bytes=VMEM_LIMIT_BYTES),
        name="peer_route",
    )(hn, wq_bf16, k1_bf16, k2_bf16)


def _peer_ffn_kernel(hn_ref, ia_ref, ib_ref, gt_ref, u_ref, v_ref, o_ref, w_sc, *, tm, n_i1):
    c = pl.program_id(1)

    @pl.when(c == 0)
    def _():
        o_ref[...] = jnp.zeros(o_ref.shape, jnp.float32)

    steps_per_half = GATE_HALF // n_i1

    @pl.when(c % steps_per_half == 0)
    def _():
        first = (c // steps_per_half) * GATE_HALF
        bf = lambda x: x.astype(jnp.float32).astype(jnp.bfloat16)
        rows_a = bf(first + lax.broadcasted_iota(jnp.int32, (GATE_HALF, PEER_SEL), 0))
        rows_b = bf(lax.broadcasted_iota(jnp.int32, (PEER_NKEYS, PEER_SEL), 0))
        one = jnp.ones((1, PEER_SEL), jnp.bfloat16)
        zero = jnp.zeros((1, PEER_SEL), jnp.bfloat16)

        def token(n, carry):
            a_row = ia_ref[pl.ds(n, 1), :].astype(jnp.bfloat16)
            b_row = ib_ref[pl.ds(n, 1), :].astype(jnp.bfloat16)
            g_row = gt_ref[pl.ds(n, 1), :]
            g_hi = g_row.astype(jnp.bfloat16)
            g_lo = (g_row - g_hi.astype(jnp.float32)).astype(jnp.bfloat16)
            oa = jnp.where(rows_a == a_row, one, zero)
            hit_b = rows_b == b_row
            gb_hi = jnp.where(hit_b, g_hi, zero)
            gb_lo = jnp.where(hit_b, g_lo, zero)
            w = lax.dot_general(jnp.concatenate([oa, oa], axis=1),
                                jnp.concatenate([gb_hi, gb_lo], axis=1), _NT,
                                preferred_element_type=jnp.float32)
            w_sc[pl.ds(pl.multiple_of(n * GATE_PITCH, 8), GATE_HALF), :] = w
            return carry

        lax.fori_loop(0, tm, token, 0, unroll=16)

    act = jax.nn.gelu(lax.dot_general(hn_ref[...], u_ref[...], _NT,
                                      preferred_element_type=jnp.float32))
    row0 = (c * n_i1) & (GATE_HALF - 1)
    parts = []
    for j in range(n_i1):
        wj = w_sc[pl.ds(row0 + j, tm, stride=GATE_PITCH), :]
        parts.append((act[:, j * PEER_NKEYS:(j + 1) * PEER_NKEYS] * wj).astype(jnp.bfloat16))
    o_ref[...] += jnp.dot(jnp.concatenate(parts, axis=1), v_ref[...],
                          preferred_element_type=jnp.float32)


def _peer_ffn(hn, ia, ib, gt, u_bf16, v_bf16, tm, n_i1):
    n, d = hn.shape
    ec = n_i1 * PEER_NKEYS
    assert GATE_HALF % n_i1 == 0
    row = lambda i, c: (i, 0)
    return pl.pallas_call(
        functools.partial(_peer_ffn_kernel, tm=tm, n_i1=n_i1),
        out_shape=jax.ShapeDtypeStruct((n, d), jnp.float32),
        grid=(n // tm, u_bf16.shape[0] // ec),
        in_specs=[pl.BlockSpec((tm, d), row),
                  pl.BlockSpec((tm, PEER_SEL), row),
                  pl.BlockSpec((tm, PEER_SEL), row),
                  pl.BlockSpec((tm, PEER_SEL), row),
                  pl.BlockSpec((ec, d), lambda i, c: (c, 0)),
                  pl.BlockSpec((ec, d), lambda i, c: (c, 0))],
        out_specs=pl.BlockSpec((tm, d), row),
        scratch_shapes=[pltpu.VMEM((tm * GATE_PITCH, PEER_NKEYS), jnp.float32)],
        compiler_params=pltpu.CompilerParams(
            dimension_semantics=("parallel", "arbitrary"),
            vmem_limit_bytes=VMEM_LIMIT_PEER_BYTES),
        name="peer_ffn",
    )(hn, ia, ib, gt, u_bf16, v_bf16)


def _residual_norm_kernel(h_ref, f_ref, g_ref, y_ref):
    y = h_ref[...] + f_ref[...]
    ms = jnp.mean(y * y, axis=-1, keepdims=True)
    y_ref[...] = y * lax.rsqrt(ms + EPS) * g_ref[...]


def _residual_norm(h, f, gain, tm):
    n, d = h.shape
    row = pl.BlockSpec((tm, d), lambda i: (i, 0))
    return pl.pallas_call(
        _residual_norm_kernel,
        out_shape=jax.ShapeDtypeStruct((n, d), jnp.float32),
        grid=(n // tm,),
        in_specs=[row, row, pl.BlockSpec((1, d), lambda i: (0, 0))],
        out_specs=row,
        compiler_params=pltpu.CompilerParams(
            dimension_semantics=("parallel",),
            vmem_limit_bytes=VMEM_LIMIT_BYTES),
        name="residual_norm",
    )(h, f, gain.reshape(1, d))


def _rmsnorm(x, g):
    xf = x.astype(jnp.float32)
    y = xf * lax.rsqrt(jnp.mean(xf * xf, axis=-1, keepdims=True) + EPS)
    return (y * g.astype(jnp.float32)).astype(x.dtype)


def _layernorm(x, g, b):
    xf = x.astype(jnp.float32)
    mu = jnp.mean(xf, axis=-1, keepdims=True)
    var = jnp.mean(jnp.square(xf - mu), axis=-1, keepdims=True)
    return ((xf - mu) * lax.rsqrt(var + EPS) * g.astype(jnp.float32) + b.astype(jnp.float32)).astype(x.dtype)


def _masked_probs(s, mask):
    s = jnp.where(mask, s, NEG)
    m = jnp.max(s, axis=-1, keepdims=True)
    e = jnp.where(mask, jnp.exp(s - m), 0.0)
    return e / jnp.maximum(jnp.sum(e, axis=-1, keepdims=True), 1e-30)


def _attn_probs(q, k, mask):
    s = jnp.einsum('...tgrd,...kgd->...grtk', q, k).astype(jnp.float32) * SCALE
    return _masked_probs(s, mask[..., None, None, :, :])


def _attn_out(p, v):
    return jnp.einsum('...grtk,...kgd->...tgrd', p.astype(v.dtype), v)


def _gather_pages(pool, page_table):
    g = pool[page_table]
    return g.reshape(page_table.shape[0], -1, pool.shape[2], pool.shape[3])


def _compress(kv, pe, w1, w2):
    B, L, G, D = kv.shape
    ch = kv.reshape(B, L // CMP_STRIDE, CMP_STRIDE, G, D)
    h_lo = jnp.einsum('bjlgd,ldh->bjgh', ch, w1[:CMP_STRIDE])
    h_hi = jnp.einsum('bjlgd,ldh->bjgh', ch, w1[CMP_STRIDE:])
    h = h_lo[:, :-1] + h_hi[:, 1:] + jnp.einsum('ld,ldh->h', pe, w1)
    return jnp.einsum('bigh,he->bige', jax.nn.gelu(h), w2)


def _cmp_to_sel(n_cmp, n_sel):
    cs = jnp.arange(n_cmp)[:, None] * CMP_STRIDE
    ss = jnp.arange(n_sel)[None, :] * SEL_LEN
    ov = jnp.clip(jnp.minimum(cs + CMP_LEN, ss + SEL_LEN) - jnp.maximum(cs, ss), 0, None)
    return ov.astype(jnp.float32) / CMP_LEN


def _sel_attend(q, kb, vb, idx, valid, qpos):
    B, T, G, R, D = q.shape
    bi = jnp.arange(B)[:, None, None, None]
    gi = jnp.arange(G)[None, :, None, None]
    kg = kb[bi, idx, :, gi].reshape(B, G, T, -1, D)
    vg = vb[bi, idx, :, gi].reshape(B, G, T, -1, D)
    kpos = idx[..., None] * SEL_LEN + jnp.arange(SEL_LEN)
    mask = (valid[..., None] & (kpos <= qpos[:, None, None])).reshape(B, G, T, -1)
    s = jnp.einsum('btgrd,bgtkd->bgrtk', q, kg).astype(jnp.float32) * SCALE
    p = _masked_probs(s, mask[:, :, None])
    return jnp.einsum('bgrtk,bgtkd->btgrd', p.astype(vg.dtype), vg)


def _nsa_cmp_sel(q, k_c, v_c, k_s, v_s, qpos, lp, sweep_queries):
    B, T, G, R, D = q.shape
    L = k_c.shape[1]
    L_pad = -(-L // SEL_LEN) * SEL_LEN
    pad = ((0, 0), (0, L_pad - L), (0, 0), (0, 0))
    k_c, v_c, k_s, v_s = jnp.pad(k_c, pad), jnp.pad(v_c, pad), jnp.pad(k_s, pad), jnp.pad(v_s, pad)
    k_cmp = _compress(k_c, lp['cmp_pe_k'], lp['cmp_w1_k'], lp['cmp_w2_k'])
    v_cmp = _compress(v_c, lp['cmp_pe_v'], lp['cmp_w1_v'], lp['cmp_w2_v'])
    n_cmp = k_cmp.shape[1]
    cmp_end = jnp.arange(n_cmp) * CMP_STRIDE + CMP_LEN - 1
    p_cmp = _attn_probs(q, k_cmp, cmp_end[None, :] <= qpos[:, None])
    o_cmp = _attn_out(p_cmp, v_cmp)
    n_sel = L_pad // SEL_LEN
    p_slc = jnp.einsum('bgrti,ij->bgtj', p_cmp, _cmp_to_sel(n_cmp, n_sel))
    blk = jnp.arange(n_sel)[None, :]
    cur = (qpos // SEL_LEN)[:, None]
    forced = (blk == 0) | (blk == cur) | (blk == cur - 1)
    score = jnp.where(blk <= cur, p_slc + jnp.where(forced, FORCE_BONUS, 0.0), NEG)
    top_s, top_i = lax.top_k(score, min(SEL_TOPK, n_sel))
    valid = top_s > 0.5 * NEG
    kb = k_s.reshape(B, n_sel, SEL_LEN, G, D)
    vb = v_s.reshape(B, n_sel, SEL_LEN, G, D)
    if sweep_queries:
        nq = T // SEL_Q_BLOCK
        xs = (q.reshape(B, nq, SEL_Q_BLOCK, G, R, D).swapaxes(0, 1),
              top_i.reshape(B, G, nq, SEL_Q_BLOCK, -1).transpose(2, 0, 1, 3, 4),
              valid.reshape(B, G, nq, SEL_Q_BLOCK, -1).transpose(2, 0, 1, 3, 4),
              qpos.reshape(nq, SEL_Q_BLOCK))
        o = lax.map(lambda a: _sel_attend(a[0], kb, vb, a[1], a[2], a[3]), xs)
        o_sel = o.swapaxes(0, 1).reshape(B, T, G, R, D)
    else:
        xs = (q[:, None], kb[:, None], vb[:, None], top_i[:, None], valid[:, None])
        o_sel = lax.map(lambda a: _sel_attend(a[0], a[1], a[2], a[3], a[4], qpos)[0], xs)
    return o_cmp, o_sel


def _window_banded(q, k, v):
    B, T, G, R, D = q.shape
    nb = T // WIN_Q_BLOCK
    pad = ((0, 0), (WINDOW, 0), (0, 0), (0, 0))
    kidx = jnp.arange(nb)[:, None] * WIN_Q_BLOCK + jnp.arange(WINDOW + WIN_Q_BLOCK)[None, :]
    kblk = jnp.pad(k, pad)[:, kidx]
    vblk = jnp.pad(v, pad)[:, kidx]
    kpos = kidx - WINDOW
    qpos = jnp.arange(T).reshape(nb, WIN_Q_BLOCK)
    d = qpos[:, :, None] - kpos[:, None, :]
    mask = (kpos[:, None, :] >= 0) & (d >= 0) & (d < WINDOW)
    o = _attn_out(_attn_probs(q.reshape(B, nb, WIN_Q_BLOCK, G, R, D), kblk, mask), vblk)
    return o.reshape(B, T, G, R, D)


def _window_dense(q, k, v, qpos, kpos):
    d = qpos[:, None] - kpos[None, :]
    return _attn_out(_attn_probs(q, k, (d >= 0) & (d < WINDOW)), v)


def _conv_module(u, buf, w_dw, b_dw, ln_g, ln_b):
    xp = jnp.concatenate([buf, u], axis=1)
    y = lax.conv_general_dilated(xp, w_dw[:, None, :], (1,), 'VALID',
                                 dimension_numbers=('NWC', 'WIO', 'NWC'),
                                 feature_group_count=CONV_CH) + b_dw
    return jax.nn.silu(_layernorm(y, ln_g, ln_b)), xp[:, -(CONV_WIDTH - 1):]


def _split_z(z, B, T):
    kv = lambda c: z[:, c:c + KVW].reshape(B, T, N_KV_HEADS, HEAD_DIM)
    q = z[:, COL_Q:COL_Q + QW].reshape(B, T, N_KV_HEADS, GROUP, HEAD_DIM)
    a = z[:, COL_GLU_A:COL_GLU_A + CONV_CH]
    b = z[:, COL_GLU_B:COL_GLU_B + CONV_CH]
    u = (a * jax.nn.sigmoid(b)).reshape(B, T, CONV_CH)
    return q, kv(COL_KC), kv(COL_VC), kv(COL_KS), kv(COL_VS), kv(COL_KW), kv(COL_VW), u


def _cmp_to_sel_t(n_cmp, n_sel, n_cmp_pad, n_sel_pad=None):
    cs = np.arange(n_cmp)[None, :] * CMP_STRIDE
    ss = np.arange(n_sel)[:, None] * SEL_LEN
    ov = np.clip(np.minimum(cs + CMP_LEN, ss + SEL_LEN) - np.maximum(cs, ss), 0, None)
    ct = np.zeros((n_sel_pad or n_sel, n_cmp_pad), np.float32)
    ct[:n_sel, :n_cmp] = ov.astype(np.float32) / CMP_LEN
    return jnp.asarray(ct, jnp.bfloat16)


def _compress_params(pe, w1, w2):
    w1cat = jnp.concatenate([w1[:CMP_STRIDE], w1[CMP_STRIDE:]], axis=-1).astype(jnp.bfloat16)
    c = jnp.einsum('ld,ldh->h', pe, w1).reshape(1, -1)
    return w1cat, c, w2.astype(jnp.bfloat16)


def _tail_page(z, col, nb):
    x = z[:, col:col + KVW].reshape(nb, DEC_SEQ * N_KV_HEADS, HEAD_DIM)
    x = jnp.pad(x, ((0, 0), (0, PAGE_ROWS - DEC_SEQ * N_KV_HEADS), (0, 0)))
    return x.reshape(nb * PAGE_ROWS, HEAD_DIM)


def _pad_cmp(x_cmp, n_pad):
    x = jnp.transpose(x_cmp, (0, 2, 1, 3))
    return jnp.pad(x, ((0, 0), (0, 0), (0, n_pad - x.shape[2]), (0, 0)))


def _layer(x, lp, past, page_table, prompt, tm):
    B, T, _ = x.shape
    n = B * T
    x2 = x.reshape(n, D_MODEL)
    z = _proj_in(x2, lp['norm_mix'], lp['w_in_p'], tm, 768)
    kv = lambda c: z[:, c:c + KVW].reshape(B, T, N_KV_HEADS, HEAD_DIM)
    k_c, v_c, k_s, v_s = kv(COL_KC), kv(COL_VC), kv(COL_KS), kv(COL_VS)
    cmp_k, cmp_v = lp['cmp_k'], lp['cmp_v']
    conv = (lp['conv_w'], lp['conv_b'], lp['conv_ln_g'], lp['conv_ln_b'])
    if prompt:
        n_ch = T // CMP_STRIDE
        ct = _cmp_to_sel_t(n_ch - 1, T // SEL_LEN, n_ch)
        kcmp, vcmp = _compress_seq(z, cmp_k[0], cmp_v[0], cmp_k[1], cmp_v[1], cmp_k[2], cmp_v[2], B, T)
        o_cmp, bias = _cmp_select(z, kcmp, vcmp, ct, B, T, 128)
        o_sel = _sel_attn(z, bias, B, T, 256, 256)
        o_win = _win_attn(z, B, T, 128)
        wl = min(WINDOW, T)
        new_wk, new_wv = kv(COL_KW)[:, T - wl:], kv(COL_VW)[:, T - wl:]
        conv_y, u_tail = _conv_seq(z, *conv, B, T, 256)
        new_conv = u_tail[:, CONV_HALO - (CONV_WIDTH - 1):]
    else:
        as_rows = lambda a: a.reshape(-1, HEAD_DIM)
        n_ch = -(-(PAST_LEN + T) // SEL_LEN) * SEL_LEN // CMP_STRIDE
        n_sel = -(-(PAST_LEN + T) // SEL_LEN)
        ct = _cmp_to_sel_t(n_ch - 1, n_sel, -(-n_ch // LANES) * LANES, -(-n_sel // 8) * 8)
        o_cmp, bias = _dec_cmp_select(
            page_table, as_rows(past['cmp_k']), as_rows(past['cmp_v']),
            _tail_page(z, COL_KC, B), _tail_page(z, COL_VC, B), z,
            cmp_k[0], cmp_v[0], cmp_k[1], cmp_v[1], cmp_k[2], cmp_v[2], ct, PAST_LEN)
        o_sel = _dec_sel_attn(page_table, as_rows(past['sel_k']), as_rows(past['sel_v']),
                              _tail_page(z, COL_KS, B), _tail_page(z, COL_VS, B), z, bias, PAST_LEN)
        wl = past['win_k'].shape[1]
        o_win, new_wk, new_wv = _dec_win_attn(
            z, as_rows(past['win_k']), as_rows(past['win_v']),
            _tail_page(z, COL_KW, B), _tail_page(z, COL_VW, B), B, PAST_LEN)
        new_wk = new_wk.reshape(B, wl, N_KV_HEADS, HEAD_DIM)
        new_wv = new_wv.reshape(B, wl, N_KV_HEADS, HEAD_DIM)
        conv_y, new_conv = _dec_conv(z, past['conv'], *conv)
    merged = _merge(o_cmp, o_sel, o_win, conv_y, z,
                    lp['w_nsa_out_b'], lp['w_conv_out_b'], 256, 512)
    h, hn = _out_proj(merged, x2, lp['w_out_b'], lp['norm_ffn'], 256)
    ia, ib, gt = _peer_route(hn, lp['peer_wq_b'], lp['peer_k1_b'], lp['peer_k2_b'], 256)
    ffn = _peer_ffn(hn, ia, ib, gt, lp['peer_u_b'], lp['peer_v_b'], min(n, 512), 8)
    y = _residual_norm(h, ffn, lp['norm_final'], 256)
    return y.reshape(B, T, D_MODEL), (k_c, v_c, k_s, v_s, new_wk, new_wv, new_conv)


def kernel(x_prompt, x_sample, cache_cmp_k, cache_cmp_v, cache_sel_k, cache_sel_v, cache_win_k, cache_win_v, state_conv, page_table, norm_mix, w_in, cmp_pe_k, cmp_w1_k, cmp_w2_k, cmp_pe_v, cmp_w1_v, cmp_w2_v, w_nsa_out, conv_w, conv_b, conv_ln_g, conv_ln_b, w_conv_out, w_out, norm_ffn, peer_wq, peer_k1, peer_k2, peer_u, peer_v, norm_final):
    assert DEPTH == 1
    bf = lambda w: w.astype(jnp.bfloat16)
    l = 0
    lp = {'norm_mix': norm_mix[l], 'w_in_p': _permute_w_in(w_in[l]),
          'cmp_k': _compress_params(cmp_pe_k[l], cmp_w1_k[l], cmp_w2_k[l]),
          'cmp_v': _compress_params(cmp_pe_v[l], cmp_w1_v[l], cmp_w2_v[l]),
          'w_nsa_out_b': bf(w_nsa_out[l]), 'conv_w': conv_w[l], 'conv_b': conv_b[l],
          'conv_ln_g': conv_ln_g[l], 'conv_ln_b': conv_ln_b[l],
          'w_conv_out_b': bf(w_conv_out[l]), 'w_out_b': bf(w_out[l]), 'norm_ffn': norm_ffn[l],
          'peer_wq_b': bf(peer_wq[l]), 'peer_k1_b': bf(peer_k1[l]), 'peer_k2_b': bf(peer_k2[l]),
          'peer_u_b': bf(peer_u[l]), 'peer_v_b': bf(peer_v[l]), 'norm_final': norm_final}
    past = {'cmp_k': cache_cmp_k[l], 'cmp_v': cache_cmp_v[l], 'sel_k': cache_sel_k[l],
            'sel_v': cache_sel_v[l], 'win_k': cache_win_k[l], 'win_v': cache_win_v[l],
            'conv': state_conv[l]}
    y_prompt, st_p = _layer(x_prompt, lp, None, None, True, 1024)
    y_sample, st_s = _layer(x_sample, lp, past, page_table, False, 256)
    p_ck, p_cv, p_sk, p_sv, p_wk, p_wv, p_conv = [a[None] for a in st_p]
    s_ck, s_cv, s_sk, s_sv, s_wk, s_wv, s_conv = [a[None] for a in st_s]
    return (y_prompt, y_sample, p_ck, s_ck, p_cv, s_cv, p_sk, s_sk, p_sv, s_sv,
            p_wk, s_wk, p_wv, s_wv, p_conv, s_conv)
```
